```python
import math
import jax, jax.numpy as jnp
from jax import lax
import numpy as np

D_MODEL = 1024
BATCH = 8
SEQ = 8192
DEPTH = 4

CHUNK = 64
N_HEADS_TOTAL = 16
HEAD_DIM = D_MODEL // N_HEADS_TOTAL
N_HEADS_A = 4
LEFT_CHUNKS = 8
BAND = (LEFT_CHUNKS + 1) * CHUNK
MAX_REL = 128
N_HEADS_B = 4
Q_BLOCK = 128
N_HEADS_C = 8
DECAY_LORA = D_MODEL // 16
AAA_LORA = D_MODEL // 16
GATE_LORA = D_MODEL // 8

WIDTH_A = N_HEADS_A * HEAD_DIM
WIDTH_B = N_HEADS_B * HEAD_DIM
WIDTH_C = N_HEADS_C * HEAD_DIM
MIX_WIDTH = WIDTH_A + WIDTH_B + WIDTH_C
COLS_A = 3 * WIDTH_A
COLS_B = 3 * WIDTH_B + N_HEADS_B
COLS_C = 3 * WIDTH_C + DECAY_LORA + AAA_LORA + GATE_LORA
IN_COLS = COLS_A + COLS_B + COLS_C
C_SPLITS = [int(s) for s in np.cumsum([WIDTH_C, WIDTH_C, WIDTH_C, DECAY_LORA, AAA_LORA])]

D_FF = ((8 * D_MODEL // 3 + 127) // 128) * 128
CONV_W = 3
RMS_EPS = 1e-6
LNX_EPS = 64e-5
NEG_INF = -1e30

kernel_name = "hybrid_chunked_fox_rwkv7_convglu"


def rms_norm(x, g):
    xf = x.astype(jnp.float32)
    y = xf * lax.rsqrt(jnp.mean(xf * xf, axis=-1, keepdims=True) + RMS_EPS)
    return (y * g.astype(jnp.float32)).astype(x.dtype)


def to_heads(z, n_heads):
    b, t, _ = z.shape
    return z.reshape(b, t, n_heads, HEAD_DIM)


def chunked_relpos_attention(q, k, v, rel_table):
    b, t, h, d = q.shape
    n_chunks = t // CHUNK
    q = q.transpose(0, 2, 1, 3)
    pad = ((0, 0), (0, 0), (LEFT_CHUNKS * CHUNK, 0), (0, 0))
    kp = jnp.pad(k.transpose(0, 2, 1, 3), pad)
    vp = jnp.pad(v.transpose(0, 2, 1, 3), pad)
    qi = jnp.arange(CHUNK)[:, None]
    kj = jnp.arange(BAND)[None, :]
    rel = kj - LEFT_CHUNKS * CHUNK - qi
    bias = rel_table[:, jnp.clip(rel, -MAX_REL, MAX_REL) + MAX_REL].astype(jnp.float32)

    def one_chunk(c):
        qc = lax.dynamic_slice_in_dim(q, c * CHUNK, CHUNK, axis=2)
        kb = lax.dynamic_slice_in_dim(kp, c * CHUNK, BAND, axis=2)
        vb = lax.dynamic_slice_in_dim(vp, c * CHUNK, BAND, axis=2)
        s = jnp.einsum('bhqd,bhkd->bhqk', qc, kb).astype(jnp.float32) + bias
        valid = (jnp.arange(BAND) + (c - LEFT_CHUNKS) * CHUNK) >= 0
        s = jnp.where(valid, s, NEG_INF)
        p = jax.nn.softmax(s, axis=-1).astype(vb.dtype)
        return jnp.einsum('bhqk,bhkd->bhqd', p, vb)

    out = lax.map(one_chunk, jnp.arange(n_chunks))
    return out.transpose(1, 0, 3, 2, 4).reshape(b, t, h * d)


def forgetting_attention(q, k, v, log_f):
    b, t, h, d = q.shape
    q = q.transpose(0, 2, 1, 3)
    k = k.transpose(0, 2, 1, 3)
    v = v.transpose(0, 2, 1, 3)
    cum = jnp.cumsum(log_f, axis=1).transpose(0, 2, 1)
    kpos = jnp.arange(t)

    def one_block(i):
        start = i * Q_BLOCK
        qb = lax.dynamic_slice_in_dim(q, start, Q_BLOCK, axis=2)
        cq = lax.dynamic_slice_in_dim(cum, start, Q_BLOCK, axis=2)
        s = jnp.einsum('bhqd,bhkd->bhqk', qb, k).astype(jnp.float32)
        s = s + cq[..., :, None] - cum[..., None, :]
        qpos = start + jnp.arange(Q_BLOCK)
        s = jnp.where(kpos[None, :] <= qpos[:, None], s, NEG_INF)
        p = jax.nn.softmax(s, axis=-1).astype(v.dtype)
        return jnp.einsum('bhqk,bhkd->bhqd', p, v)

    out = lax.map(one_block, jnp.arange(t // Q_BLOCK))
    return out.transpose(1, 0, 3, 2, 4).reshape(b, t, h * d)


def rwkv7_step(S, inp):
    r, w, k, v, kk, kka = inp
    sa = jnp.einsum('bhvk,bhk->bhv', S, -kk)
    S = S * w[:, :, None, :] + sa[..., None] * kka[:, :, None, :] + v[..., None] * k[:, :, None, :]
    y = jnp.einsum('bhvk,bhk->bhv', S, r)
    return S, y


def rwkv7_time_mix(u, mu, w0, w2, a0, a2, g2, k_k, k_a, r_k, lnx_g, lnx_b):
    b, t, _ = u.shape
    f32 = jnp.float32
    u_prev = jnp.pad(u, ((0, 0), (1, 0), (0, 0)))[:, :-1]
    u = u + (u_prev - u) * mu
    r, k, v, w_lo, a_lo, g_lo = jnp.split(u, C_SPLITS, axis=-1)
    w = w0 + jnp.tanh(w_lo) @ w2
    w = -jax.nn.softplus(-w.astype(f32)) - 0.5
    decay = jnp.exp(-jnp.exp(w))
    a = jax.nn.sigmoid((a0 + a_lo @ a2).astype(f32))
    g = jax.nn.sigmoid(g_lo) @ g2
    r = to_heads(r, N_HEADS_C).astype(f32)
    k = to_heads(k, N_HEADS_C).astype(f32)
    v = to_heads(v, N_HEADS_C).astype(f32)
    a = to_heads(a, N_HEADS_C)
    decay = to_heads(decay, N_HEADS_C)
    kk = k * k_k.astype(f32)
    kk = kk / jnp.maximum(jnp.sqrt(jnp.sum(kk * kk, axis=-1, keepdims=True)), 1e-12)
    k = k * (1.0 + (a - 1.0) * k_a.astype(f32))
    xs = tuple(z.transpose(1, 0, 2, 3) for z in (r, decay, k, v, kk, kk * a))
    S0 = jnp.zeros((b, N_HEADS_C, HEAD_DIM, HEAD_DIM), f32)
    _, y = lax.scan(rwkv7_step, S0, xs)
    y = y.transpose(1, 0, 2, 3)
    mean = jnp.mean(y, axis=-1, keepdims=True)
    var = jnp.mean(jnp.square(y - mean), axis=-1, keepdims=True)
    y = ((y - mean) * lax.rsqrt(var + LNX_EPS)).reshape(b, t, WIDTH_C)
    y = y * lnx_g.astype(f32) + lnx_b.astype(f32)
    bonus = jnp.sum(r * k * r_k.astype(f32), axis=-1, keepdims=True) * v
    y = y + bonus.reshape(b, t, WIDTH_C)
    return (y * g.astype(f32)).astype(u.dtype)


def conv_glu_ffn(h, w_up, conv_w, conv_b, w_down):
    gate, val = jnp.split(h @ w_up, 2, axis=-1)
    t = gate.shape[1]
    gp = jnp.pad(gate, ((0, 0), (CONV_W - 1, 0), (0, 0)))
    conv = conv_b
    for i in range(CONV_W):
        conv = conv + gp[:, i:i + t] * conv_w[i]
    return (jax.nn.silu(conv) * val) @ w_down


def setup_inputs(seed: int = 0) -> dict:
    key = jax.random.key(seed)
    ks = iter(jax.random.split(key, 32))
    nrm = lambda shape, s: jax.random.normal(next(ks), shape, jnp.float32) * s
    L = DEPTH
    return {
        "x": jax.random.normal(next(ks), (BATCH, SEQ, D_MODEL), jnp.float32),
        "mix_norm_g": 1.0 + nrm((L, D_MODEL), 0.02),
        "w_in": nrm((L, D_MODEL, IN_COLS), D_MODEL ** -0.5),
        "q_norm_a": 1.0 + nrm((L, HEAD_DIM), 0.02),
        "k_norm_a": 1.0 + nrm((L, HEAD_DIM), 0.02),
        "rel_bias": nrm((L, N_HEADS_A, 2 * MAX_REL + 1), 0.5),
        "q_norm_b": 1.0 + nrm((L, HEAD_DIM), 0.02),
        "k_norm_b": 1.0 + nrm((L, HEAD_DIM), 0.02),
        "forget_bias": 3.0 + nrm((L, N_HEADS_B), 0.5),
        "shift_mu": jax.random.uniform(next(ks), (L, COLS_C), jnp.float32),
        "w0": -2.0 + nrm((L, WIDTH_C), 0.5),
        "w2": nrm((L, DECAY_LORA, WIDTH_C), 0.5 * DECAY_LORA ** -0.5),
        "a0": nrm((L, WIDTH_C), 0.1),
        "a2": nrm((L, AAA_LORA, WIDTH_C), AAA_LORA ** -0.5),
        "g2": nrm((L, GATE_LORA, WIDTH_C), GATE_LORA ** -0.5),
        "k_k": 0.85 + nrm((L, N_HEADS_C, HEAD_DIM), 0.05),
        "k_a": 1.0 + nrm((L, N_HEADS_C, HEAD_DIM), 0.05),
        "r_k": nrm((L, N_HEADS_C, HEAD_DIM), 0.1),
        "lnx_g": 1.0 + nrm((L, WIDTH_C), 0.02),
        "lnx_b": nrm((L, WIDTH_C), 0.02),
        "w_out": nrm((L, MIX_WIDTH, D_MODEL), MIX_WIDTH ** -0.5),
        "ffn_norm_g": 1.0 + nrm((L, D_MODEL), 0.02),
        "w_up": nrm((L, D_MODEL, 2 * D_FF), D_MODEL ** -0.5),
        "conv_w": nrm((L, CONV_W, D_FF), CONV_W ** -0.5),
        "conv_b": nrm((L, D_FF), 0.02),
        "w_down": nrm((L, D_FF, D_MODEL), D_FF ** -0.5),
    }


def reference(x, mix_norm_g, w_in, q_norm_a, k_norm_a, rel_bias, q_norm_b, k_norm_b, forget_bias,
              shift_mu, w0, w2, a0, a2, g2, k_k, k_a, r_k, lnx_g, lnx_b, w_out,
              ffn_norm_g, w_up, conv_w, conv_b, w_down):
    scale = HEAD_DIM ** -0.5
    for l in range(DEPTH):
        h = rms_norm(x, mix_norm_g[l])
        proj = h @ w_in[l]
        pa = proj[..., :COLS_A]
        pb = proj[..., COLS_A:COLS_A + COLS_B]
        pc = proj[..., COLS_A + COLS_B:]
        qa, ka, va = jnp.split(pa, 3, axis=-1)
        qa = rms_norm(to_heads(qa, N_HEADS_A), q_norm_a[l]) * scale
        ka = rms_norm(to_heads(ka, N_HEADS_A), k_norm_a[l])
        ya = chunked_relpos_attention(qa, ka, to_heads(va, N_HEADS_A), rel_bias[l])
        qb, kb, vb = jnp.split(pb[..., :3 * WIDTH_B], 3, axis=-1)
        log_f = jax.nn.log_sigmoid((pb[..., 3 * WIDTH_B:] + forget_bias[l]).astype(jnp.float32))
        qb = rms_norm(to_heads(qb, N_HEADS_B), q_norm_b[l]) * scale
        kb = rms_norm(to_heads(kb, N_HEADS_B), k_norm_b[l])
        yb = forgetting_attention(qb, kb, to_heads(vb, N_HEADS_B), log_f)
        yc = rwkv7_time_mix(pc, shift_mu[l], w0[l], w2[l], a0[l], a2[l], g2[l],
                            k_k[l], k_a[l], r_k[l], lnx_g[l], lnx_b[l])
        x = x + jnp.concatenate([ya, yb, yc], axis=-1) @ w_out[l]
        h = rms_norm(x, ffn_norm_g[l])
        x = x + conv_glu_ffn(h, w_up[l], conv_w[l], conv_b[l], w_down[l])
    return x
```

```python
import functools
import math

import jax
import jax.numpy as jnp
from jax import lax
from jax.experimental import pallas as pl
from jax.experimental.pallas import tpu as pltpu

F32 = jnp.float32
BF16 = jnp.bfloat16

D_MODEL = 1024
HEAD_DIM = 64
CHUNK = 64
LEFT_CHUNKS = 8
MAX_REL = 128
N_HEADS_A = 4
N_HEADS_B = 4
N_HEADS_C = 8
WIDTH_A = N_HEADS_A * HEAD_DIM
WIDTH_B = N_HEADS_B * HEAD_DIM
WIDTH_C = N_HEADS_C * HEAD_DIM
DECAY_LORA = 64
AAA_LORA = 64
GATE_LORA = 128
COLS_C = 3 * WIDTH_C + DECAY_LORA + AAA_LORA + GATE_LORA
D_FF = 2816
RMS_EPS = 1e-6
LNX_EPS = 64e-5
NEG_INF = -1e30

LANES = 128
TOK_TILE = 512
ATT_A_TILE = 256
ATT_A_WIN = 3 * ATT_A_TILE
FOX_TILE = 512
RWKV_TILE = 128
FF_SPLITS = (0, 768, 1536, 2304, 2816)
VMEM_LIMIT = 56 * 1024 * 1024

PN_KA, PN_KB, PN_F, PN_C = 0, 512, 1024, 1152
PN_COLS = PN_C + COLS_C
PT_QA, PT_VA, PT_QB, PT_VB, PT_F = 0, 256, 512, 1024, 1280
PT_ROWS = PT_F + 16


def _cparams(n_axes):
    return pltpu.CompilerParams(dimension_semantics=("arbitrary",) * n_axes,
                                vmem_limit_bytes=VMEM_LIMIT)


def _split3(x):
    hi = x.astype(BF16)
    r1 = x - hi.astype(F32)
    mid = r1.astype(BF16)
    lo = (r1 - mid.astype(F32)).astype(BF16)
    return hi, mid, lo


def _split2(x):
    hi = x.astype(BF16)
    lo = (x - hi.astype(F32)).astype(BF16)
    return hi, lo


def _dot(a, b):
    return jnp.dot(a, b, preferred_element_type=F32)


def _dot_nt(a, b):
    return lax.dot_general(a, b, (((1,), (1,)), ((), ())), preferred_element_type=F32)


def _dot_tn(a, b):
    return lax.dot_general(a, b, (((0,), (0,)), ((), ())), preferred_element_type=F32)


def _log_sigmoid(x):
    return jnp.minimum(x, 0.0) - jnp.log(1.0 + jnp.exp(-jnp.abs(x)))


def _softplus(x):
    return jnp.maximum(x, 0.0) + jnp.log(1.0 + jnp.exp(-jnp.abs(x)))


def _sigmoid(x):
    return 1.0 / (1.0 + jnp.exp(-x))


def _inproj_body(x_ref, g_ref, wn_ref, wt_ref, gka_ref, gkb_ref, gqa_ref, gqb_ref, fbn_ref, fbt_ref,
                 selk_ref, selq_ref,
                 qaT_ref, vaT_ref, kA_ref, qbT_ref, vbT_ref, kB_ref, pc_ref,
                 carry_n, carry_t):
    tm = x_ref.shape[1]
    t = pl.program_id(1)

    @pl.when(t == 0)
    def _():
        carry_n[...] = jnp.zeros_like(carry_n)
        carry_t[...] = jnp.zeros_like(carry_t)

    x = x_ref[0]
    ms = jnp.mean(x * x, axis=-1, keepdims=True)
    h = (x * lax.rsqrt(ms + RMS_EPS) * g_ref[...]).astype(BF16)
    pn = _dot(h, wn_ref[...])
    pt = _dot_nt(wt_ref[...], h)

    qa = pt[PT_QA:PT_QA + WIDTH_A].reshape(N_HEADS_A, HEAD_DIM, tm)
    msq = jnp.mean(qa * qa, axis=1, keepdims=True)
    qaT_ref[0] = (qa * lax.rsqrt(msq + RMS_EPS) * gqa_ref[...][None]).astype(BF16)
    vaT_ref[0] = pt[PT_VA:PT_VA + WIDTH_A].reshape(N_HEADS_A, HEAD_DIM, tm).astype(BF16)
    for hh in range(N_HEADS_A):
        k = pn[:, PN_KA + LANES * hh:PN_KA + LANES * (hh + 1)]
        msk = jnp.sum(k * k, axis=-1, keepdims=True) * (1.0 / HEAD_DIM)
        kn = k * lax.rsqrt(msk + RMS_EPS) * gka_ref[...]
        kA_ref[0, hh] = kn[:, :HEAD_DIM].astype(BF16)

    lane_n = lax.broadcasted_iota(jnp.int32, (1, LANES), 1)
    lf_n = jnp.where(lane_n < N_HEADS_B, _log_sigmoid(pn[:, PN_F:PN_F + LANES] + fbn_ref[...]), 0.0)
    row_t = lax.broadcasted_iota(jnp.int32, (16, 1), 0)
    lf_t = jnp.where(row_t < N_HEADS_B, _log_sigmoid(pt[PT_F:PT_F + 16] + fbt_ref[...]), 0.0)
    ri = lax.broadcasted_iota(jnp.int32, (tm, tm), 0)
    ci = lax.broadcasted_iota(jnp.int32, (tm, tm), 1)
    low = (ci <= ri).astype(BF16)
    upp = (ri <= ci).astype(BF16)
    hn, mn, ln = _split3(lf_n)
    c_n = _dot(low, hn) + _dot(low, mn) + _dot(low, ln) + carry_n[0:1, :]
    ht, mt, lt = _split3(lf_t)
    c_t = _dot(ht, upp) + _dot(mt, upp) + _dot(lt, upp) + carry_t[:, 0:1]
    carry_n[...] = carry_n[...] + jnp.sum(lf_n, axis=0, keepdims=True)
    carry_t[...] = carry_t[...] + jnp.sum(lf_t, axis=1, keepdims=True)

    chn, cmn, cln = _split3(c_n)
    kaug = _dot(chn, selk_ref[0]) + _dot(cmn, selk_ref[1]) + _dot(cln, selk_ref[2])
    lane_k = lax.broadcasted_iota(jnp.int32, (1, LANES), 1)
    ones_k = jnp.where((lane_k >= HEAD_DIM) & (lane_k < HEAD_DIM + 3), 1.0, 0.0)
    for hh in range(N_HEADS_B):
        k = pn[:, PN_KB + LANES * hh:PN_KB + LANES * (hh + 1)]
        msk = jnp.sum(k * k, axis=-1, keepdims=True) * (1.0 / HEAD_DIM)
        kn = k * lax.rsqrt(msk + RMS_EPS) * gkb_ref[...]
        kB_ref[0, hh] = (kn + kaug[:, LANES * hh:LANES * (hh + 1)] + ones_k).astype(BF16)

    cht, cmt, clt = _split3(c_t)
    qaug = _dot(selq_ref[0], cht) + _dot(selq_ref[1], cmt) + _dot(selq_ref[2], clt)
    qb = pt[PT_QB:PT_QB + N_HEADS_B * LANES].reshape(N_HEADS_B, LANES, tm)
    msq = jnp.sum(qb * qb, axis=1, keepdims=True) * (1.0 / HEAD_DIM)
    row_q = lax.broadcasted_iota(jnp.int32, (LANES, 1), 0)
    ones_q = jnp.where((row_q >= HEAD_DIM + 3) & (row_q < HEAD_DIM + 6), 1.0, 0.0)
    qn = qb * lax.rsqrt(msq + RMS_EPS) * gqb_ref[...][None]
    qbT_ref[0] = (qn + qaug.reshape(N_HEADS_B, LANES, tm) + ones_q[None]).astype(BF16)
    vbT_ref[0, :, 0] = pt[PT_VB:PT_VB + WIDTH_B].reshape(N_HEADS_B, HEAD_DIM, tm).astype(BF16)

    pc_ref[0] = pn[:, PN_C:PN_C + COLS_C]


def _inproj(x, g, wn, wt, gka, gkb, gqa, gqb, fbn, fbt, selk, selq):
    b, t, d = x.shape
    tm = TOK_TILE
    nt = t // tm
    const = lambda shape: pl.BlockSpec(shape, lambda i, j: (0,) * len(shape))
    out_shape = (
        jax.ShapeDtypeStruct((b, N_HEADS_A, HEAD_DIM, t), BF16),
        jax.ShapeDtypeStruct((b, N_HEADS_A, HEAD_DIM, t), BF16),
        jax.ShapeDtypeStruct((b, N_HEADS_A, t, HEAD_DIM), BF16),
        jax.ShapeDtypeStruct((b, N_HEADS_B, LANES, t), BF16),
        jax.ShapeDtypeStruct((b, N_HEADS_B, nt, HEAD_DIM, tm), BF16),
        jax.ShapeDtypeStruct((b, N_HEADS_B, t, LANES), BF16),
        jax.ShapeDtypeStruct((b, t, COLS_C), F32),
    )
    out_specs = (
        pl.BlockSpec((1, N_HEADS_A, HEAD_DIM, tm), lambda i, j: (i, 0, 0, j)),
        pl.BlockSpec((1, N_HEADS_A, HEAD_DIM, tm), lambda i, j: (i, 0, 0, j)),
        pl.BlockSpec((1, N_HEADS_A, tm, HEAD_DIM), lambda i, j: (i, 0, j, 0)),
        pl.BlockSpec((1, N_HEADS_B, LANES, tm), lambda i, j: (i, 0, 0, j)),
        pl.BlockSpec((1, N_HEADS_B, 1, HEAD_DIM, tm), lambda i, j: (i, 0, j, 0, 0)),
        pl.BlockSpec((1, N_HEADS_B, tm, LANES), lambda i, j: (i, 0, j, 0)),
        pl.BlockSpec((1, tm, COLS_C), lambda i, j: (i, j, 0)),
    )
    in_specs = [
        pl.BlockSpec((1, tm, d), lambda i, j: (i, j, 0)),
        const((1, d)), const(wn.shape), const(wt.shape),
        const(gka.shape), const(gkb.shape), const(gqa.shape), const(gqb.shape),
        const(fbn.shape), const(fbt.shape), const(selk.shape), const(selq.shape),
    ]
    return pl.pallas_call(
        _inproj_body,
        grid=(b, nt),
        in_specs=in_specs,
        out_specs=out_specs,
        out_shape=out_shape,
        scratch_shapes=[pltpu.VMEM((8, LANES), F32), pltpu.VMEM((16, LANES), F32)],
        compiler_params=_cparams(2),
        name="inproj",
    )(x, g, wn, wt, gka, gkb, gqa, gqb, fbn, fbt, selk, selq)


def _relbias_body(tab_ref, o_ref):
    hh = pl.program_id(0)
    kj = lax.broadcasted_iota(jnp.int32, (ATT_A_WIN, ATT_A_TILE), 0)
    qi = lax.broadcasted_iota(jnp.int32, (ATT_A_WIN, ATT_A_TILE), 1)
    rel = jnp.clip(kj - LEFT_CHUNKS * CHUNK - qi, -MAX_REL, MAX_REL) + MAX_REL

    def body(r, acc):
        return jnp.where(rel == r, tab_ref[hh, r], acc)

    bias = lax.fori_loop(0, 2 * MAX_REL + 1, body, jnp.zeros((ATT_A_WIN, ATT_A_TILE), F32))
    kc = kj // CHUNK
    qc = qi // CHUNK
    band = (kc >= qc) & (kc <= qc + LEFT_CHUNKS)
    o_ref[0] = jnp.where(band, bias, NEG_INF)


def _relbias(tab):
    return pl.pallas_call(
        _relbias_body,
        grid=(N_HEADS_A,),
        in_specs=[pl.BlockSpec(memory_space=pltpu.SMEM)],
        out_specs=pl.BlockSpec((1, ATT_A_WIN, ATT_A_TILE), lambda i: (i, 0, 0)),
        out_shape=jax.ShapeDtypeStruct((N_HEADS_A, ATT_A_WIN, ATT_A_TILE), F32),
        compiler_params=_cparams(1),
        name="relbias",
    )(tab)


def _attn_a_body(q_ref, k0_ref, k1_ref, k2_ref, v0_ref, v1_ref, v2_ref, bias_ref, o_ref):
    i = pl.program_id(2)
    tq = ATT_A_TILE
    q = q_ref[0, 0]
    kj = lax.broadcasted_iota(jnp.int32, (tq, 1), 0)
    s_parts = []
    for d, k_ref in enumerate((k0_ref, k1_ref, k2_ref)):
        s = _dot(k_ref[0, 0], q) + bias_ref[0, d * tq:(d + 1) * tq, :]
        s = jnp.where(kj + (i - 2 + d) * tq >= 0, s, NEG_INF)
        s_parts.append(s)
    m = functools.reduce(jnp.maximum, [jnp.max(s, axis=0, keepdims=True) for s in s_parts])
    l = jnp.zeros((1, tq), F32)
    o = jnp.zeros((HEAD_DIM, tq), F32)
    for s, v_ref in zip(s_parts, (v0_ref, v1_ref, v2_ref)):
        p = jnp.exp(s - m)
        l = l + jnp.sum(p, axis=0, keepdims=True)
        o = o + _dot(v_ref[0, 0], p.astype(BF16))
    o_ref[0, 0] = (o / l).astype(BF16)


def _attn_a(qaT, kA, vaT, bias):
    b, nh, _, t = qaT.shape
    tq = ATT_A_TILE
    kspec = lambda d: pl.BlockSpec((1, 1, tq, HEAD_DIM),
                                   lambda bb, hh, i: (bb, hh, jnp.maximum(i - 2 + d, 0), 0))
    vspec = lambda d: pl.BlockSpec((1, 1, HEAD_DIM, tq),
                                   lambda bb, hh, i: (bb, hh, 0, jnp.maximum(i - 2 + d, 0)))
    return pl.pallas_call(
        _attn_a_body,
        grid=(b, nh, t // tq),
        in_specs=[pl.BlockSpec((1, 1, HEAD_DIM, tq), lambda bb, hh, i: (bb, hh, 0, i)),
                  kspec(0), kspec(1), kspec(2), vspec(0), vspec(1), vspec(2),
                  pl.BlockSpec((1, ATT_A_WIN, tq), lambda bb, hh, i: (hh, 0, 0))],
        out_specs=pl.BlockSpec((1, 1, HEAD_DIM, tq), lambda bb, hh, i: (bb, hh, 0, i)),
        out_shape=jax.ShapeDtypeStruct((b, nh, HEAD_DIM, t), BF16),
        compiler_params=_cparams(3),
        name="attn_a",
    )(qaT, kA, kA, kA, vaT, vaT, vaT, bias)


def _fox_body(q_ref, k_ref, v_ref, o_ref):
    i = pl.program_id(2)
    tq = FOX_TILE
    q = q_ref[0, 0]

    def tile(j, carry, diagonal):
        m, l, acc = carry
        ks = k_ref[0, 0, pl.ds(pl.multiple_of(j * tq, tq), tq), :]
        vs = v_ref[0, 0, j]
        s = _dot(ks, q)
        if diagonal:
            kj = lax.broadcasted_iota(jnp.int32, (tq, tq), 0)
            qi = lax.broadcasted_iota(jnp.int32, (tq, tq), 1)
            s = jnp.where(kj <= qi, s, NEG_INF)
        m_new = jnp.maximum(m, jnp.max(s, axis=0, keepdims=True))
        alpha = jnp.exp(m - m_new)
        p = jnp.exp(s - m_new)
        l = alpha * l + jnp.sum(p, axis=0, keepdims=True)
        acc = alpha * acc + _dot(vs, p.astype(BF16))
        return m_new, l, acc

    init = (jnp.full((1, tq), NEG_INF, F32), jnp.zeros((1, tq), F32), jnp.zeros((HEAD_DIM, tq), F32))
    carry = lax.fori_loop(0, i, lambda j, c: tile(j, c, False), init)
    m, l, acc = tile(i, carry, True)
    o_ref[0, 0] = (acc / l).astype(BF16)


def _fox(qbT, kB, vbT):
    b, nh, _, t = qbT.shape
    tq = FOX_TILE
    nk = t // tq
    return pl.pallas_call(
        _fox_body,
        grid=(b, nh, nk),
        in_specs=[pl.BlockSpec((1, 1, LANES, tq), lambda bb, hh, i: (bb, hh, 0, i)),
                  pl.BlockSpec((1, 1, t, LANES), lambda bb, hh, i: (bb, hh, 0, 0)),
                  pl.BlockSpec((1, 1, nk, HEAD_DIM, tq), lambda bb, hh, i: (bb, hh, 0, 0, 0))],
        out_specs=pl.BlockSpec((1, 1, HEAD_DIM, tq), lambda bb, hh, i: (bb, hh, 0, i)),
        out_shape=jax.ShapeDtypeStruct((b, nh, HEAD_DIM, t), BF16),
        compiler_params=_cparams(3),
        name="fox",
    )(qbT, kB, vbT)


def _rwkv_prep_body(pc_ref, mu_ref, w0_ref, w2_ref, a0_ref, a2_ref, g2_ref, kk_ref, ka_ref, rk_ref, e_ref,
                    rt_ref, at_ref, bt_ref, kt_ref, bh_ref, kh_ref, v_ref, wc_ref, bonus_ref, g_ref,
                    sbuf):
    tm = pc_ref.shape[1]
    t = pl.program_id(1)

    @pl.when(t == 0)
    def _():
        sbuf[0:8, :] = jnp.zeros((8, COLS_C), F32)

    u = pc_ref[0]
    sbuf[8:8 + tm, :] = u
    prev = sbuf[7:7 + tm, :]
    sbuf[0:8, :] = u[tm - 8:tm, :]
    u = u + (prev - u) * mu_ref[...]

    c = WIDTH_C
    r = u[:, 0:c]
    k = u[:, c:2 * c]
    v = u[:, 2 * c:3 * c]
    w_lo = u[:, 3 * c:3 * c + DECAY_LORA]
    a_lo = u[:, 3 * c + DECAY_LORA:3 * c + DECAY_LORA + AAA_LORA]
    g_lo = u[:, 3 * c + DECAY_LORA + AAA_LORA:COLS_C]

    w = w0_ref[...] + _dot(jnp.tanh(w_lo).astype(BF16), w2_ref[...])
    w = -_softplus(-w) - 0.5
    ld = -jnp.exp(w)
    a = _sigmoid(a0_ref[...] + _dot(a_lo.astype(BF16), a2_ref[...]))
    g = _dot(_sigmoid(g_lo).astype(BF16), g2_ref[...])

    e = e_ref[...]
    kk = k * kk_ref[...]
    s_hi, s_lo = _split2(kk * kk)
    nrm2 = _dot(s_hi, e) + _dot(s_lo, e)
    kkn = kk * lax.rsqrt(jnp.maximum(nrm2, 1e-24))
    k2 = k * (1.0 + (a - 1.0) * ka_ref[...])
    kka = kkn * a
    b_hi, b_lo = _split2(r * k2 * rk_ref[...])
    bonus = (_dot(b_hi, e) + _dot(b_lo, e)) * v

    ri = lax.broadcasted_iota(jnp.int32, (2 * CHUNK, CHUNK), 0)
    ci = lax.broadcasted_iota(jnp.int32, (2 * CHUNK, CHUNK), 1)
    tri = ((ci <= ri) | (ri >= CHUNK)).astype(BF16)
    for cc in range(tm // CHUNK):
        sl = slice(cc * CHUNK, (cc + 1) * CHUNK)
        h3, m3, l3 = _split3(ld[sl])
        cum = _dot(tri, h3) + _dot(tri, m3) + _dot(tri, l3)
        lc = cum[0:CHUNK]
        tot = cum[CHUNK:2 * CHUNK]
        e_neg = jnp.exp(-lc)
        e_rem = jnp.exp(tot - lc)
        rt_ref[0, sl, :] = (r[sl] * jnp.exp(lc)).astype(BF16)
        at_ref[0, sl, :] = (-kkn[sl] * jnp.exp(lc - ld[sl])).astype(BF16)
        bt_ref[0, sl, :] = (kka[sl] * e_neg).astype(BF16)
        kt_ref[0, sl, :] = (k2[sl] * e_neg).astype(BF16)
        bh_ref[0, sl, :] = (kka[sl] * e_rem).astype(BF16)
        kh_ref[0, sl, :] = (k2[sl] * e_rem).astype(BF16)
        wc_ref[0, cc] = jnp.exp(tot[0:1])
    v_ref[0] = v.astype(BF16)
    bonus_ref[0] = bonus
    g_ref[0] = g


def _rwkv_prep(pc, mu, w0, w2, a0, a2, g2, k_k, k_a, r_k, e):
    b, t, _ = pc.shape
    tm = TOK_TILE
    const = lambda shape: pl.BlockSpec(shape, lambda i, j: (0,) * len(shape))
    tok = lambda: pl.BlockSpec((1, tm, WIDTH_C), lambda i, j: (i, j, 0))
    bf = jax.ShapeDtypeStruct((b, t, WIDTH_C), BF16)
    f32 = jax.ShapeDtypeStruct((b, t, WIDTH_C), F32)
    out_shape = (bf, bf, bf, bf, bf, bf, bf,
                 jax.ShapeDtypeStruct((b, t // CHUNK, 1, WIDTH_C), F32), f32, f32)
    out_specs = (tok(), tok(), tok(), tok(), tok(), tok(), tok(),
                 pl.BlockSpec((1, tm // CHUNK, 1, WIDTH_C), lambda i, j: (i, j, 0, 0)), tok(), tok())
    return pl.pallas_call(
        _rwkv_prep_body,
        grid=(b, t // tm),
        in_specs=[pl.BlockSpec((1, tm, COLS_C), lambda i, j: (i, j, 0)),
                  const(mu.shape), const(w0.shape), const(w2.shape), const(a0.shape), const(a2.shape),
                  const(g2.shape), const(k_k.shape), const(k_a.shape), const(r_k.shape), const(e.shape)],
        out_specs=out_specs,
        out_shape=out_shape,
        scratch_shapes=[pltpu.VMEM((tm + 8, COLS_C), F32)],
        compiler_params=_cparams(2),
        name="rwkv_prep",
    )(pc, mu, w0, w2, a0, a2, g2, k_k, k_a, r_k, e)


def _bmm(a, b):
    return lax.dot_general(a, b, (((2,), (1,)), ((0,), (0,))), preferred_element_type=F32)


def _bmm_nt(a, b):
    return lax.dot_general(a, b, (((2,), (2,)), ((0,), (0,))), preferred_element_type=F32)


def _bmm_tn(a, b):
    return lax.dot_general(a, b, (((1,), (1,)), ((0,), (0,))), preferred_element_type=F32)


def _rwkv_chunk_body(rt_ref, at_ref, bt_ref, kt_ref, bh_ref, kh_ref, v_ref, wc_ref, bonus_ref, g_ref,
                     lg_ref, lb_ref, o_ref, h_ref):
    tm = rt_ref.shape[1]
    nc = tm // CHUNK
    nh = N_HEADS_C
    t = pl.program_id(1)

    @pl.when(t == 0)
    def _():
        h_ref[...] = jnp.zeros_like(h_ref)

    def heads(ref):
        x = ref[0]
        return jnp.stack([x[c * CHUNK:(c + 1) * CHUNK, hh * HEAD_DIM:(hh + 1) * HEAD_DIM]
                          for hh in range(nh) for c in range(nc)])

    rt, at, bt, kt, bh, kh, v = (heads(r) for r in (rt_ref, at_ref, bt_ref, kt_ref, bh_ref, kh_ref, v_ref))
    ri = lax.broadcasted_iota(jnp.int32, (1, CHUNK, CHUNK), 1)
    ci = lax.broadcasted_iota(jnp.int32, (1, CHUNK, CHUNK), 2)
    strict = ri > ci
    incl = ri >= ci
    eye = ri == ci

    a_ab = jnp.where(strict, _bmm_nt(at, bt), 0.0)
    a_ak = jnp.where(strict, _bmm_nt(at, kt), 0.0).astype(BF16)
    m_rb = jnp.where(incl, _bmm_nt(rt, bt), 0.0).astype(BF16)
    m_rk = jnp.where(incl, _bmm_nt(rt, kt), 0.0).astype(BF16)

    x = a_ab
    tinv = jnp.where(eye, 1.0, 0.0) + x
    for _ in range(5):
        xb = x.astype(BF16)
        x = _bmm(xb, xb)
        tinv = tinv + _bmm(tinv.astype(BF16), x.astype(BF16))
    tinv = tinv.astype(BF16)

    av = _bmm(a_ak, v)
    rv = _bmm(m_rk, v)
    p = _bmm(tinv, at).astype(BF16)
    u0 = _bmm(tinv, av.astype(BF16)).astype(BF16)
    q = rt.astype(F32) + _bmm(m_rb, p)
    y0 = rv + _bmm(m_rb, u0)
    wc = wc_ref[0]
    wdiag = jnp.stack([wc[c, :, hh * HEAD_DIM:(hh + 1) * HEAD_DIM]
                       for hh in range(nh) for c in range(nc)])
    gmat = jnp.where(eye, wdiag, 0.0) + _bmm_tn(bh, p)
    hadd = _bmm_tn(bh, u0) + _bmm_tn(kh, v)

    q = q.astype(BF16).reshape(nh, nc, CHUNK, HEAD_DIM)
    gmat = gmat.astype(BF16).reshape(nh, nc, HEAD_DIM, HEAD_DIM)
    y0 = y0.reshape(nh, nc, CHUNK, HEAD_DIM)
    hadd = hadd.reshape(nh, nc, HEAD_DIM, HEAD_DIM)
    hs = h_ref[...]
    ys = []
    for c in range(nc):
        h_hi, h_lo = _split2(hs)
        ys.append(_bmm(q[:, c], h_hi) + _bmm(q[:, c], h_lo) + y0[:, c])
        hs = _bmm(gmat[:, c], h_hi) + _bmm(gmat[:, c], h_lo) + hadd[:, c]
    h_ref[...] = hs

    y = jnp.concatenate(ys, axis=1)
    mean = jnp.mean(y, axis=-1, keepdims=True)
    yc = y - mean
    var = jnp.mean(yc * yc, axis=-1, keepdims=True)
    yn = yc * lax.rsqrt(var + LNX_EPS)
    yfull = jnp.concatenate([yn[hh] for hh in range(nh)], axis=-1)
    out = (yfull * lg_ref[...] + lb_ref[...] + bonus_ref[0]) * g_ref[0]
    o_ref[0] = out.astype(BF16)


def _rwkv_chunk(rt, at, bt, kt, bh, kh, v, wc, bonus, g, lnx_g, lnx_b):
    b, t, _ = rt.shape
    tm = RWKV_TILE
    const = lambda shape: pl.BlockSpec(shape, lambda i, j: (0,) * len(shape))
    tok = lambda: pl.BlockSpec((1, tm, WIDTH_C), lambda i, j: (i, j, 0))
    return pl.pallas_call(
        _rwkv_chunk_body,
        grid=(b, t // tm),
        in_specs=[tok(), tok(), tok(), tok(), tok(), tok(), tok(),
                  pl.BlockSpec((1, tm // CHUNK, 1, WIDTH_C), lambda i, j: (i, j, 0, 0)),
                  tok(), tok(), const(lnx_g.shape), const(lnx_b.shape)],
        out_specs=tok(),
        out_shape=jax.ShapeDtypeStruct((b, t, WIDTH_C), BF16),
        scratch_shapes=[pltpu.VMEM((N_HEADS_C, HEAD_DIM, HEAD_DIM), F32)],
        compiler_params=_cparams(2),
        name="rwkv_chunk",
    )(rt, at, bt, kt, bh, kh, v, wc, bonus, g, lnx_g, lnx_b)


def _outproj_body(x_ref, ya_ref, yb_ref, yc_ref, wa_ref, wb_ref, wc_ref, o_ref):
    acc = _dot_tn(ya_ref[0], wa_ref[...])
    acc = acc + _dot_tn(yb_ref[0], wb_ref[...])
    acc = acc + _dot(yc_ref[0], wc_ref[...])
    o_ref[0] = x_ref[0] + acc


def _outproj(x, yaT, ybT, yc, wa, wb, wc):
    b, t, d = x.shape
    tm = TOK_TILE
    const = lambda shape: pl.BlockSpec(shape, lambda i, j: (0,) * len(shape))
    return pl.pallas_call(
        _outproj_body,
        grid=(b, t // tm),
        in_specs=[pl.BlockSpec((1, tm, d), lambda i, j: (i, j, 0)),
                  pl.BlockSpec((1, WIDTH_A, tm), lambda i, j: (i, 0, j)),
                  pl.BlockSpec((1, WIDTH_B, tm), lambda i, j: (i, 0, j)),
                  pl.BlockSpec((1, tm, WIDTH_C), lambda i, j: (i, j, 0)),
                  const(wa.shape), const(wb.shape), const(wc.shape)],
        out_specs=pl.BlockSpec((1, tm, d), lambda i, j: (i, j, 0)),
        out_shape=jax.ShapeDtypeStruct((b, t, d), F32),
        compiler_params=_cparams(2),
        name="outproj",
    )(x, yaT, ybT, yc, wa, wb, wc)


def _ffn_body(x_ref, g_ref, wg_ref, wv_ref, cw_ref, cb_ref, wd_ref, o_ref, gbuf):
    tm = x_ref.shape[1]
    t = pl.program_id(1)

    @pl.when(t == 0)
    def _():
        gbuf[0:8, :] = jnp.zeros((8, D_FF), F32)

    x = x_ref[0]
    ms = jnp.mean(x * x, axis=-1, keepdims=True)
    h = (x * lax.rsqrt(ms + RMS_EPS) * g_ref[...]).astype(BF16)
    acc = x
    for lo, hi in zip(FF_SPLITS[:-1], FF_SPLITS[1:]):
        gate = _dot(h, wg_ref[:, lo:hi])
        val = _dot(h, wv_ref[:, lo:hi])
        gbuf[8:8 + tm, lo:hi] = gate
        g1 = gbuf[7:7 + tm, lo:hi]
        g2 = gbuf[6:6 + tm, lo:hi]
        gbuf[0:8, lo:hi] = gate[tm - 8:tm, :]
        conv = cb_ref[:, lo:hi] + g2 * cw_ref[0:1, lo:hi] + g1 * cw_ref[1:2, lo:hi] + gate * cw_ref[2:3, lo:hi]
        act = (conv * _sigmoid(conv) * val).astype(BF16)
        acc = acc + _dot(act, wd_ref[lo:hi, :])
    o_ref[0] = acc


def _ffn(x, g, wg, wv, cw, cb, wd):
    b, t, d = x.shape
    tm = TOK_TILE
    const = lambda shape: pl.BlockSpec(shape, lambda i, j: (0,) * len(shape),
                                       pipeline_mode=pl.Buffered(1))
    return pl.pallas_call(
        _ffn_body,
        grid=(b, t // tm),
        in_specs=[pl.BlockSpec((1, tm, d), lambda i, j: (i, j, 0)),
                  const(g.shape), const(wg.shape), const(wv.shape), const(cw.shape), const(cb.shape),
                  const(wd.shape)],
        out_specs=pl.BlockSpec((1, tm, d), lambda i, j: (i, j, 0)),
        out_shape=jax.ShapeDtypeStruct((b, t, d), F32),
        scratch_shapes=[pltpu.VMEM((tm + 8, D_FF), F32)],
        compiler_params=_cparams(2),
        name="ffn",
    )(x, g, wg, wv, cw, cb, wd)


def _pad_heads_cols(w, nh):
    d = w.shape[0]
    w = w.reshape(d, nh, HEAD_DIM)
    return jnp.pad(w, ((0, 0), (0, 0), (0, LANES - HEAD_DIM))).reshape(d, nh * LANES)


def _layer_params(l, w_in, q_norm_a, k_norm_a, q_norm_b, k_norm_b, forget_bias):
    w = w_in[l]
    scale = HEAD_DIM ** -0.5
    a0, b0 = 0, 3 * WIDTH_A
    c0 = b0 + 3 * WIDTH_B + N_HEADS_B
    qa, ka, va = (w[:, a0 + i * WIDTH_A:a0 + (i + 1) * WIDTH_A] for i in range(3))
    qb, kb, vb = (w[:, b0 + i * WIDTH_B:b0 + (i + 1) * WIDTH_B] for i in range(3))
    fg = w[:, b0 + 3 * WIDTH_B:c0]
    wc = w[:, c0:]
    wn = jnp.concatenate([_pad_heads_cols(ka, N_HEADS_A), _pad_heads_cols(kb, N_HEADS_B),
                          jnp.pad(fg, ((0, 0), (0, LANES - N_HEADS_B))), wc], axis=1).astype(BF16)
    wt = jnp.concatenate([qa, va, _pad_heads_cols(qb, N_HEADS_B), vb,
                          jnp.pad(fg, ((0, 0), (0, 16 - N_HEADS_B)))], axis=1).T.astype(BF16)
    pad = LANES - HEAD_DIM
    gka = jnp.pad(k_norm_a[l], (0, pad)).reshape(1, LANES)
    gkb = jnp.pad(k_norm_b[l], (0, pad)).reshape(1, LANES)
    gqa = (q_norm_a[l] * scale).reshape(HEAD_DIM, 1)
    gqb = jnp.pad(q_norm_b[l] * scale, (0, pad)).reshape(LANES, 1)
    fbn = jnp.pad(forget_bias[l], (0, LANES - N_HEADS_B)).reshape(1, LANES)
    fbt = jnp.pad(forget_bias[l], (0, 16 - N_HEADS_B)).reshape(16, 1)
    return wn, wt, gka, gkb, gqa, gqb, fbn, fbt


def _selectors():
    import numpy as np
    selk = np.zeros((3, LANES, N_HEADS_B * LANES), np.float32)
    selq = np.zeros((3, N_HEADS_B * LANES, 16), np.float32)
    for p in range(3):
        for hh in range(N_HEADS_B):
            selk[p, hh, hh * LANES + HEAD_DIM + 3 + p] = -1.0
            selq[p, hh * LANES + HEAD_DIM + p, hh] = 1.0
    e = np.kron(np.eye(N_HEADS_C, dtype=np.float32), np.ones((HEAD_DIM, HEAD_DIM), np.float32))
    return jnp.asarray(selk, BF16), jnp.asarray(selq, BF16), jnp.asarray(e, BF16)


def kernel(x, mix_norm_g, w_in, q_norm_a, k_norm_a, rel_bias, q_norm_b, k_norm_b, forget_bias, shift_mu, w0, w2,
           a0, a2, g2, k_k, k_a, r_k, lnx_g, lnx_b, w_out, ffn_norm_g, w_up, conv_w, conv_b, w_down):
    depth = w_in.shape[0]
    b, t, d = x.shape
    selk, selq, e = _selectors()
    row = lambda v: v.reshape(1, -1)
    for l in range(depth):
        wn, wt, gka, gkb, gqa, gqb, fbn, fbt = _layer_params(
            l, w_in, q_norm_a, k_norm_a, q_norm_b, k_norm_b, forget_bias)
        qaT, vaT, kA, qbT, vbT, kB, pc = _inproj(x, row(mix_norm_g[l]), wn, wt, gka, gkb, gqa, gqb,
                                                 fbn, fbt, selk, selq)
        bias = _relbias(rel_bias[l])
        yaT = _attn_a(qaT, kA, vaT, bias)
        ybT = _fox(qbT, kB, vbT)
        rt, at, bt, kt, bh, kh, v, wcum, bonus, g = _rwkv_prep(
            pc, row(shift_mu[l]), row(w0[l]), w2[l].astype(BF16), row(a0[l]), a2[l].astype(BF16),
            g2[l].astype(BF16), row(k_k[l]), row(k_a[l]), row(r_k[l]), e)
        yc = _rwkv_chunk(rt, at, bt, kt, bh, kh, v, wcum, bonus, g, row(lnx_g[l]), row(lnx_b[l]))
        wo = w_out[l].astype(BF16)
        x = _outproj(x, yaT.reshape(b, WIDTH_A, t), ybT.reshape(b, WIDTH_B, t), yc,
                     wo[:WIDTH_A], wo[WIDTH_A:WIDTH_A + WIDTH_B], wo[WIDTH_A + WIDTH_B:])
        wu = w_up[l].astype(BF16)
        x = _ffn(x, row(ffn_norm_g[l]), wu[:, :D_FF], wu[:, D_FF:], conv_w[l], row(conv_b[l]),
                 w_down[l].astype(BF16))
    return x
```

```python
import functools
import math

import jax
import jax.numpy as jnp
from jax import lax
from jax.experimental import pallas as pl
from jax.experimental.pallas import tpu as pltpu

F32 = jnp.float32
BF16 = jnp.bfloat16

D_MODEL = 1024
HEAD_DIM = 64
CHUNK = 64
LEFT_CHUNKS = 8
MAX_REL = 128
N_HEADS_A = 4
N_HEADS_B = 4
N_HEADS_C = 8
WIDTH_A = N_HEADS_A * HEAD_DIM
WIDTH_B = N_HEADS_B * HEAD_DIM
WIDTH_C = N_HEADS_C * HEAD_DIM
DECAY_LORA = 64
AAA_LORA = 64
GATE_LORA = 128
COLS_C = 3 * WIDTH_C + DECAY_LORA + AAA_LORA + GATE_LORA
D_FF = 2816
RMS_EPS = 1e-6
LNX_EPS = 64e-5
NEG_INF = -1e30

LANES = 128
TOK_TILE = 512
ATT_A_TILE = 256
ATT_A_WIN = 3 * ATT_A_TILE
FOX_TILE = 512
FOX_GROUP = 4
RWKV_TILE = 128
FF_SPLITS = (0, 768, 1536, 2304, 2816)
VMEM_LIMIT = 56 * 1024 * 1024

PN_KA, PN_KB, PN_F, PN_C = 0, 512, 1024, 1152
PN_COLS = PN_C + COLS_C
VB_ROWS = 80
PT_QA = 0
PT_VA = PT_QA + WIDTH_A
PT_QB = PT_VA + N_HEADS_A * VB_ROWS
PT_VB = PT_QB + N_HEADS_B * LANES
PT_F = PT_VB + N_HEADS_B * VB_ROWS
PT_ROWS = PT_F + 16
LOG2E = 1.4426950408889634
FAST_MAX_SHIFT = 40.0


def _cparams(n_axes):
    return pltpu.CompilerParams(dimension_semantics=("arbitrary",) * n_axes,
                                vmem_limit_bytes=VMEM_LIMIT)


def _split3(x):
    hi = x.astype(BF16)
    r1 = x - hi.astype(F32)
    mid = r1.astype(BF16)
    lo = (r1 - mid.astype(F32)).astype(BF16)
    return hi, mid, lo


def _split2(x):
    hi = x.astype(BF16)
    lo = (x - hi.astype(F32)).astype(BF16)
    return hi, lo


def _dot(a, b):
    return jnp.dot(a, b, preferred_element_type=F32)


def _dot_nt(a, b):
    return lax.dot_general(a, b, (((1,), (1,)), ((), ())), preferred_element_type=F32)


def _dot_tn(a, b):
    return lax.dot_general(a, b, (((0,), (0,)), ((), ())), preferred_element_type=F32)


def _log_sigmoid(x):
    return jnp.minimum(x, 0.0) - jnp.log(1.0 + jnp.exp(-jnp.abs(x)))


def _softplus(x):
    return jnp.maximum(x, 0.0) + jnp.log(1.0 + jnp.exp(-jnp.abs(x)))


def _sigmoid(x):
    return 1.0 / (1.0 + jnp.exp(-x))


def _inproj_body(x_ref, g_ref, wn_ref, wt_ref, gka_ref, gkb_ref, gqa_ref, gqb_ref, fbn_ref, fbt_ref,
                 selk_ref, selq_ref, shift_ref,
                 qaT_ref, vaT_ref, kA_ref, qbT_ref, vbT_ref, kB_ref, pc_ref,
                 carry_n, carry_t):
    tm = x_ref.shape[1]
    t = pl.program_id(1)

    @pl.when(t == 0)
    def _():
        carry_n[...] = jnp.zeros_like(carry_n)
        carry_t[...] = jnp.zeros_like(carry_t)

    x = x_ref[0]
    ms = jnp.mean(x * x, axis=-1, keepdims=True)
    h = (x * lax.rsqrt(ms + RMS_EPS) * g_ref[...]).astype(BF16)
    pn = _dot(h, wn_ref[...])
    pt = _dot_nt(wt_ref[...], h)

    qa = pt[PT_QA:PT_QA + WIDTH_A].reshape(N_HEADS_A, HEAD_DIM, tm)
    msq = jnp.mean(qa * qa, axis=1, keepdims=True)
    qaT_ref[0] = (qa * lax.rsqrt(msq + RMS_EPS) * gqa_ref[...][None]).astype(BF16)
    row_v = lax.broadcasted_iota(jnp.int32, (VB_ROWS, 1), 0)
    ones_v = jnp.where(row_v == HEAD_DIM, 1.0, 0.0)
    va = pt[PT_VA:PT_VA + N_HEADS_A * VB_ROWS].reshape(N_HEADS_A, VB_ROWS, tm)
    vaT_ref[0] = (va + ones_v[None]).astype(BF16)
    for hh in range(N_HEADS_A):
        k = pn[:, PN_KA + LANES * hh:PN_KA + LANES * (hh + 1)]
        msk = jnp.sum(k * k, axis=-1, keepdims=True) * (1.0 / HEAD_DIM)
        kn = k * lax.rsqrt(msk + RMS_EPS) * gka_ref[...]
        kA_ref[0, hh] = kn[:, :HEAD_DIM].astype(BF16)

    lane_n = lax.broadcasted_iota(jnp.int32, (1, LANES), 1)
    lf_n = jnp.where(lane_n < N_HEADS_B, LOG2E * _log_sigmoid(pn[:, PN_F:PN_F + LANES] + fbn_ref[...]), 0.0)
    row_t = lax.broadcasted_iota(jnp.int32, (16, 1), 0)
    lf_t = jnp.where(row_t < N_HEADS_B, LOG2E * _log_sigmoid(pt[PT_F:PT_F + 16] + fbt_ref[...]), 0.0)
    ri = lax.broadcasted_iota(jnp.int32, (tm, tm), 0)
    ci = lax.broadcasted_iota(jnp.int32, (tm, tm), 1)
    low = (ci <= ri).astype(BF16)
    upp = (ri <= ci).astype(BF16)
    hn, mn, ln = _split3(lf_n)
    c_n = _dot(low, hn) + _dot(low, mn) + _dot(low, ln) + carry_n[0:1, :]
    ht, mt, lt = _split3(lf_t)
    c_t = _dot(ht, upp) + _dot(mt, upp) + _dot(lt, upp) + carry_t[:, 0:1]
    carry_n[...] = carry_n[...] + jnp.sum(lf_n, axis=0, keepdims=True)
    carry_t[...] = carry_t[...] + jnp.sum(lf_t, axis=1, keepdims=True)

    chn, cmn, cln = _split3(c_n)
    kaug = _dot(chn, selk_ref[0]) + _dot(cmn, selk_ref[1]) + _dot(cln, selk_ref[2])
    lane_k = lax.broadcasted_iota(jnp.int32, (1, LANES), 1)
    ones_k = jnp.where((lane_k >= HEAD_DIM) & (lane_k < HEAD_DIM + 3), 1.0, 0.0)
    for hh in range(N_HEADS_B):
        k = pn[:, PN_KB + LANES * hh:PN_KB + LANES * (hh + 1)]
        msk = jnp.sum(k * k, axis=-1, keepdims=True) * (1.0 / HEAD_DIM)
        kn = k * lax.rsqrt(msk + RMS_EPS) * gkb_ref[...]
        kB_ref[0, hh] = (kn + kaug[:, LANES * hh:LANES * (hh + 1)] + ones_k).astype(BF16)

    cht, cmt, clt = _split3(c_t - shift_ref[...])
    qaug = _dot(selq_ref[0], cht) + _dot(selq_ref[1], cmt) + _dot(selq_ref[2], clt)
    qb = pt[PT_QB:PT_QB + N_HEADS_B * LANES].reshape(N_HEADS_B, LANES, tm)
    msq = jnp.sum(qb * qb, axis=1, keepdims=True) * (1.0 / HEAD_DIM)
    row_q = lax.broadcasted_iota(jnp.int32, (LANES, 1), 0)
    ones_q = jnp.where((row_q >= HEAD_DIM + 3) & (row_q < HEAD_DIM + 6), 1.0, 0.0)
    qn = qb * lax.rsqrt(msq + RMS_EPS) * gqb_ref[...][None]
    qbT_ref[0] = (qn + qaug.reshape(N_HEADS_B, LANES, tm) + ones_q[None]).astype(BF16)
    vb = pt[PT_VB:PT_VB + N_HEADS_B * VB_ROWS].reshape(N_HEADS_B, VB_ROWS, tm)
    vbT_ref[0, :, 0] = (vb + ones_v[None]).astype(BF16)

    pc_ref[0] = pn[:, PN_C:PN_C + COLS_C]


def _inproj(x, g, wn, wt, gka, gkb, gqa, gqb, fbn, fbt, selk, selq, shift):
    b, t, d = x.shape
    tm = TOK_TILE
    nt = t // tm
    const = lambda shape: pl.BlockSpec(shape, lambda i, j: (0,) * len(shape))
    out_shape = (
        jax.ShapeDtypeStruct((b, N_HEADS_A, HEAD_DIM, t), BF16),
        jax.ShapeDtypeStruct((b, N_HEADS_A, VB_ROWS, t), BF16),
        jax.ShapeDtypeStruct((b, N_HEADS_A, t, HEAD_DIM), BF16),
        jax.ShapeDtypeStruct((b, N_HEADS_B, LANES, t), BF16),
        jax.ShapeDtypeStruct((b, N_HEADS_B, nt, VB_ROWS, tm), BF16),
        jax.ShapeDtypeStruct((b, N_HEADS_B, t, LANES), BF16),
        jax.ShapeDtypeStruct((b, t, COLS_C), F32),
    )
    out_specs = (
        pl.BlockSpec((1, N_HEADS_A, HEAD_DIM, tm), lambda i, j: (i, 0, 0, j)),
        pl.BlockSpec((1, N_HEADS_A, VB_ROWS, tm), lambda i, j: (i, 0, 0, j)),
        pl.BlockSpec((1, N_HEADS_A, tm, HEAD_DIM), lambda i, j: (i, 0, j, 0)),
        pl.BlockSpec((1, N_HEADS_B, LANES, tm), lambda i, j: (i, 0, 0, j)),
        pl.BlockSpec((1, N_HEADS_B, 1, VB_ROWS, tm), lambda i, j: (i, 0, j, 0, 0)),
        pl.BlockSpec((1, N_HEADS_B, tm, LANES), lambda i, j: (i, 0, j, 0)),
        pl.BlockSpec((1, tm, COLS_C), lambda i, j: (i, j, 0)),
    )
    in_specs = [
        pl.BlockSpec((1, tm, d), lambda i, j: (i, j, 0)),
        const((1, d)), const(wn.shape), const(wt.shape),
        const(gka.shape), const(gkb.shape), const(gqa.shape), const(gqb.shape),
        const(fbn.shape), const(fbt.shape), const(selk.shape), const(selq.shape), const(shift.shape),
    ]
    return pl.pallas_call(
        _inproj_body,
        grid=(b, nt),
        in_specs=in_specs,
        out_specs=out_specs,
        out_shape=out_shape,
        scratch_shapes=[pltpu.VMEM((8, LANES), F32), pltpu.VMEM((16, LANES), F32)],
        compiler_params=_cparams(2),
        name="inproj",
    )(x, g, wn, wt, gka, gkb, gqa, gqb, fbn, fbt, selk, selq, shift)


def _relbias_body(tab_ref, shift_ref, o_ref):
    hh = pl.program_id(0)
    kj = lax.broadcasted_iota(jnp.int32, (ATT_A_WIN, ATT_A_TILE), 0)
    qi = lax.broadcasted_iota(jnp.int32, (ATT_A_WIN, ATT_A_TILE), 1)
    rel = jnp.clip(kj - LEFT_CHUNKS * CHUNK - qi, -MAX_REL, MAX_REL) + MAX_REL

    def body(r, acc):
        return jnp.where(rel == r, tab_ref[hh, r], acc)

    bias = lax.fori_loop(0, 2 * MAX_REL + 1, body, jnp.zeros((ATT_A_WIN, ATT_A_TILE), F32))
    kc = kj // CHUNK
    qc = qi // CHUNK
    band = (kc >= qc) & (kc <= qc + LEFT_CHUNKS)
    o_ref[0] = jnp.where(band, LOG2E * bias - shift_ref[hh], NEG_INF)


def _relbias(tab, shift):
    return pl.pallas_call(
        _relbias_body,
        grid=(N_HEADS_A,),
        in_specs=[pl.BlockSpec(memory_space=pltpu.SMEM), pl.BlockSpec(memory_space=pltpu.SMEM)],
        out_specs=pl.BlockSpec((1, ATT_A_WIN, ATT_A_TILE), lambda i: (i, 0, 0)),
        out_shape=jax.ShapeDtypeStruct((N_HEADS_A, ATT_A_WIN, ATT_A_TILE), F32),
        compiler_params=_cparams(1),
        name="relbias",
    )(tab, shift)


def _attn_a_body(q_ref, k0_ref, k1_ref, k2_ref, v0_ref, v1_ref, v2_ref, bias_ref, o_ref, *, online):
    i = pl.program_id(1)
    tq = ATT_A_TILE
    kj = lax.broadcasted_iota(jnp.int32, (tq, 1), 0)

    def run(mask_padding):
        for hh in range(N_HEADS_A):
            q = q_ref[0, hh]
            ss = []
            for d, k_ref in enumerate((k0_ref, k1_ref, k2_ref)):
                s = _dot(k_ref[0, hh], q) + bias_ref[hh, d * tq:(d + 1) * tq, :]
                if mask_padding:
                    s = jnp.where(kj + (i - 2 + d) * tq >= 0, s, NEG_INF)
                ss.append(s)
            if online:
                m = functools.reduce(jnp.maximum, [jnp.max(s, axis=0, keepdims=True) for s in ss])
                ss = [s - m for s in ss]
            acc = jnp.zeros((VB_ROWS, tq), F32)
            for s, v_ref in zip(ss, (v0_ref, v1_ref, v2_ref)):
                acc = acc + _dot(v_ref[0, hh], jnp.exp2(s).astype(BF16))
            o_ref[0, hh] = (acc[0:HEAD_DIM] / acc[HEAD_DIM:HEAD_DIM + 1]).astype(BF16)

    pl.when(i < 2)(functools.partial(run, True))
    pl.when(i >= 2)(functools.partial(run, False))


def _attn_a(qaT, kA, vaT, bias, online):
    b, nh, _, t = qaT.shape
    tq = ATT_A_TILE
    kspec = lambda d: pl.BlockSpec((1, nh, tq, HEAD_DIM),
                                   lambda bb, i: (bb, 0, jnp.maximum(i - 2 + d, 0), 0))
    vspec = lambda d: pl.BlockSpec((1, nh, VB_ROWS, tq),
                                   lambda bb, i: (bb, 0, 0, jnp.maximum(i - 2 + d, 0)))
    return pl.pallas_call(
        functools.partial(_attn_a_body, online=online),
        grid=(b, t // tq),
        in_specs=[pl.BlockSpec((1, nh, HEAD_DIM, tq), lambda bb, i: (bb, 0, 0, i)),
                  kspec(0), kspec(1), kspec(2), vspec(0), vspec(1), vspec(2),
                  pl.BlockSpec((nh, ATT_A_WIN, tq), lambda bb, i: (0, 0, 0))],
        out_specs=pl.BlockSpec((1, nh, HEAD_DIM, tq), lambda bb, i: (bb, 0, 0, i)),
        out_shape=jax.ShapeDtypeStruct((b, nh, HEAD_DIM, t), BF16),
        compiler_params=_cparams(2),
        name="attn_a_online" if online else "attn_a",
    )(qaT, kA, kA, kA, vaT, vaT, vaT, bias)


def _fox_body(q_ref, k_ref, v_ref, o_ref, *, online):
    i = pl.program_id(2)
    tq = FOX_TILE
    q = q_ref[0, 0]

    def scores(j):
        ks = k_ref[0, 0, pl.ds(pl.multiple_of(j * tq, tq), tq), :]
        return _dot(ks, q)

    def accumulate(j, s, m, acc, diagonal):
        vs = v_ref[0, 0, j]
        if diagonal:
            kj = lax.broadcasted_iota(jnp.int32, (tq, tq), 0)
            qi = lax.broadcasted_iota(jnp.int32, (tq, tq), 1)
            causal = kj <= qi
        if online:
            if diagonal:
                s = jnp.where(causal, s, NEG_INF)
            m_new = jnp.maximum(m, jnp.max(s, axis=0, keepdims=True))
            acc = jnp.exp2(m - m_new) * acc
            p = jnp.exp2(s - m_new)
            m = m_new
        else:
            p = jnp.exp2(s)
            if diagonal:
                p = jnp.where(causal, p, 0.0)
        return m, acc + _dot(vs, p.astype(BF16))

    def tiles(js, carry, last_diagonal):
        ss = [scores(j) for j in js]
        m, acc = carry
        for n, (j, s) in enumerate(zip(js, ss)):
            m, acc = accumulate(j, s, m, acc, last_diagonal and n == len(js) - 1)
        return m, acc

    init = (jnp.full((1, tq), NEG_INF, F32), jnp.zeros((VB_ROWS, tq), F32))
    if online:
        carry = lax.fori_loop(0, i, lambda j, c: tiles([j], c, False), init)
        _, acc = tiles([i], carry, True)
    else:
        g = FOX_GROUP
        carry = lax.fori_loop(0, i // g, lambda jj, c: tiles([g * jj + n for n in range(g)], c, False), init)
        tails = [functools.partial(lambda c, r: tiles([i - r + n for n in range(r + 1)], c, True), r=r)
                 for r in range(g)]
        _, acc = lax.switch(i % g, tails, carry)
    o_ref[0, 0] = (acc[0:HEAD_DIM] / acc[HEAD_DIM:HEAD_DIM + 1]).astype(BF16)


def _fox(qbT, kB, vbT, online):
    b, nh, _, t = qbT.shape
    tq = FOX_TILE
    nk = t // tq
    return pl.pallas_call(
        functools.partial(_fox_body, online=online),
        grid=(b, nh, nk),
        in_specs=[pl.BlockSpec((1, 1, LANES, tq), lambda bb, hh, i: (bb, hh, 0, i)),
                  pl.BlockSpec((1, 1, t, LANES), lambda bb, hh, i: (bb, hh, 0, 0)),
                  pl.BlockSpec((1, 1, nk, VB_ROWS, tq), lambda bb, hh, i: (bb, hh, 0, 0, 0))],
        out_specs=pl.BlockSpec((1, 1, HEAD_DIM, tq), lambda bb, hh, i: (bb, hh, 0, i)),
        out_shape=jax.ShapeDtypeStruct((b, nh, HEAD_DIM, t), BF16),
        compiler_params=_cparams(3),
        name="fox_online" if online else "fox",
    )(qbT, kB, vbT)


def _rwkv_prep_body(pc_ref, mu_ref, w0_ref, w2_ref, a0_ref, a2_ref, g2_ref, kk_ref, ka_ref, rk_ref, e_ref,
                    rt_ref, at_ref, bt_ref, kt_ref, bh_ref, kh_ref, v_ref, wc_ref, bonus_ref, g_ref,
                    sbuf):
    tm = pc_ref.shape[1]
    t = pl.program_id(1)

    @pl.when(t == 0)
    def _():
        sbuf[0:8, :] = jnp.zeros((8, COLS_C), F32)

    u = pc_ref[0]
    sbuf[8:8 + tm, :] = u
    prev = sbuf[7:7 + tm, :]
    sbuf[0:8, :] = u[tm - 8:tm, :]
    u = u + (prev - u) * mu_ref[...]

    c = WIDTH_C
    r = u[:, 0:c]
    k = u[:, c:2 * c]
    v = u[:, 2 * c:3 * c]
    w_lo = u[:, 3 * c:3 * c + DECAY_LORA]
    a_lo = u[:, 3 * c + DECAY_LORA:3 * c + DECAY_LORA + AAA_LORA]
    g_lo = u[:, 3 * c + DECAY_LORA + AAA_LORA:COLS_C]

    w = w0_ref[...] + _dot(jnp.tanh(w_lo).astype(BF16), w2_ref[...])
    w = -_softplus(-w) - 0.5
    ld = -jnp.exp(w)
    a = _sigmoid(a0_ref[...] + _dot(a_lo.astype(BF16), a2_ref[...]))
    g = _dot(_sigmoid(g_lo).astype(BF16), g2_ref[...])

    e = e_ref[...]
    kk = k * kk_ref[...]
    s_hi, s_lo = _split2(kk * kk)
    nrm2 = _dot(s_hi, e) + _dot(s_lo, e)
    kkn = kk * lax.rsqrt(jnp.maximum(nrm2, 1e-24))
    k2 = k * (1.0 + (a - 1.0) * ka_ref[...])
    kka = kkn * a
    b_hi, b_lo = _split2(r * k2 * rk_ref[...])
    bonus = (_dot(b_hi, e) + _dot(b_lo, e)) * v

    ri = lax.broadcasted_iota(jnp.int32, (2 * CHUNK, CHUNK), 0)
    ci = lax.broadcasted_iota(jnp.int32, (2 * CHUNK, CHUNK), 1)
    tri = ((ci <= ri) | (ri >= CHUNK)).astype(BF16)
    for cc in range(tm // CHUNK):
        sl = slice(cc * CHUNK, (cc + 1) * CHUNK)
        h3, m3, l3 = _split3(ld[sl])
        cum = _dot(tri, h3) + _dot(tri, m3) + _dot(tri, l3)
        lc = cum[0:CHUNK]
        tot = cum[CHUNK:2 * CHUNK]
        e_neg = jnp.exp(-lc)
        e_rem = jnp.exp(tot - lc)
        rt_ref[0, sl, :] = (r[sl] * jnp.exp(lc)).astype(BF16)
        at_ref[0, sl, :] = (-kkn[sl] * jnp.exp(lc - ld[sl])).astype(BF16)
        bt_ref[0, sl, :] = (kka[sl] * e_neg).astype(BF16)
        kt_ref[0, sl, :] = (k2[sl] * e_neg).astype(BF16)
        bh_ref[0, sl, :] = (kka[sl] * e_rem).astype(BF16)
        kh_ref[0, sl, :] = (k2[sl] * e_rem).astype(BF16)
        wc_ref[0, cc] = jnp.exp(tot[0:1])
    v_ref[0] = v.astype(BF16)
    bonus_ref[0] = bonus
    g_ref[0] = g


def _rwkv_prep(pc, mu, w0, w2, a0, a2, g2, k_k, k_a, r_k, e):
    b, t, _ = pc.shape
    tm = TOK_TILE
    const = lambda shape: pl.BlockSpec(shape, lambda i, j: (0,) * len(shape))
    tok = lambda: pl.BlockSpec((1, tm, WIDTH_C), lambda i, j: (i, j, 0))
    bf = jax.ShapeDtypeStruct((b, t, WIDTH_C), BF16)
    f32 = jax.ShapeDtypeStruct((b, t, WIDTH_C), F32)
    out_shape = (bf, bf, bf, bf, bf, bf, bf,
                 jax.ShapeDtypeStruct((b, t // CHUNK, 1, WIDTH_C), F32), f32, f32)
    out_specs = (tok(), tok(), tok(), tok(), tok(), tok(), tok(),
                 pl.BlockSpec((1, tm // CHUNK, 1, WIDTH_C), lambda i, j: (i, j, 0, 0)), tok(), tok())
    return pl.pallas_call(
        _rwkv_prep_body,
        grid=(b, t // tm),
        in_specs=[pl.BlockSpec((1, tm, COLS_C), lambda i, j: (i, j, 0)),
                  const(mu.shape), const(w0.shape), const(w2.shape), const(a0.shape), const(a2.shape),
                  const(g2.shape), const(k_k.shape), const(k_a.shape), const(r_k.shape), const(e.shape)],
        out_specs=out_specs,
        out_shape=out_shape,
        scratch_shapes=[pltpu.VMEM((tm + 8, COLS_C), F32)],
        compiler_params=_cparams(2),
        name="rwkv_prep",
    )(pc, mu, w0, w2, a0, a2, g2, k_k, k_a, r_k, e)


def _bmm(a, b):
    return lax.dot_general(a, b, (((2,), (1,)), ((0,), (0,))), preferred_element_type=F32)


def _bmm_nt(a, b):
    return lax.dot_general(a, b, (((2,), (2,)), ((0,), (0,))), preferred_element_type=F32)


def _bmm_tn(a, b):
    return lax.dot_general(a, b, (((1,), (1,)), ((0,), (0,))), preferred_element_type=F32)


def _rwkv_chunk_body(rt_ref, at_ref, bt_ref, kt_ref, bh_ref, kh_ref, v_ref, wc_ref, bonus_ref, g_ref,
                     lg_ref, lb_ref, o_ref, h_ref):
    tm = rt_ref.shape[1]
    nc = tm // CHUNK
    nh = N_HEADS_C
    t = pl.program_id(1)

    @pl.when(t == 0)
    def _():
        h_ref[...] = jnp.zeros_like(h_ref)

    def heads(ref):
        x = ref[0]
        return jnp.stack([x[c * CHUNK:(c + 1) * CHUNK, hh * HEAD_DIM:(hh + 1) * HEAD_DIM]
                          for hh in range(nh) for c in range(nc)])

    rt, at, bt, kt, bh, kh, v = (heads(r) for r in (rt_ref, at_ref, bt_ref, kt_ref, bh_ref, kh_ref, v_ref))
    ri = lax.broadcasted_iota(jnp.int32, (1, CHUNK, CHUNK), 1)
    ci = lax.broadcasted_iota(jnp.int32, (1, CHUNK, CHUNK), 2)
    strict = ri > ci
    incl = ri >= ci
    eye = ri == ci

    a_ab = jnp.where(strict, _bmm_nt(at, bt), 0.0)
    a_ak = jnp.where(strict, _bmm_nt(at, kt), 0.0).astype(BF16)
    m_rb = jnp.where(incl, _bmm_nt(rt, bt), 0.0).astype(BF16)
    m_rk = jnp.where(incl, _bmm_nt(rt, kt), 0.0).astype(BF16)

    x = a_ab
    tinv = jnp.where(eye, 1.0, 0.0) + x
    for _ in range(5):
        xb = x.astype(BF16)
        x = _bmm(xb, xb)
        tinv = tinv + _bmm(tinv.astype(BF16), x.astype(BF16))
    tinv = tinv.astype(BF16)

    av = _bmm(a_ak, v)
    rv = _bmm(m_rk, v)
    p = _bmm(tinv, at).astype(BF16)
    u0 = _bmm(tinv, av.astype(BF16)).astype(BF16)
    q = rt.astype(F32) + _bmm(m_rb, p)
    y0 = rv + _bmm(m_rb, u0)
    wc = wc_ref[0]
    wdiag = jnp.stack([wc[c, :, hh * HEAD_DIM:(hh + 1) * HEAD_DIM]
                       for hh in range(nh) for c in range(nc)])
    gmat = jnp.where(eye, wdiag, 0.0) + _bmm_tn(bh, p)
    hadd = _bmm_tn(bh, u0) + _bmm_tn(kh, v)

    q = q.astype(BF16).reshape(nh, nc, CHUNK, HEAD_DIM)
    gmat = gmat.astype(BF16).reshape(nh, nc, HEAD_DIM, HEAD_DIM)
    y0 = y0.reshape(nh, nc, CHUNK, HEAD_DIM)
    hadd = hadd.reshape(nh, nc, HEAD_DIM, HEAD_DIM)
    hs = h_ref[...]
    ys = []
    for c in range(nc):
        h_hi, h_lo = _split2(hs)
        ys.append(_bmm(q[:, c], h_hi) + _bmm(q[:, c], h_lo) + y0[:, c])
        hs = _bmm(gmat[:, c], h_hi) + _bmm(gmat[:, c], h_lo) + hadd[:, c]
    h_ref[...] = hs

    y = jnp.concatenate(ys, axis=1)
    mean = jnp.mean(y, axis=-1, keepdims=True)
    yc = y - mean
    var = jnp.mean(yc * yc, axis=-1, keepdims=True)
    yn = yc * lax.rsqrt(var + LNX_EPS)
    yfull = jnp.concatenate([yn[hh] for hh in range(nh)], axis=-1)
    out = (yfull * lg_ref[...] + lb_ref[...] + bonus_ref[0]) * g_ref[0]
    o_ref[0] = out.astype(BF16)


def _rwkv_chunk(rt, at, bt, kt, bh, kh, v, wc, bonus, g, lnx_g, lnx_b):
    b, t, _ = rt.shape
    tm = RWKV_TILE
    const = lambda shape: pl.BlockSpec(shape, lambda i, j: (0,) * len(shape))
    tok = lambda: pl.BlockSpec((1, tm, WIDTH_C), lambda i, j: (i, j, 0))
    return pl.pallas_call(
        _rwkv_chunk_body,
        grid=(b, t // tm),
        in_specs=[tok(), tok(), tok(), tok(), tok(), tok(), tok(),
                  pl.BlockSpec((1, tm // CHUNK, 1, WIDTH_C), lambda i, j: (i, j, 0, 0)),
                  tok(), tok(), const(lnx_g.shape), const(lnx_b.shape)],
        out_specs=tok(),
        out_shape=jax.ShapeDtypeStruct((b, t, WIDTH_C), BF16),
        scratch_shapes=[pltpu.VMEM((N_HEADS_C, HEAD_DIM, HEAD_DIM), F32)],
        compiler_params=_cparams(2),
        name="rwkv_chunk",
    )(rt, at, bt, kt, bh, kh, v, wc, bonus, g, lnx_g, lnx_b)


def _outproj_body(x_ref, ya_ref, yb_ref, yc_ref, wa_ref, wb_ref, wc_ref, o_ref):
    acc = _dot_tn(ya_ref[0], wa_ref[...])
    acc = acc + _dot_tn(yb_ref[0], wb_ref[...])
    acc = acc + _dot(yc_ref[0], wc_ref[...])
    o_ref[0] = x_ref[0] + acc


def _outproj(x, yaT, ybT, yc, wa, wb, wc):
    b, t, d = x.shape
    tm = TOK_TILE
    const = lambda shape: pl.BlockSpec(shape, lambda i, j: (0,) * len(shape))
    return pl.pallas_call(
        _outproj_body,
        grid=(b, t // tm),
        in_specs=[pl.BlockSpec((1, tm, d), lambda i, j: (i, j, 0)),
                  pl.BlockSpec((1, WIDTH_A, tm), lambda i, j: (i, 0, j)),
                  pl.BlockSpec((1, WIDTH_B, tm), lambda i, j: (i, 0, j)),
                  pl.BlockSpec((1, tm, WIDTH_C), lambda i, j: (i, j, 0)),
                  const(wa.shape), const(wb.shape), const(wc.shape)],
        out_specs=pl.BlockSpec((1, tm, d), lambda i, j: (i, j, 0)),
        out_shape=jax.ShapeDtypeStruct((b, t, d), F32),
        compiler_params=_cparams(2),
        name="outproj",
    )(x, yaT, ybT, yc, wa, wb, wc)


def _ffn_body(x_ref, g_ref, wg_ref, wv_ref, cw_ref, cb_ref, wd_ref, o_ref, gbuf):
    tm = x_ref.shape[1]
    t = pl.program_id(1)

    @pl.when(t == 0)
    def _():
        gbuf[0:8, :] = jnp.zeros((8, D_FF), F32)

    x = x_ref[0]
    ms = jnp.mean(x * x, axis=-1, keepdims=True)
    h = (x * lax.rsqrt(ms + RMS_EPS) * g_ref[...]).astype(BF16)
    acc = x
    for lo, hi in zip(FF_SPLITS[:-1], FF_SPLITS[1:]):
        gate = _dot(h, wg_ref[:, lo:hi])
        val = _dot(h, wv_ref[:, lo:hi])
        gbuf[8:8 + tm, lo:hi] = gate
        g1 = gbuf[7:7 + tm, lo:hi]
        g2 = gbuf[6:6 + tm, lo:hi]
        gbuf[0:8, lo:hi] = gate[tm - 8:tm, :]
        conv = cb_ref[:, lo:hi] + g2 * cw_ref[0:1, lo:hi] + g1 * cw_ref[1:2, lo:hi] + gate * cw_ref[2:3, lo:hi]
        act = (conv * _sigmoid(conv) * val).astype(BF16)
        acc = acc + _dot(act, wd_ref[lo:hi, :])
    o_ref[0] = acc


def _ffn(x, g, wg, wv, cw, cb, wd):
    b, t, d = x.shape
    tm = TOK_TILE
    const = lambda shape: pl.BlockSpec(shape, lambda i, j: (0,) * len(shape),
                                       pipeline_mode=pl.Buffered(1))
    return pl.pallas_call(
        _ffn_body,
        grid=(b, t // tm),
        in_specs=[pl.BlockSpec((1, tm, d), lambda i, j: (i, j, 0)),
                  const(g.shape), const(wg.shape), const(wv.shape), const(cw.shape), const(cb.shape),
                  const(wd.shape)],
        out_specs=pl.BlockSpec((1, tm, d), lambda i, j: (i, j, 0)),
        out_shape=jax.ShapeDtypeStruct((b, t, d), F32),
        scratch_shapes=[pltpu.VMEM((tm + 8, D_FF), F32)],
        compiler_params=_cparams(2),
        name="ffn",
    )(x, g, wg, wv, cw, cb, wd)


def _pad_heads_cols(w, nh):
    d = w.shape[0]
    w = w.reshape(d, nh, HEAD_DIM)
    return jnp.pad(w, ((0, 0), (0, 0), (0, LANES - HEAD_DIM))).reshape(d, nh * LANES)


def _qk_bound(gq, gk):
    scale = HEAD_DIM ** -0.5
    return 1.01 * HEAD_DIM * scale * jnp.max(jnp.abs(gq)) * jnp.max(jnp.abs(gk)) + 0.05


def _layer_params(l, w_in, q_norm_a, k_norm_a, rel_bias, q_norm_b, k_norm_b, forget_bias):
    w = w_in[l]
    scale = HEAD_DIM ** -0.5
    a0, b0 = 0, 3 * WIDTH_A
    c0 = b0 + 3 * WIDTH_B + N_HEADS_B
    qa, ka, va = (w[:, a0 + i * WIDTH_A:a0 + (i + 1) * WIDTH_A] for i in range(3))
    qb, kb, vb = (w[:, b0 + i * WIDTH_B:b0 + (i + 1) * WIDTH_B] for i in range(3))
    fg = w[:, b0 + 3 * WIDTH_B:c0]
    wc = w[:, c0:]
    wn = jnp.concatenate([_pad_heads_cols(ka, N_HEADS_A), _pad_heads_cols(kb, N_HEADS_B),
                          jnp.pad(fg, ((0, 0), (0, LANES - N_HEADS_B))), wc], axis=1).astype(BF16)
    pad_v = lambda v, nh: jnp.pad(v.reshape(-1, nh, HEAD_DIM),
                                  ((0, 0), (0, 0), (0, VB_ROWS - HEAD_DIM))).reshape(-1, nh * VB_ROWS)
    wt = jnp.concatenate([qa, pad_v(va, N_HEADS_A), _pad_heads_cols(qb, N_HEADS_B), pad_v(vb, N_HEADS_B),
                          jnp.pad(fg, ((0, 0), (0, 16 - N_HEADS_B)))], axis=1).T.astype(BF16)
    pad = LANES - HEAD_DIM
    gka = jnp.pad(k_norm_a[l], (0, pad)).reshape(1, LANES)
    gkb = jnp.pad(k_norm_b[l], (0, pad)).reshape(1, LANES)
    gqa = (q_norm_a[l] * (scale * LOG2E)).reshape(HEAD_DIM, 1)
    gqb = jnp.pad(q_norm_b[l] * (scale * LOG2E), (0, pad)).reshape(LANES, 1)
    fbn = jnp.pad(forget_bias[l], (0, LANES - N_HEADS_B)).reshape(1, LANES)
    fbt = jnp.pad(forget_bias[l], (0, 16 - N_HEADS_B)).reshape(16, 1)
    shift_b = (LOG2E * _qk_bound(q_norm_b[l], k_norm_b[l])).reshape(1, 1)
    bound_a = _qk_bound(q_norm_a[l], k_norm_a[l])
    tab = rel_bias[l]
    shift_a = LOG2E * (bound_a + jnp.max(tab, axis=1))
    depth_a = jnp.max(shift_a + LOG2E * (bound_a - tab[:, MAX_REL]))
    return wn, wt, gka, gkb, gqa, gqb, fbn, fbt, shift_b, shift_a, depth_a


def _selectors():
    import numpy as np
    selk = np.zeros((3, LANES, N_HEADS_B * LANES), np.float32)
    selq = np.zeros((3, N_HEADS_B * LANES, 16), np.float32)
    for p in range(3):
        for hh in range(N_HEADS_B):
            selk[p, hh, hh * LANES + HEAD_DIM + 3 + p] = -1.0
            selq[p, hh * LANES + HEAD_DIM + p, hh] = 1.0
    e = np.kron(np.eye(N_HEADS_C, dtype=np.float32), np.ones((HEAD_DIM, HEAD_DIM), np.float32))
    return jnp.asarray(selk, BF16), jnp.asarray(selq, BF16), jnp.asarray(e, BF16)


def kernel(x, mix_norm_g, w_in, q_norm_a, k_norm_a, rel_bias, q_norm_b, k_norm_b, forget_bias, shift_mu, w0, w2,
           a0, a2, g2, k_k, k_a, r_k, lnx_g, lnx_b, w_out, ffn_norm_g, w_up, conv_w, conv_b, w_down):
    depth = w_in.shape[0]
    b, t, d = x.shape
    selk, selq, e = _selectors()
    row = lambda v: v.reshape(1, -1)
    for l in range(depth):
        wn, wt, gka, gkb, gqa, gqb, fbn, fbt, shift_b, shift_a, depth_a = _layer_params(
            l, w_in, q_norm_a, k_norm_a, rel_bias, q_norm_b, k_norm_b, forget_bias)
        qaT, vaT, kA, qbT, vbT, kB, pc = _inproj(x, row(mix_norm_g[l]), wn, wt, gka, gkb, gqa, gqb,
                                                 fbn, fbt, selk, selq, shift_b)
        bias = _relbias(rel_bias[l], shift_a)
        yaT = lax.cond(depth_a <= 2 * FAST_MAX_SHIFT,
                       functools.partial(_attn_a, online=False), functools.partial(_attn_a, online=True),
                       qaT, kA, vaT, bias)
        ybT = lax.cond(shift_b[0, 0] <= FAST_MAX_SHIFT,
                       functools.partial(_fox, online=False), functools.partial(_fox, online=True),
                       qbT, kB, vbT)
        rt, at, bt, kt, bh, kh, v, wcum, bonus, g = _rwkv_prep(
            pc, row(shift_mu[l]), row(w0[l]), w2[l].astype(BF16), row(a0[l]), a2[l].astype(BF16),
            g2[l].astype(BF16), row(k_k[l]), row(k_a[l]), row(r_k[l]), e)
        yc = _rwkv_chunk(rt, at, bt, kt, bh, kh, v, wcum, bonus, g, row(lnx_g[l]), row(lnx_b[l]))
        wo = w_out[l].astype(BF16)
        x = _outproj(x, yaT.reshape(b, WIDTH_A, t), ybT.reshape(b, WIDTH_B, t), yc,
                     wo[:WIDTH_A], wo[WIDTH_A:WIDTH_A + WIDTH_B], wo[WIDTH_A + WIDTH_B:])
        wu = w_up[l].astype(BF16)
        x = _ffn(x, row(ffn_norm_g[l]), wu[:, :D_FF], wu[:, D_FF:], conv_w[l], row(conv_b[l]),
                 w_down[l].astype(BF16))
    return x
```

```python
import functools
import math

import jax
import jax.numpy as jnp
from jax import lax
from jax.experimental import pallas as pl
from jax.experimental.pallas import tpu as pltpu

F32 = jnp.float32
BF16 = jnp.bfloat16

D_MODEL = 1024
HEAD_DIM = 64
CHUNK = 64
LEFT_CHUNKS = 8
MAX_REL = 128
N_HEADS_A = 4
N_HEADS_B = 4
N_HEADS_C = 8
WIDTH_A = N_HEADS_A * HEAD_DIM
WIDTH_B = N_HEADS_B * HEAD_DIM
WIDTH_C = N_HEADS_C * HEAD_DIM
DECAY_LORA = 64
AAA_LORA = 64
GATE_LORA = 128
COLS_C = 3 * WIDTH_C + DECAY_LORA + AAA_LORA + GATE_LORA
D_FF = 2816
RMS_EPS = 1e-6
LNX_EPS = 64e-5
NEG_INF = -1e30

LANES = 128
TOK_TILE = 512
ATT_A_TILE = 256
ATT_A_WIN = 3 * ATT_A_TILE
FOX_TILE = 512
FOX_GROUP = 4
RWKV_GROUP = 4
RWKV_AHEAD = 4
FF_SPLITS = (0, 768, 1536, 2304, 2816)
VMEM_LIMIT = 56 * 1024 * 1024

PN_KA, PN_KB, PN_F, PN_C = 0, 512, 1024, 1152
PN_COLS = PN_C + COLS_C
VB_ROWS = 80
PT_QA = 0
PT_VA = PT_QA + WIDTH_A
PT_QB = PT_VA + N_HEADS_A * VB_ROWS
PT_VB = PT_QB + N_HEADS_B * LANES
PT_F = PT_VB + N_HEADS_B * VB_ROWS
PT_ROWS = PT_F + 16
LOG2E = 1.4426950408889634
FAST_MAX_SHIFT = 40.0


def _cparams(n_axes):
    return pltpu.CompilerParams(dimension_semantics=("arbitrary",) * n_axes,
                                vmem_limit_bytes=VMEM_LIMIT)


def _split3(x):
    hi = x.astype(BF16)
    r1 = x - hi.astype(F32)
    mid = r1.astype(BF16)
    lo = (r1 - mid.astype(F32)).astype(BF16)
    return hi, mid, lo


def _split2(x):
    hi = x.astype(BF16)
    lo = (x - hi.astype(F32)).astype(BF16)
    return hi, lo


def _dot(a, b):
    return jnp.dot(a, b, preferred_element_type=F32)


def _dot_nt(a, b):
    return lax.dot_general(a, b, (((1,), (1,)), ((), ())), preferred_element_type=F32)


def _dot_tn(a, b):
    return lax.dot_general(a, b, (((0,), (0,)), ((), ())), preferred_element_type=F32)


def _log_sigmoid(x):
    return jnp.minimum(x, 0.0) - jnp.log(1.0 + jnp.exp(-jnp.abs(x)))


def _softplus(x):
    return jnp.maximum(x, 0.0) + jnp.log(1.0 + jnp.exp(-jnp.abs(x)))


def _sigmoid(x):
    return 1.0 / (1.0 + jnp.exp(-x))


def _inproj_body(x_ref, g_ref, wn_ref, wt_ref, gka_ref, gkb_ref, gqa_ref, gqb_ref, fbn_ref, fbt_ref,
                 selk_ref, selq_ref, shift_ref,
                 qaT_ref, vaT_ref, kA_ref, qbT_ref, vbT_ref, kB_ref, pc_ref,
                 carry_n, carry_t):
    tm = x_ref.shape[1]
    t = pl.program_id(1)

    @pl.when(t == 0)
    def _():
        carry_n[...] = jnp.zeros_like(carry_n)
        carry_t[...] = jnp.zeros_like(carry_t)

    x = x_ref[0]
    ms = jnp.mean(x * x, axis=-1, keepdims=True)
    h = (x * lax.rsqrt(ms + RMS_EPS) * g_ref[...]).astype(BF16)
    pn = _dot(h, wn_ref[...])
    pt = _dot_nt(wt_ref[...], h)

    qa = pt[PT_QA:PT_QA + WIDTH_A].reshape(N_HEADS_A, HEAD_DIM, tm)
    msq = jnp.mean(qa * qa, axis=1, keepdims=True)
    qaT_ref[0] = (qa * lax.rsqrt(msq + RMS_EPS) * gqa_ref[...][None]).astype(BF16)
    row_v = lax.broadcasted_iota(jnp.int32, (VB_ROWS, 1), 0)
    ones_v = jnp.where(row_v == HEAD_DIM, 1.0, 0.0)
    va = pt[PT_VA:PT_VA + N_HEADS_A * VB_ROWS].reshape(N_HEADS_A, VB_ROWS, tm)
    vaT_ref[0] = (va + ones_v[None]).astype(BF16)
    for hh in range(N_HEADS_A):
        k = pn[:, PN_KA + LANES * hh:PN_KA + LANES * (hh + 1)]
        msk = jnp.sum(k * k, axis=-1, keepdims=True) * (1.0 / HEAD_DIM)
        kn = k * lax.rsqrt(msk + RMS_EPS) * gka_ref[...]
        kA_ref[0, hh] = kn[:, :HEAD_DIM].astype(BF16)

    lane_n = lax.broadcasted_iota(jnp.int32, (1, LANES), 1)
    lf_n = jnp.where(lane_n < N_HEADS_B, LOG2E * _log_sigmoid(pn[:, PN_F:PN_F + LANES] + fbn_ref[...]), 0.0)
    row_t = lax.broadcasted_iota(jnp.int32, (16, 1), 0)
    lf_t = jnp.where(row_t < N_HEADS_B, LOG2E * _log_sigmoid(pt[PT_F:PT_F + 16] + fbt_ref[...]), 0.0)
    ri = lax.broadcasted_iota(jnp.int32, (tm, tm), 0)
    ci = lax.broadcasted_iota(jnp.int32, (tm, tm), 1)
    low = (ci <= ri).astype(BF16)
    upp = (ri <= ci).astype(BF16)
    hn, mn, ln = _split3(lf_n)
    c_n = _dot(low, hn) + _dot(low, mn) + _dot(low, ln) + carry_n[0:1, :]
    ht, mt, lt = _split3(lf_t)
    c_t = _dot(ht, upp) + _dot(mt, upp) + _dot(lt, upp) + carry_t[:, 0:1]
    carry_n[...] = carry_n[...] + jnp.sum(lf_n, axis=0, keepdims=True)
    carry_t[...] = carry_t[...] + jnp.sum(lf_t, axis=1, keepdims=True)

    chn, cmn, cln = _split3(c_n)
    kaug = _dot(chn, selk_ref[0]) + _dot(cmn, selk_ref[1]) + _dot(cln, selk_ref[2])
    lane_k = lax.broadcasted_iota(jnp.int32, (1, LANES), 1)
    ones_k = jnp.where((lane_k >= HEAD_DIM) & (lane_k < HEAD_DIM + 3), 1.0, 0.0)
    for hh in range(N_HEADS_B):
        k = pn[:, PN_KB + LANES * hh:PN_KB + LANES * (hh + 1)]
        msk = jnp.sum(k * k, axis=-1, keepdims=True) * (1.0 / HEAD_DIM)
        kn = k * lax.rsqrt(msk + RMS_EPS) * gkb_ref[...]
        kB_ref[0, hh] = (kn + kaug[:, LANES * hh:LANES * (hh + 1)] + ones_k).astype(BF16)

    cht, cmt, clt = _split3(c_t - shift_ref[...])
    qaug = _dot(selq_ref[0], cht) + _dot(selq_ref[1], cmt) + _dot(selq_ref[2], clt)
    qb = pt[PT_QB:PT_QB + N_HEADS_B * LANES].reshape(N_HEADS_B, LANES, tm)
    msq = jnp.sum(qb * qb, axis=1, keepdims=True) * (1.0 / HEAD_DIM)
    row_q = lax.broadcasted_iota(jnp.int32, (LANES, 1), 0)
    ones_q = jnp.where((row_q >= HEAD_DIM + 3) & (row_q < HEAD_DIM + 6), 1.0, 0.0)
    qn = qb * lax.rsqrt(msq + RMS_EPS) * gqb_ref[...][None]
    qbT_ref[0] = (qn + qaug.reshape(N_HEADS_B, LANES, tm) + ones_q[None]).astype(BF16)
    vb = pt[PT_VB:PT_VB + N_HEADS_B * VB_ROWS].reshape(N_HEADS_B, VB_ROWS, tm)
    vbT_ref[0, :, 0] = (vb + ones_v[None]).astype(BF16)

    pc_ref[0] = pn[:, PN_C:PN_C + COLS_C]


def _inproj(x, g, wn, wt, gka, gkb, gqa, gqb, fbn, fbt, selk, selq, shift):
    b, t, d = x.shape
    tm = TOK_TILE
    nt = t // tm
    const = lambda shape: pl.BlockSpec(shape, lambda i, j: (0,) * len(shape))
    out_shape = (
        jax.ShapeDtypeStruct((b, N_HEADS_A, HEAD_DIM, t), BF16),
        jax.ShapeDtypeStruct((b, N_HEADS_A, VB_ROWS, t), BF16),
        jax.ShapeDtypeStruct((b, N_HEADS_A, t, HEAD_DIM), BF16),
        jax.ShapeDtypeStruct((b, N_HEADS_B, LANES, t), BF16),
        jax.ShapeDtypeStruct((b, N_HEADS_B, nt, VB_ROWS, tm), BF16),
        jax.ShapeDtypeStruct((b, N_HEADS_B, t, LANES), BF16),
        jax.ShapeDtypeStruct((b, t, COLS_C), F32),
    )
    out_specs = (
        pl.BlockSpec((1, N_HEADS_A, HEAD_DIM, tm), lambda i, j: (i, 0, 0, j)),
        pl.BlockSpec((1, N_HEADS_A, VB_ROWS, tm), lambda i, j: (i, 0, 0, j)),
        pl.BlockSpec((1, N_HEADS_A, tm, HEAD_DIM), lambda i, j: (i, 0, j, 0)),
        pl.BlockSpec((1, N_HEADS_B, LANES, tm), lambda i, j: (i, 0, 0, j)),
        pl.BlockSpec((1, N_HEADS_B, 1, VB_ROWS, tm), lambda i, j: (i, 0, j, 0, 0)),
        pl.BlockSpec((1, N_HEADS_B, tm, LANES), lambda i, j: (i, 0, j, 0)),
        pl.BlockSpec((1, tm, COLS_C), lambda i, j: (i, j, 0)),
    )
    in_specs = [
        pl.BlockSpec((1, tm, d), lambda i, j: (i, j, 0)),
        const((1, d)), const(wn.shape), const(wt.shape),
        const(gka.shape), const(gkb.shape), const(gqa.shape), const(gqb.shape),
        const(fbn.shape), const(fbt.shape), const(selk.shape), const(selq.shape), const(shift.shape),
    ]
    return pl.pallas_call(
        _inproj_body,
        grid=(b, nt),
        in_specs=in_specs,
        out_specs=out_specs,
        out_shape=out_shape,
        scratch_shapes=[pltpu.VMEM((8, LANES), F32), pltpu.VMEM((16, LANES), F32)],
        compiler_params=_cparams(2),
        name="inproj",
    )(x, g, wn, wt, gka, gkb, gqa, gqb, fbn, fbt, selk, selq, shift)


def _relbias_body(tab_ref, shift_ref, o_ref):
    hh = pl.program_id(0)
    kj = lax.broadcasted_iota(jnp.int32, (ATT_A_WIN, ATT_A_TILE), 0)
    qi = lax.broadcasted_iota(jnp.int32, (ATT_A_WIN, ATT_A_TILE), 1)
    rel = jnp.clip(kj - LEFT_CHUNKS * CHUNK - qi, -MAX_REL, MAX_REL) + MAX_REL

    def body(r, acc):
        return jnp.where(rel == r, tab_ref[hh, r], acc)

    bias = lax.fori_loop(0, 2 * MAX_REL + 1, body, jnp.zeros((ATT_A_WIN, ATT_A_TILE), F32))
    kc = kj // CHUNK
    qc = qi // CHUNK
    band = (kc >= qc) & (kc <= qc + LEFT_CHUNKS)
    o_ref[0] = jnp.where(band, LOG2E * bias - shift_ref[hh], NEG_INF)


def _relbias(tab, shift):
    return pl.pallas_call(
        _relbias_body,
        grid=(N_HEADS_A,),
        in_specs=[pl.BlockSpec(memory_space=pltpu.SMEM), pl.BlockSpec(memory_space=pltpu.SMEM)],
        out_specs=pl.BlockSpec((1, ATT_A_WIN, ATT_A_TILE), lambda i: (i, 0, 0)),
        out_shape=jax.ShapeDtypeStruct((N_HEADS_A, ATT_A_WIN, ATT_A_TILE), F32),
        compiler_params=_cparams(1),
        name="relbias",
    )(tab, shift)


def _attn_a_body(q_ref, k0_ref, k1_ref, k2_ref, v0_ref, v1_ref, v2_ref, bias_ref, o_ref, *, online):
    i = pl.program_id(1)
    tq = ATT_A_TILE
    kj = lax.broadcasted_iota(jnp.int32, (tq, 1), 0)

    def run(mask_padding):
        for hh in range(N_HEADS_A):
            q = q_ref[0, hh]
            ss = []
            for d, k_ref in enumerate((k0_ref, k1_ref, k2_ref)):
                s = _dot(k_ref[0, hh], q) + bias_ref[hh, d * tq:(d + 1) * tq, :]
                if mask_padding:
                    s = jnp.where(kj + (i - 2 + d) * tq >= 0, s, NEG_INF)
                ss.append(s)
            if online:
                m = functools.reduce(jnp.maximum, [jnp.max(s, axis=0, keepdims=True) for s in ss])
                ss = [s - m for s in ss]
            acc = jnp.zeros((VB_ROWS, tq), F32)
            for s, v_ref in zip(ss, (v0_ref, v1_ref, v2_ref)):
                acc = acc + _dot(v_ref[0, hh], jnp.exp2(s).astype(BF16))
            o_ref[0, hh] = (acc[0:HEAD_DIM] / acc[HEAD_DIM:HEAD_DIM + 1]).astype(BF16)

    pl.when(i < 2)(functools.partial(run, True))
    pl.when(i >= 2)(functools.partial(run, False))


def _attn_a(qaT, kA, vaT, bias, online):
    b, nh, _, t = qaT.shape
    tq = ATT_A_TILE
    kspec = lambda d: pl.BlockSpec((1, nh, tq, HEAD_DIM),
                                   lambda bb, i: (bb, 0, jnp.maximum(i - 2 + d, 0), 0))
    vspec = lambda d: pl.BlockSpec((1, nh, VB_ROWS, tq),
                                   lambda bb, i: (bb, 0, 0, jnp.maximum(i - 2 + d, 0)))
    return pl.pallas_call(
        functools.partial(_attn_a_body, online=online),
        grid=(b, t // tq),
        in_specs=[pl.BlockSpec((1, nh, HEAD_DIM, tq), lambda bb, i: (bb, 0, 0, i)),
                  kspec(0), kspec(1), kspec(2), vspec(0), vspec(1), vspec(2),
                  pl.BlockSpec((nh, ATT_A_WIN, tq), lambda bb, i: (0, 0, 0))],
        out_specs=pl.BlockSpec((1, nh, HEAD_DIM, tq), lambda bb, i: (bb, 0, 0, i)),
        out_shape=jax.ShapeDtypeStruct((b, nh, HEAD_DIM, t), BF16),
        compiler_params=_cparams(2),
        name="attn_a_online" if online else "attn_a",
    )(qaT, kA, kA, kA, vaT, vaT, vaT, bias)


def _fox_body(q_ref, k_ref, v_ref, o_ref, *, online):
    i = pl.program_id(2)
    tq = FOX_TILE
    q = q_ref[0, 0]

    def scores(j):
        ks = k_ref[0, 0, pl.ds(pl.multiple_of(j * tq, tq), tq), :]
        return _dot(ks, q)

    def accumulate(j, s, m, acc, diagonal):
        vs = v_ref[0, 0, j]
        if diagonal:
            kj = lax.broadcasted_iota(jnp.int32, (tq, tq), 0)
            qi = lax.broadcasted_iota(jnp.int32, (tq, tq), 1)
            causal = kj <= qi
        if online:
            if diagonal:
                s = jnp.where(causal, s, NEG_INF)
            m_new = jnp.maximum(m, jnp.max(s, axis=0, keepdims=True))
            acc = jnp.exp2(m - m_new) * acc
            p = jnp.exp2(s - m_new)
            m = m_new
        else:
            p = jnp.exp2(s)
            if diagonal:
                p = jnp.where(causal, p, 0.0)
        return m, acc + _dot(vs, p.astype(BF16))

    def tiles(js, carry, last_diagonal):
        ss = [scores(j) for j in js]
        m, acc = carry
        for n, (j, s) in enumerate(zip(js, ss)):
            m, acc = accumulate(j, s, m, acc, last_diagonal and n == len(js) - 1)
        return m, acc

    init = (jnp.full((1, tq), NEG_INF, F32), jnp.zeros((VB_ROWS, tq), F32))
    if online:
        carry = lax.fori_loop(0, i, lambda j, c: tiles([j], c, False), init)
        _, acc = tiles([i], carry, True)
    else:
        g = FOX_GROUP
        carry = lax.fori_loop(0, i // g, lambda jj, c: tiles([g * jj + n for n in range(g)], c, False), init)
        tails = [functools.partial(lambda c, r: tiles([i - r + n for n in range(r + 1)], c, True), r=r)
                 for r in range(g)]
        _, acc = lax.switch(i % g, tails, carry)
    o_ref[0, 0] = (acc[0:HEAD_DIM] / acc[HEAD_DIM:HEAD_DIM + 1]).astype(BF16)


def _fox(qbT, kB, vbT, online):
    b, nh, _, t = qbT.shape
    tq = FOX_TILE
    nk = t // tq
    return pl.pallas_call(
        functools.partial(_fox_body, online=online),
        grid=(b, nh, nk),
        in_specs=[pl.BlockSpec((1, 1, LANES, tq), lambda bb, hh, i: (bb, hh, 0, i)),
                  pl.BlockSpec((1, 1, t, LANES), lambda bb, hh, i: (bb, hh, 0, 0)),
                  pl.BlockSpec((1, 1, nk, VB_ROWS, tq), lambda bb, hh, i: (bb, hh, 0, 0, 0))],
        out_specs=pl.BlockSpec((1, 1, HEAD_DIM, tq), lambda bb, hh, i: (bb, hh, 0, i)),
        out_shape=jax.ShapeDtypeStruct((b, nh, HEAD_DIM, t), BF16),
        compiler_params=_cparams(3),
        name="fox_online" if online else "fox",
    )(qbT, kB, vbT)


def _rwkv_prep_body(pc_ref, mu_ref, w0_ref, w2_ref, a0_ref, a2_ref, g2_ref, kk_ref, ka_ref, rk_ref, e_ref,
                    rt_ref, at_ref, bt_ref, kt_ref, bh_ref, kh_ref, v_ref, wc_ref, bonus_ref, g_ref,
                    sbuf):
    tm = pc_ref.shape[1]
    t = pl.program_id(1)

    @pl.when(t == 0)
    def _():
        sbuf[0:8, :] = jnp.zeros((8, COLS_C), F32)

    u = pc_ref[0]
    sbuf[8:8 + tm, :] = u
    prev = sbuf[7:7 + tm, :]
    sbuf[0:8, :] = u[tm - 8:tm, :]
    u = u + (prev - u) * mu_ref[...]

    c = WIDTH_C
    r = u[:, 0:c]
    k = u[:, c:2 * c]
    v = u[:, 2 * c:3 * c]
    w_lo = u[:, 3 * c:3 * c + DECAY_LORA]
    a_lo = u[:, 3 * c + DECAY_LORA:3 * c + DECAY_LORA + AAA_LORA]
    g_lo = u[:, 3 * c + DECAY_LORA + AAA_LORA:COLS_C]

    w = w0_ref[...] + _dot(jnp.tanh(w_lo).astype(BF16), w2_ref[...])
    w = -_softplus(-w) - 0.5
    ld = -jnp.exp(w)
    a = _sigmoid(a0_ref[...] + _dot(a_lo.astype(BF16), a2_ref[...]))
    g = _dot(_sigmoid(g_lo).astype(BF16), g2_ref[...])

    e = e_ref[...]
    kk = k * kk_ref[...]
    s_hi, s_lo = _split2(kk * kk)
    nrm2 = _dot(s_hi, e) + _dot(s_lo, e)
    kkn = kk * lax.rsqrt(jnp.maximum(nrm2, 1e-24))
    k2 = k * (1.0 + (a - 1.0) * ka_ref[...])
    kka = kkn * a
    b_hi, b_lo = _split2(r * k2 * rk_ref[...])
    bonus = (_dot(b_hi, e) + _dot(b_lo, e)) * v

    ri = lax.broadcasted_iota(jnp.int32, (2 * CHUNK, CHUNK), 0)
    ci = lax.broadcasted_iota(jnp.int32, (2 * CHUNK, CHUNK), 1)
    tri = ((ci <= ri) | (ri >= CHUNK)).astype(BF16)
    for cc in range(tm // CHUNK):
        sl = slice(cc * CHUNK, (cc + 1) * CHUNK)
        h3, m3, l3 = _split3(ld[sl])
        cum = _dot(tri, h3) + _dot(tri, m3) + _dot(tri, l3)
        lc = cum[0:CHUNK]
        tot = cum[CHUNK:2 * CHUNK]
        e_neg = jnp.exp(-lc)
        e_rem = jnp.exp(tot - lc)
        rt_ref[0, sl, :] = (r[sl] * jnp.exp(lc)).astype(BF16)
        at_ref[0, sl, :] = (-kkn[sl] * jnp.exp(lc - ld[sl])).astype(BF16)
        bt_ref[0, sl, :] = (kka[sl] * e_neg).astype(BF16)
        kt_ref[0, sl, :] = (k2[sl] * e_neg).astype(BF16)
        bh_ref[0, sl, :] = (kka[sl] * e_rem).astype(BF16)
        kh_ref[0, sl, :] = (k2[sl] * e_rem).astype(BF16)
    seg =(lax.broadcasted_iota(jnp.int32, (16, tm), 1) // CHUNK
           == lax.broadcasted_iota(jnp.int32, (16, tm), 0)).astype(BF16)
    h3, m3, l3 = _split3(ld)
    tots = _dot(seg, h3) + _dot(seg, m3) + _dot(seg, l3)
    put = (lax.broadcasted_iota(jnp.int32, (48, LANES), 0) % 16
           == lax.broadcasted_iota(jnp.int32, (48, LANES), 1)).astype(BF16)
    wc_ref[0, 0] = jnp.exp(_dot_tn(jnp.concatenate(_split3(tots), axis=0), put))
    v_ref[0] = v.astype(BF16)
    bonus_ref[0] = bonus
    g_ref[0] = g


def _rwkv_prep(pc, mu, w0, w2, a0, a2, g2, k_k, k_a, r_k, e):
    b, t, _ = pc.shape
    tm = TOK_TILE
    const = lambda shape: pl.BlockSpec(shape, lambda i, j: (0,) * len(shape))
    tok = lambda: pl.BlockSpec((1, tm, WIDTH_C), lambda i, j: (i, j, 0))
    bf = jax.ShapeDtypeStruct((b, t, WIDTH_C), BF16)
    f32 = jax.ShapeDtypeStruct((b, t, WIDTH_C), F32)
    out_shape = (bf, bf, bf, bf, bf, bf, bf,
                 jax.ShapeDtypeStruct((b, t // tm, WIDTH_C, LANES), F32), f32, f32)
    out_specs = (tok(), tok(), tok(), tok(), tok(), tok(), tok(),
                 pl.BlockSpec((1, 1, WIDTH_C, LANES), lambda i, j: (i, j, 0, 0)), tok(), tok())
    return pl.pallas_call(
        _rwkv_prep_body,
        grid=(b, t // tm),
        in_specs=[pl.BlockSpec((1, tm, COLS_C), lambda i, j: (i, j, 0)),
                  const(mu.shape), const(w0.shape), const(w2.shape), const(a0.shape), const(a2.shape),
                  const(g2.shape), const(k_k.shape), const(k_a.shape), const(r_k.shape), const(e.shape)],
        out_specs=out_specs,
        out_shape=out_shape,
        scratch_shapes=[pltpu.VMEM((tm + 8, COLS_C), F32)],
        compiler_params=_cparams(2),
        name="rwkv_prep",
    )(pc, mu, w0, w2, a0, a2, g2, k_k, k_a, r_k, e)


def _bmm(a, b):
    return lax.dot_general(a, b, (((2,), (1,)), ((0,), (0,))), preferred_element_type=F32)


def _bmm_nt(a, b):
    return lax.dot_general(a, b, (((2,), (2,)), ((0,), (0,))), preferred_element_type=F32)


def _bmm_tn(a, b):
    return lax.dot_general(a, b, (((1,), (1,)), ((0,), (0,))), preferred_element_type=F32)


def _rwkv_chunk_body(rt_ref, at_ref, bt_ref, kt_ref, bh_ref, kh_ref, v_ref, wc_ref, bonus_ref, g_ref,
                     lg_ref, lb_ref, e_ref, o_ref, h_ref):
    tm = rt_ref.shape[1]
    nc = tm // CHUNK
    ng = N_HEADS_C // RWKV_GROUP
    gl = RWKV_GROUP * HEAD_DIM
    t = pl.program_id(1)

    @pl.when(t == 0)
    def _():
        h_ref[...] = jnp.zeros_like(h_ref)

    lane_head = lax.broadcasted_iota(jnp.int32, (1, CHUNK, gl), 2) // HEAD_DIM
    ri = lax.broadcasted_iota(jnp.int32, (1, 2 * CHUNK, gl), 1)
    ci = lax.broadcasted_iota(jnp.int32, (1, 2 * CHUNK, gl), 2) & (CHUNK - 1)
    rr = ri & (CHUNK - 1)
    keep = (rr > ci) | ((ri >= CHUNK) & (rr == ci))
    eye = (lax.broadcasted_iota(jnp.int32, (1, CHUNK, gl), 1)
           == (lax.broadcasted_iota(jnp.int32, (1, CHUNK, gl), 2) & (CHUNK - 1)))
    same_head = (lax.broadcasted_iota(jnp.int32, (1, gl, gl), 1) // HEAD_DIM
                 == lax.broadcasted_iota(jnp.int32, (1, gl, gl), 2) // HEAD_DIM)

    def bdiag(x):
        return jnp.concatenate([jnp.where(lane_head == hh, x, jnp.zeros_like(x)) for hh in range(RWKV_GROUP)],
                               axis=1)

    def chunk_local(cs):
        tile = lambda ref: jnp.stack([ref[0, c * CHUNK:(c + 1) * CHUNK, g * gl:(g + 1) * gl]
                                      for c in cs for g in range(ng)])
        rt, at, bt, kt, bh, kh, v = (tile(r) for r in (rt_ref, at_ref, bt_ref, kt_ref, bh_ref, kh_ref, v_ref))
        ar = jnp.concatenate([at, rt], axis=1)
        sb = jnp.where(keep, _bmm_nt(ar, bdiag(bt)), 0.0)
        sk = jnp.where(keep, _bmm_nt(ar, bdiag(kt)), 0.0).astype(BF16)
        m_rb = sb[:, CHUNK:].astype(BF16)
        a = sb[:, :CHUNK]
        tinv = jnp.where(eye, 1.0, 0.0) + a
        ab = a.astype(BF16)
        x = _bmm(ab, bdiag(ab))
        for _ in range(4):
            xb = x.astype(BF16)
            r = _bmm(jnp.concatenate([tinv.astype(BF16), xb], axis=1), bdiag(xb))
            tinv = tinv + r[:, :CHUNK]
            x = r[:, CHUNK:]
        tinv = (tinv + _bmm(tinv.astype(BF16), bdiag(x.astype(BF16)))).astype(BF16)
        bdv = bdiag(v)
        av = _bmm(sk[:, :CHUNK], bdv)
        p = _bmm(tinv, bdiag(at)).astype(BF16)
        u0 = _bmm(tinv, bdiag(av.astype(BF16)))
        q = (rt.astype(F32) + _bmm(m_rb, bdiag(p))).astype(BF16)
        y0 = _bmm(m_rb, bdiag(u0.astype(BF16))) + _bmm(sk[:, CHUNK:], bdv)
        return (jnp.concatenate([q, p], axis=1), jnp.concatenate([y0, u0], axis=1),
                jnp.concatenate([bh, kh], axis=1), v)

    wct = wc_ref[0, 0]
    hs = h_ref[...]
    ys = []
    local = []
    for c0 in range(0, nc, RWKV_AHEAD):
        cs = list(range(c0, min(c0 + RWKV_AHEAD, nc)))
        parts = chunk_local(cs)
        local += [tuple(z[n * ng:(n + 1) * ng] for z in parts) for n in range(len(cs))]
    for c in range(nc):
        qp, yu0, bk, v = local[c]
        h_hi, h_lo = _split2(hs)
        yu = _bmm(qp, h_hi) + _bmm(qp, h_lo) + yu0
        ys.append(yu[:, :CHUNK])
        upd = _bmm_tn(bk, jnp.concatenate([yu[:, CHUNK:].astype(BF16), v], axis=1))
        wcol = jnp.stack([wct[g * gl:(g + 1) * gl, c:c + 1] for g in range(ng)])
        hs = jnp.where(same_head, upd, 0.0) + wcol * hs
    h_ref[...] = hs

    y = jnp.concatenate([jnp.concatenate([yc[g] for g in range(ng)], axis=-1) for yc in ys], axis=0)
    e = e_ref[...]

    def head_mean(z):
        return _dot(z.astype(BF16), e) * (1.0 / HEAD_DIM)

    d = y - head_mean(y)
    yn = d * lax.rsqrt(head_mean(d * d) + LNX_EPS)
    out = (yn * lg_ref[...] + lb_ref[...] + bonus_ref[0]) * g_ref[0]
    o_ref[0] = out.astype(BF16)


def _rwkv_chunk(rt, at, bt, kt, bh, kh, v, wc, bonus, g, lnx_g, lnx_b, e):
    b, t, _ = rt.shape
    tm = TOK_TILE
    gl = RWKV_GROUP * HEAD_DIM
    const = lambda shape: pl.BlockSpec(shape, lambda i, j: (0,) * len(shape))
    tok = lambda: pl.BlockSpec((1, tm, WIDTH_C), lambda i, j: (i, j, 0))
    return pl.pallas_call(
        _rwkv_chunk_body,
        grid=(b, t // tm),
        in_specs=[tok(), tok(), tok(), tok(), tok(), tok(), tok(),
                  pl.BlockSpec((1, 1, WIDTH_C, LANES), lambda i, j: (i, j, 0, 0)),
                  tok(), tok(), const(lnx_g.shape), const(lnx_b.shape), const(e.shape)],
        out_specs=tok(),
        out_shape=jax.ShapeDtypeStruct((b, t, WIDTH_C), BF16),
        scratch_shapes=[pltpu.VMEM((N_HEADS_C // RWKV_GROUP, gl, gl), F32)],
        compiler_params=_cparams(2),
        name="rwkv_chunk",
    )(rt, at, bt, kt, bh, kh, v, wc, bonus, g, lnx_g, lnx_b, e)


def _outproj_body(x_ref, ya_ref, yb_ref, yc_ref, wa_ref, wb_ref, wc_ref, o_ref):
    acc = _dot_tn(ya_ref[0], wa_ref[...])
    acc = acc + _dot_tn(yb_ref[0], wb_ref[...])
    acc = acc + _dot(yc_ref[0], wc_ref[...])
    o_ref[0] = x_ref[0] + acc


def _outproj(x, yaT, ybT, yc, wa, wb, wc):
    b, t, d = x.shape
    tm = TOK_TILE
    const = lambda shape: pl.BlockSpec(shape, lambda i, j: (0,) * len(shape))
    return pl.pallas_call(
        _outproj_body,
        grid=(b, t // tm),
        in_specs=[pl.BlockSpec((1, tm, d), lambda i, j: (i, j, 0)),
                  pl.BlockSpec((1, WIDTH_A, tm), lambda i, j: (i, 0, j)),
                  pl.BlockSpec((1, WIDTH_B, tm), lambda i, j: (i, 0, j)),
                  pl.BlockSpec((1, tm, WIDTH_C), lambda i, j: (i, j, 0)),
                  const(wa.shape), const(wb.shape), const(wc.shape)],
        out_specs=pl.BlockSpec((1, tm, d), lambda i, j: (i, j, 0)),
        out_shape=jax.ShapeDtypeStruct((b, t, d), F32),
        compiler_params=_cparams(2),
        name="outproj",
    )(x, yaT, ybT, yc, wa, wb, wc)


def _ffn_body(x_ref, g_ref, wg_ref, wv_ref, cw_ref, cb_ref, wd_ref, o_ref, gbuf):
    tm = x_ref.shape[1]
    t = pl.program_id(1)

    @pl.when(t == 0)
    def _():
        gbuf[0:8, :] = jnp.zeros((8, D_FF), F32)

    x = x_ref[0]
    ms = jnp.mean(x * x, axis=-1, keepdims=True)
    h = (x * lax.rsqrt(ms + RMS_EPS) * g_ref[...]).astype(BF16)
    acc = x
    for lo, hi in zip(FF_SPLITS[:-1], FF_SPLITS[1:]):
        gate = _dot(h, wg_ref[:, lo:hi])
        val = _dot(h, wv_ref[:, lo:hi])
        gbuf[8:8 + tm, lo:hi] = gate
        g1 = gbuf[7:7 + tm, lo:hi]
        g2 = gbuf[6:6 + tm, lo:hi]
        gbuf[0:8, lo:hi] = gate[tm - 8:tm, :]
        conv = cb_ref[:, lo:hi] + g2 * cw_ref[0:1, lo:hi] + g1 * cw_ref[1:2, lo:hi] + gate * cw_ref[2:3, lo:hi]
        act = (conv * _sigmoid(conv) * val).astype(BF16)
        acc = acc + _dot(act, wd_ref[lo:hi, :])
    o_ref[0] = acc


def _ffn(x, g, wg, wv, cw, cb, wd):
    b, t, d = x.shape
    tm = TOK_TILE
    const = lambda shape: pl.BlockSpec(shape, lambda i, j: (0,) * len(shape),
                                       pipeline_mode=pl.Buffered(1))
    return pl.pallas_call(
        _ffn_body,
        grid=(b, t // tm),
        in_specs=[pl.BlockSpec((1, tm, d), lambda i, j: (i, j, 0)),
                  const(g.shape), const(wg.shape), const(wv.shape), const(cw.shape), const(cb.shape),
                  const(wd.shape)],
        out_specs=pl.BlockSpec((1, tm, d), lambda i, j: (i, j, 0)),
        out_shape=jax.ShapeDtypeStruct((b, t, d), F32),
        scratch_shapes=[pltpu.VMEM((tm + 8, D_FF), F32)],
        compiler_params=_cparams(2),
        name="ffn",
    )(x, g, wg, wv, cw, cb, wd)


def _pad_heads_cols(w, nh):
    d = w.shape[0]
    w = w.reshape(d, nh, HEAD_DIM)
    return jnp.pad(w, ((0, 0), (0, 0), (0, LANES - HEAD_DIM))).reshape(d, nh * LANES)


def _qk_bound(gq, gk):
    scale = HEAD_DIM ** -0.5
    return 1.01 * HEAD_DIM * scale * jnp.max(jnp.abs(gq)) * jnp.max(jnp.abs(gk)) + 0.05


def _layer_params(l, w_in, q_norm_a, k_norm_a, rel_bias, q_norm_b, k_norm_b, forget_bias):
    w = w_in[l]
    scale = HEAD_DIM ** -0.5
    a0, b0 = 0, 3 * WIDTH_A
    c0 = b0 + 3 * WIDTH_B + N_HEADS_B
    qa, ka, va = (w[:, a0 + i * WIDTH_A:a0 + (i + 1) * WIDTH_A] for i in range(3))
    qb, kb, vb = (w[:, b0 + i * WIDTH_B:b0 + (i + 1) * WIDTH_B] for i in range(3))
    fg = w[:, b0 + 3 * WIDTH_B:c0]
    wc = w[:, c0:]
    wn = jnp.concatenate([_pad_heads_cols(ka, N_HEADS_A), _pad_heads_cols(kb, N_HEADS_B),
                          jnp.pad(fg, ((0, 0), (0, LANES - N_HEADS_B))), wc], axis=1).astype(BF16)
    pad_v = lambda v, nh: jnp.pad(v.reshape(-1, nh, HEAD_DIM),
                                  ((0, 0), (0, 0), (0, VB_ROWS - HEAD_DIM))).reshape(-1, nh * VB_ROWS)
    wt = jnp.concatenate([qa, pad_v(va, N_HEADS_A), _pad_heads_cols(qb, N_HEADS_B), pad_v(vb, N_HEADS_B),
                          jnp.pad(fg, ((0, 0), (0, 16 - N_HEADS_B)))], axis=1).T.astype(BF16)
    pad = LANES - HEAD_DIM
    gka = jnp.pad(k_norm_a[l], (0, pad)).reshape(1, LANES)
    gkb = jnp.pad(k_norm_b[l], (0, pad)).reshape(1, LANES)
    gqa = (q_norm_a[l] * (scale * LOG2E)).reshape(HEAD_DIM, 1)
    gqb = jnp.pad(q_norm_b[l] * (scale * LOG2E), (0, pad)).reshape(LANES, 1)
    fbn = jnp.pad(forget_bias[l], (0, LANES - N_HEADS_B)).reshape(1, LANES)
    fbt = jnp.pad(forget_bias[l], (0, 16 - N_HEADS_B)).reshape(16, 1)
    shift_b = (LOG2E * _qk_bound(q_norm_b[l], k_norm_b[l])).reshape(1, 1)
    bound_a = _qk_bound(q_norm_a[l], k_norm_a[l])
    tab = rel_bias[l]
    shift_a = LOG2E * (bound_a + jnp.max(tab, axis=1))
    depth_a = jnp.max(shift_a + LOG2E * (bound_a - tab[:, MAX_REL]))
    return wn, wt, gka, gkb, gqa, gqb, fbn, fbt, shift_b, shift_a, depth_a


def _selectors():
    import numpy as np
    selk = np.zeros((3, LANES, N_HEADS_B * LANES), np.float32)
    selq = np.zeros((3, N_HEADS_B * LANES, 16), np.float32)
    for p in range(3):
        for hh in range(N_HEADS_B):
            selk[p, hh, hh * LANES + HEAD_DIM + 3 + p] = -1.0
            selq[p, hh * LANES + HEAD_DIM + p, hh] = 1.0
    e = np.kron(np.eye(N_HEADS_C, dtype=np.float32), np.ones((HEAD_DIM, HEAD_DIM), np.float32))
    return jnp.asarray(selk, BF16), jnp.asarray(selq, BF16), jnp.asarray(e, BF16)


def kernel(x, mix_norm_g, w_in, q_norm_a, k_norm_a, rel_bias, q_norm_b, k_norm_b, forget_bias, shift_mu, w0, w2,
           a0, a2, g2, k_k, k_a, r_k, lnx_g, lnx_b, w_out, ffn_norm_g, w_up, conv_w, conv_b, w_down):
    depth = w_in.shape[0]
    b, t, d = x.shape
    selk, selq, e = _selectors()
    row = lambda v: v.reshape(1, -1)
    for l in range(depth):
        wn, wt, gka, gkb, gqa, gqb, fbn, fbt, shift_b, shift_a, depth_a = _layer_params(
            l, w_in, q_norm_a, k_norm_a, rel_bias, q_norm_b, k_norm_b, forget_bias)
        qaT, vaT, kA, qbT, vbT, kB, pc = _inproj(x, row(mix_norm_g[l]), wn, wt, gka, gkb, gqa, gqb,
                                                 fbn, fbt, selk, selq, shift_b)
        bias = _relbias(rel_bias[l], shift_a)
        yaT = lax.cond(depth_a <= 2 * FAST_MAX_SHIFT,
                       functools.partial(_attn_a, online=False), functools.partial(_attn_a, online=True),
                       qaT, kA, vaT, bias)
        ybT = lax.cond(shift_b[0, 0] <= FAST_MAX_SHIFT,
                       functools.partial(_fox, online=False), functools.partial(_fox, online=True),
                       qbT, kB, vbT)
        rt, at, bt, kt, bh, kh, v, wcum, bonus, g = _rwkv_prep(
            pc, row(shift_mu[l]), row(w0[l]), w2[l].astype(BF16), row(a0[l]), a2[l].astype(BF16),
            g2[l].astype(BF16), row(k_k[l]), row(k_a[l]), row(r_k[l]), e)
        yc = _rwkv_chunk(rt, at, bt, kt, bh, kh, v, wcum, bonus, g, row(lnx_g[l]), row(lnx_b[l]), e)
        wo = w_out[l].astype(BF16)
        x = _outproj(x, yaT.reshape(b, WIDTH_A, t), ybT.reshape(b, WIDTH_B, t), yc,
                     wo[:WIDTH_A], wo[WIDTH_A:WIDTH_A + WIDTH_B], wo[WIDTH_A + WIDTH_B:])
        wu = w_up[l].astype(BF16)
        x = _ffn(x, row(ffn_norm_g[l]), wu[:, :D_FF], wu[:, D_FF:], conv_w[l], row(conv_b[l]),
                 w_down[l].astype(BF16))
    return x
```

```python
import functools
import math

import jax
import jax.numpy as jnp
from jax import lax
from jax.experimental import pallas as pl
from jax.experimental.pallas import tpu as pltpu

F32 = jnp.float32
BF16 = jnp.bfloat16

D_MODEL = 1024
HEAD_DIM = 64
CHUNK = 64
LEFT_CHUNKS = 8
MAX_REL = 128
N_HEADS_A = 4
N_HEADS_B = 4
N_HEADS_C = 8
WIDTH_A = N_HEADS_A * HEAD_DIM
WIDTH_B = N_HEADS_B * HEAD_DIM
WIDTH_C = N_HEADS_C * HEAD_DIM
DECAY_LORA = 64
AAA_LORA = 64
GATE_LORA = 128
COLS_C = 3 * WIDTH_C + DECAY_LORA + AAA_LORA + GATE_LORA
D_FF = 2816
RMS_EPS = 1e-6
LNX_EPS = 64e-5
NEG_INF = -1e30

LANES = 128
TOK_TILE = 512
ATT_A_TILE = 256
ATT_A_WIN = 3 * ATT_A_TILE
FOX_TILE = 512
FOX_GROUP = 4
RWKV_GROUP = 4
RWKV_TILE = 256
RWKV_ROWS = 4
RWKV_SCHEDULE = ((0, 1), (2, 3))
RWKV_STAGES = 9
FF_SPLITS = (0, 768, 1536, 2304, 2816)
VMEM_LIMIT = 56 * 1024 * 1024

PN_KA, PN_KB, PN_F, PN_C = 0, 512, 1024, 1152
PN_COLS = PN_C + COLS_C
VB_ROWS = 80
PT_QA = 0
PT_VA = PT_QA + WIDTH_A
PT_QB = PT_VA + N_HEADS_A * VB_ROWS
PT_VB = PT_QB + N_HEADS_B * LANES
PT_F = PT_VB + N_HEADS_B * VB_ROWS
PT_ROWS = PT_F + 16
LOG2E = 1.4426950408889634
FAST_MAX_SHIFT = 40.0


def _cparams(n_axes):
    return pltpu.CompilerParams(dimension_semantics=("arbitrary",) * n_axes,
                                vmem_limit_bytes=VMEM_LIMIT)


def _split3(x):
    hi = x.astype(BF16)
    r1 = x - hi.astype(F32)
    mid = r1.astype(BF16)
    lo = (r1 - mid.astype(F32)).astype(BF16)
    return hi, mid, lo


def _split2(x):
    hi = x.astype(BF16)
    lo = (x - hi.astype(F32)).astype(BF16)
    return hi, lo


def _dot(a, b):
    return jnp.dot(a, b, preferred_element_type=F32)


def _dot_nt(a, b):
    return lax.dot_general(a, b, (((1,), (1,)), ((), ())), preferred_element_type=F32)


def _dot_tn(a, b):
    return lax.dot_general(a, b, (((0,), (0,)), ((), ())), preferred_element_type=F32)


def _log_sigmoid(x):
    return jnp.minimum(x, 0.0) - jnp.log(1.0 + jnp.exp(-jnp.abs(x)))


def _softplus(x):
    return jnp.maximum(x, 0.0) + jnp.log(1.0 + jnp.exp(-jnp.abs(x)))


def _sigmoid(x):
    return 1.0 / (1.0 + jnp.exp(-x))


def _inproj_body(x_ref, g_ref, wn_ref, wt_ref, gka_ref, gkb_ref, gqa_ref, gqb_ref, fbn_ref, fbt_ref,
                 selk_ref, selq_ref, shift_ref,
                 qaT_ref, vaT_ref, kA_ref, qbT_ref, vbT_ref, kB_ref, pc_ref,
                 carry_n, carry_t):
    tm = x_ref.shape[1]
    t = pl.program_id(1)

    @pl.when(t == 0)
    def _():
        carry_n[...] = jnp.zeros_like(carry_n)
        carry_t[...] = jnp.zeros_like(carry_t)

    x = x_ref[0]
    ms = jnp.mean(x * x, axis=-1, keepdims=True)
    h = (x * lax.rsqrt(ms + RMS_EPS) * g_ref[...]).astype(BF16)
    pn = _dot(h, wn_ref[...])
    pt = _dot_nt(wt_ref[...], h)

    qa = pt[PT_QA:PT_QA + WIDTH_A].reshape(N_HEADS_A, HEAD_DIM, tm)
    msq = jnp.mean(qa * qa, axis=1, keepdims=True)
    qaT_ref[0] = (qa * lax.rsqrt(msq + RMS_EPS) * gqa_ref[...][None]).astype(BF16)
    row_v = lax.broadcasted_iota(jnp.int32, (VB_ROWS, 1), 0)
    ones_v = jnp.where(row_v == HEAD_DIM, 1.0, 0.0)
    va = pt[PT_VA:PT_VA + N_HEADS_A * VB_ROWS].reshape(N_HEADS_A, VB_ROWS, tm)
    vaT_ref[0] = (va + ones_v[None]).astype(BF16)
    for hh in range(N_HEADS_A):
        k = pn[:, PN_KA + LANES * hh:PN_KA + LANES * (hh + 1)]
        msk = jnp.sum(k * k, axis=-1, keepdims=True) * (1.0 / HEAD_DIM)
        kn = k * lax.rsqrt(msk + RMS_EPS) * gka_ref[...]
        kA_ref[0, hh] = kn[:, :HEAD_DIM].astype(BF16)

    lane_n = lax.broadcasted_iota(jnp.int32, (1, LANES), 1)
    lf_n = jnp.where(lane_n < N_HEADS_B, LOG2E * _log_sigmoid(pn[:, PN_F:PN_F + LANES] + fbn_ref[...]), 0.0)
    row_t = lax.broadcasted_iota(jnp.int32, (16, 1), 0)
    lf_t = jnp.where(row_t < N_HEADS_B, LOG2E * _log_sigmoid(pt[PT_F:PT_F + 16] + fbt_ref[...]), 0.0)
    ri = lax.broadcasted_iota(jnp.int32, (tm, tm), 0)
    ci = lax.broadcasted_iota(jnp.int32, (tm, tm), 1)
    low = (ci <= ri).astype(BF16)
    upp = (ri <= ci).astype(BF16)
    hn, mn, ln = _split3(lf_n)
    c_n = _dot(low, hn) + _dot(low, mn) + _dot(low, ln) + carry_n[0:1, :]
    ht, mt, lt = _split3(lf_t)
    c_t = _dot(ht, upp) + _dot(mt, upp) + _dot(lt, upp) + carry_t[:, 0:1]
    carry_n[...] = carry_n[...] + jnp.sum(lf_n, axis=0, keepdims=True)
    carry_t[...] = carry_t[...] + jnp.sum(lf_t, axis=1, keepdims=True)

    chn, cmn, cln = _split3(c_n)
    kaug = _dot(chn, selk_ref[0]) + _dot(cmn, selk_ref[1]) + _dot(cln, selk_ref[2])
    lane_k = lax.broadcasted_iota(jnp.int32, (1, LANES), 1)
    ones_k = jnp.where((lane_k >= HEAD_DIM) & (lane_k < HEAD_DIM + 3), 1.0, 0.0)
    for hh in range(N_HEADS_B):
        k = pn[:, PN_KB + LANES * hh:PN_KB + LANES * (hh + 1)]
        msk = jnp.sum(k * k, axis=-1, keepdims=True) * (1.0 / HEAD_DIM)
        kn = k * lax.rsqrt(msk + RMS_EPS) * gkb_ref[...]
        kB_ref[0, hh] = (kn + kaug[:, LANES * hh:LANES * (hh + 1)] + ones_k).astype(BF16)

    cht, cmt, clt = _split3(c_t - shift_ref[...])
    qaug = _dot(selq_ref[0], cht) + _dot(selq_ref[1], cmt) + _dot(selq_ref[2], clt)
    qb = pt[PT_QB:PT_QB + N_HEADS_B * LANES].reshape(N_HEADS_B, LANES, tm)
    msq = jnp.sum(qb * qb, axis=1, keepdims=True) * (1.0 / HEAD_DIM)
    row_q = lax.broadcasted_iota(jnp.int32, (LANES, 1), 0)
    ones_q = jnp.where((row_q >= HEAD_DIM + 3) & (row_q < HEAD_DIM + 6), 1.0, 0.0)
    qn = qb * lax.rsqrt(msq + RMS_EPS) * gqb_ref[...][None]
    qbT_ref[0] = (qn + qaug.reshape(N_HEADS_B, LANES, tm) + ones_q[None]).astype(BF16)
    vb = pt[PT_VB:PT_VB + N_HEADS_B * VB_ROWS].reshape(N_HEADS_B, VB_ROWS, tm)
    vbT_ref[0, :, 0] = (vb + ones_v[None]).astype(BF16)

    pc_ref[0] = pn[:, PN_C:PN_C + COLS_C]


def _inproj(x, g, wn, wt, gka, gkb, gqa, gqb, fbn, fbt, selk, selq, shift):
    b, t, d = x.shape
    tm = TOK_TILE
    nt = t // tm
    const = lambda shape: pl.BlockSpec(shape, lambda i, j: (0,) * len(shape))
    out_shape = (
        jax.ShapeDtypeStruct((b, N_HEADS_A, HEAD_DIM, t), BF16),
        jax.ShapeDtypeStruct((b, N_HEADS_A, VB_ROWS, t), BF16),
        jax.ShapeDtypeStruct((b, N_HEADS_A, t, HEAD_DIM), BF16),
        jax.ShapeDtypeStruct((b, N_HEADS_B, LANES, t), BF16),
        jax.ShapeDtypeStruct((b, N_HEADS_B, nt, VB_ROWS, tm), BF16),
        jax.ShapeDtypeStruct((b, N_HEADS_B, t, LANES), BF16),
        jax.ShapeDtypeStruct((b, t, COLS_C), F32),
    )
    out_specs = (
        pl.BlockSpec((1, N_HEADS_A, HEAD_DIM, tm), lambda i, j: (i, 0, 0, j)),
        pl.BlockSpec((1, N_HEADS_A, VB_ROWS, tm), lambda i, j: (i, 0, 0, j)),
        pl.BlockSpec((1, N_HEADS_A, tm, HEAD_DIM), lambda i, j: (i, 0, j, 0)),
        pl.BlockSpec((1, N_HEADS_B, LANES, tm), lambda i, j: (i, 0, 0, j)),
        pl.BlockSpec((1, N_HEADS_B, 1, VB_ROWS, tm), lambda i, j: (i, 0, j, 0, 0)),
        pl.BlockSpec((1, N_HEADS_B, tm, LANES), lambda i, j: (i, 0, j, 0)),
        pl.BlockSpec((1, tm, COLS_C), lambda i, j: (i, j, 0)),
    )
    in_specs = [
        pl.BlockSpec((1, tm, d), lambda i, j: (i, j, 0)),
        const((1, d)), const(wn.shape), const(wt.shape),
        const(gka.shape), const(gkb.shape), const(gqa.shape), const(gqb.shape),
        const(fbn.shape), const(fbt.shape), const(selk.shape), const(selq.shape), const(shift.shape),
    ]
    return pl.pallas_call(
        _inproj_body,
        grid=(b, nt),
        in_specs=in_specs,
        out_specs=out_specs,
        out_shape=out_shape,
        scratch_shapes=[pltpu.VMEM((8, LANES), F32), pltpu.VMEM((16, LANES), F32)],
        compiler_params=_cparams(2),
        name="inproj",
    )(x, g, wn, wt, gka, gkb, gqa, gqb, fbn, fbt, selk, selq, shift)


def _relbias_body(tab_ref, shift_ref, o_ref):
    hh = pl.program_id(0)
    kj = lax.broadcasted_iota(jnp.int32, (ATT_A_WIN, ATT_A_TILE), 0)
    qi = lax.broadcasted_iota(jnp.int32, (ATT_A_WIN, ATT_A_TILE), 1)
    rel = jnp.clip(kj - LEFT_CHUNKS * CHUNK - qi, -MAX_REL, MAX_REL) + MAX_REL

    def body(r, acc):
        return jnp.where(rel == r, tab_ref[hh, r], acc)

    bias = lax.fori_loop(0, 2 * MAX_REL + 1, body, jnp.zeros((ATT_A_WIN, ATT_A_TILE), F32))
    kc = kj // CHUNK
    qc = qi // CHUNK
    band = (kc >= qc) & (kc <= qc + LEFT_CHUNKS)
    o_ref[0] = jnp.where(band, LOG2E * bias - shift_ref[hh], NEG_INF)


def _relbias(tab, shift):
    return pl.pallas_call(
        _relbias_body,
        grid=(N_HEADS_A,),
        in_specs=[pl.BlockSpec(memory_space=pltpu.SMEM), pl.BlockSpec(memory_space=pltpu.SMEM)],
        out_specs=pl.BlockSpec((1, ATT_A_WIN, ATT_A_TILE), lambda i: (i, 0, 0)),
        out_shape=jax.ShapeDtypeStruct((N_HEADS_A, ATT_A_WIN, ATT_A_TILE), F32),
        compiler_params=_cparams(1),
        name="relbias",
    )(tab, shift)


def _attn_a_body(q_ref, k0_ref, k1_ref, k2_ref, v0_ref, v1_ref, v2_ref, bias_ref, o_ref, *, online):
    i = pl.program_id(1)
    tq = ATT_A_TILE
    kj = lax.broadcasted_iota(jnp.int32, (tq, 1), 0)

    def run(mask_padding):
        for hh in range(N_HEADS_A):
            q = q_ref[0, hh]
            ss = []
            for d, k_ref in enumerate((k0_ref, k1_ref, k2_ref)):
                s = _dot(k_ref[0, hh], q) + bias_ref[hh, d * tq:(d + 1) * tq, :]
                if mask_padding:
                    s = jnp.where(kj + (i - 2 + d) * tq >= 0, s, NEG_INF)
                ss.append(s)
            if online:
                m = functools.reduce(jnp.maximum, [jnp.max(s, axis=0, keepdims=True) for s in ss])
                ss = [s - m for s in ss]
            acc = jnp.zeros((VB_ROWS, tq), F32)
            for s, v_ref in zip(ss, (v0_ref, v1_ref, v2_ref)):
                acc = acc + _dot(v_ref[0, hh], jnp.exp2(s).astype(BF16))
            o_ref[0, hh] = (acc[0:HEAD_DIM] / acc[HEAD_DIM:HEAD_DIM + 1]).astype(BF16)

    pl.when(i < 2)(functools.partial(run, True))
    pl.when(i >= 2)(functools.partial(run, False))


def _attn_a(qaT, kA, vaT, bias, online):
    b, nh, _, t = qaT.shape
    tq = ATT_A_TILE
    kspec = lambda d: pl.BlockSpec((1, nh, tq, HEAD_DIM),
                                   lambda bb, i: (bb, 0, jnp.maximum(i - 2 + d, 0), 0))
    vspec = lambda d: pl.BlockSpec((1, nh, VB_ROWS, tq),
                                   lambda bb, i: (bb, 0, 0, jnp.maximum(i - 2 + d, 0)))
    return pl.pallas_call(
        functools.partial(_attn_a_body, online=online),
        grid=(b, t // tq),
        in_specs=[pl.BlockSpec((1, nh, HEAD_DIM, tq), lambda bb, i: (bb, 0, 0, i)),
                  kspec(0), kspec(1), kspec(2), vspec(0), vspec(1), vspec(2),
                  pl.BlockSpec((nh, ATT_A_WIN, tq), lambda bb, i: (0, 0, 0))],
        out_specs=pl.BlockSpec((1, nh, HEAD_DIM, tq), lambda bb, i: (bb, 0, 0, i)),
        out_shape=jax.ShapeDtypeStruct((b, nh, HEAD_DIM, t), BF16),
        compiler_params=_cparams(2),
        name="attn_a_online" if online else "attn_a",
    )(qaT, kA, kA, kA, vaT, vaT, vaT, bias)


def _fox_body(q_ref, k_ref, v_ref, o_ref, *, online):
    i = pl.program_id(2)
    tq = FOX_TILE
    q = q_ref[0, 0]

    def scores(j):
        ks = k_ref[0, 0, pl.ds(pl.multiple_of(j * tq, tq), tq), :]
        return _dot(ks, q)

    def accumulate(j, s, m, acc, diagonal):
        vs = v_ref[0, 0, j]
        if diagonal:
            kj = lax.broadcasted_iota(jnp.int32, (tq, tq), 0)
            qi = lax.broadcasted_iota(jnp.int32, (tq, tq), 1)
            causal = kj <= qi
        if online:
            if diagonal:
                s = jnp.where(causal, s, NEG_INF)
            m_new = jnp.maximum(m, jnp.max(s, axis=0, keepdims=True))
            acc = jnp.exp2(m - m_new) * acc
            p = jnp.exp2(s - m_new)
            m = m_new
        else:
            p = jnp.exp2(s)
            if diagonal:
                p = jnp.where(causal, p, 0.0)
        return m, acc + _dot(vs, p.astype(BF16))

    def tiles(js, carry, last_diagonal):
        ss = [scores(j) for j in js]
        m, acc = carry
        for n, (j, s) in enumerate(zip(js, ss)):
            m, acc = accumulate(j, s, m, acc, last_diagonal and n == len(js) - 1)
        return m, acc

    init = (jnp.full((1, tq), NEG_INF, F32), jnp.zeros((VB_ROWS, tq), F32))
    if online:
        carry = lax.fori_loop(0, i, lambda j, c: tiles([j], c, False), init)
        _, acc = tiles([i], carry, True)
    else:
        g = FOX_GROUP
        carry = lax.fori_loop(0, i // g, lambda jj, c: tiles([g * jj + n for n in range(g)], c, False), init)
        tails = [functools.partial(lambda c, r: tiles([i - r + n for n in range(r + 1)], c, True), r=r)
                 for r in range(g)]
        _, acc = lax.switch(i % g, tails, carry)
    o_ref[0, 0] = (acc[0:HEAD_DIM] / acc[HEAD_DIM:HEAD_DIM + 1]).astype(BF16)


def _fox(qbT, kB, vbT, online):
    b, nh, _, t = qbT.shape
    tq = FOX_TILE
    nk = t // tq
    return pl.pallas_call(
        functools.partial(_fox_body, online=online),
        grid=(b, nh, nk),
        in_specs=[pl.BlockSpec((1, 1, LANES, tq), lambda bb, hh, i: (bb, hh, 0, i)),
                  pl.BlockSpec((1, 1, t, LANES), lambda bb, hh, i: (bb, hh, 0, 0)),
                  pl.BlockSpec((1, 1, nk, VB_ROWS, tq), lambda bb, hh, i: (bb, hh, 0, 0, 0))],
        out_specs=pl.BlockSpec((1, 1, HEAD_DIM, tq), lambda bb, hh, i: (bb, hh, 0, i)),
        out_shape=jax.ShapeDtypeStruct((b, nh, HEAD_DIM, t), BF16),
        compiler_params=_cparams(3),
        name="fox_online" if online else "fox",
    )(qbT, kB, vbT)


def _rwkv_prep_body(pc_ref, mu_ref, w0_ref, w2_ref, a0_ref, a2_ref, g2_ref, kk_ref, ka_ref, rk_ref, e_ref,
                    rt_ref, at_ref, bt_ref, kt_ref, bh_ref, kh_ref, v_ref, wc_ref, bonus_ref, g_ref,
                    sbuf):
    tm = pc_ref.shape[1]
    t = pl.program_id(1)

    @pl.when(t == 0)
    def _():
        sbuf[0:8, :] = jnp.zeros((8, COLS_C), F32)

    u = pc_ref[0]
    sbuf[8:8 + tm, :] = u
    prev = sbuf[7:7 + tm, :]
    sbuf[0:8, :] = u[tm - 8:tm, :]
    u = u + (prev - u) * mu_ref[...]

    c = WIDTH_C
    r = u[:, 0:c]
    k = u[:, c:2 * c]
    v = u[:, 2 * c:3 * c]
    w_lo = u[:, 3 * c:3 * c + DECAY_LORA]
    a_lo = u[:, 3 * c + DECAY_LORA:3 * c + DECAY_LORA + AAA_LORA]
    g_lo = u[:, 3 * c + DECAY_LORA + AAA_LORA:COLS_C]

    w = w0_ref[...] + _dot(jnp.tanh(w_lo).astype(BF16), w2_ref[...])
    w = -_softplus(-w) - 0.5
    ld = -jnp.exp(w)
    a = _sigmoid(a0_ref[...] + _dot(a_lo.astype(BF16), a2_ref[...]))
    g = _dot(_sigmoid(g_lo).astype(BF16), g2_ref[...])

    e = e_ref[...]
    kk = k * kk_ref[...]
    s_hi, s_lo = _split2(kk * kk)
    nrm2 = _dot(s_hi, e) + _dot(s_lo, e)
    kkn = kk * lax.rsqrt(jnp.maximum(nrm2, 1e-24))
    k2 = k * (1.0 + (a - 1.0) * ka_ref[...])
    kka = kkn * a
    b_hi, b_lo = _split2(r * k2 * rk_ref[...])
    bonus = (_dot(b_hi, e) + _dot(b_lo, e)) * v

    ri = lax.broadcasted_iota(jnp.int32, (2 * CHUNK, CHUNK), 0)
    ci = lax.broadcasted_iota(jnp.int32, (2 * CHUNK, CHUNK), 1)
    tri = ((ci <= ri) | (ri >= CHUNK)).astype(BF16)
    for cc in range(tm // CHUNK):
        sl = slice(cc * CHUNK, (cc + 1) * CHUNK)
        h3, m3, l3 = _split3(ld[sl])
        cum = _dot(tri, h3) + _dot(tri, m3) + _dot(tri, l3)
        lc = cum[0:CHUNK]
        tot = cum[CHUNK:2 * CHUNK]
        e_neg = jnp.exp(-lc)
        e_rem = jnp.exp(tot - lc)
        rt_ref[0, sl, :] = (r[sl] * jnp.exp(lc)).astype(BF16)
        at_ref[0, sl, :] = (-kkn[sl] * jnp.exp(lc - ld[sl])).astype(BF16)
        bt_ref[0, sl, :] = (kka[sl] * e_neg).astype(BF16)
        kt_ref[0, sl, :] = (k2[sl] * e_neg).astype(BF16)
        bh_ref[0, sl, :] = (kka[sl] * e_rem).astype(BF16)
        kh_ref[0, sl, :] = (k2[sl] * e_rem).astype(BF16)
    seg = (lax.broadcasted_iota(jnp.int32, (16, tm), 1) // CHUNK
           == lax.broadcasted_iota(jnp.int32, (16, tm), 0)).astype(BF16)
    h3, m3, l3 = _split3(ld)
    tots3 = jnp.concatenate(_split3(_dot(seg, h3) + _dot(seg, m3) + _dot(seg, l3)), axis=0)
    per_blk = RWKV_TILE // CHUNK
    for blk in range(tm // RWKV_TILE):
        put = (lax.broadcasted_iota(jnp.int32, (48, LANES), 0) % 16
               == lax.broadcasted_iota(jnp.int32, (48, LANES), 1) + blk * per_blk).astype(BF16)
        wc_ref[0, blk] = jnp.exp(_dot_tn(tots3, put))
    v_ref[0] = v.astype(BF16)
    bonus_ref[0] = bonus
    g_ref[0] = g


def _rwkv_prep(pc, mu, w0, w2, a0, a2, g2, k_k, k_a, r_k, e):
    b, t, _ = pc.shape
    tm = TOK_TILE
    const = lambda shape: pl.BlockSpec(shape, lambda i, j: (0,) * len(shape))
    tok = lambda: pl.BlockSpec((1, tm, WIDTH_C), lambda i, j: (i, j, 0))
    bf = jax.ShapeDtypeStruct((b, t, WIDTH_C), BF16)
    f32 = jax.ShapeDtypeStruct((b, t, WIDTH_C), F32)
    out_shape = (bf, bf, bf, bf, bf, bf, bf,
                 jax.ShapeDtypeStruct((b, t // RWKV_TILE, WIDTH_C, LANES), F32), f32, f32)
    out_specs = (tok(), tok(), tok(), tok(), tok(), tok(), tok(),
                 pl.BlockSpec((1, tm // RWKV_TILE, WIDTH_C, LANES), lambda i, j: (i, j, 0, 0)), tok(), tok())
    return pl.pallas_call(
        _rwkv_prep_body,
        grid=(b, t // tm),
        in_specs=[pl.BlockSpec((1, tm, COLS_C), lambda i, j: (i, j, 0)),
                  const(mu.shape), const(w0.shape), const(w2.shape), const(a0.shape), const(a2.shape),
                  const(g2.shape), const(k_k.shape), const(k_a.shape), const(r_k.shape), const(e.shape)],
        out_specs=out_specs,
        out_shape=out_shape,
        scratch_shapes=[pltpu.VMEM((tm + 8, COLS_C), F32)],
        compiler_params=_cparams(2),
        name="rwkv_prep",
    )(pc, mu, w0, w2, a0, a2, g2, k_k, k_a, r_k, e)


def _bmm(a, b):
    return lax.dot_general(a, b, (((2,), (1,)), ((0,), (0,))), preferred_element_type=F32)


def _bmm_nt(a, b):
    return lax.dot_general(a, b, (((2,), (2,)), ((0,), (0,))), preferred_element_type=F32)


def _bmm_tn(a, b):
    return lax.dot_general(a, b, (((1,), (1,)), ((0,), (0,))), preferred_element_type=F32)


def _rwkv_chunk_body(rt_ref, at_ref, bt_ref, kt_ref, bh_ref, kh_ref, v_ref, wc_ref, bonus_ref, g_ref,
                     lg_ref, lb_ref, e_ref, o_ref, h_ref):
    rows, tm = rt_ref.shape[0], rt_ref.shape[1]
    nc = tm // CHUNK
    ng = N_HEADS_C // RWKV_GROUP
    gl = RWKV_GROUP * HEAD_DIM
    units = [(r, g) for r in range(rows) for g in range(ng)]
    nu = len(units)
    t = pl.program_id(1)

    @pl.when(t == 0)
    def _():
        h_ref[...] = jnp.zeros_like(h_ref)

    lane_head = lax.broadcasted_iota(jnp.int32, (1, CHUNK, gl), 2) // HEAD_DIM
    ri = lax.broadcasted_iota(jnp.int32, (1, 2 * CHUNK, gl), 1)
    ci = lax.broadcasted_iota(jnp.int32, (1, 2 * CHUNK, gl), 2) & (CHUNK - 1)
    rr = ri & (CHUNK - 1)
    keep = (rr > ci) | ((ri >= CHUNK) & (rr == ci))
    eye = (lax.broadcasted_iota(jnp.int32, (1, CHUNK, gl), 1)
           == (lax.broadcasted_iota(jnp.int32, (1, CHUNK, gl), 2) & (CHUNK - 1)))
    same_head = (lax.broadcasted_iota(jnp.int32, (1, gl, gl), 1) // HEAD_DIM
                 == lax.broadcasted_iota(jnp.int32, (1, gl, gl), 2) // HEAD_DIM)

    def bdiag(x):
        return jnp.concatenate([jnp.where(lane_head == hh, x, jnp.zeros_like(x)) for hh in range(RWKV_GROUP)],
                               axis=1)

    def chunk_local(cs, out):
        tile = lambda ref: jnp.stack([ref[r, c * CHUNK:(c + 1) * CHUNK, g * gl:(g + 1) * gl]
                                      for c in cs for r, g in units])
        rt, at, bt, kt, bh, kh, v = (tile(r) for r in (rt_ref, at_ref, bt_ref, kt_ref, bh_ref, kh_ref, v_ref))
        ar = jnp.concatenate([at, rt], axis=1)
        sb = jnp.where(keep, _bmm_nt(ar, bdiag(bt)), 0.0)
        sk = jnp.where(keep, _bmm_nt(ar, bdiag(kt)), 0.0).astype(BF16)
        m_rb = sb[:, CHUNK:].astype(BF16)
        yield
        a = sb[:, :CHUNK]
        tinv = jnp.where(eye, 1.0, 0.0) + a
        ab = a.astype(BF16)
        x = _bmm(ab, bdiag(ab))
        yield
        for _ in range(4):
            xb = x.astype(BF16)
            r = _bmm(jnp.concatenate([tinv.astype(BF16), xb], axis=1), bdiag(xb))
            tinv = tinv + r[:, :CHUNK]
            x = r[:, CHUNK:]
            yield
        tinv = (tinv + _bmm(tinv.astype(BF16), bdiag(x.astype(BF16)))).astype(BF16)
        bdv = bdiag(v)
        av = _bmm(sk[:, :CHUNK], bdv)
        yield
        p = _bmm(tinv, bdiag(at)).astype(BF16)
        u0 = _bmm(tinv, bdiag(av.astype(BF16)))
        yield
        q = (rt.astype(F32) + _bmm(m_rb, bdiag(p))).astype(BF16)
        y0 = _bmm(m_rb, bdiag(u0.astype(BF16))) + _bmm(sk[:, CHUNK:], bdv)
        parts = (jnp.concatenate([q, p], axis=1), jnp.concatenate([y0, u0], axis=1),
                 jnp.concatenate([bh, kh], axis=1), v)
        out += [tuple(z[n * nu:(n + 1) * nu] for z in parts) for n in range(len(cs))]
        yield

    wct = [wc_ref[r, 0] for r in range(rows)]
    state = [h_ref[...]]
    ys = []
    local = []

    def recur(c):
        qp, yu0, bk, v = local[c]
        h_hi, h_lo = _split2(state[0])
        yu = _bmm(qp, h_hi) + _bmm(qp, h_lo) + yu0
        ys.append(yu[:, :CHUNK])
        upd = _bmm_tn(bk, jnp.concatenate([yu[:, CHUNK:].astype(BF16), v], axis=1))
        wcol = jnp.stack([wct[r][g * gl:(g + 1) * gl, c:c + 1] for r, g in units])
        state[0] = jnp.where(same_head, upd, 0.0) + wcol * state[0]

    assert sorted(c for cs in RWKV_SCHEDULE for c in cs) == list(range(nc))
    done = 0
    for cs in RWKV_SCHEDULE:
        pending = list(range(done, len(local)))
        slots = {(k + 1) * RWKV_STAGES // (len(pending) + 1): c for k, c in enumerate(pending)}
        for n, _ in enumerate(chunk_local(list(cs), local), start=1):
            if n in slots:
                recur(slots[n])
                done += 1
    for c in range(done, nc):
        recur(c)
    h_ref[...] = state[0]

    y = jnp.concatenate([jnp.concatenate([yc[r * ng + g] for g in range(ng)], axis=-1)
                         for r in range(rows) for yc in ys], axis=0)
    e = e_ref[...]

    def head_mean(z):
        return _dot(z.astype(BF16), e) * (1.0 / HEAD_DIM)

    d = y - head_mean(y)
    yn = d * lax.rsqrt(head_mean(d * d) + LNX_EPS)
    flat = lambda ref: ref[...].reshape(rows * tm, WIDTH_C)
    out = (yn * lg_ref[...] + lb_ref[...] + flat(bonus_ref)) * flat(g_ref)
    o_ref[...] = out.astype(BF16).reshape(rows, tm, WIDTH_C)


def _rwkv_chunk(rt, at, bt, kt, bh, kh, v, wc, bonus, g, lnx_g, lnx_b, e):
    b, t, _ = rt.shape
    tm = RWKV_TILE
    rows = math.gcd(RWKV_ROWS, b)
    gl = RWKV_GROUP * HEAD_DIM
    const = lambda shape: pl.BlockSpec(shape, lambda i, j: (0,) * len(shape))
    tok = lambda: pl.BlockSpec((rows, tm, WIDTH_C), lambda i, j: (i, j, 0))
    return pl.pallas_call(
        _rwkv_chunk_body,
        grid=(b // rows, t // tm),
        in_specs=[tok(), tok(), tok(), tok(), tok(), tok(), tok(),
                  pl.BlockSpec((rows, 1, WIDTH_C, LANES), lambda i, j: (i, j, 0, 0)),
                  tok(), tok(), const(lnx_g.shape), const(lnx_b.shape), const(e.shape)],
        out_specs=tok(),
        out_shape=jax.ShapeDtypeStruct((b, t, WIDTH_C), BF16),
        scratch_shapes=[pltpu.VMEM((rows * (N_HEADS_C // RWKV_GROUP), gl, gl), F32)],
        compiler_params=_cparams(2),
        name="rwkv_chunk",
    )(rt, at, bt, kt, bh, kh, v, wc, bonus, g, lnx_g, lnx_b, e)


def _outproj_body(x_ref, ya_ref, yb_ref, yc_ref, wa_ref, wb_ref, wc_ref, o_ref):
    acc = _dot_tn(ya_ref[0], wa_ref[...])
    acc = acc + _dot_tn(yb_ref[0], wb_ref[...])
    acc = acc + _dot(yc_ref[0], wc_ref[...])
    o_ref[0] = x_ref[0] + acc


def _outproj(x, yaT, ybT, yc, wa, wb, wc):
    b, t, d = x.shape
    tm = TOK_TILE
    const = lambda shape: pl.BlockSpec(shape, lambda i, j: (0,) * len(shape))
    return pl.pallas_call(
        _outproj_body,
        grid=(b, t // tm),
        in_specs=[pl.BlockSpec((1, tm, d), lambda i, j: (i, j, 0)),
                  pl.BlockSpec((1, WIDTH_A, tm), lambda i, j: (i, 0, j)),
                  pl.BlockSpec((1, WIDTH_B, tm), lambda i, j: (i, 0, j)),
                  pl.BlockSpec((1, tm, WIDTH_C), lambda i, j: (i, j, 0)),
                  const(wa.shape), const(wb.shape), const(wc.shape)],
        out_specs=pl.BlockSpec((1, tm, d), lambda i, j: (i, j, 0)),
        out_shape=jax.ShapeDtypeStruct((b, t, d), F32),
        compiler_params=_cparams(2),
        name="outproj",
    )(x, yaT, ybT, yc, wa, wb, wc)


def _ffn_body(x_ref, g_ref, wg_ref, wv_ref, cw_ref, cb_ref, wd_ref, o_ref, gbuf):
    tm = x_ref.shape[1]
    t = pl.program_id(1)

    @pl.when(t == 0)
    def _():
        gbuf[0:8, :] = jnp.zeros((8, D_FF), F32)

    x = x_ref[0]
    ms = jnp.mean(x * x, axis=-1, keepdims=True)
    h = (x * lax.rsqrt(ms + RMS_EPS) * g_ref[...]).astype(BF16)
    acc = x
    for lo, hi in zip(FF_SPLITS[:-1], FF_SPLITS[1:]):
        gate = _dot(h, wg_ref[:, lo:hi])
        val = _dot(h, wv_ref[:, lo:hi])
        gbuf[8:8 + tm, lo:hi] = gate
        g1 = gbuf[7:7 + tm, lo:hi]
        g2 = gbuf[6:6 + tm, lo:hi]
        gbuf[0:8, lo:hi] = gate[tm - 8:tm, :]
        conv = cb_ref[:, lo:hi] + g2 * cw_ref[0:1, lo:hi] + g1 * cw_ref[1:2, lo:hi] + gate * cw_ref[2:3, lo:hi]
        act = (conv * _sigmoid(conv) * val).astype(BF16)
        acc = acc + _dot(act, wd_ref[lo:hi, :])
    o_ref[0] = acc


def _ffn(x, g, wg, wv, cw, cb, wd):
    b, t, d = x.shape
    tm = TOK_TILE
    const = lambda shape: pl.BlockSpec(shape, lambda i, j: (0,) * len(shape),
                                       pipeline_mode=pl.Buffered(1))
    return pl.pallas_call(
        _ffn_body,
        grid=(b, t // tm),
        in_specs=[pl.BlockSpec((1, tm, d), lambda i, j: (i, j, 0)),
                  const(g.shape), const(wg.shape), const(wv.shape), const(cw.shape), const(cb.shape),
                  const(wd.shape)],
        out_specs=pl.BlockSpec((1, tm, d), lambda i, j: (i, j, 0)),
        out_shape=jax.ShapeDtypeStruct((b, t, d), F32),
        scratch_shapes=[pltpu.VMEM((tm + 8, D_FF), F32)],
        compiler_params=_cparams(2),
        name="ffn",
    )(x, g, wg, wv, cw, cb, wd)


def _pad_heads_cols(w, nh):
    d = w.shape[0]
    w = w.reshape(d, nh, HEAD_DIM)
    return jnp.pad(w, ((0, 0), (0, 0), (0, LANES - HEAD_DIM))).reshape(d, nh * LANES)


def _qk_bound(gq, gk):
    scale = HEAD_DIM ** -0.5
    return 1.01 * HEAD_DIM * scale * jnp.max(jnp.abs(gq)) * jnp.max(jnp.abs(gk)) + 0.05


def _layer_params(l, w_in, q_norm_a, k_norm_a, rel_bias, q_norm_b, k_norm_b, forget_bias):
    w = w_in[l]
    scale = HEAD_DIM ** -0.5
    a0, b0 = 0, 3 * WIDTH_A
    c0 = b0 + 3 * WIDTH_B + N_HEADS_B
    qa, ka, va = (w[:, a0 + i * WIDTH_A:a0 + (i + 1) * WIDTH_A] for i in range(3))
    qb, kb, vb = (w[:, b0 + i * WIDTH_B:b0 + (i + 1) * WIDTH_B] for i in range(3))
    fg = w[:, b0 + 3 * WIDTH_B:c0]
    wc = w[:, c0:]
    wn = jnp.concatenate([_pad_heads_cols(ka, N_HEADS_A), _pad_heads_cols(kb, N_HEADS_B),
                          jnp.pad(fg, ((0, 0), (0, LANES - N_HEADS_B))), wc], axis=1).astype(BF16)
    pad_v = lambda v, nh: jnp.pad(v.reshape(-1, nh, HEAD_DIM),
                                  ((0, 0), (0, 0), (0, VB_ROWS - HEAD_DIM))).reshape(-1, nh * VB_ROWS)
    wt = jnp.concatenate([qa, pad_v(va, N_HEADS_A), _pad_heads_cols(qb, N_HEADS_B), pad_v(vb, N_HEADS_B),
                          jnp.pad(fg, ((0, 0), (0, 16 - N_HEADS_B)))], axis=1).T.astype(BF16)
    pad = LANES - HEAD_DIM
    gka = jnp.pad(k_norm_a[l], (0, pad)).reshape(1, LANES)
    gkb = jnp.pad(k_norm_b[l], (0, pad)).reshape(1, LANES)
    gqa = (q_norm_a[l] * (scale * LOG2E)).reshape(HEAD_DIM, 1)
    gqb = jnp.pad(q_norm_b[l] * (scale * LOG2E), (0, pad)).reshape(LANES, 1)
    fbn = jnp.pad(forget_bias[l], (0, LANES - N_HEADS_B)).reshape(1, LANES)
    fbt = jnp.pad(forget_bias[l], (0, 16 - N_HEADS_B)).reshape(16, 1)
    shift_b = (LOG2E * _qk_bound(q_norm_b[l], k_norm_b[l])).reshape(1, 1)
    bound_a = _qk_bound(q_norm_a[l], k_norm_a[l])
    tab = rel_bias[l]
    shift_a = LOG2E * (bound_a + jnp.max(tab, axis=1))
    depth_a = jnp.max(shift_a + LOG2E * (bound_a - tab[:, MAX_REL]))
    return wn, wt, gka, gkb, gqa, gqb, fbn, fbt, shift_b, shift_a, depth_a


def _selectors():
    import numpy as np
    selk = np.zeros((3, LANES, N_HEADS_B * LANES), np.float32)
    selq = np.zeros((3, N_HEADS_B * LANES, 16), np.float32)
    for p in range(3):
        for hh in range(N_HEADS_B):
            selk[p, hh, hh * LANES + HEAD_DIM + 3 + p] = -1.0
            selq[p, hh * LANES + HEAD_DIM + p, hh] = 1.0
    e = np.kron(np.eye(N_HEADS_C, dtype=np.float32), np.ones((HEAD_DIM, HEAD_DIM), np.float32))
    return jnp.asarray(selk, BF16), jnp.asarray(selq, BF16), jnp.asarray(e, BF16)


def kernel(x, mix_norm_g, w_in, q_norm_a, k_norm_a, rel_bias, q_norm_b, k_norm_b, forget_bias, shift_mu, w0, w2,
           a0, a2, g2, k_k, k_a, r_k, lnx_g, lnx_b, w_out, ffn_norm_g, w_up, conv_w, conv_b, w_down):
    depth = w_in.shape[0]
    b, t, d = x.shape
    selk, selq, e = _selectors()
    row = lambda v: v.reshape(1, -1)
    for l in range(depth):
        wn, wt, gka, gkb, gqa, gqb, fbn, fbt, shift_b, shift_a, depth_a = _layer_params(
            l, w_in, q_norm_a, k_norm_a, rel_bias, q_norm_b, k_norm_b, forget_bias)
        qaT, vaT, kA, qbT, vbT, kB, pc = _inproj(x, row(mix_norm_g[l]), wn, wt, gka, gkb, gqa, gqb,
                                                 fbn, fbt, selk, selq, shift_b)
        bias = _relbias(rel_bias[l], shift_a)
        yaT = lax.cond(depth_a <= 2 * FAST_MAX_SHIFT,
                       functools.partial(_attn_a, online=False), functools.partial(_attn_a, online=True),
                       qaT, kA, vaT, bias)
        ybT = lax.cond(shift_b[0, 0] <= FAST_MAX_SHIFT,
                       functools.partial(_fox, online=False), functools.partial(_fox, online=True),
                       qbT, kB, vbT)
        rt, at, bt, kt, bh, kh, v, wcum, bonus, g = _rwkv_prep(
            pc, row(shift_mu[l]), row(w0[l]), w2[l].astype(BF16), row(a0[l]), a2[l].astype(BF16),
            g2[l].astype(BF16), row(k_k[l]), row(k_a[l]), row(r_k[l]), e)
        yc = _rwkv_chunk(rt, at, bt, kt, bh, kh, v, wcum, bonus, g, row(lnx_g[l]), row(lnx_b[l]), e)
        wo = w_out[l].astype(BF16)
        x = _outproj(x, yaT.reshape(b, WIDTH_A, t), ybT.reshape(b, WIDTH_B, t), yc,
                     wo[:WIDTH_A], wo[WIDTH_A:WIDTH_A + WIDTH_B], wo[WIDTH_A + WIDTH_B:])
        wu = w_up[l].astype(BF16)
        x = _ffn(x, row(ffn_norm_g[l]), wu[:, :D_FF], wu[:, D_FF:], conv_w[l], row(conv_b[l]),
                 w_down[l].astype(BF16))
    return x
```

```python
import functools
import math

import jax
import jax.numpy as jnp
from jax import lax
from jax.experimental import pallas as pl
from jax.experimental.pallas import tpu as pltpu

F32 = jnp.float32
BF16 = jnp.bfloat16

D_MODEL = 1024
HEAD_DIM = 64
CHUNK = 64
LEFT_CHUNKS = 8
MAX_REL = 128
N_HEADS_A = 4
N_HEADS_B = 4
N_HEADS_C = 8
WIDTH_A = N_HEADS_A * HEAD_DIM
WIDTH_B = N_HEADS_B * HEAD_DIM
WIDTH_C = N_HEADS_C * HEAD_DIM
DECAY_LORA = 64
AAA_LORA = 64
GATE_LORA = 128
COLS_C = 3 * WIDTH_C + DECAY_LORA + AAA_LORA + GATE_LORA
D_FF = 2816
RMS_EPS = 1e-6
LNX_EPS = 64e-5
NEG_INF = -1e30

LANES = 128
TOK_TILE = 512
ATT_A_TILE = 256
ATT_A_WIN = 3 * ATT_A_TILE
FOX_TILE = 512
FOX_GROUP = 8
RWKV_GROUP = 4
RWKV_TILE = 256
RWKV_ROWS = 4
RWKV_SCHEDULE = ((0, 1), (2, 3))
RWKV_STAGES = 9
FF_SPLITS = (0, 768, 1536, 2304, 2816)
VMEM_LIMIT = 56 * 1024 * 1024

PN_KA, PN_KB, PN_F, PN_C = 0, 512, 1024, 1152
PN_COLS = PN_C + COLS_C
VB_ROWS = 80
PT_QA = 0
PT_VA = PT_QA + WIDTH_A
PT_QB = PT_VA + N_HEADS_A * VB_ROWS
PT_VB = PT_QB + N_HEADS_B * LANES
PT_F = PT_VB + N_HEADS_B * VB_ROWS
PT_ROWS = PT_F + 16
LOG2E = 1.4426950408889634
FAST_MAX_SHIFT = 40.0


def _cparams(n_axes):
    return pltpu.CompilerParams(dimension_semantics=("arbitrary",) * n_axes,
                                vmem_limit_bytes=VMEM_LIMIT)


def _split3(x):
    hi = x.astype(BF16)
    r1 = x - hi.astype(F32)
    mid = r1.astype(BF16)
    lo = (r1 - mid.astype(F32)).astype(BF16)
    return hi, mid, lo


def _split2(x):
    hi = x.astype(BF16)
    lo = (x - hi.astype(F32)).astype(BF16)
    return hi, lo


def _dot(a, b):
    return jnp.dot(a, b, preferred_element_type=F32)


def _dot_nt(a, b):
    return lax.dot_general(a, b, (((1,), (1,)), ((), ())), preferred_element_type=F32)


def _dot_tn(a, b):
    return lax.dot_general(a, b, (((0,), (0,)), ((), ())), preferred_element_type=F32)


def _log_sigmoid(x):
    return jnp.minimum(x, 0.0) - jnp.log(1.0 + jnp.exp(-jnp.abs(x)))


def _softplus(x):
    return jnp.maximum(x, 0.0) + jnp.log(1.0 + jnp.exp(-jnp.abs(x)))


def _sigmoid(x):
    return 1.0 / (1.0 + jnp.exp(-x))


def _inproj_body(x_ref, g_ref, wn_ref, wt_ref, gka_ref, gkb_ref, gqa_ref, gqb_ref, fbn_ref, fbt_ref,
                 selk_ref, selq_ref, shift_ref,
                 qaT_ref, vaT_ref, kA_ref, qbT_ref, vbT_ref, kB_ref, pc_ref, pclo_ref,
                 carry_n, carry_t):
    tm = x_ref.shape[1]
    t = pl.program_id(1)

    @pl.when(t == 0)
    def _():
        carry_n[...] = jnp.zeros_like(carry_n)
        carry_t[...] = jnp.zeros_like(carry_t)

    x = x_ref[0]
    ms = jnp.mean(x * x, axis=-1, keepdims=True)
    h = (x * lax.rsqrt(ms + RMS_EPS) * g_ref[...]).astype(BF16)
    pn = _dot(h, wn_ref[...])
    pt = _dot_nt(wt_ref[...], h)

    qa = pt[PT_QA:PT_QA + WIDTH_A].reshape(N_HEADS_A, HEAD_DIM, tm)
    msq = jnp.mean(qa * qa, axis=1, keepdims=True)
    qaT_ref[0] = (qa * lax.rsqrt(msq + RMS_EPS) * gqa_ref[...][None]).astype(BF16)
    row_v = lax.broadcasted_iota(jnp.int32, (VB_ROWS, 1), 0)
    ones_v = jnp.where(row_v == HEAD_DIM, 1.0, 0.0)
    va = pt[PT_VA:PT_VA + N_HEADS_A * VB_ROWS].reshape(N_HEADS_A, VB_ROWS, tm)
    vaT_ref[0] = (va + ones_v[None]).astype(BF16)
    for hh in range(N_HEADS_A):
        k = pn[:, PN_KA + LANES * hh:PN_KA + LANES * (hh + 1)]
        msk = jnp.sum(k * k, axis=-1, keepdims=True) * (1.0 / HEAD_DIM)
        kn = k * lax.rsqrt(msk + RMS_EPS) * gka_ref[...]
        kA_ref[0, hh] = kn[:, :HEAD_DIM].astype(BF16)

    lane_n = lax.broadcasted_iota(jnp.int32, (1, LANES), 1)
    lf_n = jnp.where(lane_n < N_HEADS_B, LOG2E * _log_sigmoid(pn[:, PN_F:PN_F + LANES] + fbn_ref[...]), 0.0)
    row_t = lax.broadcasted_iota(jnp.int32, (16, 1), 0)
    lf_t = jnp.where(row_t < N_HEADS_B, LOG2E * _log_sigmoid(pt[PT_F:PT_F + 16] + fbt_ref[...]), 0.0)
    ri = lax.broadcasted_iota(jnp.int32, (tm, tm), 0)
    ci = lax.broadcasted_iota(jnp.int32, (tm, tm), 1)
    low = (ci <= ri).astype(BF16)
    upp = (ri <= ci).astype(BF16)
    hn, mn, ln = _split3(lf_n)
    c_n = _dot(low, hn) + _dot(low, mn) + _dot(low, ln) + carry_n[0:1, :]
    ht, mt, lt = _split3(lf_t)
    c_t = _dot(ht, upp) + _dot(mt, upp) + _dot(lt, upp) + carry_t[:, 0:1]
    carry_n[...] = carry_n[...] + jnp.sum(lf_n, axis=0, keepdims=True)
    carry_t[...] = carry_t[...] + jnp.sum(lf_t, axis=1, keepdims=True)

    chn, cmn, cln = _split3(c_n)
    kaug = _dot(chn, selk_ref[0]) + _dot(cmn, selk_ref[1]) + _dot(cln, selk_ref[2])
    lane_k = lax.broadcasted_iota(jnp.int32, (1, LANES), 1)
    ones_k = jnp.where((lane_k >= HEAD_DIM) & (lane_k < HEAD_DIM + 3), 1.0, 0.0)
    for hh in range(N_HEADS_B):
        k = pn[:, PN_KB + LANES * hh:PN_KB + LANES * (hh + 1)]
        msk = jnp.sum(k * k, axis=-1, keepdims=True) * (1.0 / HEAD_DIM)
        kn = k * lax.rsqrt(msk + RMS_EPS) * gkb_ref[...]
        kB_ref[0, hh] = (kn + kaug[:, LANES * hh:LANES * (hh + 1)] + ones_k).astype(BF16)

    cht, cmt, clt = _split3(c_t - shift_ref[...])
    qaug = _dot(selq_ref[0], cht) + _dot(selq_ref[1], cmt) + _dot(selq_ref[2], clt)
    qb = pt[PT_QB:PT_QB + N_HEADS_B * LANES].reshape(N_HEADS_B, LANES, tm)
    msq = jnp.sum(qb * qb, axis=1, keepdims=True) * (1.0 / HEAD_DIM)
    row_q = lax.broadcasted_iota(jnp.int32, (LANES, 1), 0)
    ones_q = jnp.where((row_q >= HEAD_DIM + 3) & (row_q < HEAD_DIM + 6), 1.0, 0.0)
    qn = qb * lax.rsqrt(msq + RMS_EPS) * gqb_ref[...][None]
    qbT_ref[0] = (qn + qaug.reshape(N_HEADS_B, LANES, tm) + ones_q[None]).astype(BF16)
    vb = pt[PT_VB:PT_VB + N_HEADS_B * VB_ROWS].reshape(N_HEADS_B, VB_ROWS, tm)
    vbT_ref[0, :, 0] = (vb + ones_v[None]).astype(BF16)

    pc_ref[0] = pn[:, PN_C:PN_C + 3 * WIDTH_C].astype(BF16)
    pclo_ref[0] = pn[:, PN_C + 3 * WIDTH_C:PN_C + COLS_C]


def _inproj(x, g, wn, wt, gka, gkb, gqa, gqb, fbn, fbt, selk, selq, shift):
    b, t, d = x.shape
    tm = TOK_TILE
    nt = t // tm
    const = lambda shape: pl.BlockSpec(shape, lambda i, j: (0,) * len(shape))
    out_shape = (
        jax.ShapeDtypeStruct((b, N_HEADS_A, HEAD_DIM, t), BF16),
        jax.ShapeDtypeStruct((b, N_HEADS_A, VB_ROWS, t), BF16),
        jax.ShapeDtypeStruct((b, N_HEADS_A, t, HEAD_DIM), BF16),
        jax.ShapeDtypeStruct((b, N_HEADS_B, LANES, t), BF16),
        jax.ShapeDtypeStruct((b, N_HEADS_B, nt, VB_ROWS, tm), BF16),
        jax.ShapeDtypeStruct((b, N_HEADS_B, t, LANES), BF16),
        jax.ShapeDtypeStruct((b, t, 3 * WIDTH_C), BF16),
        jax.ShapeDtypeStruct((b, t, COLS_C - 3 * WIDTH_C), F32),
    )
    out_specs = (
        pl.BlockSpec((1, N_HEADS_A, HEAD_DIM, tm), lambda i, j: (i, 0, 0, j)),
        pl.BlockSpec((1, N_HEADS_A, VB_ROWS, tm), lambda i, j: (i, 0, 0, j)),
        pl.BlockSpec((1, N_HEADS_A, tm, HEAD_DIM), lambda i, j: (i, 0, j, 0)),
        pl.BlockSpec((1, N_HEADS_B, LANES, tm), lambda i, j: (i, 0, 0, j)),
        pl.BlockSpec((1, N_HEADS_B, 1, VB_ROWS, tm), lambda i, j: (i, 0, j, 0, 0)),
        pl.BlockSpec((1, N_HEADS_B, tm, LANES), lambda i, j: (i, 0, j, 0)),
        pl.BlockSpec((1, tm, 3 * WIDTH_C), lambda i, j: (i, j, 0)),
        pl.BlockSpec((1, tm, COLS_C - 3 * WIDTH_C), lambda i, j: (i, j, 0)),
    )
    in_specs = [
        pl.BlockSpec((1, tm, d), lambda i, j: (i, j, 0)),
        const((1, d)), const(wn.shape), const(wt.shape),
        const(gka.shape), const(gkb.shape), const(gqa.shape), const(gqb.shape),
        const(fbn.shape), const(fbt.shape), const(selk.shape), const(selq.shape), const(shift.shape),
    ]
    return pl.pallas_call(
        _inproj_body,
        grid=(b, nt),
        in_specs=in_specs,
        out_specs=out_specs,
        out_shape=out_shape,
        scratch_shapes=[pltpu.VMEM((8, LANES), F32), pltpu.VMEM((16, LANES), F32)],
        compiler_params=_cparams(2),
        name="inproj",
    )(x, g, wn, wt, gka, gkb, gqa, gqb, fbn, fbt, selk, selq, shift)


def _relbias_body(tab_ref, shift_ref, o_ref):
    hh = pl.program_id(0)
    kj = lax.broadcasted_iota(jnp.int32, (ATT_A_WIN, ATT_A_TILE), 0)
    qi = lax.broadcasted_iota(jnp.int32, (ATT_A_WIN, ATT_A_TILE), 1)
    rel = jnp.clip(kj - LEFT_CHUNKS * CHUNK - qi, -MAX_REL, MAX_REL) + MAX_REL

    def body(r, acc):
        return jnp.where(rel == r, tab_ref[hh, r], acc)

    bias = lax.fori_loop(0, 2 * MAX_REL + 1, body, jnp.zeros((ATT_A_WIN, ATT_A_TILE), F32))
    kc = kj // CHUNK
    qc = qi // CHUNK
    band = (kc >= qc) & (kc <= qc + LEFT_CHUNKS)
    o_ref[0] = jnp.where(band, LOG2E * bias - shift_ref[hh], NEG_INF)


def _relbias(tab, shift):
    return pl.pallas_call(
        _relbias_body,
        grid=(N_HEADS_A,),
        in_specs=[pl.BlockSpec(memory_space=pltpu.SMEM), pl.BlockSpec(memory_space=pltpu.SMEM)],
        out_specs=pl.BlockSpec((1, ATT_A_WIN, ATT_A_TILE), lambda i: (i, 0, 0)),
        out_shape=jax.ShapeDtypeStruct((N_HEADS_A, ATT_A_WIN, ATT_A_TILE), F32),
        compiler_params=_cparams(1),
        name="relbias",
    )(tab, shift)


def _attn_a_body(q_ref, k0_ref, k1_ref, k2_ref, v0_ref, v1_ref, v2_ref, bias_ref, o_ref, *, online):
    i = pl.program_id(1)
    tq = ATT_A_TILE
    kj = lax.broadcasted_iota(jnp.int32, (tq, 1), 0)

    def run(mask_padding):
        scores = []
        for hh in range(N_HEADS_A):
            q = q_ref[0, hh]
            ss = []
            for d, k_ref in enumerate((k0_ref, k1_ref, k2_ref)):
                s = _dot(k_ref[0, hh], q) + bias_ref[hh, d * tq:(d + 1) * tq, :]
                if mask_padding:
                    s = jnp.where(kj + (i - 2 + d) * tq >= 0, s, NEG_INF)
                ss.append(s)
            scores.append(ss)
        for hh, ss in enumerate(scores):
            if online:
                m = functools.reduce(jnp.maximum, [jnp.max(s, axis=0, keepdims=True) for s in ss])
                ss = [s - m for s in ss]
            acc = jnp.zeros((VB_ROWS, tq), F32)
            for s, v_ref in zip(ss, (v0_ref, v1_ref, v2_ref)):
                acc = acc + _dot(v_ref[0, hh], jnp.exp2(s).astype(BF16))
            o_ref[0, hh] = (acc[0:HEAD_DIM] / acc[HEAD_DIM:HEAD_DIM + 1]).astype(BF16)

    pl.when(i < 2)(functools.partial(run, True))
    pl.when(i >= 2)(functools.partial(run, False))


def _attn_a(qaT, kA, vaT, bias, online):
    b, nh, _, t = qaT.shape
    tq = ATT_A_TILE
    kspec = lambda d: pl.BlockSpec((1, nh, tq, HEAD_DIM),
                                   lambda bb, i: (bb, 0, jnp.maximum(i - 2 + d, 0), 0))
    vspec = lambda d: pl.BlockSpec((1, nh, VB_ROWS, tq),
                                   lambda bb, i: (bb, 0, 0, jnp.maximum(i - 2 + d, 0)))
    return pl.pallas_call(
        functools.partial(_attn_a_body, online=online),
        grid=(b, t // tq),
        in_specs=[pl.BlockSpec((1, nh, HEAD_DIM, tq), lambda bb, i: (bb, 0, 0, i)),
                  kspec(0), kspec(1), kspec(2), vspec(0), vspec(1), vspec(2),
                  pl.BlockSpec((nh, ATT_A_WIN, tq), lambda bb, i: (0, 0, 0))],
        out_specs=pl.BlockSpec((1, nh, HEAD_DIM, tq), lambda bb, i: (bb, 0, 0, i)),
        out_shape=jax.ShapeDtypeStruct((b, nh, HEAD_DIM, t), BF16),
        compiler_params=_cparams(2),
        name="attn_a_online" if online else "attn_a",
    )(qaT, kA, kA, kA, vaT, vaT, vaT, bias)


def _fox_body(q_ref, k_ref, v_ref, o_ref, *, online):
    i = pl.program_id(2)
    tq = FOX_TILE
    q = q_ref[0, 0]

    def scores(j):
        ks = k_ref[0, 0, pl.ds(pl.multiple_of(j * tq, tq), tq), :]
        return _dot(ks, q)

    def accumulate(j, s, m, acc, diagonal):
        vs = v_ref[0, 0, j]
        if diagonal:
            kj = lax.broadcasted_iota(jnp.int32, (tq, tq), 0)
            qi = lax.broadcasted_iota(jnp.int32, (tq, tq), 1)
            causal = kj <= qi
        if online:
            if diagonal:
                s = jnp.where(causal, s, NEG_INF)
            m_new = jnp.maximum(m, jnp.max(s, axis=0, keepdims=True))
            acc = jnp.exp2(m - m_new) * acc
            p = jnp.exp2(s - m_new)
            m = m_new
        else:
            p = jnp.exp2(s)
            if diagonal:
                p = jnp.where(causal, p, 0.0)
        return m, acc + _dot(vs, p.astype(BF16))

    def tiles(js, carry, last_diagonal):
        ss = [scores(j) for j in js]
        m, acc = carry
        for n, (j, s) in enumerate(zip(js, ss)):
            m, acc = accumulate(j, s, m, acc, last_diagonal and n == len(js) - 1)
        return m, acc

    init = (jnp.full((1, tq), NEG_INF, F32), jnp.zeros((VB_ROWS, tq), F32))
    if online:
        carry = lax.fori_loop(0, i, lambda j, c: tiles([j], c, False), init)
        _, acc = tiles([i], carry, True)
    else:
        g = FOX_GROUP
        carry = lax.fori_loop(0, i // g, lambda jj, c: tiles([g * jj + n for n in range(g)], c, False), init)
        tails = [functools.partial(lambda c, r: tiles([i - r + n for n in range(r + 1)], c, True), r=r)
                 for r in range(g)]
        _, acc = lax.switch(i % g, tails, carry)
    o_ref[0, 0] = (acc[0:HEAD_DIM] / acc[HEAD_DIM:HEAD_DIM + 1]).astype(BF16)


def _fox(qbT, kB, vbT, online):
    b, nh, _, t = qbT.shape
    tq = FOX_TILE
    nk = t // tq
    return pl.pallas_call(
        functools.partial(_fox_body, online=online),
        grid=(b, nh, nk),
        in_specs=[pl.BlockSpec((1, 1, LANES, tq), lambda bb, hh, i: (bb, hh, 0, i)),
                  pl.BlockSpec((1, 1, t, LANES), lambda bb, hh, i: (bb, hh, 0, 0)),
                  pl.BlockSpec((1, 1, nk, VB_ROWS, tq), lambda bb, hh, i: (bb, hh, 0, 0, 0))],
        out_specs=pl.BlockSpec((1, 1, HEAD_DIM, tq), lambda bb, hh, i: (bb, hh, 0, i)),
        out_shape=jax.ShapeDtypeStruct((b, nh, HEAD_DIM, t), BF16),
        compiler_params=_cparams(3),
        name="fox_online" if online else "fox",
    )(qbT, kB, vbT)


def _rwkv_prep_body(pc_ref, pclo_ref, mu_ref, w0_ref, w2_ref, a0_ref, a2_ref, g2_ref, kk_ref, ka_ref, rk_ref, e_ref,
                    rt_ref, at_ref, bt_ref, kt_ref, bh_ref, kh_ref, v_ref, wc_ref, bonus_ref, g_ref,
                    sbuf):
    tm = pc_ref.shape[1]
    t = pl.program_id(1)

    @pl.when(t == 0)
    def _():
        sbuf[0:8, :] = jnp.zeros((8, COLS_C), F32)

    u = jnp.concatenate([pc_ref[0].astype(F32), pclo_ref[0]], axis=1)
    sbuf[8:8 + tm, :] = u
    prev = sbuf[7:7 + tm, :]
    sbuf[0:8, :] = u[tm - 8:tm, :]
    u = u + (prev - u) * mu_ref[...]

    c = WIDTH_C
    r = u[:, 0:c]
    k = u[:, c:2 * c]
    v = u[:, 2 * c:3 * c]
    w_lo = u[:, 3 * c:3 * c + DECAY_LORA]
    a_lo = u[:, 3 * c + DECAY_LORA:3 * c + DECAY_LORA + AAA_LORA]
    g_lo = u[:, 3 * c + DECAY_LORA + AAA_LORA:COLS_C]

    w = w0_ref[...] + _dot(jnp.tanh(w_lo).astype(BF16), w2_ref[...])
    w = -_softplus(-w) - 0.5
    ld = -jnp.exp(w)
    a = _sigmoid(a0_ref[...] + _dot(a_lo.astype(BF16), a2_ref[...]))
    g = _dot(_sigmoid(g_lo).astype(BF16), g2_ref[...])

    e = e_ref[...]
    kk = k * kk_ref[...]
    s_hi, s_lo = _split2(kk * kk)
    nrm2 = _dot(s_hi, e) + _dot(s_lo, e)
    kkn = kk * lax.rsqrt(jnp.maximum(nrm2, 1e-24))
    k2 = k * (1.0 + (a - 1.0) * ka_ref[...])
    kka = kkn * a
    b_hi, b_lo = _split2(r * k2 * rk_ref[...])
    bonus = (_dot(b_hi, e) + _dot(b_lo, e)) * v

    ri = lax.broadcasted_iota(jnp.int32, (2 * CHUNK, CHUNK), 0)
    ci = lax.broadcasted_iota(jnp.int32, (2 * CHUNK, CHUNK), 1)
    tri = ((ci <= ri) | (ri >= CHUNK)).astype(BF16)
    ld_hi, ld_lo = _split2(ld)
    for cc in range(tm // CHUNK):
        sl = slice(cc * CHUNK, (cc + 1) * CHUNK)
        cum = _dot(tri, ld_hi[sl]) + _dot(tri, ld_lo[sl])
        lc = cum[0:CHUNK]
        tot = cum[CHUNK:2 * CHUNK]
        e_neg = jnp.exp(-lc)
        e_rem = jnp.exp(tot - lc)
        rt_ref[0, sl, :] = (r[sl] * jnp.exp(lc)).astype(BF16)
        at_ref[0, sl, :] = (-kkn[sl] * jnp.exp(lc - ld[sl])).astype(BF16)
        bt_ref[0, sl, :] = (kka[sl] * e_neg).astype(BF16)
        kt_ref[0, sl, :] = (k2[sl] * e_neg).astype(BF16)
        bh_ref[0, sl, :] = (kka[sl] * e_rem).astype(BF16)
        kh_ref[0, sl, :] = (k2[sl] * e_rem).astype(BF16)
    seg = (lax.broadcasted_iota(jnp.int32, (16, tm), 1) // CHUNK
           == lax.broadcasted_iota(jnp.int32, (16, tm), 0)).astype(BF16)
    tots3 = jnp.concatenate(_split3(_dot(seg, ld_hi) + _dot(seg, ld_lo)), axis=0)
    per_blk = RWKV_TILE // CHUNK
    for blk in range(tm // RWKV_TILE):
        put = (lax.broadcasted_iota(jnp.int32, (48, LANES), 0) % 16
               == lax.broadcasted_iota(jnp.int32, (48, LANES), 1) + blk * per_blk).astype(BF16)
        wc_ref[0, blk] = jnp.exp(_dot_tn(tots3, put))
    v_ref[0] = v.astype(BF16)
    bonus_ref[0] = bonus.astype(BF16)
    g_ref[0] = g.astype(BF16)


def _rwkv_prep(pc, pclo, mu, w0, w2, a0, a2, g2, k_k, k_a, r_k, e):
    b, t, _ = pc.shape
    tm = TOK_TILE
    const = lambda shape: pl.BlockSpec(shape, lambda i, j: (0,) * len(shape))
    tok = lambda: pl.BlockSpec((1, tm, WIDTH_C), lambda i, j: (i, j, 0))
    bf = jax.ShapeDtypeStruct((b, t, WIDTH_C), BF16)
    out_shape = (bf, bf, bf, bf, bf, bf, bf,
                 jax.ShapeDtypeStruct((b, t // RWKV_TILE, WIDTH_C, LANES), F32), bf, bf)
    out_specs = (tok(), tok(), tok(), tok(), tok(), tok(), tok(),
                 pl.BlockSpec((1, tm // RWKV_TILE, WIDTH_C, LANES), lambda i, j: (i, j, 0, 0)), tok(), tok())
    return pl.pallas_call(
        _rwkv_prep_body,
        grid=(b, t // tm),
        in_specs=[pl.BlockSpec((1, tm, 3 * WIDTH_C), lambda i, j: (i, j, 0)),
                  pl.BlockSpec((1, tm, COLS_C - 3 * WIDTH_C), lambda i, j: (i, j, 0)),
                  const(mu.shape), const(w0.shape), const(w2.shape), const(a0.shape), const(a2.shape),
                  const(g2.shape), const(k_k.shape), const(k_a.shape), const(r_k.shape), const(e.shape)],
        out_specs=out_specs,
        out_shape=out_shape,
        scratch_shapes=[pltpu.VMEM((tm + 8, COLS_C), F32)],
        compiler_params=_cparams(2),
        name="rwkv_prep",
    )(pc, pclo, mu, w0, w2, a0, a2, g2, k_k, k_a, r_k, e)


def _bmm(a, b):
    return lax.dot_general(a, b, (((2,), (1,)), ((0,), (0,))), preferred_element_type=F32)


def _bmm_nt(a, b):
    return lax.dot_general(a, b, (((2,), (2,)), ((0,), (0,))), preferred_element_type=F32)


def _bmm_tn(a, b):
    return lax.dot_general(a, b, (((1,), (1,)), ((0,), (0,))), preferred_element_type=F32)


def _rwkv_chunk_body(rt_ref, at_ref, bt_ref, kt_ref, bh_ref, kh_ref, v_ref, wc_ref, bonus_ref, g_ref,
                     lg_ref, lb_ref, e_ref, o_ref, h_ref):
    rows, tm = rt_ref.shape[0], rt_ref.shape[1]
    nc = tm // CHUNK
    ng = N_HEADS_C // RWKV_GROUP
    gl = RWKV_GROUP * HEAD_DIM
    units = [(r, g) for r in range(rows) for g in range(ng)]
    nu = len(units)
    t = pl.program_id(1)

    @pl.when(t == 0)
    def _():
        h_ref[...] = jnp.zeros_like(h_ref)

    lane_head = lax.broadcasted_iota(jnp.int32, (1, CHUNK, gl), 2) // HEAD_DIM
    ri = lax.broadcasted_iota(jnp.int32, (1, 2 * CHUNK, gl), 1)
    ci = lax.broadcasted_iota(jnp.int32, (1, 2 * CHUNK, gl), 2) & (CHUNK - 1)
    rr = ri & (CHUNK - 1)
    keep = (rr > ci) | ((ri >= CHUNK) & (rr == ci))
    eye = (lax.broadcasted_iota(jnp.int32, (1, CHUNK, gl), 1)
           == (lax.broadcasted_iota(jnp.int32, (1, CHUNK, gl), 2) & (CHUNK - 1)))
    same_head = (lax.broadcasted_iota(jnp.int32, (1, gl, gl), 1) // HEAD_DIM
                 == lax.broadcasted_iota(jnp.int32, (1, gl, gl), 2) // HEAD_DIM)

    def bdiag(x):
        return jnp.concatenate([jnp.where(lane_head == hh, x, jnp.zeros_like(x)) for hh in range(RWKV_GROUP)],
                               axis=1)

    def chunk_local(cs, out):
        tile = lambda ref: jnp.stack([ref[r, c * CHUNK:(c + 1) * CHUNK, g * gl:(g + 1) * gl]
                                      for c in cs for r, g in units])
        rt, at, bt, kt, bh, kh, v = (tile(r) for r in (rt_ref, at_ref, bt_ref, kt_ref, bh_ref, kh_ref, v_ref))
        ar = jnp.concatenate([at, rt], axis=1)
        sb = jnp.where(keep, _bmm_nt(ar, bdiag(bt)), 0.0)
        sk = jnp.where(keep, _bmm_nt(ar, bdiag(kt)), 0.0).astype(BF16)
        m_rb = sb[:, CHUNK:].astype(BF16)
        yield
        a = sb[:, :CHUNK]
        tinv = jnp.where(eye, 1.0, 0.0) + a
        ab = a.astype(BF16)
        x = _bmm(ab, bdiag(ab))
        yield
        for _ in range(4):
            xb = x.astype(BF16)
            r = _bmm(jnp.concatenate([tinv.astype(BF16), xb], axis=1), bdiag(xb))
            tinv = tinv + r[:, :CHUNK]
            x = r[:, CHUNK:]
            yield
        tinv = (tinv + _bmm(tinv.astype(BF16), bdiag(x.astype(BF16)))).astype(BF16)
        bdv = bdiag(v)
        av = _bmm(sk[:, :CHUNK], bdv)
        yield
        p = _bmm(tinv, bdiag(at)).astype(BF16)
        u0 = _bmm(tinv, bdiag(av.astype(BF16)))
        yield
        q = (rt.astype(F32) + _bmm(m_rb, bdiag(p))).astype(BF16)
        y0 = _bmm(m_rb, bdiag(u0.astype(BF16))) + _bmm(sk[:, CHUNK:], bdv)
        parts = (jnp.concatenate([q, p], axis=1), jnp.concatenate([y0, u0], axis=1),
                 jnp.concatenate([bh, kh], axis=1), v)
        out += [tuple(z[n * nu:(n + 1) * nu] for z in parts) for n in range(len(cs))]
        yield

    wct = [wc_ref[r, 0] for r in range(rows)]
    state = [h_ref[...]]
    ys = []
    local = []

    def recur(c):
        qp, yu0, bk, v = local[c]
        h_hi, h_lo = _split2(state[0])
        yu = _bmm(qp, h_hi) + _bmm(qp, h_lo) + yu0
        ys.append(yu[:, :CHUNK])
        upd = _bmm_tn(bk, jnp.concatenate([yu[:, CHUNK:].astype(BF16), v], axis=1))
        wcol = jnp.stack([wct[r][g * gl:(g + 1) * gl, c:c + 1] for r, g in units])
        state[0] = jnp.where(same_head, upd, 0.0) + wcol * state[0]

    assert sorted(c for cs in RWKV_SCHEDULE for c in cs) == list(range(nc))
    done = 0
    for cs in RWKV_SCHEDULE:
        pending = list(range(done, len(local)))
        slots = {(k + 1) * RWKV_STAGES // (len(pending) + 1): c for k, c in enumerate(pending)}
        for n, _ in enumerate(chunk_local(list(cs), local), start=1):
            if n in slots:
                recur(slots[n])
                done += 1
    for c in range(done, nc):
        recur(c)
    h_ref[...] = state[0]

    y = jnp.concatenate([jnp.concatenate([yc[r * ng + g] for g in range(ng)], axis=-1)
                         for r in range(rows) for yc in ys], axis=0)
    e = e_ref[...]

    def head_mean(z):
        return _dot(z.astype(BF16), e) * (1.0 / HEAD_DIM)

    d = y - head_mean(y)
    yn = d * lax.rsqrt(head_mean(d * d) + LNX_EPS)
    flat = lambda ref: ref[...].astype(F32).reshape(rows * tm, WIDTH_C)
    out = (yn * lg_ref[...] + lb_ref[...] + flat(bonus_ref)) * flat(g_ref)
    o_ref[...] = out.astype(BF16).reshape(rows, tm, WIDTH_C)


def _rwkv_chunk(rt, at, bt, kt, bh, kh, v, wc, bonus, g, lnx_g, lnx_b, e):
    b, t, _ = rt.shape
    tm = RWKV_TILE
    rows = math.gcd(RWKV_ROWS, b)
    gl = RWKV_GROUP * HEAD_DIM
    const = lambda shape: pl.BlockSpec(shape, lambda i, j: (0,) * len(shape))
    tok = lambda: pl.BlockSpec((rows, tm, WIDTH_C), lambda i, j: (i, j, 0))
    return pl.pallas_call(
        _rwkv_chunk_body,
        grid=(b // rows, t // tm),
        in_specs=[tok(), tok(), tok(), tok(), tok(), tok(), tok(),
                  pl.BlockSpec((rows, 1, WIDTH_C, LANES), lambda i, j: (i, j, 0, 0)),
                  tok(), tok(), const(lnx_g.shape), const(lnx_b.shape), const(e.shape)],
        out_specs=tok(),
        out_shape=jax.ShapeDtypeStruct((b, t, WIDTH_C), BF16),
        scratch_shapes=[pltpu.VMEM((rows * (N_HEADS_C // RWKV_GROUP), gl, gl), F32)],
        compiler_params=_cparams(2),
        name="rwkv_chunk",
    )(rt, at, bt, kt, bh, kh, v, wc, bonus, g, lnx_g, lnx_b, e)


def _outproj_body(x_ref, ya_ref, yb_ref, yc_ref, wa_ref, wb_ref, wc_ref, o_ref):
    acc = _dot_tn(ya_ref[0], wa_ref[...])
    acc = acc + _dot_tn(yb_ref[0], wb_ref[...])
    acc = acc + _dot(yc_ref[0], wc_ref[...])
    o_ref[0] = x_ref[0] + acc


def _outproj(x, yaT, ybT, yc, wa, wb, wc):
    b, t, d = x.shape
    tm = TOK_TILE
    const = lambda shape: pl.BlockSpec(shape, lambda i, j: (0,) * len(shape))
    return pl.pallas_call(
        _outproj_body,
        grid=(b, t // tm),
        in_specs=[pl.BlockSpec((1, tm, d), lambda i, j: (i, j, 0)),
                  pl.BlockSpec((1, WIDTH_A, tm), lambda i, j: (i, 0, j)),
                  pl.BlockSpec((1, WIDTH_B, tm), lambda i, j: (i, 0, j)),
                  pl.BlockSpec((1, tm, WIDTH_C), lambda i, j: (i, j, 0)),
                  const(wa.shape), const(wb.shape), const(wc.shape)],
        out_specs=pl.BlockSpec((1, tm, d), lambda i, j: (i, j, 0)),
        out_shape=jax.ShapeDtypeStruct((b, t, d), F32),
        compiler_params=_cparams(2),
        name="outproj",
    )(x, yaT, ybT, yc, wa, wb, wc)


def _ffn_body(x_ref, g_ref, wg_ref, wv_ref, cw_ref, cb_ref, wd_ref, o_ref, gbuf):
    tm = x_ref.shape[1]
    t = pl.program_id(1)

    @pl.when(t == 0)
    def _():
        gbuf[0:8, :] = jnp.zeros((8, D_FF), F32)

    x = x_ref[0]
    ms = jnp.mean(x * x, axis=-1, keepdims=True)
    h = (x * lax.rsqrt(ms + RMS_EPS) * g_ref[...]).astype(BF16)
    acc = x
    for lo, hi in zip(FF_SPLITS[:-1], FF_SPLITS[1:]):
        gate = _dot(h, wg_ref[:, lo:hi])
        val = _dot(h, wv_ref[:, lo:hi])
        gbuf[8:8 + tm, lo:hi] = gate
        g1 = gbuf[7:7 + tm, lo:hi]
        g2 = gbuf[6:6 + tm, lo:hi]
        gbuf[0:8, lo:hi] = gate[tm - 8:tm, :]
        conv = cb_ref[:, lo:hi] + g2 * cw_ref[0:1, lo:hi] + g1 * cw_ref[1:2, lo:hi] + gate * cw_ref[2:3, lo:hi]
        act = (conv * _sigmoid(conv) * val).astype(BF16)
        acc = acc + _dot(act, wd_ref[lo:hi, :])
    o_ref[0] = acc


def _ffn(x, g, wg, wv, cw, cb, wd):
    b, t, d = x.shape
    tm = TOK_TILE
    const = lambda shape: pl.BlockSpec(shape, lambda i, j: (0,) * len(shape),
                                       pipeline_mode=pl.Buffered(1))
    return pl.pallas_call(
        _ffn_body,
        grid=(b, t // tm),
        in_specs=[pl.BlockSpec((1, tm, d), lambda i, j: (i, j, 0)),
                  const(g.shape), const(wg.shape), const(wv.shape), const(cw.shape), const(cb.shape),
                  const(wd.shape)],
        out_specs=pl.BlockSpec((1, tm, d), lambda i, j: (i, j, 0)),
        out_shape=jax.ShapeDtypeStruct((b, t, d), F32),
        scratch_shapes=[pltpu.VMEM((tm + 8, D_FF), F32)],
        compiler_params=_cparams(2),
        name="ffn",
    )(x, g, wg, wv, cw, cb, wd)


def _pad_heads_cols(w, nh):
    d = w.shape[0]
    w = w.reshape(d, nh, HEAD_DIM)
    return jnp.pad(w, ((0, 0), (0, 0), (0, LANES - HEAD_DIM))).reshape(d, nh * LANES)


def _qk_bound(gq, gk):
    scale = HEAD_DIM ** -0.5
    return 1.01 * HEAD_DIM * scale * jnp.max(jnp.abs(gq)) * jnp.max(jnp.abs(gk)) + 0.05


def _layer_params(l, w_in, q_norm_a, k_norm_a, rel_bias, q_norm_b, k_norm_b, forget_bias):
    w = w_in[l]
    scale = HEAD_DIM ** -0.5
    a0, b0 = 0, 3 * WIDTH_A
    c0 = b0 + 3 * WIDTH_B + N_HEADS_B
    qa, ka, va = (w[:, a0 + i * WIDTH_A:a0 + (i + 1) * WIDTH_A] for i in range(3))
    qb, kb, vb = (w[:, b0 + i * WIDTH_B:b0 + (i + 1) * WIDTH_B] for i in range(3))
    fg = w[:, b0 + 3 * WIDTH_B:c0]
    wc = w[:, c0:]
    wn = jnp.concatenate([_pad_heads_cols(ka, N_HEADS_A), _pad_heads_cols(kb, N_HEADS_B),
                          jnp.pad(fg, ((0, 0), (0, LANES - N_HEADS_B))), wc], axis=1).astype(BF16)
    pad_v = lambda v, nh: jnp.pad(v.reshape(-1, nh, HEAD_DIM),
                                  ((0, 0), (0, 0), (0, VB_ROWS - HEAD_DIM))).reshape(-1, nh * VB_ROWS)
    wt = jnp.concatenate([qa, pad_v(va, N_HEADS_A), _pad_heads_cols(qb, N_HEADS_B), pad_v(vb, N_HEADS_B),
                          jnp.pad(fg, ((0, 0), (0, 16 - N_HEADS_B)))], axis=1).T.astype(BF16)
    pad = LANES - HEAD_DIM
    gka = jnp.pad(k_norm_a[l], (0, pad)).reshape(1, LANES)
    gkb = jnp.pad(k_norm_b[l], (0, pad)).reshape(1, LANES)
    gqa = (q_norm_a[l] * (scale * LOG2E)).reshape(HEAD_DIM, 1)
    gqb = jnp.pad(q_norm_b[l] * (scale * LOG2E), (0, pad)).reshape(LANES, 1)
    fbn = jnp.pad(forget_bias[l], (0, LANES - N_HEADS_B)).reshape(1, LANES)
    fbt = jnp.pad(forget_bias[l], (0, 16 - N_HEADS_B)).reshape(16, 1)
    shift_b = (LOG2E * _qk_bound(q_norm_b[l], k_norm_b[l])).reshape(1, 1)
    bound_a = _qk_bound(q_norm_a[l], k_norm_a[l])
    tab = rel_bias[l]
    shift_a = LOG2E * (bound_a + jnp.max(tab, axis=1))
    depth_a = jnp.max(shift_a + LOG2E * (bound_a - tab[:, MAX_REL]))
    return wn, wt, gka, gkb, gqa, gqb, fbn, fbt, shift_b, shift_a, depth_a


def _selectors():
    import numpy as np
    selk = np.zeros((3, LANES, N_HEADS_B * LANES), np.float32)
    selq = np.zeros((3, N_HEADS_B * LANES, 16), np.float32)
    for p in range(3):
        for hh in range(N_HEADS_B):
            selk[p, hh, hh * LANES + HEAD_DIM + 3 + p] = -1.0
            selq[p, hh * LANES + HEAD_DIM + p, hh] = 1.0
    e = np.kron(np.eye(N_HEADS_C, dtype=np.float32), np.ones((HEAD_DIM, HEAD_DIM), np.float32))
    return jnp.asarray(selk, BF16), jnp.asarray(selq, BF16), jnp.asarray(e, BF16)


def kernel(x, mix_norm_g, w_in, q_norm_a, k_norm_a, rel_bias, q_norm_b, k_norm_b, forget_bias, shift_mu, w0, w2,
           a0, a2, g2, k_k, k_a, r_k, lnx_g, lnx_b, w_out, ffn_norm_g, w_up, conv_w, conv_b, w_down):
    depth = w_in.shape[0]
    b, t, d = x.shape
    selk, selq, e = _selectors()
    row = lambda v: v.reshape(1, -1)
    for l in range(depth):
        wn, wt, gka, gkb, gqa, gqb, fbn, fbt, shift_b, shift_a, depth_a = _layer_params(
            l, w_in, q_norm_a, k_norm_a, rel_bias, q_norm_b, k_norm_b, forget_bias)
        qaT, vaT, kA, qbT, vbT, kB, pc, pclo = _inproj(x, row(mix_norm_g[l]), wn, wt, gka, gkb, gqa, gqb,
                                                       fbn, fbt, selk, selq, shift_b)
        bias = _relbias(rel_bias[l], shift_a)
        yaT = lax.cond(depth_a <= 2 * FAST_MAX_SHIFT,
                       functools.partial(_attn_a, online=False), functools.partial(_attn_a, online=True),
                       qaT, kA, vaT, bias)
        ybT = lax.cond(shift_b[0, 0] <= FAST_MAX_SHIFT,
                       functools.partial(_fox, online=False), functools.partial(_fox, online=True),
                       qbT, kB, vbT)
        rt, at, bt, kt, bh, kh, v, wcum, bonus, g = _rwkv_prep(
            pc, pclo, row(shift_mu[l]), row(w0[l]), w2[l].astype(BF16), row(a0[l]), a2[l].astype(BF16),
            g2[l].astype(BF16), row(k_k[l]), row(k_a[l]), row(r_k[l]), e)
        yc = _rwkv_chunk(rt, at, bt, kt, bh, kh, v, wcum, bonus, g, row(lnx_g[l]), row(lnx_b[l]), e)
        wo = w_out[l].astype(BF16)
        x = _outproj(x, yaT.reshape(b, WIDTH_A, t), ybT.reshape(b, WIDTH_B, t), yc,
                     wo[:WIDTH_A], wo[WIDTH_A:WIDTH_A + WIDTH_B], wo[WIDTH_A + WIDTH_B:])
        wu = w_up[l].astype(BF16)
        x = _ffn(x, row(ffn_norm_g[l]), wu[:, :D_FF], wu[:, D_FF:], conv_w[l], row(conv_b[l]),
                 w_down[l].astype(BF16))
    return x
```

```python
import functools
import math

import jax
import jax.numpy as jnp
from jax import lax
from jax.experimental import pallas as pl
from jax.experimental.pallas import tpu as pltpu

F32 = jnp.float32
BF16 = jnp.bfloat16

D_MODEL = 1024
HEAD_DIM = 64
CHUNK = 64
LEFT_CHUNKS = 8
MAX_REL = 128
N_HEADS_A = 4
N_HEADS_B = 4
N_HEADS_C = 8
WIDTH_A = N_HEADS_A * HEAD_DIM
WIDTH_B = N_HEADS_B * HEAD_DIM
WIDTH_C = N_HEADS_C * HEAD_DIM
DECAY_LORA = 64
AAA_LORA = 64
GATE_LORA = 128
COLS_C = 3 * WIDTH_C + DECAY_LORA + AAA_LORA + GATE_LORA
D_FF = 2816
RMS_EPS = 1e-6
LNX_EPS = 64e-5
NEG_INF = -1e30

LANES = 128
TOK_TILE = 512
ATT_A_TILE = 256
ATT_A_WIN = 3 * ATT_A_TILE
FOX_TILE = 512
FOX_GROUP = 8
RWKV_GROUP = 4
RWKV_TILE = 256
RWKV_ROWS = 4
RWKV_SCHEDULE = ((0, 1), (2, 3))
RWKV_STAGES = 9
FF_SPLITS = (0, 768, 1536, 2304, 2816)
VMEM_LIMIT = 56 * 1024 * 1024

PN_KB, PN_F, PN_C = 0, 512, 640
PN_COLS = PN_C + COLS_C
VB_ROWS = 80
PT_QA = 0
PT_KA = PT_QA + WIDTH_A
PT_VA = PT_KA + WIDTH_A
PT_QB = PT_VA + N_HEADS_A * VB_ROWS
QB_ROWS = 80
PT_VB = PT_QB + N_HEADS_B * QB_ROWS
PT_F = PT_VB + N_HEADS_B * VB_ROWS
PT_ROWS = PT_F + 16
LOG2E = 1.4426950408889634
FAST_MAX_SHIFT = 40.0


def _cparams(n_axes):
    return pltpu.CompilerParams(dimension_semantics=("arbitrary",) * n_axes,
                                vmem_limit_bytes=VMEM_LIMIT)


def _split3(x):
    hi = x.astype(BF16)
    r1 = x - hi.astype(F32)
    mid = r1.astype(BF16)
    lo = (r1 - mid.astype(F32)).astype(BF16)
    return hi, mid, lo


def _split2(x):
    hi = x.astype(BF16)
    lo = (x - hi.astype(F32)).astype(BF16)
    return hi, lo


def _dot(a, b):
    return jnp.dot(a, b, preferred_element_type=F32)


def _dot_nt(a, b):
    return lax.dot_general(a, b, (((1,), (1,)), ((), ())), preferred_element_type=F32)


def _dot_tn(a, b):
    return lax.dot_general(a, b, (((0,), (0,)), ((), ())), preferred_element_type=F32)


def _log_sigmoid(x):
    return jnp.minimum(x, 0.0) - jnp.log(1.0 + jnp.exp(-jnp.abs(x)))


def _softplus(x):
    return jnp.maximum(x, 0.0) + jnp.log(1.0 + jnp.exp(-jnp.abs(x)))


def _sigmoid(x):
    return 1.0 / (1.0 + jnp.exp(-x))


def _inproj_body(x_ref, g_ref, wn_ref, wt_ref, gka_ref, gkb_ref, gqa_ref, gqb_ref, fbn_ref, fbt_ref,
                 selk_ref, selq_ref, shift_ref, *rest):
    rwkv_params, rest = rest[:10], rest[10:]
    qaT_ref, vaT_ref, kA_ref, qbT_ref, vbT_ref, kB_ref = rest[:6]
    rwkv_outs = rest[6:16]
    carry_n, carry_t, sbuf = rest[16:]
    tm = x_ref.shape[1]
    t = pl.program_id(1)

    @pl.when(t == 0)
    def _():
        carry_n[...] = jnp.zeros_like(carry_n)
        carry_t[...] = jnp.zeros_like(carry_t)

    x = x_ref[0]
    ms = jnp.mean(x * x, axis=-1, keepdims=True)
    h = (x * lax.rsqrt(ms + RMS_EPS) * g_ref[...]).astype(BF16)
    rows_t = lambda start, n: _dot_nt(wt_ref[start:start + n, :], h)
    cols_n = lambda start, n: _dot(h, wn_ref[:, start:start + n])
    row_v = lax.broadcasted_iota(jnp.int32, (VB_ROWS, 1), 0)
    ones_v = jnp.where(row_v == HEAD_DIM, 1.0, 0.0)
    val = {}

    def norm_rows(p, gain_ref):
        p = p.reshape(N_HEADS_A, HEAD_DIM, tm)
        return (p * lax.rsqrt(jnp.mean(p * p, axis=1, keepdims=True) + RMS_EPS) * gain_ref[...][None]).astype(BF16)

    def gates_epilogue(p):
        pn_f, pt_f = p
        lane_n = lax.broadcasted_iota(jnp.int32, (1, LANES), 1)
        lf_n = jnp.where(lane_n < N_HEADS_B, LOG2E * _log_sigmoid(pn_f + fbn_ref[...]), 0.0)
        row_t = lax.broadcasted_iota(jnp.int32, (16, 1), 0)
        lf_t = jnp.where(row_t < N_HEADS_B, LOG2E * _log_sigmoid(pt_f + fbt_ref[...]), 0.0)
        ri = lax.broadcasted_iota(jnp.int32, (tm, tm), 0)
        ci = lax.broadcasted_iota(jnp.int32, (tm, tm), 1)
        low = (ci <= ri).astype(BF16)
        upp = (ri <= ci).astype(BF16)
        hn, mn, ln = _split3(lf_n)
        val["c_n"] = _dot(low, hn) + _dot(low, mn) + _dot(low, ln) + carry_n[0:1, :]
        ht, mt, lt = _split3(lf_t)
        val["c_t"] = _dot(ht, upp) + _dot(mt, upp) + _dot(lt, upp) + carry_t[:, 0:1]
        carry_n[...] = carry_n[...] + jnp.sum(lf_n, axis=0, keepdims=True)
        carry_t[...] = carry_t[...] + jnp.sum(lf_t, axis=1, keepdims=True)

    def kb_epilogue(pn_kb):
        chn, cmn, cln = _split3(val["c_n"])
        kaug = _dot(chn, selk_ref[0]) + _dot(cmn, selk_ref[1]) + _dot(cln, selk_ref[2])
        lane_k = lax.broadcasted_iota(jnp.int32, (1, LANES), 1)
        ones_k = jnp.where((lane_k >= HEAD_DIM) & (lane_k < HEAD_DIM + 3), 1.0, 0.0)
        for hh in range(N_HEADS_B):
            k = pn_kb[:, LANES * hh:LANES * (hh + 1)]
            msk = jnp.sum(k * k, axis=-1, keepdims=True) * (1.0 / HEAD_DIM)
            kn = k * lax.rsqrt(msk + RMS_EPS) * gkb_ref[...]
            kB_ref[0, hh] = (kn + kaug[:, LANES * hh:LANES * (hh + 1)] + ones_k).astype(BF16)

    def qb_epilogue(pt_qb):
        cht, cmt, clt = _split3(val["c_t"] - shift_ref[...])
        qaug = _dot(selq_ref[0], cht) + _dot(selq_ref[1], cmt) + _dot(selq_ref[2], clt)
        qb = pt_qb.reshape(N_HEADS_B, QB_ROWS, tm)
        msq = jnp.sum(qb * qb, axis=1, keepdims=True) * (1.0 / HEAD_DIM)
        row_q = lax.broadcasted_iota(jnp.int32, (QB_ROWS, 1), 0)
        ones_q = jnp.where((row_q >= HEAD_DIM + 3) & (row_q < HEAD_DIM + 6), 1.0, 0.0)
        qn = qb * lax.rsqrt(msq + RMS_EPS) * gqb_ref[...][None]
        qbT_ref[0] = (qn + qaug.reshape(N_HEADS_B, QB_ROWS, tm) + ones_q[None]).astype(BF16)

    def store(ref, value):
        ref[...] = value

    sections = [
        (lambda: rows_t(PT_QA, WIDTH_A), lambda p: store(qaT_ref, norm_rows(p, gqa_ref)[None])),
        (lambda: rows_t(PT_KA, WIDTH_A), lambda p: store(kA_ref, norm_rows(p, gka_ref)[None])),
        (lambda: rows_t(PT_VA, N_HEADS_A * VB_ROWS),
         lambda p: store(vaT_ref, (p.reshape(N_HEADS_A, VB_ROWS, tm) + ones_v[None]).astype(BF16)[None])),
        (lambda: (cols_n(PN_F, LANES), rows_t(PT_F, 16)), gates_epilogue),
        (lambda: rows_t(PT_VB, N_HEADS_B * VB_ROWS),
         lambda p: store(vbT_ref, (p.reshape(N_HEADS_B, VB_ROWS, tm) + ones_v[None]).astype(BF16)[None, :, None])),
        (lambda: cols_n(PN_KB, N_HEADS_B * LANES), kb_epilogue),
        (lambda: rows_t(PT_QB, N_HEADS_B * QB_ROWS), qb_epilogue),
    ]
    pn_c = cols_n(PN_C, COLS_C)
    slots = _rwkv_prep(pn_c, t, *rwkv_params, *rwkv_outs, sbuf)
    pending = None
    for mm, epilogue in sections:
        p = mm()
        if pending is not None:
            pending[1](pending[0])
        pending = (p, epilogue)
        next(slots, None)
    pending[1](pending[0])
    for _ in slots:
        pass


def _inproj(x, g, wn, wt, gka, gkb, gqa, gqb, fbn, fbt, selk, selq, shift, rwkv_params):
    b, t, d = x.shape
    tm = TOK_TILE
    nt = t // tm
    const = lambda shape: pl.BlockSpec(shape, lambda i, j: (0,) * len(shape))
    tok = lambda: pl.BlockSpec((1, tm, WIDTH_C), lambda i, j: (i, j, 0))
    bf = jax.ShapeDtypeStruct((b, t, WIDTH_C), BF16)
    out_shape = (
        jax.ShapeDtypeStruct((b, N_HEADS_A, HEAD_DIM, t), BF16),
        jax.ShapeDtypeStruct((b, N_HEADS_A, VB_ROWS, t), BF16),
        jax.ShapeDtypeStruct((b, N_HEADS_A, HEAD_DIM, t), BF16),
        jax.ShapeDtypeStruct((b, N_HEADS_B, QB_ROWS, t), BF16),
        jax.ShapeDtypeStruct((b, N_HEADS_B, nt, VB_ROWS, tm), BF16),
        jax.ShapeDtypeStruct((b, N_HEADS_B, t, LANES), BF16),
        bf, bf, bf, bf, bf, bf, bf,
        jax.ShapeDtypeStruct((b, t // RWKV_TILE, WIDTH_C, LANES), F32),
        bf, bf,
    )
    out_specs = (
        pl.BlockSpec((1, N_HEADS_A, HEAD_DIM, tm), lambda i, j: (i, 0, 0, j)),
        pl.BlockSpec((1, N_HEADS_A, VB_ROWS, tm), lambda i, j: (i, 0, 0, j)),
        pl.BlockSpec((1, N_HEADS_A, HEAD_DIM, tm), lambda i, j: (i, 0, 0, j)),
        pl.BlockSpec((1, N_HEADS_B, QB_ROWS, tm), lambda i, j: (i, 0, 0, j)),
        pl.BlockSpec((1, N_HEADS_B, 1, VB_ROWS, tm), lambda i, j: (i, 0, j, 0, 0)),
        pl.BlockSpec((1, N_HEADS_B, tm, LANES), lambda i, j: (i, 0, j, 0)),
        tok(), tok(), tok(), tok(), tok(), tok(), tok(),
        pl.BlockSpec((1, tm // RWKV_TILE, WIDTH_C, LANES), lambda i, j: (i, j, 0, 0)),
        tok(), tok(),
    )
    in_specs = [
        pl.BlockSpec((1, tm, d), lambda i, j: (i, j, 0)),
        const((1, d)), const(wn.shape), const(wt.shape),
        const(gka.shape), const(gkb.shape), const(gqa.shape), const(gqb.shape),
        const(fbn.shape), const(fbt.shape), const(selk.shape), const(selq.shape), const(shift.shape),
    ] + [const(p.shape) for p in rwkv_params]
    return pl.pallas_call(
        _inproj_body,
        grid=(b, nt),
        in_specs=in_specs,
        out_specs=out_specs,
        out_shape=out_shape,
        scratch_shapes=[pltpu.VMEM((8, LANES), F32), pltpu.VMEM((16, LANES), F32),
                        pltpu.VMEM((tm + 8, COLS_C), F32)],
        compiler_params=_cparams(2),
        name="inproj",
    )(x, g, wn, wt, gka, gkb, gqa, gqb, fbn, fbt, selk, selq, shift, *rwkv_params)


def _relbias_body(tab_ref, shift_ref, o_ref):
    hh = pl.program_id(0)
    kj = lax.broadcasted_iota(jnp.int32, (ATT_A_WIN, ATT_A_TILE), 0)
    qi = lax.broadcasted_iota(jnp.int32, (ATT_A_WIN, ATT_A_TILE), 1)
    rel = jnp.clip(kj - LEFT_CHUNKS * CHUNK - qi, -MAX_REL, MAX_REL) + MAX_REL

    def body(r, acc):
        return jnp.where(rel == r, tab_ref[hh, r], acc)

    bias = lax.fori_loop(0, 2 * MAX_REL + 1, body, jnp.zeros((ATT_A_WIN, ATT_A_TILE), F32))
    kc = kj // CHUNK
    qc = qi // CHUNK
    band = (kc >= qc) & (kc <= qc + LEFT_CHUNKS)
    o_ref[0] = jnp.where(band, LOG2E * bias - shift_ref[hh], NEG_INF)


def _relbias(tab, shift):
    return pl.pallas_call(
        _relbias_body,
        grid=(N_HEADS_A,),
        in_specs=[pl.BlockSpec(memory_space=pltpu.SMEM), pl.BlockSpec(memory_space=pltpu.SMEM)],
        out_specs=pl.BlockSpec((1, ATT_A_WIN, ATT_A_TILE), lambda i: (i, 0, 0)),
        out_shape=jax.ShapeDtypeStruct((N_HEADS_A, ATT_A_WIN, ATT_A_TILE), F32),
        compiler_params=_cparams(1),
        name="relbias",
    )(tab, shift)


def _attn_a_body(q_ref, k0_ref, k1_ref, k2_ref, v0_ref, v1_ref, v2_ref, bias_ref, o_ref, *, online):
    i = pl.program_id(1)
    tq = ATT_A_TILE
    kj = lax.broadcasted_iota(jnp.int32, (tq, 1), 0)

    def run(mask_padding):
        scores = []
        for hh in range(N_HEADS_A):
            q = q_ref[0, hh]
            ss = []
            for d, k_ref in enumerate((k0_ref, k1_ref, k2_ref)):
                s = _dot_tn(k_ref[0, hh], q) + bias_ref[hh, d * tq:(d + 1) * tq, :]
                if mask_padding:
                    s = jnp.where(kj + (i - 2 + d) * tq >= 0, s, NEG_INF)
                ss.append(s)
            scores.append(ss)
        for hh, ss in enumerate(scores):
            if online:
                m = functools.reduce(jnp.maximum, [jnp.max(s, axis=0, keepdims=True) for s in ss])
                ss = [s - m for s in ss]
            acc = jnp.zeros((VB_ROWS, tq), F32)
            for s, v_ref in zip(ss, (v0_ref, v1_ref, v2_ref)):
                acc = acc + _dot(v_ref[0, hh], jnp.exp2(s).astype(BF16))
            o_ref[0, hh] = (acc[0:HEAD_DIM] / acc[HEAD_DIM:HEAD_DIM + 1]).astype(BF16)

    pl.when(i < 2)(functools.partial(run, True))
    pl.when(i >= 2)(functools.partial(run, False))


def _attn_a(qaT, kA, vaT, bias, online):
    b, nh, _, t = qaT.shape
    tq = ATT_A_TILE
    kspec = lambda d: pl.BlockSpec((1, nh, HEAD_DIM, tq),
                                   lambda bb, i: (bb, 0, 0, jnp.maximum(i - 2 + d, 0)))
    vspec = lambda d: pl.BlockSpec((1, nh, VB_ROWS, tq),
                                   lambda bb, i: (bb, 0, 0, jnp.maximum(i - 2 + d, 0)))
    return pl.pallas_call(
        functools.partial(_attn_a_body, online=online),
        grid=(b, t // tq),
        in_specs=[pl.BlockSpec((1, nh, HEAD_DIM, tq), lambda bb, i: (bb, 0, 0, i)),
                  kspec(0), kspec(1), kspec(2), vspec(0), vspec(1), vspec(2),
                  pl.BlockSpec((nh, ATT_A_WIN, tq), lambda bb, i: (0, 0, 0))],
        out_specs=pl.BlockSpec((1, nh, HEAD_DIM, tq), lambda bb, i: (bb, 0, 0, i)),
        out_shape=jax.ShapeDtypeStruct((b, nh, HEAD_DIM, t), BF16),
        compiler_params=_cparams(2),
        name="attn_a_online" if online else "attn_a",
    )(qaT, kA, kA, kA, vaT, vaT, vaT, bias)


def _fox_body(q_ref, k_ref, v_ref, o_ref, *, online):
    i = pl.program_id(2)
    tq = FOX_TILE
    q = jnp.concatenate([q_ref[0, 0], jnp.zeros((LANES - QB_ROWS, tq), BF16)], axis=0)

    def scores(j):
        ks = k_ref[0, 0, pl.ds(pl.multiple_of(j * tq, tq), tq), :]
        return _dot(ks, q)

    def accumulate(j, s, m, acc, diagonal):
        vs = v_ref[0, 0, j]
        if diagonal:
            kj = lax.broadcasted_iota(jnp.int32, (tq, tq), 0)
            qi = lax.broadcasted_iota(jnp.int32, (tq, tq), 1)
            causal = kj <= qi
        if online:
            if diagonal:
                s = jnp.where(causal, s, NEG_INF)
            m_new = jnp.maximum(m, jnp.max(s, axis=0, keepdims=True))
            acc = jnp.exp2(m - m_new) * acc
            p = jnp.exp2(s - m_new)
            m = m_new
        else:
            p = jnp.exp2(s)
            if diagonal:
                p = jnp.where(causal, p, 0.0)
        return m, acc + _dot(vs, p.astype(BF16))

    def tiles(js, carry, last_diagonal):
        ss = [scores(j) for j in js]
        m, acc = carry
        for n, (j, s) in enumerate(zip(js, ss)):
            m, acc = accumulate(j, s, m, acc, last_diagonal and n == len(js) - 1)
        return m, acc

    init = (jnp.full((1, tq), NEG_INF, F32), jnp.zeros((VB_ROWS, tq), F32))
    if online:
        carry = lax.fori_loop(0, i, lambda j, c: tiles([j], c, False), init)
        _, acc = tiles([i], carry, True)
    else:
        g = FOX_GROUP
        carry = lax.fori_loop(0, i // g, lambda jj, c: tiles([g * jj + n for n in range(g)], c, False), init)
        tails = [functools.partial(lambda c, r: tiles([i - r + n for n in range(r + 1)], c, True), r=r)
                 for r in range(g)]
        _, acc = lax.switch(i % g, tails, carry)
    o_ref[0, 0] = (acc[0:HEAD_DIM] / acc[HEAD_DIM:HEAD_DIM + 1]).astype(BF16)


def _fox(qbT, kB, vbT, online):
    b, nh, _, t = qbT.shape
    tq = FOX_TILE
    nk = t // tq
    return pl.pallas_call(
        functools.partial(_fox_body, online=online),
        grid=(b, nh, nk),
        in_specs=[pl.BlockSpec((1, 1, QB_ROWS, tq), lambda bb, hh, i: (bb, hh, 0, i)),
                  pl.BlockSpec((1, 1, t, LANES), lambda bb, hh, i: (bb, hh, 0, 0)),
                  pl.BlockSpec((1, 1, nk, VB_ROWS, tq), lambda bb, hh, i: (bb, hh, 0, 0, 0))],
        out_specs=pl.BlockSpec((1, 1, HEAD_DIM, tq), lambda bb, hh, i: (bb, hh, 0, i)),
        out_shape=jax.ShapeDtypeStruct((b, nh, HEAD_DIM, t), BF16),
        compiler_params=_cparams(3),
        name="fox_online" if online else "fox",
    )(qbT, kB, vbT)


def _rwkv_prep(u, t, mu_ref, w0_ref, w2_ref, a0_ref, a2_ref, g2_ref, kk_ref, ka_ref, rk_ref, e_ref,
               rt_ref, at_ref, bt_ref, kt_ref, bh_ref, kh_ref, v_ref, wc_ref, bonus_ref, g_ref, sbuf):
    tm = u.shape[0]

    @pl.when(t == 0)
    def _():
        sbuf[0:8, :] = jnp.zeros((8, COLS_C), F32)

    sbuf[8:8 + tm, :] = u
    prev = sbuf[7:7 + tm, :]
    sbuf[0:8, :] = u[tm - 8:tm, :]
    u = u + (prev - u) * mu_ref[...]
    yield

    c = WIDTH_C
    r = u[:, 0:c]
    k = u[:, c:2 * c]
    v = u[:, 2 * c:3 * c]
    w_lo = u[:, 3 * c:3 * c + DECAY_LORA]
    a_lo = u[:, 3 * c + DECAY_LORA:3 * c + DECAY_LORA + AAA_LORA]
    g_lo = u[:, 3 * c + DECAY_LORA + AAA_LORA:COLS_C]

    w = w0_ref[...] + _dot(jnp.tanh(w_lo).astype(BF16), w2_ref[...])
    w = -_softplus(-w) - 0.5
    ld = -jnp.exp(w)
    yield
    a = _sigmoid(a0_ref[...] + _dot(a_lo.astype(BF16), a2_ref[...]))
    g = _dot(_sigmoid(g_lo).astype(BF16), g2_ref[...])
    v_ref[0] = v.astype(BF16)
    g_ref[0] = g.astype(BF16)
    yield

    e = e_ref[...]
    kk = k * kk_ref[...]
    nrm2 = _dot((kk * kk).astype(BF16), e)
    kkn = kk * lax.rsqrt(jnp.maximum(nrm2, 1e-24))
    k2 = k * (1.0 + (a - 1.0) * ka_ref[...])
    kka = kkn * a
    b_hi, b_lo = _split2(r * k2 * rk_ref[...])
    bonus_ref[0] = ((_dot(b_hi, e) + _dot(b_lo, e)) * v).astype(BF16)
    yield

    ri = lax.broadcasted_iota(jnp.int32, (2 * CHUNK, CHUNK), 0)
    ci = lax.broadcasted_iota(jnp.int32, (2 * CHUNK, CHUNK), 1)
    tri = ((ci <= ri) | (ri >= CHUNK)).astype(BF16)
    ld_hi, ld_lo = _split2(ld)
    for cc in range(tm // CHUNK):
        sl = slice(cc * CHUNK, (cc + 1) * CHUNK)
        cum = _dot(tri, ld_hi[sl]) + _dot(tri, ld_lo[sl])
        lc = cum[0:CHUNK]
        tot = cum[CHUNK:2 * CHUNK]
        e_neg = jnp.exp(-lc)
        e_rem = jnp.exp(tot - lc)
        rt_ref[0, sl, :] = (r[sl] * jnp.exp(lc)).astype(BF16)
        at_ref[0, sl, :] = (-kkn[sl] * jnp.exp(lc - ld[sl])).astype(BF16)
        bt_ref[0, sl, :] = (kka[sl] * e_neg).astype(BF16)
        kt_ref[0, sl, :] = (k2[sl] * e_neg).astype(BF16)
        bh_ref[0, sl, :] = (kka[sl] * e_rem).astype(BF16)
        kh_ref[0, sl, :] = (k2[sl] * e_rem).astype(BF16)
        if cc % 2 == 1:
            yield
    seg = (lax.broadcasted_iota(jnp.int32, (16, tm), 1) // CHUNK
           == lax.broadcasted_iota(jnp.int32, (16, tm), 0)).astype(BF16)
    tots3 = jnp.concatenate(_split3(_dot(seg, ld_hi) + _dot(seg, ld_lo)), axis=0)
    per_blk = RWKV_TILE // CHUNK
    for blk in range(tm // RWKV_TILE):
        put = (lax.broadcasted_iota(jnp.int32, (48, LANES), 0) % 16
               == lax.broadcasted_iota(jnp.int32, (48, LANES), 1) + blk * per_blk).astype(BF16)
        wc_ref[0, blk] = jnp.exp(_dot_tn(tots3, put))


def _bmm(a, b):
    return lax.dot_general(a, b, (((2,), (1,)), ((0,), (0,))), preferred_element_type=F32)


def _bmm_nt(a, b):
    return lax.dot_general(a, b, (((2,), (2,)), ((0,), (0,))), preferred_element_type=F32)


def _bmm_tn(a, b):
    return lax.dot_general(a, b, (((1,), (1,)), ((0,), (0,))), preferred_element_type=F32)


def _rwkv_chunk_body(rt_ref, at_ref, bt_ref, kt_ref, bh_ref, kh_ref, v_ref, wc_ref, bonus_ref, g_ref,
                     lg_ref, lb_ref, e_ref, o_ref, h_ref):
    rows, tm = rt_ref.shape[0], rt_ref.shape[1]
    nc = tm // CHUNK
    ng = N_HEADS_C // RWKV_GROUP
    gl = RWKV_GROUP * HEAD_DIM
    units = [(r, g) for r in range(rows) for g in range(ng)]
    nu = len(units)
    t = pl.program_id(1)

    @pl.when(t == 0)
    def _():
        h_ref[...] = jnp.zeros_like(h_ref)

    lane_head = lax.broadcasted_iota(jnp.int32, (1, CHUNK, gl), 2) // HEAD_DIM
    ri = lax.broadcasted_iota(jnp.int32, (1, 2 * CHUNK, gl), 1)
    ci = lax.broadcasted_iota(jnp.int32, (1, 2 * CHUNK, gl), 2) & (CHUNK - 1)
    rr = ri & (CHUNK - 1)
    keep = (rr > ci) | ((ri >= CHUNK) & (rr == ci))
    eye = (lax.broadcasted_iota(jnp.int32, (1, CHUNK, gl), 1)
           == (lax.broadcasted_iota(jnp.int32, (1, CHUNK, gl), 2) & (CHUNK - 1)))
    same_head = (lax.broadcasted_iota(jnp.int32, (1, gl, gl), 1) // HEAD_DIM
                 == lax.broadcasted_iota(jnp.int32, (1, gl, gl), 2) // HEAD_DIM)

    def bdiag(x):
        return jnp.concatenate([jnp.where(lane_head == hh, x, jnp.zeros_like(x)) for hh in range(RWKV_GROUP)],
                               axis=1)

    def chunk_local(cs, out):
        tile = lambda ref: jnp.stack([ref[r, c * CHUNK:(c + 1) * CHUNK, g * gl:(g + 1) * gl]
                                      for c in cs for r, g in units])
        rt, at, bt, kt, bh, kh, v = (tile(r) for r in (rt_ref, at_ref, bt_ref, kt_ref, bh_ref, kh_ref, v_ref))
        ar = jnp.concatenate([at, rt], axis=1)
        sb = jnp.where(keep, _bmm_nt(ar, bdiag(bt)), 0.0)
        sk = jnp.where(keep, _bmm_nt(ar, bdiag(kt)), 0.0).astype(BF16)
        m_rb = sb[:, CHUNK:].astype(BF16)
        yield
        a = sb[:, :CHUNK]
        tinv = jnp.where(eye, 1.0, 0.0) + a
        ab = a.astype(BF16)
        x = _bmm(ab, bdiag(ab))
        yield
        for _ in range(4):
            xb = x.astype(BF16)
            r = _bmm(jnp.concatenate([tinv.astype(BF16), xb], axis=1), bdiag(xb))
            tinv = tinv + r[:, :CHUNK]
            x = r[:, CHUNK:]
            yield
        tinv = (tinv + _bmm(tinv.astype(BF16), bdiag(x.astype(BF16)))).astype(BF16)
        bdv = bdiag(v)
        av = _bmm(sk[:, :CHUNK], bdv)
        yield
        p = _bmm(tinv, bdiag(at)).astype(BF16)
        u0 = _bmm(tinv, bdiag(av.astype(BF16)))
        yield
        q = (rt.astype(F32) + _bmm(m_rb, bdiag(p))).astype(BF16)
        y0 = _bmm(m_rb, bdiag(u0.astype(BF16))) + _bmm(sk[:, CHUNK:], bdv)
        parts = (jnp.concatenate([q, p], axis=1), jnp.concatenate([y0, u0], axis=1),
                 jnp.concatenate([bh, kh], axis=1), v)
        out += [tuple(z[n * nu:(n + 1) * nu] for z in parts) for n in range(len(cs))]
        yield

    wct = [wc_ref[r, 0] for r in range(rows)]
    state = [h_ref[...]]
    ys = []
    local = []

    def recur(c):
        qp, yu0, bk, v = local[c]
        yu = _bmm(qp, state[0].astype(BF16)) + yu0
        ys.append(yu[:, :CHUNK])
        upd = _bmm_tn(bk, jnp.concatenate([yu[:, CHUNK:].astype(BF16), v], axis=1))
        wcol = jnp.stack([wct[r][g * gl:(g + 1) * gl, c:c + 1] for r, g in units])
        state[0] = jnp.where(same_head, upd, 0.0) + wcol * state[0]

    assert sorted(c for cs in RWKV_SCHEDULE for c in cs) == list(range(nc))
    done = 0
    for cs in RWKV_SCHEDULE:
        pending = list(range(done, len(local)))
        slots = {(k + 1) * RWKV_STAGES // (len(pending) + 1): c for k, c in enumerate(pending)}
        for n, _ in enumerate(chunk_local(list(cs), local), start=1):
            if n in slots:
                recur(slots[n])
                done += 1
    for c in range(done, nc):
        recur(c)
    h_ref[...] = state[0]

    y = jnp.concatenate([jnp.concatenate([yc[r * ng + g] for g in range(ng)], axis=-1)
                         for r in range(rows) for yc in ys], axis=0)
    e = e_ref[...]

    def head_mean(z):
        return _dot(z.astype(BF16), e) * (1.0 / HEAD_DIM)

    d = y - head_mean(y)
    yn = d * lax.rsqrt(head_mean(d * d) + LNX_EPS)
    flat = lambda ref: ref[...].astype(F32).reshape(rows * tm, WIDTH_C)
    out = (yn * lg_ref[...] + lb_ref[...] + flat(bonus_ref)) * flat(g_ref)
    o_ref[...] = out.astype(BF16).reshape(rows, tm, WIDTH_C)


def _rwkv_chunk(rt, at, bt, kt, bh, kh, v, wc, bonus, g, lnx_g, lnx_b, e):
    b, t, _ = rt.shape
    tm = RWKV_TILE
    rows = math.gcd(RWKV_ROWS, b)
    gl = RWKV_GROUP * HEAD_DIM
    const = lambda shape: pl.BlockSpec(shape, lambda i, j: (0,) * len(shape))
    tok = lambda: pl.BlockSpec((rows, tm, WIDTH_C), lambda i, j: (i, j, 0))
    return pl.pallas_call(
        _rwkv_chunk_body,
        grid=(b // rows, t // tm),
        in_specs=[tok(), tok(), tok(), tok(), tok(), tok(), tok(),
                  pl.BlockSpec((rows, 1, WIDTH_C, LANES), lambda i, j: (i, j, 0, 0)),
                  tok(), tok(), const(lnx_g.shape), const(lnx_b.shape), const(e.shape)],
        out_specs=tok(),
        out_shape=jax.ShapeDtypeStruct((b, t, WIDTH_C), BF16),
        scratch_shapes=[pltpu.VMEM((rows * (N_HEADS_C // RWKV_GROUP), gl, gl), F32)],
        compiler_params=_cparams(2),
        name="rwkv_chunk",
    )(rt, at, bt, kt, bh, kh, v, wc, bonus, g, lnx_g, lnx_b, e)


def _outproj_body(x_ref, ya_ref, yb_ref, yc_ref, wa_ref, wb_ref, wc_ref, o_ref):
    acc = _dot_tn(ya_ref[0], wa_ref[...])
    acc = acc + _dot_tn(yb_ref[0], wb_ref[...])
    acc = acc + _dot(yc_ref[0], wc_ref[...])
    o_ref[0] = x_ref[0] + acc


def _outproj(x, yaT, ybT, yc, wa, wb, wc):
    b, t, d = x.shape
    tm = TOK_TILE
    const = lambda shape: pl.BlockSpec(shape, lambda i, j: (0,) * len(shape))
    return pl.pallas_call(
        _outproj_body,
        grid=(b, t // tm),
        in_specs=[pl.BlockSpec((1, tm, d), lambda i, j: (i, j, 0)),
                  pl.BlockSpec((1, WIDTH_A, tm), lambda i, j: (i, 0, j)),
                  pl.BlockSpec((1, WIDTH_B, tm), lambda i, j: (i, 0, j)),
                  pl.BlockSpec((1, tm, WIDTH_C), lambda i, j: (i, j, 0)),
                  const(wa.shape), const(wb.shape), const(wc.shape)],
        out_specs=pl.BlockSpec((1, tm, d), lambda i, j: (i, j, 0)),
        out_shape=jax.ShapeDtypeStruct((b, t, d), F32),
        compiler_params=_cparams(2),
        name="outproj",
    )(x, yaT, ybT, yc, wa, wb, wc)


def _ffn_body(x_ref, g_ref, wg_ref, wv_ref, cw_ref, cb_ref, wd_ref, o_ref, gbuf):
    tm = x_ref.shape[1]
    t = pl.program_id(1)

    @pl.when(t == 0)
    def _():
        gbuf[0:8, :] = jnp.zeros((8, D_FF), F32)

    x = x_ref[0]
    ms = jnp.mean(x * x, axis=-1, keepdims=True)
    h = (x * lax.rsqrt(ms + RMS_EPS) * g_ref[...]).astype(BF16)
    acc = x
    for lo, hi in zip(FF_SPLITS[:-1], FF_SPLITS[1:]):
        gate = _dot(h, wg_ref[:, lo:hi])
        val = _dot(h, wv_ref[:, lo:hi])
        gbuf[8:8 + tm, lo:hi] = gate
        g1 = gbuf[7:7 + tm, lo:hi]
        g2 = gbuf[6:6 + tm, lo:hi]
        gbuf[0:8, lo:hi] = gate[tm - 8:tm, :]
        conv = cb_ref[:, lo:hi] + g2 * cw_ref[0:1, lo:hi] + g1 * cw_ref[1:2, lo:hi] + gate * cw_ref[2:3, lo:hi]
        act = (conv * _sigmoid(conv) * val).astype(BF16)
        acc = acc + _dot(act, wd_ref[lo:hi, :])
    o_ref[0] = acc


def _ffn(x, g, wg, wv, cw, cb, wd):
    b, t, d = x.shape
    tm = TOK_TILE
    const = lambda shape: pl.BlockSpec(shape, lambda i, j: (0,) * len(shape),
                                       pipeline_mode=pl.Buffered(1))
    return pl.pallas_call(
        _ffn_body,
        grid=(b, t // tm),
        in_specs=[pl.BlockSpec((1, tm, d), lambda i, j: (i, j, 0)),
                  const(g.shape), const(wg.shape), const(wv.shape), const(cw.shape), const(cb.shape),
                  const(wd.shape)],
        out_specs=pl.BlockSpec((1, tm, d), lambda i, j: (i, j, 0)),
        out_shape=jax.ShapeDtypeStruct((b, t, d), F32),
        scratch_shapes=[pltpu.VMEM((tm + 8, D_FF), F32)],
        compiler_params=_cparams(2),
        name="ffn",
    )(x, g, wg, wv, cw, cb, wd)


def _pad_heads_cols(w, nh):
    d = w.shape[0]
    w = w.reshape(d, nh, HEAD_DIM)
    return jnp.pad(w, ((0, 0), (0, 0), (0, LANES - HEAD_DIM))).reshape(d, nh * LANES)


def _qk_bound(gq, gk):
    scale = HEAD_DIM ** -0.5
    return 1.01 * HEAD_DIM * scale * jnp.max(jnp.abs(gq)) * jnp.max(jnp.abs(gk)) + 0.05


def _layer_params(l, w_in, q_norm_a, k_norm_a, rel_bias, q_norm_b, k_norm_b, forget_bias):
    w = w_in[l]
    scale = HEAD_DIM ** -0.5
    a0, b0 = 0, 3 * WIDTH_A
    c0 = b0 + 3 * WIDTH_B + N_HEADS_B
    qa, ka, va = (w[:, a0 + i * WIDTH_A:a0 + (i + 1) * WIDTH_A] for i in range(3))
    qb, kb, vb = (w[:, b0 + i * WIDTH_B:b0 + (i + 1) * WIDTH_B] for i in range(3))
    fg = w[:, b0 + 3 * WIDTH_B:c0]
    wc = w[:, c0:]
    wn = jnp.concatenate([_pad_heads_cols(kb, N_HEADS_B),
                          jnp.pad(fg, ((0, 0), (0, LANES - N_HEADS_B))), wc], axis=1).astype(BF16)
    pad_v = lambda v, nh: jnp.pad(v.reshape(-1, nh, HEAD_DIM),
                                  ((0, 0), (0, 0), (0, VB_ROWS - HEAD_DIM))).reshape(-1, nh * VB_ROWS)
    assert QB_ROWS == VB_ROWS
    wt = jnp.concatenate([qa, ka, pad_v(va, N_HEADS_A), pad_v(qb, N_HEADS_B), pad_v(vb, N_HEADS_B),
                          jnp.pad(fg, ((0, 0), (0, 16 - N_HEADS_B)))], axis=1).T.astype(BF16)
    pad = LANES - HEAD_DIM
    gka = k_norm_a[l].reshape(HEAD_DIM, 1)
    gkb = jnp.pad(k_norm_b[l], (0, pad)).reshape(1, LANES)
    gqa = (q_norm_a[l] * (scale * LOG2E)).reshape(HEAD_DIM, 1)
    gqb = jnp.pad(q_norm_b[l] * (scale * LOG2E), (0, QB_ROWS - HEAD_DIM)).reshape(QB_ROWS, 1)
    fbn = jnp.pad(forget_bias[l], (0, LANES - N_HEADS_B)).reshape(1, LANES)
    fbt = jnp.pad(forget_bias[l], (0, 16 - N_HEADS_B)).reshape(16, 1)
    shift_b = (LOG2E * _qk_bound(q_norm_b[l], k_norm_b[l])).reshape(1, 1)
    bound_a = _qk_bound(q_norm_a[l], k_norm_a[l])
    tab = rel_bias[l]
    shift_a = LOG2E * (bound_a + jnp.max(tab, axis=1))
    depth_a = jnp.max(shift_a + LOG2E * (bound_a - tab[:, MAX_REL]))
    return wn, wt, gka, gkb, gqa, gqb, fbn, fbt, shift_b, shift_a, depth_a


def _selectors():
    import numpy as np
    selk = np.zeros((3, LANES, N_HEADS_B * LANES), np.float32)
    selq = np.zeros((3, N_HEADS_B * QB_ROWS, 16), np.float32)
    for p in range(3):
        for hh in range(N_HEADS_B):
            selk[p, hh, hh * LANES + HEAD_DIM + 3 + p] = -1.0
            selq[p, hh * QB_ROWS + HEAD_DIM + p, hh] = 1.0
    e = np.kron(np.eye(N_HEADS_C, dtype=np.float32), np.ones((HEAD_DIM, HEAD_DIM), np.float32))
    return jnp.asarray(selk, BF16), jnp.asarray(selq, BF16), jnp.asarray(e, BF16)


def kernel(x, mix_norm_g, w_in, q_norm_a, k_norm_a, rel_bias, q_norm_b, k_norm_b, forget_bias, shift_mu, w0, w2,
           a0, a2, g2, k_k, k_a, r_k, lnx_g, lnx_b, w_out, ffn_norm_g, w_up, conv_w, conv_b, w_down):
    depth = w_in.shape[0]
    b, t, d = x.shape
    selk, selq, e = _selectors()
    row = lambda v: v.reshape(1, -1)
    for l in range(depth):
        wn, wt, gka, gkb, gqa, gqb, fbn, fbt, shift_b, shift_a, depth_a = _layer_params(
            l, w_in, q_norm_a, k_norm_a, rel_bias, q_norm_b, k_norm_b, forget_bias)
        rwkv_params = (row(shift_mu[l]), row(w0[l]), w2[l].astype(BF16), row(a0[l]), a2[l].astype(BF16),
                       g2[l].astype(BF16), row(k_k[l]), row(k_a[l]), row(r_k[l]), e)
        (qaT, vaT, kA, qbT, vbT, kB, rt, at, bt, kt, bh, kh, v, wcum, bonus, g) = _inproj(
            x, row(mix_norm_g[l]), wn, wt, gka, gkb, gqa, gqb, fbn, fbt, selk, selq, shift_b, rwkv_params)
        bias = _relbias(rel_bias[l], shift_a)
        yaT = lax.cond(depth_a <= 2 * FAST_MAX_SHIFT,
                       functools.partial(_attn_a, online=False), functools.partial(_attn_a, online=True),
                       qaT, kA, vaT, bias)
        ybT = lax.cond(shift_b[0, 0] <= FAST_MAX_SHIFT,
                       functools.partial(_fox, online=False), functools.partial(_fox, online=True),
                       qbT, kB, vbT)
        yc = _rwkv_chunk(rt, at, bt, kt, bh, kh, v, wcum, bonus, g, row(lnx_g[l]), row(lnx_b[l]), e)
        wo = w_out[l].astype(BF16)
        x = _outproj(x, yaT.reshape(b, WIDTH_A, t), ybT.reshape(b, WIDTH_B, t), yc,
                     wo[:WIDTH_A], wo[WIDTH_A:WIDTH_A + WIDTH_B], wo[WIDTH_A + WIDTH_B:])
        wu = w_up[l].astype(BF16)
        x = _ffn(x, row(ffn_norm_g[l]), wu[:, :D_FF], wu[:, D_FF:], conv_w[l], row(conv_b[l]),
                 w_down[l].astype(BF16))
    return x
```

```python
import functools
import math

import jax
import jax.numpy as jnp
from jax import lax
from jax.experimental import pallas as pl
from jax.experimental.pallas import tpu as pltpu

F32 = jnp.float32
BF16 = jnp.bfloat16

D_MODEL = 1024
HEAD_DIM = 64
CHUNK = 64
LEFT_CHUNKS = 8
MAX_REL = 128
N_HEADS_A = 4
N_HEADS_B = 4
N_HEADS_C = 8
WIDTH_A = N_HEADS_A * HEAD_DIM
WIDTH_B = N_HEADS_B * HEAD_DIM
WIDTH_C = N_HEADS_C * HEAD_DIM
DECAY_LORA = 64
AAA_LORA = 64
GATE_LORA = 128
COLS_C = 3 * WIDTH_C + DECAY_LORA + AAA_LORA + GATE_LORA
D_FF = 2816
RMS_EPS = 1e-6
LNX_EPS = 64e-5
NEG_INF = -1e30

LANES = 128
TOK_TILE = 512
ATT_A_TILE = 256
ATT_A_WIN = 3 * ATT_A_TILE
FOX_TILE = 512
FOX_GROUP = 8
RWKV_GROUP = 4
RWKV_TILE = 256
RWKV_ROWS = 4
RWKV_SCHEDULE = ((0, 1), (2, 3))
RWKV_STAGES = 9
FF_SPLITS = (0, 768, 1536, 2304, 2816)
VMEM_LIMIT = 56 * 1024 * 1024

PN_KB, PN_F, PN_C = 0, 512, 640
PN_COLS = PN_C + COLS_C
VB_ROWS = 80
PT_QA = 0
PT_KA = PT_QA + WIDTH_A
PT_VA = PT_KA + WIDTH_A
PT_QB = PT_VA + N_HEADS_A * VB_ROWS
QB_ROWS = 80
PT_VB = PT_QB + N_HEADS_B * QB_ROWS
PT_F = PT_VB + N_HEADS_B * VB_ROWS
PT_ROWS = PT_F + 16
LOG2E = 1.4426950408889634
FAST_MAX_SHIFT = 40.0


def _cparams(n_axes):
    return pltpu.CompilerParams(dimension_semantics=("arbitrary",) * n_axes,
                                vmem_limit_bytes=VMEM_LIMIT)


def _split3(x):
    hi = x.astype(BF16)
    r1 = x - hi.astype(F32)
    mid = r1.astype(BF16)
    lo = (r1 - mid.astype(F32)).astype(BF16)
    return hi, mid, lo


def _split2(x):
    hi = x.astype(BF16)
    lo = (x - hi.astype(F32)).astype(BF16)
    return hi, lo


def _dot(a, b):
    return jnp.dot(a, b, preferred_element_type=F32)


def _dot_nt(a, b):
    return lax.dot_general(a, b, (((1,), (1,)), ((), ())), preferred_element_type=F32)


def _dot_tn(a, b):
    return lax.dot_general(a, b, (((0,), (0,)), ((), ())), preferred_element_type=F32)


def _log_sigmoid(x):
    return jnp.minimum(x, 0.0) - jnp.log(1.0 + jnp.exp(-jnp.abs(x)))


def _softplus(x):
    return jnp.maximum(x, 0.0) + jnp.log(1.0 + jnp.exp(-jnp.abs(x)))


def _sigmoid(x):
    return 1.0 / (1.0 + jnp.exp(-x))


def _inproj_body(x_ref, g_ref, wn_ref, wt_ref, gka_ref, gkb_ref, gqa_ref, gqb_ref, fbn_ref, fbt_ref,
                 selk_ref, selq_ref, shift_ref, *rest):
    rwkv_params, rest = rest[:10], rest[10:]
    qaT_ref, vaT_ref, kA_ref, qbT_ref, vbT_ref, kB_ref = rest[:6]
    rwkv_outs = rest[6:16]
    carry_n, carry_t, sbuf = rest[16:]
    tm = x_ref.shape[1]
    t = pl.program_id(1)

    @pl.when(t == 0)
    def _():
        carry_n[...] = jnp.zeros_like(carry_n)
        carry_t[...] = jnp.zeros_like(carry_t)

    x = x_ref[0]
    ms = jnp.mean(x * x, axis=-1, keepdims=True)
    h = (x * lax.rsqrt(ms + RMS_EPS) * g_ref[...]).astype(BF16)
    rows_t = lambda start, n: _dot_nt(wt_ref[start:start + n, :], h)
    cols_n = lambda start, n: _dot(h, wn_ref[:, start:start + n])
    row_v = lax.broadcasted_iota(jnp.int32, (VB_ROWS, 1), 0)
    ones_v = jnp.where(row_v == HEAD_DIM, 1.0, 0.0)
    val = {}

    def norm_rows(p, gain_ref):
        p = p.reshape(N_HEADS_A, HEAD_DIM, tm)
        return (p * lax.rsqrt(jnp.mean(p * p, axis=1, keepdims=True) + RMS_EPS) * gain_ref[...][None]).astype(BF16)

    def gates_epilogue(p):
        pn_f, pt_f = p
        lane_n = lax.broadcasted_iota(jnp.int32, (1, LANES), 1)
        lf_n = jnp.where(lane_n < N_HEADS_B, LOG2E * _log_sigmoid(pn_f + fbn_ref[...]), 0.0)
        row_t = lax.broadcasted_iota(jnp.int32, (16, 1), 0)
        lf_t = jnp.where(row_t < N_HEADS_B, LOG2E * _log_sigmoid(pt_f + fbt_ref[...]), 0.0)
        ri = lax.broadcasted_iota(jnp.int32, (tm, tm), 0)
        ci = lax.broadcasted_iota(jnp.int32, (tm, tm), 1)
        low = (ci <= ri).astype(BF16)
        upp = (ri <= ci).astype(BF16)
        hn, mn, ln = _split3(lf_n)
        val["c_n"] = _dot(low, hn) + _dot(low, mn) + _dot(low, ln) + carry_n[0:1, :]
        ht, mt, lt = _split3(lf_t)
        val["c_t"] = _dot(ht, upp) + _dot(mt, upp) + _dot(lt, upp) + carry_t[:, 0:1]
        carry_n[...] = carry_n[...] + jnp.sum(lf_n, axis=0, keepdims=True)
        carry_t[...] = carry_t[...] + jnp.sum(lf_t, axis=1, keepdims=True)

    def kb_epilogue(pn_kb):
        chn, cmn, cln = _split3(val["c_n"])
        kaug = _dot(chn, selk_ref[0]) + _dot(cmn, selk_ref[1]) + _dot(cln, selk_ref[2])
        lane_k = lax.broadcasted_iota(jnp.int32, (1, LANES), 1)
        ones_k = jnp.where((lane_k >= HEAD_DIM) & (lane_k < HEAD_DIM + 3), 1.0, 0.0)
        for hh in range(N_HEADS_B):
            k = pn_kb[:, LANES * hh:LANES * (hh + 1)]
            msk = jnp.sum(k * k, axis=-1, keepdims=True) * (1.0 / HEAD_DIM)
            kn = k * lax.rsqrt(msk + RMS_EPS) * gkb_ref[...]
            kB_ref[0, hh] = (kn + kaug[:, LANES * hh:LANES * (hh + 1)] + ones_k).astype(BF16)

    def qb_epilogue(pt_qb):
        cht, cmt, clt = _split3(val["c_t"] - shift_ref[...])
        qaug = _dot(selq_ref[0], cht) + _dot(selq_ref[1], cmt) + _dot(selq_ref[2], clt)
        qb = pt_qb.reshape(N_HEADS_B, QB_ROWS, tm)
        msq = jnp.sum(qb * qb, axis=1, keepdims=True) * (1.0 / HEAD_DIM)
        row_q = lax.broadcasted_iota(jnp.int32, (QB_ROWS, 1), 0)
        ones_q = jnp.where((row_q >= HEAD_DIM + 3) & (row_q < HEAD_DIM + 6), 1.0, 0.0)
        qn = qb * lax.rsqrt(msq + RMS_EPS) * gqb_ref[...][None]
        qbT_ref[0] = (qn + qaug.reshape(N_HEADS_B, QB_ROWS, tm) + ones_q[None]).astype(BF16)

    def store(ref, value):
        ref[...] = value

    sections = [
        (lambda: rows_t(PT_QA, WIDTH_A), lambda p: store(qaT_ref, norm_rows(p, gqa_ref)[None])),
        (lambda: rows_t(PT_KA, WIDTH_A), lambda p: store(kA_ref, norm_rows(p, gka_ref)[None])),
        (lambda: rows_t(PT_VA, N_HEADS_A * VB_ROWS),
         lambda p: store(vaT_ref, (p.reshape(N_HEADS_A, VB_ROWS, tm) + ones_v[None]).astype(BF16)[None])),
        (lambda: (cols_n(PN_F, LANES), rows_t(PT_F, 16)), gates_epilogue),
        (lambda: rows_t(PT_VB, N_HEADS_B * VB_ROWS),
         lambda p: store(vbT_ref, (p.reshape(N_HEADS_B, VB_ROWS, tm) + ones_v[None]).astype(BF16)[None, :, None])),
        (lambda: cols_n(PN_KB, N_HEADS_B * LANES), kb_epilogue),
        (lambda: rows_t(PT_QB, N_HEADS_B * QB_ROWS), qb_epilogue),
    ]
    slots = _rwkv_prep(lambda start, n: cols_n(PN_C + start, n), tm, t, *rwkv_params, *rwkv_outs, sbuf)
    pending = None
    for mm, epilogue in sections:
        p = mm()
        if pending is not None:
            pending[1](pending[0])
        pending = (p, epilogue)
        next(slots, None)
    pending[1](pending[0])
    for _ in slots:
        pass


def _inproj(x, g, wn, wt, gka, gkb, gqa, gqb, fbn, fbt, selk, selq, shift, rwkv_params):
    b, t, d = x.shape
    tm = TOK_TILE
    nt = t // tm
    const = lambda shape: pl.BlockSpec(shape, lambda i, j: (0,) * len(shape))
    tok = lambda: pl.BlockSpec((1, tm, WIDTH_C), lambda i, j: (i, j, 0))
    bf = jax.ShapeDtypeStruct((b, t, WIDTH_C), BF16)
    out_shape = (
        jax.ShapeDtypeStruct((b, N_HEADS_A, HEAD_DIM, t), BF16),
        jax.ShapeDtypeStruct((b, N_HEADS_A, VB_ROWS, t), BF16),
        jax.ShapeDtypeStruct((b, N_HEADS_A, HEAD_DIM, t), BF16),
        jax.ShapeDtypeStruct((b, N_HEADS_B, QB_ROWS, t), BF16),
        jax.ShapeDtypeStruct((b, N_HEADS_B, nt, VB_ROWS, tm), BF16),
        jax.ShapeDtypeStruct((b, N_HEADS_B, t, LANES), BF16),
        bf, bf, bf, bf, bf, bf, bf,
        jax.ShapeDtypeStruct((b, t // RWKV_TILE, WIDTH_C, LANES), F32),
        bf, bf,
    )
    out_specs = (
        pl.BlockSpec((1, N_HEADS_A, HEAD_DIM, tm), lambda i, j: (i, 0, 0, j)),
        pl.BlockSpec((1, N_HEADS_A, VB_ROWS, tm), lambda i, j: (i, 0, 0, j)),
        pl.BlockSpec((1, N_HEADS_A, HEAD_DIM, tm), lambda i, j: (i, 0, 0, j)),
        pl.BlockSpec((1, N_HEADS_B, QB_ROWS, tm), lambda i, j: (i, 0, 0, j)),
        pl.BlockSpec((1, N_HEADS_B, 1, VB_ROWS, tm), lambda i, j: (i, 0, j, 0, 0)),
        pl.BlockSpec((1, N_HEADS_B, tm, LANES), lambda i, j: (i, 0, j, 0)),
        tok(), tok(), tok(), tok(), tok(), tok(), tok(),
        pl.BlockSpec((1, tm // RWKV_TILE, WIDTH_C, LANES), lambda i, j: (i, j, 0, 0)),
        tok(), tok(),
    )
    in_specs = [
        pl.BlockSpec((1, tm, d), lambda i, j: (i, j, 0)),
        const((1, d)), const(wn.shape), const(wt.shape),
        const(gka.shape), const(gkb.shape), const(gqa.shape), const(gqb.shape),
        const(fbn.shape), const(fbt.shape), const(selk.shape), const(selq.shape), const(shift.shape),
    ] + [const(p.shape) for p in rwkv_params]
    return pl.pallas_call(
        _inproj_body,
        grid=(b, nt),
        in_specs=in_specs,
        out_specs=out_specs,
        out_shape=out_shape,
        scratch_shapes=[pltpu.VMEM((8, LANES), F32), pltpu.VMEM((16, LANES), F32),
                        pltpu.VMEM((tm + 8, COLS_C), F32)],
        compiler_params=_cparams(2),
        name="inproj",
    )(x, g, wn, wt, gka, gkb, gqa, gqb, fbn, fbt, selk, selq, shift, *rwkv_params)


def _relbias_body(tab_ref, shift_ref, o_ref):
    hh = pl.program_id(0)
    kj = lax.broadcasted_iota(jnp.int32, (ATT_A_WIN, ATT_A_TILE), 0)
    qi = lax.broadcasted_iota(jnp.int32, (ATT_A_WIN, ATT_A_TILE), 1)
    first = LEFT_CHUNKS * CHUNK - MAX_REL
    kj_v = lax.broadcasted_iota(jnp.int32, (ATT_A_WIN - first, ATT_A_TILE), 0) + first
    qi_v = lax.broadcasted_iota(jnp.int32, (ATT_A_WIN - first, ATT_A_TILE), 1)
    rel = jnp.clip(kj_v - LEFT_CHUNKS * CHUNK - qi_v, -MAX_REL, MAX_REL) + MAX_REL

    def body(r, acc):
        return jnp.where(rel == r, tab_ref[hh, r], acc)

    varying = lax.fori_loop(0, 2 * MAX_REL + 1, body, jnp.zeros((ATT_A_WIN - first, ATT_A_TILE), F32))
    bias = jnp.concatenate([jnp.full((first, ATT_A_TILE), tab_ref[hh, 0], F32), varying], axis=0)
    kc = kj // CHUNK
    qc = qi // CHUNK
    band = (kc >= qc) & (kc <= qc + LEFT_CHUNKS)
    o_ref[0] = jnp.where(band, LOG2E * bias - shift_ref[hh], NEG_INF)


def _relbias(tab, shift):
    return pl.pallas_call(
        _relbias_body,
        grid=(N_HEADS_A,),
        in_specs=[pl.BlockSpec(memory_space=pltpu.SMEM), pl.BlockSpec(memory_space=pltpu.SMEM)],
        out_specs=pl.BlockSpec((1, ATT_A_WIN, ATT_A_TILE), lambda i: (i, 0, 0)),
        out_shape=jax.ShapeDtypeStruct((N_HEADS_A, ATT_A_WIN, ATT_A_TILE), F32),
        compiler_params=_cparams(1),
        name="relbias",
    )(tab, shift)


def _attn_a_body(q_ref, k0_ref, k1_ref, k2_ref, v0_ref, v1_ref, v2_ref, bias_ref, o_ref, *, online):
    i = pl.program_id(1)
    tq = ATT_A_TILE
    kj = lax.broadcasted_iota(jnp.int32, (tq, 1), 0)

    def run(mask_padding):
        scores = []
        for hh in range(N_HEADS_A):
            q = q_ref[0, hh]
            ss = []
            for d, k_ref in enumerate((k0_ref, k1_ref, k2_ref)):
                s = _dot_tn(k_ref[0, hh], q) + bias_ref[hh, d * tq:(d + 1) * tq, :]
                if mask_padding:
                    s = jnp.where(kj + (i - 2 + d) * tq >= 0, s, NEG_INF)
                ss.append(s)
            scores.append(ss)
        for hh, ss in enumerate(scores):
            if online:
                m = functools.reduce(jnp.maximum, [jnp.max(s, axis=0, keepdims=True) for s in ss])
                ss = [s - m for s in ss]
            acc = jnp.zeros((VB_ROWS, tq), F32)
            for s, v_ref in zip(ss, (v0_ref, v1_ref, v2_ref)):
                acc = acc + _dot(v_ref[0, hh], jnp.exp2(s).astype(BF16))
            o_ref[0, hh] = (acc[0:HEAD_DIM] / acc[HEAD_DIM:HEAD_DIM + 1]).astype(BF16)

    pl.when(i < 2)(functools.partial(run, True))
    pl.when(i >= 2)(functools.partial(run, False))


def _attn_a(qaT, kA, vaT, bias, online):
    b, nh, _, t = qaT.shape
    tq = ATT_A_TILE
    kspec = lambda d: pl.BlockSpec((1, nh, HEAD_DIM, tq),
                                   lambda bb, i: (bb, 0, 0, jnp.maximum(i - 2 + d, 0)))
    vspec = lambda d: pl.BlockSpec((1, nh, VB_ROWS, tq),
                                   lambda bb, i: (bb, 0, 0, jnp.maximum(i - 2 + d, 0)))
    return pl.pallas_call(
        functools.partial(_attn_a_body, online=online),
        grid=(b, t // tq),
        in_specs=[pl.BlockSpec((1, nh, HEAD_DIM, tq), lambda bb, i: (bb, 0, 0, i)),
                  kspec(0), kspec(1), kspec(2), vspec(0), vspec(1), vspec(2),
                  pl.BlockSpec((nh, ATT_A_WIN, tq), lambda bb, i: (0, 0, 0))],
        out_specs=pl.BlockSpec((1, nh, HEAD_DIM, tq), lambda bb, i: (bb, 0, 0, i)),
        out_shape=jax.ShapeDtypeStruct((b, nh, HEAD_DIM, t), BF16),
        compiler_params=_cparams(2),
        name="attn_a_online" if online else "attn_a",
    )(qaT, kA, kA, kA, vaT, vaT, vaT, bias)


def _fox_body(q_ref, k_ref, v_ref, o_ref, *, online):
    i = pl.program_id(2)
    tq = FOX_TILE
    q = jnp.concatenate([q_ref[0, 0], jnp.zeros((LANES - QB_ROWS, tq), BF16)], axis=0)

    def scores(j):
        ks = k_ref[0, 0, pl.ds(pl.multiple_of(j * tq, tq), tq), :]
        return _dot(ks, q)

    def accumulate(j, s, m, acc, diagonal):
        vs = v_ref[0, 0, j]
        if diagonal:
            kj = lax.broadcasted_iota(jnp.int32, (tq, tq), 0)
            qi = lax.broadcasted_iota(jnp.int32, (tq, tq), 1)
            causal = kj <= qi
        if online:
            if diagonal:
                s = jnp.where(causal, s, NEG_INF)
            m_new = jnp.maximum(m, jnp.max(s, axis=0, keepdims=True))
            acc = jnp.exp2(m - m_new) * acc
            p = jnp.exp2(s - m_new)
            m = m_new
        else:
            p = jnp.exp2(s)
            if diagonal:
                p = jnp.where(causal, p, 0.0)
        return m, acc + _dot(vs, p.astype(BF16))

    def tiles(js, carry, last_diagonal):
        ss = [scores(j) for j in js]
        m, acc = carry
        for n, (j, s) in enumerate(zip(js, ss)):
            m, acc = accumulate(j, s, m, acc, last_diagonal and n == len(js) - 1)
        return m, acc

    init = (jnp.full((1, tq), NEG_INF, F32), jnp.zeros((VB_ROWS, tq), F32))
    if online:
        carry = lax.fori_loop(0, i, lambda j, c: tiles([j], c, False), init)
        _, acc = tiles([i], carry, True)
    else:
        g = FOX_GROUP
        carry = lax.fori_loop(0, i // g, lambda jj, c: tiles([g * jj + n for n in range(g)], c, False), init)
        tails = [functools.partial(lambda c, r: tiles([i - r + n for n in range(r + 1)], c, True), r=r)
                 for r in range(g)]
        _, acc = lax.switch(i % g, tails, carry)
    o_ref[0, 0] = (acc[0:HEAD_DIM] / acc[HEAD_DIM:HEAD_DIM + 1]).astype(BF16)


def _fox(qbT, kB, vbT, online):
    b, nh, _, t = qbT.shape
    tq = FOX_TILE
    nk = t // tq
    return pl.pallas_call(
        functools.partial(_fox_body, online=online),
        grid=(b, nh, nk),
        in_specs=[pl.BlockSpec((1, 1, QB_ROWS, tq), lambda bb, hh, i: (bb, hh, 0, i)),
                  pl.BlockSpec((1, 1, t, LANES), lambda bb, hh, i: (bb, hh, 0, 0)),
                  pl.BlockSpec((1, 1, nk, VB_ROWS, tq), lambda bb, hh, i: (bb, hh, 0, 0, 0))],
        out_specs=pl.BlockSpec((1, 1, HEAD_DIM, tq), lambda bb, hh, i: (bb, hh, 0, i)),
        out_shape=jax.ShapeDtypeStruct((b, nh, HEAD_DIM, t), BF16),
        compiler_params=_cparams(3),
        name="fox_online" if online else "fox",
    )(qbT, kB, vbT)


def _rwkv_prep(proj, tm, t, mu_ref, w0_ref, w2_ref, a0_ref, a2_ref, g2_ref, kk_ref, ka_ref, rk_ref, e_ref,
               rt_ref, at_ref, bt_ref, kt_ref, bh_ref, kh_ref, v_ref, wc_ref, bonus_ref, g_ref, sbuf):
    @pl.when(t == 0)
    def _():
        sbuf[0:8, :] = jnp.zeros((8, COLS_C), F32)

    def shifted(p, start):
        cols = slice(start, start + p.shape[1])
        sbuf[8:8 + tm, cols] = p
        prev = sbuf[7:7 + tm, cols]
        sbuf[0:8, cols] = p[tm - 8:tm, :]
        return p + (prev - p) * mu_ref[:, cols]

    c = WIDTH_C
    p_lo = proj(3 * c, COLS_C - 3 * c)
    p_k = proj(c, c)
    u_lo = shifted(p_lo, 3 * c)
    yield
    w_lo = u_lo[:, 0:DECAY_LORA]
    a_lo = u_lo[:, DECAY_LORA:DECAY_LORA + AAA_LORA]
    g_lo = u_lo[:, DECAY_LORA + AAA_LORA:]
    w = w0_ref[...] + _dot(jnp.tanh(w_lo).astype(BF16), w2_ref[...])
    w = -_softplus(-w) - 0.5
    ld = -jnp.exp(w)
    p_r = proj(0, c)
    yield
    a = _sigmoid(a0_ref[...] + _dot(a_lo.astype(BF16), a2_ref[...]))
    g_ref[0] = _dot(_sigmoid(g_lo).astype(BF16), g2_ref[...]).astype(BF16)
    k = shifted(p_k, c)
    p_v = proj(2 * c, c)
    yield

    e = e_ref[...]
    kk = k * kk_ref[...]
    nrm2 = _dot((kk * kk).astype(BF16), e)
    kkn = kk * lax.rsqrt(jnp.maximum(nrm2, 1e-24))
    k2 = k * (1.0 + (a - 1.0) * ka_ref[...])
    kka = kkn * a
    r = shifted(p_r, 0)
    v = shifted(p_v, 2 * c)
    v_ref[0] = v.astype(BF16)
    b_hi, b_lo = _split2(r * k2 * rk_ref[...])
    bonus_ref[0] = ((_dot(b_hi, e) + _dot(b_lo, e)) * v).astype(BF16)
    yield

    ri = lax.broadcasted_iota(jnp.int32, (2 * CHUNK, CHUNK), 0)
    ci = lax.broadcasted_iota(jnp.int32, (2 * CHUNK, CHUNK), 1)
    tri = ((ci <= ri) | (ri >= CHUNK)).astype(BF16)
    ld_hi, ld_lo = _split2(ld)
    for cc in range(tm // CHUNK):
        sl = slice(cc * CHUNK, (cc + 1) * CHUNK)
        cum = _dot(tri, ld_hi[sl]) + _dot(tri, ld_lo[sl])
        lc = cum[0:CHUNK]
        tot = cum[CHUNK:2 * CHUNK]
        e_neg = jnp.exp(-lc)
        e_rem = jnp.exp(tot - lc)
        rt_ref[0, sl, :] = (r[sl] * jnp.exp(lc)).astype(BF16)
        at_ref[0, sl, :] = (-kkn[sl] * jnp.exp(lc - ld[sl])).astype(BF16)
        bt_ref[0, sl, :] = (kka[sl] * e_neg).astype(BF16)
        kt_ref[0, sl, :] = (k2[sl] * e_neg).astype(BF16)
        bh_ref[0, sl, :] = (kka[sl] * e_rem).astype(BF16)
        kh_ref[0, sl, :] = (k2[sl] * e_rem).astype(BF16)
        if cc % 2 == 1:
            yield
    seg = (lax.broadcasted_iota(jnp.int32, (16, tm), 1) // CHUNK
           == lax.broadcasted_iota(jnp.int32, (16, tm), 0)).astype(BF16)
    tots3 = jnp.concatenate(_split3(_dot(seg, ld_hi) + _dot(seg, ld_lo)), axis=0)
    per_blk = RWKV_TILE // CHUNK
    for blk in range(tm // RWKV_TILE):
        put = (lax.broadcasted_iota(jnp.int32, (48, LANES), 0) % 16
               == lax.broadcasted_iota(jnp.int32, (48, LANES), 1) + blk * per_blk).astype(BF16)
        wc_ref[0, blk] = jnp.exp(_dot_tn(tots3, put))


def _bmm(a, b):
    return lax.dot_general(a, b, (((2,), (1,)), ((0,), (0,))), preferred_element_type=F32)


def _bmm_nt(a, b):
    return lax.dot_general(a, b, (((2,), (2,)), ((0,), (0,))), preferred_element_type=F32)


def _bmm_tn(a, b):
    return lax.dot_general(a, b, (((1,), (1,)), ((0,), (0,))), preferred_element_type=F32)


def _rwkv_chunk_body(rt_ref, at_ref, bt_ref, kt_ref, bh_ref, kh_ref, v_ref, wc_ref, bonus_ref, g_ref,
                     lg_ref, lb_ref, e_ref, o_ref, h_ref):
    rows, tm = rt_ref.shape[0], rt_ref.shape[1]
    nc = tm // CHUNK
    ng = N_HEADS_C // RWKV_GROUP
    gl = RWKV_GROUP * HEAD_DIM
    units = [(r, g) for r in range(rows) for g in range(ng)]
    nu = len(units)
    t = pl.program_id(1)

    @pl.when(t == 0)
    def _():
        h_ref[...] = jnp.zeros_like(h_ref)

    lane_head = lax.broadcasted_iota(jnp.int32, (1, CHUNK, gl), 2) // HEAD_DIM
    ri = lax.broadcasted_iota(jnp.int32, (1, 2 * CHUNK, gl), 1)
    ci = lax.broadcasted_iota(jnp.int32, (1, 2 * CHUNK, gl), 2) & (CHUNK - 1)
    rr = ri & (CHUNK - 1)
    keep = (rr > ci) | ((ri >= CHUNK) & (rr == ci))
    eye = (lax.broadcasted_iota(jnp.int32, (1, CHUNK, gl), 1)
           == (lax.broadcasted_iota(jnp.int32, (1, CHUNK, gl), 2) & (CHUNK - 1)))
    same_head = (lax.broadcasted_iota(jnp.int32, (1, gl, gl), 1) // HEAD_DIM
                 == lax.broadcasted_iota(jnp.int32, (1, gl, gl), 2) // HEAD_DIM)

    def bdiag(x):
        return jnp.concatenate([jnp.where(lane_head == hh, x, jnp.zeros_like(x)) for hh in range(RWKV_GROUP)],
                               axis=1)

    def chunk_local(cs, out):
        tile = lambda ref: jnp.stack([ref[r, c * CHUNK:(c + 1) * CHUNK, g * gl:(g + 1) * gl]
                                      for c in cs for r, g in units])
        rt, at, bt, kt, bh, kh, v = (tile(r) for r in (rt_ref, at_ref, bt_ref, kt_ref, bh_ref, kh_ref, v_ref))
        ar = jnp.concatenate([at, rt], axis=1)
        sb = jnp.where(keep, _bmm_nt(ar, bdiag(bt)), 0.0)
        sk = jnp.where(keep, _bmm_nt(ar, bdiag(kt)), 0.0).astype(BF16)
        m_rb = sb[:, CHUNK:].astype(BF16)
        yield
        a = sb[:, :CHUNK]
        tinv = jnp.where(eye, 1.0, 0.0) + a
        ab = a.astype(BF16)
        x = _bmm(ab, bdiag(ab))
        yield
        for _ in range(4):
            xb = x.astype(BF16)
            r = _bmm(jnp.concatenate([tinv.astype(BF16), xb], axis=1), bdiag(xb))
            tinv = tinv + r[:, :CHUNK]
            x = r[:, CHUNK:]
            yield
        tinv = (tinv + _bmm(tinv.astype(BF16), bdiag(x.astype(BF16)))).astype(BF16)
        bdv = bdiag(v)
        av = _bmm(sk[:, :CHUNK], bdv)
        yield
        p = _bmm(tinv, bdiag(at)).astype(BF16)
        u0 = _bmm(tinv, bdiag(av.astype(BF16)))
        yield
        q = (rt.astype(F32) + _bmm(m_rb, bdiag(p))).astype(BF16)
        y0 = _bmm(m_rb, bdiag(u0.astype(BF16))) + _bmm(sk[:, CHUNK:], bdv)
        parts = (jnp.concatenate([q, p], axis=1), jnp.concatenate([y0, u0], axis=1),
                 jnp.concatenate([bh, kh], axis=1), v)
        out += [tuple(z[n * nu:(n + 1) * nu] for z in parts) for n in range(len(cs))]
        yield

    wct = [wc_ref[r, 0] for r in range(rows)]
    state = [h_ref[...]]
    ys = []
    local = []

    def recur(c):
        qp, yu0, bk, v = local[c]
        yu = _bmm(qp, state[0].astype(BF16)) + yu0
        ys.append(yu[:, :CHUNK])
        upd = _bmm_tn(bk, jnp.concatenate([yu[:, CHUNK:].astype(BF16), v], axis=1))
        wcol = jnp.stack([wct[r][g * gl:(g + 1) * gl, c:c + 1] for r, g in units])
        state[0] = jnp.where(same_head, upd, 0.0) + wcol * state[0]

    assert sorted(c for cs in RWKV_SCHEDULE for c in cs) == list(range(nc))
    done = 0
    for cs in RWKV_SCHEDULE:
        pending = list(range(done, len(local)))
        slots = {(k + 1) * RWKV_STAGES // (len(pending) + 1): c for k, c in enumerate(pending)}
        for n, _ in enumerate(chunk_local(list(cs), local), start=1):
            if n in slots:
                recur(slots[n])
                done += 1
    for c in range(done, nc):
        recur(c)
    h_ref[...] = state[0]

    y = jnp.concatenate([jnp.concatenate([yc[r * ng + g] for g in range(ng)], axis=-1)
                         for r in range(rows) for yc in ys], axis=0)
    e = e_ref[...]

    def head_mean(z):
        return _dot(z.astype(BF16), e) * (1.0 / HEAD_DIM)

    d = y - head_mean(y)
    yn = d * lax.rsqrt(head_mean(d * d) + LNX_EPS)
    flat = lambda ref: ref[...].astype(F32).reshape(rows * tm, WIDTH_C)
    out = (yn * lg_ref[...] + lb_ref[...] + flat(bonus_ref)) * flat(g_ref)
    o_ref[...] = out.astype(BF16).reshape(rows, tm, WIDTH_C)


def _rwkv_chunk(rt, at, bt, kt, bh, kh, v, wc, bonus, g, lnx_g, lnx_b, e):
    b, t, _ = rt.shape
    tm = RWKV_TILE
    rows = math.gcd(RWKV_ROWS, b)
    gl = RWKV_GROUP * HEAD_DIM
    const = lambda shape: pl.BlockSpec(shape, lambda i, j: (0,) * len(shape))
    tok = lambda: pl.BlockSpec((rows, tm, WIDTH_C), lambda i, j: (i, j, 0))
    return pl.pallas_call(
        _rwkv_chunk_body,
        grid=(b // rows, t // tm),
        in_specs=[tok(), tok(), tok(), tok(), tok(), tok(), tok(),
                  pl.BlockSpec((rows, 1, WIDTH_C, LANES), lambda i, j: (i, j, 0, 0)),
                  tok(), tok(), const(lnx_g.shape), const(lnx_b.shape), const(e.shape)],
        out_specs=tok(),
        out_shape=jax.ShapeDtypeStruct((b, t, WIDTH_C), BF16),
        scratch_shapes=[pltpu.VMEM((rows * (N_HEADS_C // RWKV_GROUP), gl, gl), F32)],
        compiler_params=_cparams(2),
        name="rwkv_chunk",
    )(rt, at, bt, kt, bh, kh, v, wc, bonus, g, lnx_g, lnx_b, e)


def _outproj_body(x_ref, ya_ref, yb_ref, yc_ref, wa_ref, wb_ref, wc_ref, o_ref):
    acc = _dot_tn(ya_ref[0], wa_ref[...])
    acc = acc + _dot_tn(yb_ref[0], wb_ref[...])
    acc = acc + _dot(yc_ref[0], wc_ref[...])
    o_ref[0] = x_ref[0] + acc


def _outproj(x, yaT, ybT, yc, wa, wb, wc):
    b, t, d = x.shape
    tm = TOK_TILE
    const = lambda shape: pl.BlockSpec(shape, lambda i, j: (0,) * len(shape))
    return pl.pallas_call(
        _outproj_body,
        grid=(b, t // tm),
        in_specs=[pl.BlockSpec((1, tm, d), lambda i, j: (i, j, 0)),
                  pl.BlockSpec((1, WIDTH_A, tm), lambda i, j: (i, 0, j)),
                  pl.BlockSpec((1, WIDTH_B, tm), lambda i, j: (i, 0, j)),
                  pl.BlockSpec((1, tm, WIDTH_C), lambda i, j: (i, j, 0)),
                  const(wa.shape), const(wb.shape), const(wc.shape)],
        out_specs=pl.BlockSpec((1, tm, d), lambda i, j: (i, j, 0)),
        out_shape=jax.ShapeDtypeStruct((b, t, d), F32),
        compiler_params=_cparams(2),
        name="outproj",
    )(x, yaT, ybT, yc, wa, wb, wc)


def _ffn_body(x_ref, g_ref, wg_ref, wv_ref, cw_ref, cb_ref, wd_ref, o_ref, gbuf):
    tm = x_ref.shape[1]
    t = pl.program_id(1)

    @pl.when(t == 0)
    def _():
        gbuf[0:8, :] = jnp.zeros((8, D_FF), F32)

    x = x_ref[0]
    ms = jnp.mean(x * x, axis=-1, keepdims=True)
    h = (x * lax.rsqrt(ms + RMS_EPS) * g_ref[...]).astype(BF16)
    acc = x
    for lo, hi in zip(FF_SPLITS[:-1], FF_SPLITS[1:]):
        gate = _dot(h, wg_ref[:, lo:hi])
        val = _dot(h, wv_ref[:, lo:hi])
        gbuf[8:8 + tm, lo:hi] = gate
        g1 = gbuf[7:7 + tm, lo:hi]
        g2 = gbuf[6:6 + tm, lo:hi]
        gbuf[0:8, lo:hi] = gate[tm - 8:tm, :]
        conv = cb_ref[:, lo:hi] + g2 * cw_ref[0:1, lo:hi] + g1 * cw_ref[1:2, lo:hi] + gate * cw_ref[2:3, lo:hi]
        act = (conv * _sigmoid(conv) * val).astype(BF16)
        acc = acc + _dot(act, wd_ref[lo:hi, :])
    o_ref[0] = acc


def _ffn(x, g, wg, wv, cw, cb, wd):
    b, t, d = x.shape
    tm = TOK_TILE
    const = lambda shape: pl.BlockSpec(shape, lambda i, j: (0,) * len(shape),
                                       pipeline_mode=pl.Buffered(1))
    return pl.pallas_call(
        _ffn_body,
        grid=(b, t // tm),
        in_specs=[pl.BlockSpec((1, tm, d), lambda i, j: (i, j, 0)),
                  const(g.shape), const(wg.shape), const(wv.shape), const(cw.shape), const(cb.shape),
                  const(wd.shape)],
        out_specs=pl.BlockSpec((1, tm, d), lambda i, j: (i, j, 0)),
        out_shape=jax.ShapeDtypeStruct((b, t, d), F32),
        scratch_shapes=[pltpu.VMEM((tm + 8, D_FF), F32)],
        compiler_params=_cparams(2),
        name="ffn",
    )(x, g, wg, wv, cw, cb, wd)


def _pad_heads_cols(w, nh):
    d = w.shape[0]
    w = w.reshape(d, nh, HEAD_DIM)
    return jnp.pad(w, ((0, 0), (0, 0), (0, LANES - HEAD_DIM))).reshape(d, nh * LANES)


def _qk_bound(gq, gk):
    scale = HEAD_DIM ** -0.5
    return 1.01 * HEAD_DIM * scale * jnp.max(jnp.abs(gq)) * jnp.max(jnp.abs(gk)) + 0.05


def _layer_params(l, w_in, q_norm_a, k_norm_a, rel_bias, q_norm_b, k_norm_b, forget_bias):
    w = w_in[l]
    scale = HEAD_DIM ** -0.5
    a0, b0 = 0, 3 * WIDTH_A
    c0 = b0 + 3 * WIDTH_B + N_HEADS_B
    qa, ka, va = (w[:, a0 + i * WIDTH_A:a0 + (i + 1) * WIDTH_A] for i in range(3))
    qb, kb, vb = (w[:, b0 + i * WIDTH_B:b0 + (i + 1) * WIDTH_B] for i in range(3))
    fg = w[:, b0 + 3 * WIDTH_B:c0]
    wc = w[:, c0:]
    wn = jnp.concatenate([_pad_heads_cols(kb, N_HEADS_B),
                          jnp.pad(fg, ((0, 0), (0, LANES - N_HEADS_B))), wc], axis=1).astype(BF16)
    pad_v = lambda v, nh: jnp.pad(v.reshape(-1, nh, HEAD_DIM),
                                  ((0, 0), (0, 0), (0, VB_ROWS - HEAD_DIM))).reshape(-1, nh * VB_ROWS)
    assert QB_ROWS == VB_ROWS
    wt = jnp.concatenate([qa, ka, pad_v(va, N_HEADS_A), pad_v(qb, N_HEADS_B), pad_v(vb, N_HEADS_B),
                          jnp.pad(fg, ((0, 0), (0, 16 - N_HEADS_B)))], axis=1).T.astype(BF16)
    pad = LANES - HEAD_DIM
    gka = k_norm_a[l].reshape(HEAD_DIM, 1)
    gkb = jnp.pad(k_norm_b[l], (0, pad)).reshape(1, LANES)
    gqa = (q_norm_a[l] * (scale * LOG2E)).reshape(HEAD_DIM, 1)
    gqb = jnp.pad(q_norm_b[l] * (scale * LOG2E), (0, QB_ROWS - HEAD_DIM)).reshape(QB_ROWS, 1)
    fbn = jnp.pad(forget_bias[l], (0, LANES - N_HEADS_B)).reshape(1, LANES)
    fbt = jnp.pad(forget_bias[l], (0, 16 - N_HEADS_B)).reshape(16, 1)
    shift_b = (LOG2E * _qk_bound(q_norm_b[l], k_norm_b[l])).reshape(1, 1)
    bound_a = _qk_bound(q_norm_a[l], k_norm_a[l])
    tab = rel_bias[l]
    shift_a = LOG2E * (bound_a + jnp.max(tab, axis=1))
    depth_a = jnp.max(shift_a + LOG2E * (bound_a - tab[:, MAX_REL]))
    return wn, wt, gka, gkb, gqa, gqb, fbn, fbt, shift_b, shift_a, depth_a


def _selectors():
    import numpy as np
    selk = np.zeros((3, LANES, N_HEADS_B * LANES), np.float32)
    selq = np.zeros((3, N_HEADS_B * QB_ROWS, 16), np.float32)
    for p in range(3):
        for hh in range(N_HEADS_B):
            selk[p, hh, hh * LANES + HEAD_DIM + 3 + p] = -1.0
            selq[p, hh * QB_ROWS + HEAD_DIM + p, hh] = 1.0
    e = np.kron(np.eye(N_HEADS_C, dtype=np.float32), np.ones((HEAD_DIM, HEAD_DIM), np.float32))
    return jnp.asarray(selk, BF16), jnp.asarray(selq, BF16), jnp.asarray(e, BF16)


def kernel(x, mix_norm_g, w_in, q_norm_a, k_norm_a, rel_bias, q_norm_b, k_norm_b, forget_bias, shift_mu, w0, w2,
           a0, a2, g2, k_k, k_a, r_k, lnx_g, lnx_b, w_out, ffn_norm_g, w_up, conv_w, conv_b, w_down):
    depth = w_in.shape[0]
    b, t, d = x.shape
    selk, selq, e = _selectors()
    row = lambda v: v.reshape(1, -1)
    for l in range(depth):
        wn, wt, gka, gkb, gqa, gqb, fbn, fbt, shift_b, shift_a, depth_a = _layer_params(
            l, w_in, q_norm_a, k_norm_a, rel_bias, q_norm_b, k_norm_b, forget_bias)
        rwkv_params = (row(shift_mu[l]), row(w0[l]), w2[l].astype(BF16), row(a0[l]), a2[l].astype(BF16),
                       g2[l].astype(BF16), row(k_k[l]), row(k_a[l]), row(r_k[l]), e)
        (qaT, vaT, kA, qbT, vbT, kB, rt, at, bt, kt, bh, kh, v, wcum, bonus, g) = _inproj(
            x, row(mix_norm_g[l]), wn, wt, gka, gkb, gqa, gqb, fbn, fbt, selk, selq, shift_b, rwkv_params)
        bias = _relbias(rel_bias[l], shift_a)
        yaT = lax.cond(depth_a <= 2 * FAST_MAX_SHIFT,
                       functools.partial(_attn_a, online=False), functools.partial(_attn_a, online=True),
                       qaT, kA, vaT, bias)
        ybT = lax.cond(shift_b[0, 0] <= FAST_MAX_SHIFT,
                       functools.partial(_fox, online=False), functools.partial(_fox, online=True),
                       qbT, kB, vbT)
        yc = _rwkv_chunk(rt, at, bt, kt, bh, kh, v, wcum, bonus, g, row(lnx_g[l]), row(lnx_b[l]), e)
        wo = w_out[l].astype(BF16)
        x = _outproj(x, yaT.reshape(b, WIDTH_A, t), ybT.reshape(b, WIDTH_B, t), yc,
                     wo[:WIDTH_A], wo[WIDTH_A:WIDTH_A + WIDTH_B], wo[WIDTH_A + WIDTH_B:])
        wu = w_up[l].astype(BF16)
        x = _ffn(x, row(ffn_norm_g[l]), wu[:, :D_FF], wu[:, D_FF:], conv_w[l], row(conv_b[l]),
                 w_down[l].astype(BF16))
    return x
```

```python
import functools
import math

import jax
import jax.numpy as jnp
from jax import lax
from jax.experimental import pallas as pl
from jax.experimental.pallas import tpu as pltpu

F32 = jnp.float32
BF16 = jnp.bfloat16

D_MODEL = 1024
HEAD_DIM = 64
CHUNK = 64
LEFT_CHUNKS = 8
MAX_REL = 128
N_HEADS_A = 4
N_HEADS_B = 4
N_HEADS_C = 8
WIDTH_A = N_HEADS_A * HEAD_DIM
WIDTH_B = N_HEADS_B * HEAD_DIM
WIDTH_C = N_HEADS_C * HEAD_DIM
DECAY_LORA = 64
AAA_LORA = 64
GATE_LORA = 128
COLS_C = 3 * WIDTH_C + DECAY_LORA + AAA_LORA + GATE_LORA
D_FF = 2816
RMS_EPS = 1e-6
LNX_EPS = 64e-5
NEG_INF = -1e30

LANES = 128
TOK_TILE = 512
ATT_A_TILE = 256
ATT_A_WIN = 3 * ATT_A_TILE
FOX_TILE = 512
FOX_GROUP = 8
RWKV_GROUP = 4
RWKV_TILE = 256
RWKV_ROWS = 4
RWKV_SCHEDULE = ((0, 1), (2, 3))
RWKV_STAGES = 9
FF_SPLITS = (0, 768, 1536, 2304, 2816)
VMEM_LIMIT = 56 * 1024 * 1024

PN_KB, PN_F, PN_C = 0, 512, 640
PN_COLS = PN_C + COLS_C
VB_ROWS = 80
PT_QA = 0
PT_KA = PT_QA + WIDTH_A
PT_VA = PT_KA + WIDTH_A
PT_QB = PT_VA + N_HEADS_A * VB_ROWS
QB_ROWS = 80
PT_VB = PT_QB + N_HEADS_B * QB_ROWS
PT_F = PT_VB + N_HEADS_B * VB_ROWS
PT_ROWS = PT_F + 16
LOG2E = 1.4426950408889634
FAST_MAX_SHIFT = 40.0


def _cparams(n_axes):
    return pltpu.CompilerParams(dimension_semantics=("arbitrary",) * n_axes,
                                vmem_limit_bytes=VMEM_LIMIT)


def _split3(x):
    hi = x.astype(BF16)
    r1 = x - hi.astype(F32)
    mid = r1.astype(BF16)
    lo = (r1 - mid.astype(F32)).astype(BF16)
    return hi, mid, lo


def _split2(x):
    hi = x.astype(BF16)
    lo = (x - hi.astype(F32)).astype(BF16)
    return hi, lo


def _dot(a, b):
    return jnp.dot(a, b, preferred_element_type=F32)


def _dot_nt(a, b):
    return lax.dot_general(a, b, (((1,), (1,)), ((), ())), preferred_element_type=F32)


def _dot_tn(a, b):
    return lax.dot_general(a, b, (((0,), (0,)), ((), ())), preferred_element_type=F32)


def _log_sigmoid(x):
    return jnp.minimum(x, 0.0) - jnp.log(1.0 + jnp.exp(-jnp.abs(x)))


def _softplus(x):
    return jnp.maximum(x, 0.0) + jnp.log(1.0 + jnp.exp(-jnp.abs(x)))


def _sigmoid(x):
    return 1.0 / (1.0 + jnp.exp(-x))


def _inproj_body(x_ref, g_ref, wn_ref, wt_ref, gka_ref, gkb_ref, gqa_ref, gqb_ref, fbn_ref, fbt_ref,
                 selk_ref, selq_ref, shift_ref, *rest):
    rwkv_params, rest = rest[:10], rest[10:]
    qaT_ref, vaT_ref, kA_ref, qbT_ref, vbT_ref, kB_ref = rest[:6]
    rwkv_outs = rest[6:16]
    carry_n, carry_t, sbuf = rest[16:]
    tm = x_ref.shape[1]
    t = pl.program_id(1)

    @pl.when(t == 0)
    def _():
        carry_n[...] = jnp.zeros_like(carry_n)
        carry_t[...] = jnp.zeros_like(carry_t)

    x = x_ref[0]
    ms = jnp.mean(x * x, axis=-1, keepdims=True)
    h = (x * lax.rsqrt(ms + RMS_EPS) * g_ref[...]).astype(BF16)
    rows_t = lambda start, n: _dot_nt(wt_ref[start:start + n, :], h)
    cols_n = lambda start, n: _dot(h, wn_ref[:, start:start + n])
    row_v = lax.broadcasted_iota(jnp.int32, (VB_ROWS, 1), 0)
    ones_v = jnp.where(row_v == HEAD_DIM, 1.0, 0.0)
    val = {}

    def norm_rows(p, gain_ref):
        p = p.reshape(N_HEADS_A, HEAD_DIM, tm)
        return (p * lax.rsqrt(jnp.mean(p * p, axis=1, keepdims=True) + RMS_EPS) * gain_ref[...][None]).astype(BF16)

    def lane_packed3(z):
        z_hi, z_mid, z_lo = _split3(z)
        return (z_hi.astype(F32) + pltpu.roll(z_mid.astype(F32), N_HEADS_B, axis=1)
                + pltpu.roll(z_lo.astype(F32), 2 * N_HEADS_B, axis=1)).astype(BF16)

    def gates_epilogue(p):
        pn_f, pt_f = p
        lane_n = lax.broadcasted_iota(jnp.int32, (1, LANES), 1)
        lf_n = jnp.where(lane_n < N_HEADS_B, LOG2E * _log_sigmoid(pn_f + fbn_ref[...]), 0.0)
        row_t = lax.broadcasted_iota(jnp.int32, (16, 1), 0)
        lf_t = jnp.where(row_t < N_HEADS_B, LOG2E * _log_sigmoid(pt_f + fbt_ref[...]), 0.0)
        ri = lax.broadcasted_iota(jnp.int32, (tm, tm), 0)
        ci = lax.broadcasted_iota(jnp.int32, (tm, tm), 1)
        low = (ci <= ri).astype(BF16)
        upp = (ri <= ci).astype(BF16)
        cum3 = _dot(low, lane_packed3(lf_n))
        cum = cum3 + pltpu.roll(cum3, LANES - N_HEADS_B, axis=1) + pltpu.roll(cum3, LANES - 2 * N_HEADS_B, axis=1)
        val["c_n"] = jnp.where(lane_n < N_HEADS_B, cum, 0.0) + carry_n[0:1, :]
        ht, mt, lt = _split3(lf_t)
        val["c_t"] = _dot(ht, upp) + _dot(mt, upp) + _dot(lt, upp) + carry_t[:, 0:1]
        carry_n[...] = carry_n[...] + jnp.sum(lf_n, axis=0, keepdims=True)
        carry_t[...] = carry_t[...] + jnp.sum(lf_t, axis=1, keepdims=True)

    def kb_epilogue(pn_kb):
        kaug = _dot(lane_packed3(val["c_n"]), selk_ref[...])
        lane_k = lax.broadcasted_iota(jnp.int32, (1, LANES), 1)
        ones_k = jnp.where((lane_k >= HEAD_DIM) & (lane_k < HEAD_DIM + 3), 1.0, 0.0)
        for hh in range(N_HEADS_B):
            k = pn_kb[:, LANES * hh:LANES * (hh + 1)]
            msk = jnp.sum(k * k, axis=-1, keepdims=True) * (1.0 / HEAD_DIM)
            kn = k * lax.rsqrt(msk + RMS_EPS) * gkb_ref[...]
            kB_ref[0, hh] = (kn + kaug[:, LANES * hh:LANES * (hh + 1)] + ones_k).astype(BF16)

    def qb_epilogue(pt_qb):
        c3 = jnp.concatenate(_split3(val["c_t"] - shift_ref[...]), axis=0)
        qaug = _dot(selq_ref[...], c3)
        qb = pt_qb.reshape(N_HEADS_B, QB_ROWS, tm)
        msq = jnp.sum(qb * qb, axis=1, keepdims=True) * (1.0 / HEAD_DIM)
        row_q = lax.broadcasted_iota(jnp.int32, (QB_ROWS, 1), 0)
        ones_q = jnp.where((row_q >= HEAD_DIM + 3) & (row_q < HEAD_DIM + 6), 1.0, 0.0)
        qn = qb * lax.rsqrt(msq + RMS_EPS) * gqb_ref[...][None]
        qbT_ref[0] = (qn + qaug.reshape(N_HEADS_B, QB_ROWS, tm) + ones_q[None]).astype(BF16)

    def store(ref, value):
        ref[...] = value

    sections = [
        (lambda: rows_t(PT_QA, WIDTH_A), lambda p: store(qaT_ref, norm_rows(p, gqa_ref)[None])),
        (lambda: rows_t(PT_KA, WIDTH_A), lambda p: store(kA_ref, norm_rows(p, gka_ref)[None])),
        (lambda: rows_t(PT_VA, N_HEADS_A * VB_ROWS),
         lambda p: store(vaT_ref, (p.reshape(N_HEADS_A, VB_ROWS, tm) + ones_v[None]).astype(BF16)[None])),
        (lambda: (cols_n(PN_F, LANES), rows_t(PT_F, 16)), gates_epilogue),
        (lambda: rows_t(PT_VB, N_HEADS_B * VB_ROWS),
         lambda p: store(vbT_ref, (p.reshape(N_HEADS_B, VB_ROWS, tm) + ones_v[None]).astype(BF16)[None, :, None])),
        (lambda: cols_n(PN_KB, N_HEADS_B * LANES), kb_epilogue),
        (lambda: rows_t(PT_QB, N_HEADS_B * QB_ROWS), qb_epilogue),
    ]
    slots = _rwkv_prep(lambda start, n: cols_n(PN_C + start, n), tm, t, *rwkv_params, *rwkv_outs, sbuf)
    pending = None
    for mm, epilogue in sections:
        p = mm()
        if pending is not None:
            pending[1](pending[0])
        pending = (p, epilogue)
        next(slots, None)
    pending[1](pending[0])
    for _ in slots:
        pass


def _inproj(x, g, wn, wt, gka, gkb, gqa, gqb, fbn, fbt, selk, selq, shift, rwkv_params):
    b, t, d = x.shape
    tm = TOK_TILE
    nt = t // tm
    const = lambda shape: pl.BlockSpec(shape, lambda i, j: (0,) * len(shape))
    tok = lambda: pl.BlockSpec((1, tm, WIDTH_C), lambda i, j: (i, j, 0))
    bf = jax.ShapeDtypeStruct((b, t, WIDTH_C), BF16)
    out_shape = (
        jax.ShapeDtypeStruct((b, N_HEADS_A, HEAD_DIM, t), BF16),
        jax.ShapeDtypeStruct((b, N_HEADS_A, VB_ROWS, t), BF16),
        jax.ShapeDtypeStruct((b, N_HEADS_A, HEAD_DIM, t), BF16),
        jax.ShapeDtypeStruct((b, N_HEADS_B, QB_ROWS, t), BF16),
        jax.ShapeDtypeStruct((b, N_HEADS_B, nt, VB_ROWS, tm), BF16),
        jax.ShapeDtypeStruct((b, N_HEADS_B, t, LANES), BF16),
        bf, bf, bf, bf, bf, bf, bf,
        jax.ShapeDtypeStruct((b, t // RWKV_TILE, WIDTH_C, LANES), F32),
        bf, bf,
    )
    out_specs = (
        pl.BlockSpec((1, N_HEADS_A, HEAD_DIM, tm), lambda i, j: (i, 0, 0, j)),
        pl.BlockSpec((1, N_HEADS_A, VB_ROWS, tm), lambda i, j: (i, 0, 0, j)),
        pl.BlockSpec((1, N_HEADS_A, HEAD_DIM, tm), lambda i, j: (i, 0, 0, j)),
        pl.BlockSpec((1, N_HEADS_B, QB_ROWS, tm), lambda i, j: (i, 0, 0, j)),
        pl.BlockSpec((1, N_HEADS_B, 1, VB_ROWS, tm), lambda i, j: (i, 0, j, 0, 0)),
        pl.BlockSpec((1, N_HEADS_B, tm, LANES), lambda i, j: (i, 0, j, 0)),
        tok(), tok(), tok(), tok(), tok(), tok(), tok(),
        pl.BlockSpec((1, tm // RWKV_TILE, WIDTH_C, LANES), lambda i, j: (i, j, 0, 0)),
        tok(), tok(),
    )
    in_specs = [
        pl.BlockSpec((1, tm, d), lambda i, j: (i, j, 0)),
        const((1, d)), const(wn.shape), const(wt.shape),
        const(gka.shape), const(gkb.shape), const(gqa.shape), const(gqb.shape),
        const(fbn.shape), const(fbt.shape), const(selk.shape), const(selq.shape), const(shift.shape),
    ] + [const(p.shape) for p in rwkv_params]
    return pl.pallas_call(
        _inproj_body,
        grid=(b, nt),
        in_specs=in_specs,
        out_specs=out_specs,
        out_shape=out_shape,
        scratch_shapes=[pltpu.VMEM((8, LANES), F32), pltpu.VMEM((16, LANES), F32),
                        pltpu.VMEM((tm + 8, COLS_C), F32)],
        compiler_params=_cparams(2),
        name="inproj",
    )(x, g, wn, wt, gka, gkb, gqa, gqb, fbn, fbt, selk, selq, shift, *rwkv_params)


def _relbias_body(tab_ref, shift_ref, o_ref):
    hh = pl.program_id(0)
    kj = lax.broadcasted_iota(jnp.int32, (ATT_A_WIN, ATT_A_TILE), 0)
    qi = lax.broadcasted_iota(jnp.int32, (ATT_A_WIN, ATT_A_TILE), 1)
    rel = jnp.clip(kj - LEFT_CHUNKS * CHUNK - qi, -MAX_REL, MAX_REL) + MAX_REL

    def body(r, acc):
        return jnp.where(rel == r, tab_ref[hh, r], acc)

    bias = lax.fori_loop(0, 2 * MAX_REL + 1, body, jnp.zeros((ATT_A_WIN, ATT_A_TILE), F32))
    kc = kj // CHUNK
    qc = qi // CHUNK
    band = (kc >= qc) & (kc <= qc + LEFT_CHUNKS)
    o_ref[0] = jnp.where(band, LOG2E * bias - shift_ref[hh], NEG_INF)


def _relbias(tab, shift):
    return pl.pallas_call(
        _relbias_body,
        grid=(N_HEADS_A,),
        in_specs=[pl.BlockSpec(memory_space=pltpu.SMEM), pl.BlockSpec(memory_space=pltpu.SMEM)],
        out_specs=pl.BlockSpec((1, ATT_A_WIN, ATT_A_TILE), lambda i: (i, 0, 0)),
        out_shape=jax.ShapeDtypeStruct((N_HEADS_A, ATT_A_WIN, ATT_A_TILE), F32),
        compiler_params=_cparams(1),
        name="relbias",
    )(tab, shift)


def _attn_a_body(q_ref, k0_ref, k1_ref, k2_ref, v0_ref, v1_ref, v2_ref, bias_ref, o_ref, *, online):
    i = pl.program_id(1)
    tq = ATT_A_TILE
    kj = lax.broadcasted_iota(jnp.int32, (tq, 1), 0)

    def run(mask_padding):
        scores = []
        for hh in range(N_HEADS_A):
            q = q_ref[0, hh]
            ss = []
            for d, k_ref in enumerate((k0_ref, k1_ref, k2_ref)):
                s = _dot_tn(k_ref[0, hh], q) + bias_ref[hh, d * tq:(d + 1) * tq, :]
                if mask_padding:
                    s = jnp.where(kj + (i - 2 + d) * tq >= 0, s, NEG_INF)
                ss.append(s)
            scores.append(ss)
        for hh, ss in enumerate(scores):
            if online:
                m = functools.reduce(jnp.maximum, [jnp.max(s, axis=0, keepdims=True) for s in ss])
                ss = [s - m for s in ss]
            acc = jnp.zeros((VB_ROWS, tq), F32)
            for s, v_ref in zip(ss, (v0_ref, v1_ref, v2_ref)):
                acc = acc + _dot(v_ref[0, hh], jnp.exp2(s).astype(BF16))
            o_ref[0, hh] = (acc[0:HEAD_DIM] / acc[HEAD_DIM:HEAD_DIM + 1]).astype(BF16)

    pl.when(i < 2)(functools.partial(run, True))
    pl.when(i >= 2)(functools.partial(run, False))


def _attn_a(qaT, kA, vaT, bias, online):
    b, nh, _, t = qaT.shape
    tq = ATT_A_TILE
    kspec = lambda d: pl.BlockSpec((1, nh, HEAD_DIM, tq),
                                   lambda bb, i: (bb, 0, 0, jnp.maximum(i - 2 + d, 0)))
    vspec = lambda d: pl.BlockSpec((1, nh, VB_ROWS, tq),
                                   lambda bb, i: (bb, 0, 0, jnp.maximum(i - 2 + d, 0)))
    return pl.pallas_call(
        functools.partial(_attn_a_body, online=online),
        grid=(b, t // tq),
        in_specs=[pl.BlockSpec((1, nh, HEAD_DIM, tq), lambda bb, i: (bb, 0, 0, i)),
                  kspec(0), kspec(1), kspec(2), vspec(0), vspec(1), vspec(2),
                  pl.BlockSpec((nh, ATT_A_WIN, tq), lambda bb, i: (0, 0, 0))],
        out_specs=pl.BlockSpec((1, nh, HEAD_DIM, tq), lambda bb, i: (bb, 0, 0, i)),
        out_shape=jax.ShapeDtypeStruct((b, nh, HEAD_DIM, t), BF16),
        compiler_params=_cparams(2),
        name="attn_a_online" if online else "attn_a",
    )(qaT, kA, kA, kA, vaT, vaT, vaT, bias)


def _fox_body(q_ref, k_ref, v_ref, o_ref, *, online):
    i = pl.program_id(2)
    tq = FOX_TILE
    q = jnp.concatenate([q_ref[0, 0], jnp.zeros((LANES - QB_ROWS, tq), BF16)], axis=0)

    def scores(j):
        ks = k_ref[0, 0, pl.ds(pl.multiple_of(j * tq, tq), tq), :]
        return _dot(ks, q)

    def accumulate(j, s, m, acc, diagonal):
        vs = v_ref[0, 0, j]
        if diagonal:
            kj = lax.broadcasted_iota(jnp.int32, (tq, tq), 0)
            qi = lax.broadcasted_iota(jnp.int32, (tq, tq), 1)
            causal = kj <= qi
        if online:
            if diagonal:
                s = jnp.where(causal, s, NEG_INF)
            m_new = jnp.maximum(m, jnp.max(s, axis=0, keepdims=True))
            acc = jnp.exp2(m - m_new) * acc
            p = jnp.exp2(s - m_new)
            m = m_new
        else:
            p = jnp.exp2(s)
            if diagonal:
                p = jnp.where(causal, p, 0.0)
        return m, acc + _dot(vs, p.astype(BF16))

    def tiles(js, carry, last_diagonal):
        ss = [scores(j) for j in js]
        m, acc = carry
        for n, (j, s) in enumerate(zip(js, ss)):
            m, acc = accumulate(j, s, m, acc, last_diagonal and n == len(js) - 1)
        return m, acc

    init = (jnp.full((1, tq), NEG_INF, F32), jnp.zeros((VB_ROWS, tq), F32))
    if online:
        carry = lax.fori_loop(0, i, lambda j, c: tiles([j], c, False), init)
        _, acc = tiles([i], carry, True)
    else:
        g = FOX_GROUP
        carry = lax.fori_loop(0, i // g, lambda jj, c: tiles([g * jj + n for n in range(g)], c, False), init)
        tails = [functools.partial(lambda c, r: tiles([i - r + n for n in range(r + 1)], c, True), r=r)
                 for r in range(g)]
        _, acc = lax.switch(i % g, tails, carry)
    o_ref[0, 0] = (acc[0:HEAD_DIM] / acc[HEAD_DIM:HEAD_DIM + 1]).astype(BF16)


def _fox(qbT, kB, vbT, online):
    b, nh, _, t = qbT.shape
    tq = FOX_TILE
    nk = t // tq
    return pl.pallas_call(
        functools.partial(_fox_body, online=online),
        grid=(b, nh, nk),
        in_specs=[pl.BlockSpec((1, 1, QB_ROWS, tq), lambda bb, hh, i: (bb, hh, 0, i)),
                  pl.BlockSpec((1, 1, t, LANES), lambda bb, hh, i: (bb, hh, 0, 0)),
                  pl.BlockSpec((1, 1, nk, VB_ROWS, tq), lambda bb, hh, i: (bb, hh, 0, 0, 0))],
        out_specs=pl.BlockSpec((1, 1, HEAD_DIM, tq), lambda bb, hh, i: (bb, hh, 0, i)),
        out_shape=jax.ShapeDtypeStruct((b, nh, HEAD_DIM, t), BF16),
        compiler_params=_cparams(3),
        name="fox_online" if online else "fox",
    )(qbT, kB, vbT)


def _rwkv_prep(proj, tm, t, mu_ref, w0_ref, w2_ref, a0_ref, a2_ref, g2_ref, kk_ref, ka_ref, rk_ref, e_ref,
               rt_ref, at_ref, bt_ref, kt_ref, bh_ref, kh_ref, v_ref, wc_ref, bonus_ref, g_ref, sbuf):
    @pl.when(t == 0)
    def _():
        sbuf[0:8, :] = jnp.zeros((8, COLS_C), F32)

    def shifted(p, start):
        cols = slice(start, start + p.shape[1])
        sbuf[8:8 + tm, cols] = p
        prev = sbuf[7:7 + tm, cols]
        sbuf[0:8, cols] = p[tm - 8:tm, :]
        return p + (prev - p) * mu_ref[:, cols]

    c = WIDTH_C
    p_lo = proj(3 * c, COLS_C - 3 * c)
    p_k = proj(c, c)
    u_lo = shifted(p_lo, 3 * c)
    yield
    w_lo = u_lo[:, 0:DECAY_LORA]
    a_lo = u_lo[:, DECAY_LORA:DECAY_LORA + AAA_LORA]
    g_lo = u_lo[:, DECAY_LORA + AAA_LORA:]
    w = w0_ref[...] + _dot(jnp.tanh(w_lo).astype(BF16), w2_ref[...])
    w = -_softplus(-w) - 0.5
    ld = -jnp.exp(w)
    p_r = proj(0, c)
    yield
    a = _sigmoid(a0_ref[...] + _dot(a_lo.astype(BF16), a2_ref[...]))
    g_ref[0] = _dot(_sigmoid(g_lo).astype(BF16), g2_ref[...]).astype(BF16)
    k = shifted(p_k, c)
    p_v = proj(2 * c, c)
    yield

    e = e_ref[...]
    kk = k * kk_ref[...]
    nrm2 = _dot((kk * kk).astype(BF16), e)
    kkn = kk * lax.rsqrt(jnp.maximum(nrm2, 1e-24))
    k2 = k * (1.0 + (a - 1.0) * ka_ref[...])
    kka = kkn * a
    r = shifted(p_r, 0)
    v = shifted(p_v, 2 * c)
    v_ref[0] = v.astype(BF16)
    b_hi, b_lo = _split2(r * k2 * rk_ref[...])
    bonus_ref[0] = ((_dot(b_hi, e) + _dot(b_lo, e)) * v).astype(BF16)
    yield

    ri = lax.broadcasted_iota(jnp.int32, (2 * CHUNK, CHUNK), 0)
    ci = lax.broadcasted_iota(jnp.int32, (2 * CHUNK, CHUNK), 1)
    tri = ((ci <= ri) | (ri >= CHUNK)).astype(BF16)
    tri2 = jnp.concatenate([tri, tri], axis=1)
    ld_hi, ld_lo = _split2(ld)
    for cc in range(tm // CHUNK):
        sl = slice(cc * CHUNK, (cc + 1) * CHUNK)
        cum = _dot(tri2, jnp.concatenate([ld_hi[sl], ld_lo[sl]], axis=0))
        lc = cum[0:CHUNK]
        tot = cum[CHUNK:2 * CHUNK]
        e_neg = jnp.exp(-lc)
        e_rem = jnp.exp(tot - lc)
        rt_ref[0, sl, :] = (r[sl] * jnp.exp(lc)).astype(BF16)
        at_ref[0, sl, :] = (-kkn[sl] * jnp.exp(lc - ld[sl])).astype(BF16)
        bt_ref[0, sl, :] = (kka[sl] * e_neg).astype(BF16)
        kt_ref[0, sl, :] = (k2[sl] * e_neg).astype(BF16)
        bh_ref[0, sl, :] = (kka[sl] * e_rem).astype(BF16)
        kh_ref[0, sl, :] = (k2[sl] * e_rem).astype(BF16)
        if cc % 2 == 1:
            yield
    seg = (lax.broadcasted_iota(jnp.int32, (16, tm), 1) // CHUNK
           == lax.broadcasted_iota(jnp.int32, (16, tm), 0)).astype(BF16)
    tots3 = jnp.concatenate(_split3(_dot(seg, ld_hi) + _dot(seg, ld_lo)), axis=0)
    per_blk = RWKV_TILE // CHUNK
    for blk in range(tm // RWKV_TILE):
        put = (lax.broadcasted_iota(jnp.int32, (48, LANES), 0) % 16
               == lax.broadcasted_iota(jnp.int32, (48, LANES), 1) + blk * per_blk).astype(BF16)
        wc_ref[0, blk] = jnp.exp(_dot_tn(tots3, put))


def _bmm(a, b):
    return lax.dot_general(a, b, (((2,), (1,)), ((0,), (0,))), preferred_element_type=F32)


def _bmm_nt(a, b):
    return lax.dot_general(a, b, (((2,), (2,)), ((0,), (0,))), preferred_element_type=F32)


def _bmm_tn(a, b):
    return lax.dot_general(a, b, (((1,), (1,)), ((0,), (0,))), preferred_element_type=F32)


def _rwkv_chunk_body(rt_ref, at_ref, bt_ref, kt_ref, bh_ref, kh_ref, v_ref, wc_ref, bonus_ref, g_ref,
                     lg_ref, lb_ref, e_ref, o_ref, h_ref):
    rows, tm = rt_ref.shape[0], rt_ref.shape[1]
    nc = tm // CHUNK
    ng = N_HEADS_C // RWKV_GROUP
    gl = RWKV_GROUP * HEAD_DIM
    units = [(r, g) for r in range(rows) for g in range(ng)]
    nu = len(units)
    t = pl.program_id(1)

    @pl.when(t == 0)
    def _():
        h_ref[...] = jnp.zeros_like(h_ref)

    lane_head = lax.broadcasted_iota(jnp.int32, (1, CHUNK, gl), 2) // HEAD_DIM
    ri = lax.broadcasted_iota(jnp.int32, (1, 2 * CHUNK, gl), 1)
    ci = lax.broadcasted_iota(jnp.int32, (1, 2 * CHUNK, gl), 2) & (CHUNK - 1)
    rr = ri & (CHUNK - 1)
    keep = (rr > ci) | ((ri >= CHUNK) & (rr == ci))
    eye = (lax.broadcasted_iota(jnp.int32, (1, CHUNK, gl), 1)
           == (lax.broadcasted_iota(jnp.int32, (1, CHUNK, gl), 2) & (CHUNK - 1)))
    same_head = (lax.broadcasted_iota(jnp.int32, (1, gl, gl), 1) // HEAD_DIM
                 == lax.broadcasted_iota(jnp.int32, (1, gl, gl), 2) // HEAD_DIM)

    def bdiag(x):
        return jnp.concatenate([jnp.where(lane_head == hh, x, jnp.zeros_like(x)) for hh in range(RWKV_GROUP)],
                               axis=1)

    def chunk_local(cs, out):
        tile = lambda ref: jnp.stack([ref[r, c * CHUNK:(c + 1) * CHUNK, g * gl:(g + 1) * gl]
                                      for c in cs for r, g in units])
        rt, at, bt, kt, bh, kh, v = (tile(r) for r in (rt_ref, at_ref, bt_ref, kt_ref, bh_ref, kh_ref, v_ref))
        ar = jnp.concatenate([at, rt], axis=1)
        sb = jnp.where(keep, _bmm_nt(ar, bdiag(bt)), 0.0)
        sk = jnp.where(keep, _bmm_nt(ar, bdiag(kt)), 0.0).astype(BF16)
        m_rb = sb[:, CHUNK:].astype(BF16)
        yield
        a = sb[:, :CHUNK]
        tinv = jnp.where(eye, 1.0, 0.0) + a
        ab = a.astype(BF16)
        x = _bmm(ab, bdiag(ab))
        yield
        for _ in range(4):
            xb = x.astype(BF16)
            r = _bmm(jnp.concatenate([tinv.astype(BF16), xb], axis=1), bdiag(xb))
            tinv = tinv + r[:, :CHUNK]
            x = r[:, CHUNK:]
            yield
        tinv = (tinv + _bmm(tinv.astype(BF16), bdiag(x.astype(BF16)))).astype(BF16)
        bdv = bdiag(v)
        av = _bmm(sk[:, :CHUNK], bdv)
        yield
        p = _bmm(tinv, bdiag(at)).astype(BF16)
        u0 = _bmm(tinv, bdiag(av.astype(BF16)))
        yield
        q = (rt.astype(F32) + _bmm(m_rb, bdiag(p))).astype(BF16)
        y0 = _bmm(m_rb, bdiag(u0.astype(BF16))) + _bmm(sk[:, CHUNK:], bdv)
        parts = (jnp.concatenate([q, p], axis=1), jnp.concatenate([y0, u0], axis=1),
                 jnp.concatenate([bh, kh], axis=1), v)
        out += [tuple(z[n * nu:(n + 1) * nu] for z in parts) for n in range(len(cs))]
        yield

    wct = [wc_ref[r, 0] for r in range(rows)]
    state = [h_ref[...]]
    ys = []
    local = []

    def recur(c):
        qp, yu0, bk, v = local[c]
        yu = _bmm(qp, state[0].astype(BF16)) + yu0
        ys.append(yu[:, :CHUNK])
        upd = _bmm_tn(bk, jnp.concatenate([yu[:, CHUNK:].astype(BF16), v], axis=1))
        wcol = jnp.stack([wct[r][g * gl:(g + 1) * gl, c:c + 1] for r, g in units])
        state[0] = jnp.where(same_head, upd, 0.0) + wcol * state[0]

    assert sorted(c for cs in RWKV_SCHEDULE for c in cs) == list(range(nc))
    done = 0
    for cs in RWKV_SCHEDULE:
        pending = list(range(done, len(local)))
        slots = {(k + 1) * RWKV_STAGES // (len(pending) + 1): c for k, c in enumerate(pending)}
        for n, _ in enumerate(chunk_local(list(cs), local), start=1):
            if n in slots:
                recur(slots[n])
                done += 1
    for c in range(done, nc):
        recur(c)
    h_ref[...] = state[0]

    y = jnp.concatenate([jnp.concatenate([yc[r * ng + g] for g in range(ng)], axis=-1)
                         for r in range(rows) for yc in ys], axis=0)
    e = e_ref[...]

    def head_mean(z):
        return _dot(z.astype(BF16), e) * (1.0 / HEAD_DIM)

    d = y - head_mean(y)
    yn = d * lax.rsqrt(head_mean(d * d) + LNX_EPS)
    flat = lambda ref: ref[...].astype(F32).reshape(rows * tm, WIDTH_C)
    out = (yn * lg_ref[...] + lb_ref[...] + flat(bonus_ref)) * flat(g_ref)
    o_ref[...] = out.astype(BF16).reshape(rows, tm, WIDTH_C)


def _rwkv_chunk(rt, at, bt, kt, bh, kh, v, wc, bonus, g, lnx_g, lnx_b, e):
    b, t, _ = rt.shape
    tm = RWKV_TILE
    rows = math.gcd(RWKV_ROWS, b)
    gl = RWKV_GROUP * HEAD_DIM
    const = lambda shape: pl.BlockSpec(shape, lambda i, j: (0,) * len(shape))
    tok = lambda: pl.BlockSpec((rows, tm, WIDTH_C), lambda i, j: (i, j, 0))
    return pl.pallas_call(
        _rwkv_chunk_body,
        grid=(b // rows, t // tm),
        in_specs=[tok(), tok(), tok(), tok(), tok(), tok(), tok(),
                  pl.BlockSpec((rows, 1, WIDTH_C, LANES), lambda i, j: (i, j, 0, 0)),
                  tok(), tok(), const(lnx_g.shape), const(lnx_b.shape), const(e.shape)],
        out_specs=tok(),
        out_shape=jax.ShapeDtypeStruct((b, t, WIDTH_C), BF16),
        scratch_shapes=[pltpu.VMEM((rows * (N_HEADS_C // RWKV_GROUP), gl, gl), F32)],
        compiler_params=_cparams(2),
        name="rwkv_chunk",
    )(rt, at, bt, kt, bh, kh, v, wc, bonus, g, lnx_g, lnx_b, e)


def _outproj_body(x_ref, ya_ref, yb_ref, yc_ref, wa_ref, wb_ref, wc_ref, o_ref):
    acc = _dot_tn(ya_ref[0], wa_ref[...])
    acc = acc + _dot_tn(yb_ref[0], wb_ref[...])
    acc = acc + _dot(yc_ref[0], wc_ref[...])
    o_ref[0] = x_ref[0] + acc


def _outproj(x, yaT, ybT, yc, wa, wb, wc):
    b, t, d = x.shape
    tm = TOK_TILE
    const = lambda shape: pl.BlockSpec(shape, lambda i, j: (0,) * len(shape))
    return pl.pallas_call(
        _outproj_body,
        grid=(b, t // tm),
        in_specs=[pl.BlockSpec((1, tm, d), lambda i, j: (i, j, 0)),
                  pl.BlockSpec((1, WIDTH_A, tm), lambda i, j: (i, 0, j)),
                  pl.BlockSpec((1, WIDTH_B, tm), lambda i, j: (i, 0, j)),
                  pl.BlockSpec((1, tm, WIDTH_C), lambda i, j: (i, j, 0)),
                  const(wa.shape), const(wb.shape), const(wc.shape)],
        out_specs=pl.BlockSpec((1, tm, d), lambda i, j: (i, j, 0)),
        out_shape=jax.ShapeDtypeStruct((b, t, d), F32),
        compiler_params=_cparams(2),
        name="outproj",
    )(x, yaT, ybT, yc, wa, wb, wc)


def _ffn_body(x_ref, g_ref, wg_ref, wv_ref, cw_ref, cb_ref, wd_ref, o_ref, gbuf):
    tm = x_ref.shape[1]
    t = pl.program_id(1)

    @pl.when(t == 0)
    def _():
        gbuf[0:8, :] = jnp.zeros((8, D_FF), F32)

    x = x_ref[0]
    ms = jnp.mean(x * x, axis=-1, keepdims=True)
    h = (x * lax.rsqrt(ms + RMS_EPS) * g_ref[...]).astype(BF16)
    acc = x
    for lo, hi in zip(FF_SPLITS[:-1], FF_SPLITS[1:]):
        gate = _dot(h, wg_ref[:, lo:hi])
        val = _dot(h, wv_ref[:, lo:hi])
        gbuf[8:8 + tm, lo:hi] = gate
        g1 = gbuf[7:7 + tm, lo:hi]
        g2 = gbuf[6:6 + tm, lo:hi]
        gbuf[0:8, lo:hi] = gate[tm - 8:tm, :]
        conv = cb_ref[:, lo:hi] + g2 * cw_ref[0:1, lo:hi] + g1 * cw_ref[1:2, lo:hi] + gate * cw_ref[2:3, lo:hi]
        act = (conv * _sigmoid(conv) * val).astype(BF16)
        acc = acc + _dot(act, wd_ref[lo:hi, :])
    o_ref[0] = acc


def _ffn(x, g, wg, wv, cw, cb, wd):
    b, t, d = x.shape
    tm = TOK_TILE
    const = lambda shape: pl.BlockSpec(shape, lambda i, j: (0,) * len(shape),
                                       pipeline_mode=pl.Buffered(1))
    return pl.pallas_call(
        _ffn_body,
        grid=(b, t // tm),
        in_specs=[pl.BlockSpec((1, tm, d), lambda i, j: (i, j, 0)),
                  const(g.shape), const(wg.shape), const(wv.shape), const(cw.shape), const(cb.shape),
                  const(wd.shape)],
        out_specs=pl.BlockSpec((1, tm, d), lambda i, j: (i, j, 0)),
        out_shape=jax.ShapeDtypeStruct((b, t, d), F32),
        scratch_shapes=[pltpu.VMEM((tm + 8, D_FF), F32)],
        compiler_params=_cparams(2),
        name="ffn",
    )(x, g, wg, wv, cw, cb, wd)


def _pad_heads_cols(w, nh):
    d = w.shape[0]
    w = w.reshape(d, nh, HEAD_DIM)
    return jnp.pad(w, ((0, 0), (0, 0), (0, LANES - HEAD_DIM))).reshape(d, nh * LANES)


def _qk_bound(gq, gk):
    scale = HEAD_DIM ** -0.5
    return 1.01 * HEAD_DIM * scale * jnp.max(jnp.abs(gq)) * jnp.max(jnp.abs(gk)) + 0.05


def _layer_params(l, w_in, q_norm_a, k_norm_a, rel_bias, q_norm_b, k_norm_b, forget_bias):
    w = w_in[l]
    scale = HEAD_DIM ** -0.5
    a0, b0 = 0, 3 * WIDTH_A
    c0 = b0 + 3 * WIDTH_B + N_HEADS_B
    qa, ka, va = (w[:, a0 + i * WIDTH_A:a0 + (i + 1) * WIDTH_A] for i in range(3))
    qb, kb, vb = (w[:, b0 + i * WIDTH_B:b0 + (i + 1) * WIDTH_B] for i in range(3))
    fg = w[:, b0 + 3 * WIDTH_B:c0]
    wc = w[:, c0:]
    wn = jnp.concatenate([_pad_heads_cols(kb, N_HEADS_B),
                          jnp.pad(fg, ((0, 0), (0, LANES - N_HEADS_B))), wc], axis=1).astype(BF16)
    pad_v = lambda v, nh: jnp.pad(v.reshape(-1, nh, HEAD_DIM),
                                  ((0, 0), (0, 0), (0, VB_ROWS - HEAD_DIM))).reshape(-1, nh * VB_ROWS)
    assert QB_ROWS == VB_ROWS
    wt = jnp.concatenate([qa, ka, pad_v(va, N_HEADS_A), pad_v(qb, N_HEADS_B), pad_v(vb, N_HEADS_B),
                          jnp.pad(fg, ((0, 0), (0, 16 - N_HEADS_B)))], axis=1).T.astype(BF16)
    pad = LANES - HEAD_DIM
    gka = k_norm_a[l].reshape(HEAD_DIM, 1)
    gkb = jnp.pad(k_norm_b[l], (0, pad)).reshape(1, LANES)
    gqa = (q_norm_a[l] * (scale * LOG2E)).reshape(HEAD_DIM, 1)
    gqb = jnp.pad(q_norm_b[l] * (scale * LOG2E), (0, QB_ROWS - HEAD_DIM)).reshape(QB_ROWS, 1)
    fbn = jnp.pad(forget_bias[l], (0, LANES - N_HEADS_B)).reshape(1, LANES)
    fbt = jnp.pad(forget_bias[l], (0, 16 - N_HEADS_B)).reshape(16, 1)
    shift_b = (LOG2E * _qk_bound(q_norm_b[l], k_norm_b[l])).reshape(1, 1)
    bound_a = _qk_bound(q_norm_a[l], k_norm_a[l])
    tab = rel_bias[l]
    shift_a = LOG2E * (bound_a + jnp.max(tab, axis=1))
    depth_a = jnp.max(shift_a + LOG2E * (bound_a - tab[:, MAX_REL]))
    return wn, wt, gka, gkb, gqa, gqb, fbn, fbt, shift_b, shift_a, depth_a


def _selectors():
    import numpy as np
    selk = np.zeros((LANES, N_HEADS_B * LANES), np.float32)
    selq = np.zeros((N_HEADS_B * QB_ROWS, 3 * 16), np.float32)
    for p in range(3):
        for hh in range(N_HEADS_B):
            selk[p * N_HEADS_B + hh, hh * LANES + HEAD_DIM + 3 + p] = -1.0
            selq[hh * QB_ROWS + HEAD_DIM + p, 16 * p + hh] = 1.0
    e = np.kron(np.eye(N_HEADS_C, dtype=np.float32), np.ones((HEAD_DIM, HEAD_DIM), np.float32))
    return jnp.asarray(selk, BF16), jnp.asarray(selq, BF16), jnp.asarray(e, BF16)


def kernel(x, mix_norm_g, w_in, q_norm_a, k_norm_a, rel_bias, q_norm_b, k_norm_b, forget_bias, shift_mu, w0, w2,
           a0, a2, g2, k_k, k_a, r_k, lnx_g, lnx_b, w_out, ffn_norm_g, w_up, conv_w, conv_b, w_down):
    depth = w_in.shape[0]
    b, t, d = x.shape
    selk, selq, e = _selectors()
    row = lambda v: v.reshape(1, -1)
    for l in range(depth):
        wn, wt, gka, gkb, gqa, gqb, fbn, fbt, shift_b, shift_a, depth_a = _layer_params(
            l, w_in, q_norm_a, k_norm_a, rel_bias, q_norm_b, k_norm_b, forget_bias)
        rwkv_params = (row(shift_mu[l]), row(w0[l]), w2[l].astype(BF16), row(a0[l]), a2[l].astype(BF16),
                       g2[l].astype(BF16), row(k_k[l]), row(k_a[l]), row(r_k[l]), e)
        (qaT, vaT, kA, qbT, vbT, kB, rt, at, bt, kt, bh, kh, v, wcum, bonus, g) = _inproj(
            x, row(mix_norm_g[l]), wn, wt, gka, gkb, gqa, gqb, fbn, fbt, selk, selq, shift_b, rwkv_params)
        bias = _relbias(rel_bias[l], shift_a)
        yaT = lax.cond(depth_a <= 2 * FAST_MAX_SHIFT,
                       functools.partial(_attn_a, online=False), functools.partial(_attn_a, online=True),
                       qaT, kA, vaT, bias)
        ybT = lax.cond(shift_b[0, 0] <= FAST_MAX_SHIFT,
                       functools.partial(_fox, online=False), functools.partial(_fox, online=True),
                       qbT, kB, vbT)
        yc = _rwkv_chunk(rt, at, bt, kt, bh, kh, v, wcum, bonus, g, row(lnx_g[l]), row(lnx_b[l]), e)
        wo = w_out[l].astype(BF16)
        x = _outproj(x, yaT.reshape(b, WIDTH_A, t), ybT.reshape(b, WIDTH_B, t), yc,
                     wo[:WIDTH_A], wo[WIDTH_A:WIDTH_A + WIDTH_B], wo[WIDTH_A + WIDTH_B:])
        wu = w_up[l].astype(BF16)
        x = _ffn(x, row(ffn_norm_g[l]), wu[:, :D_FF], wu[:, D_FF:], conv_w[l], row(conv_b[l]),
                 w_down[l].astype(BF16))
    return x
```

```python
import functools
import math

import jax
import jax.numpy as jnp
from jax import lax
from jax.experimental import pallas as pl
from jax.experimental.pallas import tpu as pltpu

F32 = jnp.float32
BF16 = jnp.bfloat16

D_MODEL = 1024
HEAD_DIM = 64
CHUNK = 64
LEFT_CHUNKS = 8
MAX_REL = 128
N_HEADS_A = 4
N_HEADS_B = 4
N_HEADS_C = 8
WIDTH_A = N_HEADS_A * HEAD_DIM
WIDTH_B = N_HEADS_B * HEAD_DIM
WIDTH_C = N_HEADS_C * HEAD_DIM
DECAY_LORA = 64
AAA_LORA = 64
GATE_LORA = 128
COLS_C = 3 * WIDTH_C + DECAY_LORA + AAA_LORA + GATE_LORA
D_FF = 2816
RMS_EPS = 1e-6
LNX_EPS = 64e-5
NEG_INF = -1e30

LANES = 128
TOK_TILE = 512
ATT_A_TILE = 256
ATT_A_WIN = 3 * ATT_A_TILE
FOX_TILE = 512
FOX_HEADS = 2
FOX_GROUP = 4
FOX_AHEAD = 2
RWKV_GROUP = 4
RWKV_TILE = 256
RWKV_ROWS = 4
RWKV_SCHEDULE = ((0, 1), (2, 3))
RWKV_STAGES = 9
FF_SPLITS = (0, 768, 1536, 2304, 2816)
VMEM_LIMIT = 56 * 1024 * 1024

PN_KB, PN_F, PN_C = 0, 512, 640
PN_COLS = PN_C + COLS_C
VB_ROWS = 80
PT_QA = 0
PT_KA = PT_QA + WIDTH_A
PT_VA = PT_KA + WIDTH_A
PT_QB = PT_VA + N_HEADS_A * VB_ROWS
QB_ROWS = 80
PT_VB = PT_QB + N_HEADS_B * QB_ROWS
PT_F = PT_VB + N_HEADS_B * VB_ROWS
PT_ROWS = PT_F + 16
LOG2E = 1.4426950408889634
FAST_MAX_SHIFT = 40.0


def _cparams(n_axes):
    return pltpu.CompilerParams(dimension_semantics=("arbitrary",) * n_axes,
                                vmem_limit_bytes=VMEM_LIMIT)


def _split3(x):
    hi = x.astype(BF16)
    r1 = x - hi.astype(F32)
    mid = r1.astype(BF16)
    lo = (r1 - mid.astype(F32)).astype(BF16)
    return hi, mid, lo


def _split2(x):
    hi = x.astype(BF16)
    lo = (x - hi.astype(F32)).astype(BF16)
    return hi, lo


def _dot(a, b):
    return jnp.dot(a, b, preferred_element_type=F32)


def _dot_nt(a, b):
    return lax.dot_general(a, b, (((1,), (1,)), ((), ())), preferred_element_type=F32)


def _dot_tn(a, b):
    return lax.dot_general(a, b, (((0,), (0,)), ((), ())), preferred_element_type=F32)


def _log_sigmoid(x):
    return jnp.minimum(x, 0.0) - jnp.log(1.0 + jnp.exp(-jnp.abs(x)))


def _softplus(x):
    return jnp.maximum(x, 0.0) + jnp.log(1.0 + jnp.exp(-jnp.abs(x)))


def _sigmoid(x):
    return 1.0 / (1.0 + jnp.exp(-x))


def _inproj_body(x_ref, g_ref, wn_ref, wt_ref, gka_ref, gkb_ref, gqa_ref, gqb_ref, fbn_ref, fbt_ref,
                 selk_ref, selq_ref, shift_ref, *rest):
    rwkv_params, rest = rest[:10], rest[10:]
    qaT_ref, vaT_ref, kA_ref, qbT_ref, vbT_ref, kB_ref = rest[:6]
    rwkv_outs = rest[6:16]
    carry_n, carry_t, sbuf = rest[16:]
    tm = x_ref.shape[1]
    t = pl.program_id(1)

    @pl.when(t == 0)
    def _():
        carry_n[...] = jnp.zeros_like(carry_n)
        carry_t[...] = jnp.zeros_like(carry_t)

    x = x_ref[0]
    ms = jnp.mean(x * x, axis=-1, keepdims=True)
    h = (x * lax.rsqrt(ms + RMS_EPS) * g_ref[...]).astype(BF16)
    rows_t = lambda start, n: _dot_nt(wt_ref[start:start + n, :], h)
    cols_n = lambda start, n: _dot(h, wn_ref[:, start:start + n])
    row_v = lax.broadcasted_iota(jnp.int32, (VB_ROWS, 1), 0)
    ones_v = jnp.where(row_v == HEAD_DIM, 1.0, 0.0)
    val = {}

    def norm_rows(p, gain_ref):
        p = p.reshape(N_HEADS_A, HEAD_DIM, tm)
        return (p * lax.rsqrt(jnp.mean(p * p, axis=1, keepdims=True) + RMS_EPS) * gain_ref[...][None]).astype(BF16)

    def lane_packed3(z):
        z_hi, z_mid, z_lo = _split3(z)
        return (z_hi.astype(F32) + pltpu.roll(z_mid.astype(F32), N_HEADS_B, axis=1)
                + pltpu.roll(z_lo.astype(F32), 2 * N_HEADS_B, axis=1)).astype(BF16)

    def gates_epilogue(p):
        pn_f, pt_f = p
        lane_n = lax.broadcasted_iota(jnp.int32, (1, LANES), 1)
        lf_n = jnp.where(lane_n < N_HEADS_B, LOG2E * _log_sigmoid(pn_f + fbn_ref[...]), 0.0)
        row_t = lax.broadcasted_iota(jnp.int32, (16, 1), 0)
        lf_t = jnp.where(row_t < N_HEADS_B, LOG2E * _log_sigmoid(pt_f + fbt_ref[...]), 0.0)
        ri = lax.broadcasted_iota(jnp.int32, (tm, tm), 0)
        ci = lax.broadcasted_iota(jnp.int32, (tm, tm), 1)
        low = (ci <= ri).astype(BF16)
        upp = (ri <= ci).astype(BF16)
        cum3 = _dot(low, lane_packed3(lf_n))
        cum = cum3 + pltpu.roll(cum3, LANES - N_HEADS_B, axis=1) + pltpu.roll(cum3, LANES - 2 * N_HEADS_B, axis=1)
        val["c_n"] = jnp.where(lane_n < N_HEADS_B, cum, 0.0) + carry_n[0:1, :]
        ht, mt, lt = _split3(lf_t)
        val["c_t"] = _dot(ht, upp) + _dot(mt, upp) + _dot(lt, upp) + carry_t[:, 0:1]
        carry_n[...] = carry_n[...] + jnp.sum(lf_n, axis=0, keepdims=True)
        carry_t[...] = carry_t[...] + jnp.sum(lf_t, axis=1, keepdims=True)

    def kb_epilogue(pn_kb):
        kaug = _dot(lane_packed3(val["c_n"]), selk_ref[...])
        lane_k = lax.broadcasted_iota(jnp.int32, (1, LANES), 1)
        ones_k = jnp.where((lane_k >= HEAD_DIM) & (lane_k < HEAD_DIM + 3), 1.0, 0.0)
        for hh in range(N_HEADS_B):
            k = pn_kb[:, LANES * hh:LANES * (hh + 1)]
            msk = jnp.sum(k * k, axis=-1, keepdims=True) * (1.0 / HEAD_DIM)
            kn = k * lax.rsqrt(msk + RMS_EPS) * gkb_ref[...]
            kB_ref[0, hh] = (kn + kaug[:, LANES * hh:LANES * (hh + 1)] + ones_k).astype(BF16)

    def qb_epilogue(pt_qb):
        c3 = jnp.concatenate(_split3(val["c_t"] - shift_ref[...]), axis=0)
        qaug = _dot(selq_ref[...], c3)
        qb = pt_qb.reshape(N_HEADS_B, QB_ROWS, tm)
        msq = jnp.sum(qb * qb, axis=1, keepdims=True) * (1.0 / HEAD_DIM)
        row_q = lax.broadcasted_iota(jnp.int32, (QB_ROWS, 1), 0)
        ones_q = jnp.where((row_q >= HEAD_DIM + 3) & (row_q < HEAD_DIM + 6), 1.0, 0.0)
        qn = qb * lax.rsqrt(msq + RMS_EPS) * gqb_ref[...][None]
        qbT_ref[0] = (qn + qaug.reshape(N_HEADS_B, QB_ROWS, tm) + ones_q[None]).astype(BF16)

    def store(ref, value):
        ref[...] = value

    sections = [
        (lambda: rows_t(PT_QA, WIDTH_A), lambda p: store(qaT_ref, norm_rows(p, gqa_ref)[None])),
        (lambda: rows_t(PT_KA, WIDTH_A), lambda p: store(kA_ref, norm_rows(p, gka_ref)[None])),
        (lambda: rows_t(PT_VA, N_HEADS_A * VB_ROWS),
         lambda p: store(vaT_ref, (p.reshape(N_HEADS_A, VB_ROWS, tm) + ones_v[None]).astype(BF16)[None])),
        (lambda: (cols_n(PN_F, LANES), rows_t(PT_F, 16)), gates_epilogue),
        (lambda: rows_t(PT_VB, N_HEADS_B * VB_ROWS),
         lambda p: store(vbT_ref, (p.reshape(N_HEADS_B, VB_ROWS, tm) + ones_v[None]).astype(BF16)[None, :, None])),
        (lambda: cols_n(PN_KB, N_HEADS_B * LANES), kb_epilogue),
        (lambda: rows_t(PT_QB, N_HEADS_B * QB_ROWS), qb_epilogue),
    ]
    slots = _rwkv_prep(lambda start, n: cols_n(PN_C + start, n), tm, t, *rwkv_params, *rwkv_outs, sbuf)
    pending = None
    for mm, epilogue in sections:
        p = mm()
        if pending is not None:
            pending[1](pending[0])
        pending = (p, epilogue)
        next(slots, None)
    pending[1](pending[0])
    for _ in slots:
        pass


def _inproj(x, g, wn, wt, gka, gkb, gqa, gqb, fbn, fbt, selk, selq, shift, rwkv_params):
    b, t, d = x.shape
    tm = TOK_TILE
    nt = t // tm
    const = lambda shape: pl.BlockSpec(shape, lambda i, j: (0,) * len(shape))
    tok = lambda: pl.BlockSpec((1, tm, WIDTH_C), lambda i, j: (i, j, 0))
    bf = jax.ShapeDtypeStruct((b, t, WIDTH_C), BF16)
    out_shape = (
        jax.ShapeDtypeStruct((b, N_HEADS_A, HEAD_DIM, t), BF16),
        jax.ShapeDtypeStruct((b, N_HEADS_A, VB_ROWS, t), BF16),
        jax.ShapeDtypeStruct((b, N_HEADS_A, HEAD_DIM, t), BF16),
        jax.ShapeDtypeStruct((b, N_HEADS_B, QB_ROWS, t), BF16),
        jax.ShapeDtypeStruct((b, N_HEADS_B, nt, VB_ROWS, tm), BF16),
        jax.ShapeDtypeStruct((b, N_HEADS_B, t, LANES), BF16),
        bf, bf, bf, bf, bf, bf, bf,
        jax.ShapeDtypeStruct((b, t // RWKV_TILE, WIDTH_C, LANES), F32),
        bf, bf,
    )
    out_specs = (
        pl.BlockSpec((1, N_HEADS_A, HEAD_DIM, tm), lambda i, j: (i, 0, 0, j)),
        pl.BlockSpec((1, N_HEADS_A, VB_ROWS, tm), lambda i, j: (i, 0, 0, j)),
        pl.BlockSpec((1, N_HEADS_A, HEAD_DIM, tm), lambda i, j: (i, 0, 0, j)),
        pl.BlockSpec((1, N_HEADS_B, QB_ROWS, tm), lambda i, j: (i, 0, 0, j)),
        pl.BlockSpec((1, N_HEADS_B, 1, VB_ROWS, tm), lambda i, j: (i, 0, j, 0, 0)),
        pl.BlockSpec((1, N_HEADS_B, tm, LANES), lambda i, j: (i, 0, j, 0)),
        tok(), tok(), tok(), tok(), tok(), tok(), tok(),
        pl.BlockSpec((1, tm // RWKV_TILE, WIDTH_C, LANES), lambda i, j: (i, j, 0, 0)),
        tok(), tok(),
    )
    in_specs = [
        pl.BlockSpec((1, tm, d), lambda i, j: (i, j, 0)),
        const((1, d)), const(wn.shape), const(wt.shape),
        const(gka.shape), const(gkb.shape), const(gqa.shape), const(gqb.shape),
        const(fbn.shape), const(fbt.shape), const(selk.shape), const(selq.shape), const(shift.shape),
    ] + [const(p.shape) for p in rwkv_params]
    return pl.pallas_call(
        _inproj_body,
        grid=(b, nt),
        in_specs=in_specs,
        out_specs=out_specs,
        out_shape=out_shape,
        scratch_shapes=[pltpu.VMEM((8, LANES), F32), pltpu.VMEM((16, LANES), F32),
                        pltpu.VMEM((tm + 8, COLS_C), F32)],
        compiler_params=_cparams(2),
        name="inproj",
    )(x, g, wn, wt, gka, gkb, gqa, gqb, fbn, fbt, selk, selq, shift, *rwkv_params)


def _relbias_body(tab_ref, shift_ref, o_ref):
    hh = pl.program_id(0)
    kj = lax.broadcasted_iota(jnp.int32, (ATT_A_WIN, ATT_A_TILE), 0)
    qi = lax.broadcasted_iota(jnp.int32, (ATT_A_WIN, ATT_A_TILE), 1)
    rel = jnp.clip(kj - LEFT_CHUNKS * CHUNK - qi, -MAX_REL, MAX_REL) + MAX_REL

    def body(r, acc):
        return jnp.where(rel == r, tab_ref[hh, r], acc)

    bias = lax.fori_loop(0, 2 * MAX_REL + 1, body, jnp.zeros((ATT_A_WIN, ATT_A_TILE), F32))
    kc = kj // CHUNK
    qc = qi // CHUNK
    band = (kc >= qc) & (kc <= qc + LEFT_CHUNKS)
    o_ref[0] = jnp.where(band, LOG2E * bias - shift_ref[hh], NEG_INF)


def _relbias(tab, shift):
    return pl.pallas_call(
        _relbias_body,
        grid=(N_HEADS_A,),
        in_specs=[pl.BlockSpec(memory_space=pltpu.SMEM), pl.BlockSpec(memory_space=pltpu.SMEM)],
        out_specs=pl.BlockSpec((1, ATT_A_WIN, ATT_A_TILE), lambda i: (i, 0, 0)),
        out_shape=jax.ShapeDtypeStruct((N_HEADS_A, ATT_A_WIN, ATT_A_TILE), F32),
        compiler_params=_cparams(1),
        name="relbias",
    )(tab, shift)


def _attn_a_body(q_ref, k0_ref, k1_ref, k2_ref, v0_ref, v1_ref, v2_ref, bias_ref, o_ref, *, online):
    i = pl.program_id(1)
    tq = ATT_A_TILE
    kj = lax.broadcasted_iota(jnp.int32, (tq, 1), 0)

    def run(mask_padding):
        scores = []
        for hh in range(N_HEADS_A):
            q = q_ref[0, hh]
            ss = []
            for d, k_ref in enumerate((k0_ref, k1_ref, k2_ref)):
                s = _dot_tn(k_ref[0, hh], q) + bias_ref[hh, d * tq:(d + 1) * tq, :]
                if mask_padding:
                    s = jnp.where(kj + (i - 2 + d) * tq >= 0, s, NEG_INF)
                ss.append(s)
            scores.append(ss)
        for hh, ss in enumerate(scores):
            if online:
                m = functools.reduce(jnp.maximum, [jnp.max(s, axis=0, keepdims=True) for s in ss])
                ss = [s - m for s in ss]
            acc = jnp.zeros((VB_ROWS, tq), F32)
            for s, v_ref in zip(ss, (v0_ref, v1_ref, v2_ref)):
                acc = acc + _dot(v_ref[0, hh], jnp.exp2(s).astype(BF16))
            o_ref[0, hh] = (acc[0:HEAD_DIM] / acc[HEAD_DIM:HEAD_DIM + 1]).astype(BF16)

    pl.when(i < 2)(functools.partial(run, True))
    pl.when(i >= 2)(functools.partial(run, False))


def _attn_a(qaT, kA, vaT, bias, online):
    b, nh, _, t = qaT.shape
    tq = ATT_A_TILE
    kspec = lambda d: pl.BlockSpec((1, nh, HEAD_DIM, tq),
                                   lambda bb, i: (bb, 0, 0, jnp.maximum(i - 2 + d, 0)))
    vspec = lambda d: pl.BlockSpec((1, nh, VB_ROWS, tq),
                                   lambda bb, i: (bb, 0, 0, jnp.maximum(i - 2 + d, 0)))
    return pl.pallas_call(
        functools.partial(_attn_a_body, online=online),
        grid=(b, t // tq),
        in_specs=[pl.BlockSpec((1, nh, HEAD_DIM, tq), lambda bb, i: (bb, 0, 0, i)),
                  kspec(0), kspec(1), kspec(2), vspec(0), vspec(1), vspec(2),
                  pl.BlockSpec((nh, ATT_A_WIN, tq), lambda bb, i: (0, 0, 0))],
        out_specs=pl.BlockSpec((1, nh, HEAD_DIM, tq), lambda bb, i: (bb, 0, 0, i)),
        out_shape=jax.ShapeDtypeStruct((b, nh, HEAD_DIM, t), BF16),
        compiler_params=_cparams(2),
        name="attn_a_online" if online else "attn_a",
    )(qaT, kA, kA, kA, vaT, vaT, vaT, bias)


def _fox_body(q_ref, k_ref, v_ref, o_ref, *, online):
    i = pl.program_id(2)
    tq = FOX_TILE
    nh = q_ref.shape[1]
    q = [jnp.concatenate([q_ref[0, hh], jnp.zeros((LANES - QB_ROWS, tq), BF16)], axis=0) for hh in range(nh)]

    def scores(item):
        j, hh = item
        ks = k_ref[0, hh, pl.ds(pl.multiple_of(j * tq, tq), tq), :]
        return _dot(ks, q[hh])

    def accumulate(item, s, m, acc, diagonal):
        j, hh = item
        vs = v_ref[0, hh, j]
        if diagonal:
            kj = lax.broadcasted_iota(jnp.int32, (tq, tq), 0)
            qi = lax.broadcasted_iota(jnp.int32, (tq, tq), 1)
            causal = kj <= qi
        if online:
            if diagonal:
                s = jnp.where(causal, s, NEG_INF)
            m_new = jnp.maximum(m, jnp.max(s, axis=0, keepdims=True))
            acc = jnp.exp2(m - m_new) * acc
            p = jnp.exp2(s - m_new)
            m = m_new
        else:
            p = jnp.exp2(s)
            if diagonal:
                p = jnp.where(causal, p, 0.0)
        return m, acc + _dot(vs, p.astype(BF16))

    def tiles(js, carry, last_diagonal):
        state = [list(c) for c in carry]
        items = [(j, hh) for j in js for hh in range(nh)]
        ss = []

        def finish(n):
            hh = items[n][1]
            diagonal = last_diagonal and n >= len(items) - nh
            state[hh] = list(accumulate(items[n], ss[n], state[hh][0], state[hh][1], diagonal))

        for n, item in enumerate(items):
            ss.append(scores(item))
            if n >= FOX_AHEAD:
                finish(n - FOX_AHEAD)
        for n in range(max(len(items) - FOX_AHEAD, 0), len(items)):
            finish(n)
        return tuple(tuple(c) for c in state)

    init = tuple((jnp.full((1, tq), NEG_INF, F32), jnp.zeros((VB_ROWS, tq), F32)) for _ in range(nh))
    if online:
        carry = lax.fori_loop(0, i, lambda j, c: tiles([j], c, False), init)
        carry = tiles([i], carry, True)
    else:
        g = FOX_GROUP
        carry = lax.fori_loop(0, i // g, lambda jj, c: tiles([g * jj + n for n in range(g)], c, False), init)
        tails = [functools.partial(lambda c, r: tiles([i - r + n for n in range(r + 1)], c, True), r=r)
                 for r in range(g)]
        carry = lax.switch(i % g, tails, carry)
    for hh, (_, acc) in enumerate(carry):
        o_ref[0, hh] = (acc[0:HEAD_DIM] / acc[HEAD_DIM:HEAD_DIM + 1]).astype(BF16)


def _fox(qbT, kB, vbT, online):
    b, nh, _, t = qbT.shape
    tq = FOX_TILE
    nk = t // tq
    return pl.pallas_call(
        functools.partial(_fox_body, online=online),
        grid=(b, nh // FOX_HEADS, nk),
        in_specs=[pl.BlockSpec((1, FOX_HEADS, QB_ROWS, tq), lambda bb, hh, i: (bb, hh, 0, i)),
                  pl.BlockSpec((1, FOX_HEADS, t, LANES), lambda bb, hh, i: (bb, hh, 0, 0)),
                  pl.BlockSpec((1, FOX_HEADS, nk, VB_ROWS, tq), lambda bb, hh, i: (bb, hh, 0, 0, 0))],
        out_specs=pl.BlockSpec((1, FOX_HEADS, HEAD_DIM, tq), lambda bb, hh, i: (bb, hh, 0, i)),
        out_shape=jax.ShapeDtypeStruct((b, nh, HEAD_DIM, t), BF16),
        compiler_params=_cparams(3),
        name="fox_online" if online else "fox",
    )(qbT, kB, vbT)


def _rwkv_prep(proj, tm, t, mu_ref, w0_ref, w2_ref, a0_ref, a2_ref, g2_ref, kk_ref, ka_ref, rk_ref, e_ref,
               rt_ref, at_ref, bt_ref, kt_ref, bh_ref, kh_ref, v_ref, wc_ref, bonus_ref, g_ref, sbuf):
    @pl.when(t == 0)
    def _():
        sbuf[0:8, :] = jnp.zeros((8, COLS_C), F32)

    def shifted(p, start):
        cols = slice(start, start + p.shape[1])
        sbuf[8:8 + tm, cols] = p
        prev = sbuf[7:7 + tm, cols]
        sbuf[0:8, cols] = p[tm - 8:tm, :]
        return p + (prev - p) * mu_ref[:, cols]

    c = WIDTH_C
    p_lo = proj(3 * c, COLS_C - 3 * c)
    p_k = proj(c, c)
    u_lo = shifted(p_lo, 3 * c)
    yield
    w_lo = u_lo[:, 0:DECAY_LORA]
    a_lo = u_lo[:, DECAY_LORA:DECAY_LORA + AAA_LORA]
    g_lo = u_lo[:, DECAY_LORA + AAA_LORA:]
    w = w0_ref[...] + _dot(jnp.tanh(w_lo).astype(BF16), w2_ref[...])
    w = -_softplus(-w) - 0.5
    ld = -jnp.exp(w)
    p_r = proj(0, c)
    yield
    a = _sigmoid(a0_ref[...] + _dot(a_lo.astype(BF16), a2_ref[...]))
    g_ref[0] = _dot(_sigmoid(g_lo).astype(BF16), g2_ref[...]).astype(BF16)
    k = shifted(p_k, c)
    p_v = proj(2 * c, c)
    yield

    e = e_ref[...]
    kk = k * kk_ref[...]
    nrm2 = _dot((kk * kk).astype(BF16), e)
    kkn = kk * lax.rsqrt(jnp.maximum(nrm2, 1e-24))
    k2 = k * (1.0 + (a - 1.0) * ka_ref[...])
    kka = kkn * a
    r = shifted(p_r, 0)
    v = shifted(p_v, 2 * c)
    v_ref[0] = v.astype(BF16)
    b_hi, b_lo = _split2(r * k2 * rk_ref[...])
    bonus_ref[0] = ((_dot(b_hi, e) + _dot(b_lo, e)) * v).astype(BF16)
    yield

    ri = lax.broadcasted_iota(jnp.int32, (2 * CHUNK, CHUNK), 0)
    ci = lax.broadcasted_iota(jnp.int32, (2 * CHUNK, CHUNK), 1)
    tri = ((ci <= ri) | (ri >= CHUNK)).astype(BF16)
    tri2 = jnp.concatenate([tri, tri], axis=1)
    ld_hi, ld_lo = _split2(ld)
    for cc in range(tm // CHUNK):
        sl = slice(cc * CHUNK, (cc + 1) * CHUNK)
        cum = _dot(tri2, jnp.concatenate([ld_hi[sl], ld_lo[sl]], axis=0))
        lc = cum[0:CHUNK]
        tot = cum[CHUNK:2 * CHUNK]
        e_neg = jnp.exp(-lc)
        e_rem = jnp.exp(tot - lc)
        rt_ref[0, sl, :] = (r[sl] * jnp.exp(lc)).astype(BF16)
        at_ref[0, sl, :] = (-kkn[sl] * jnp.exp(lc - ld[sl])).astype(BF16)
        bt_ref[0, sl, :] = (kka[sl] * e_neg).astype(BF16)
        kt_ref[0, sl, :] = (k2[sl] * e_neg).astype(BF16)
        bh_ref[0, sl, :] = (kka[sl] * e_rem).astype(BF16)
        kh_ref[0, sl, :] = (k2[sl] * e_rem).astype(BF16)
        if cc % 2 == 1:
            yield
    seg = (lax.broadcasted_iota(jnp.int32, (16, tm), 1) // CHUNK
           == lax.broadcasted_iota(jnp.int32, (16, tm), 0)).astype(BF16)
    tots3 = jnp.concatenate(_split3(_dot(seg, ld_hi) + _dot(seg, ld_lo)), axis=0)
    per_blk = RWKV_TILE // CHUNK
    for blk in range(tm // RWKV_TILE):
        put = (lax.broadcasted_iota(jnp.int32, (48, LANES), 0) % 16
               == lax.broadcasted_iota(jnp.int32, (48, LANES), 1) + blk * per_blk).astype(BF16)
        wc_ref[0, blk] = jnp.exp(_dot_tn(tots3, put))


def _bmm(a, b):
    return lax.dot_general(a, b, (((2,), (1,)), ((0,), (0,))), preferred_element_type=F32)


def _bmm_nt(a, b):
    return lax.dot_general(a, b, (((2,), (2,)), ((0,), (0,))), preferred_element_type=F32)


def _bmm_tn(a, b):
    return lax.dot_general(a, b, (((1,), (1,)), ((0,), (0,))), preferred_element_type=F32)


def _rwkv_chunk_body(rt_ref, at_ref, bt_ref, kt_ref, bh_ref, kh_ref, v_ref, wc_ref, bonus_ref, g_ref,
                     lg_ref, lb_ref, e_ref, o_ref, h_ref):
    rows, tm = rt_ref.shape[0], rt_ref.shape[1]
    nc = tm // CHUNK
    ng = N_HEADS_C // RWKV_GROUP
    gl = RWKV_GROUP * HEAD_DIM
    units = [(r, g) for r in range(rows) for g in range(ng)]
    nu = len(units)
    t = pl.program_id(1)

    @pl.when(t == 0)
    def _():
        h_ref[...] = jnp.zeros_like(h_ref)

    lane_head = lax.broadcasted_iota(jnp.int32, (1, CHUNK, gl), 2) // HEAD_DIM
    ri = lax.broadcasted_iota(jnp.int32, (1, 2 * CHUNK, gl), 1)
    ci = lax.broadcasted_iota(jnp.int32, (1, 2 * CHUNK, gl), 2) & (CHUNK - 1)
    rr = ri & (CHUNK - 1)
    keep = (rr > ci) | ((ri >= CHUNK) & (rr == ci))
    eye = (lax.broadcasted_iota(jnp.int32, (1, CHUNK, gl), 1)
           == (lax.broadcasted_iota(jnp.int32, (1, CHUNK, gl), 2) & (CHUNK - 1)))
    same_head = (lax.broadcasted_iota(jnp.int32, (1, gl, gl), 1) // HEAD_DIM
                 == lax.broadcasted_iota(jnp.int32, (1, gl, gl), 2) // HEAD_DIM)

    def bdiag(x):
        return jnp.concatenate([jnp.where(lane_head == hh, x, jnp.zeros_like(x)) for hh in range(RWKV_GROUP)],
                               axis=1)

    def chunk_local(cs, out):
        tile = lambda ref: jnp.stack([ref[r, c * CHUNK:(c + 1) * CHUNK, g * gl:(g + 1) * gl]
                                      for c in cs for r, g in units])
        rt, at, bt, kt, bh, kh, v = (tile(r) for r in (rt_ref, at_ref, bt_ref, kt_ref, bh_ref, kh_ref, v_ref))
        ar = jnp.concatenate([at, rt], axis=1)
        sb = jnp.where(keep, _bmm_nt(ar, bdiag(bt)), 0.0)
        sk = jnp.where(keep, _bmm_nt(ar, bdiag(kt)), 0.0).astype(BF16)
        m_rb = sb[:, CHUNK:].astype(BF16)
        yield
        a = sb[:, :CHUNK]
        tinv = jnp.where(eye, 1.0, 0.0) + a
        ab = a.astype(BF16)
        x = _bmm(ab, bdiag(ab))
        yield
        for _ in range(4):
            xb = x.astype(BF16)
            r = _bmm(jnp.concatenate([tinv.astype(BF16), xb], axis=1), bdiag(xb))
            tinv = tinv + r[:, :CHUNK]
            x = r[:, CHUNK:]
            yield
        tinv = (tinv + _bmm(tinv.astype(BF16), bdiag(x.astype(BF16)))).astype(BF16)
        bdv = bdiag(v)
        av = _bmm(sk[:, :CHUNK], bdv)
        yield
        p = _bmm(tinv, bdiag(at)).astype(BF16)
        u0 = _bmm(tinv, bdiag(av.astype(BF16)))
        yield
        q = (rt.astype(F32) + _bmm(m_rb, bdiag(p))).astype(BF16)
        y0 = _bmm(m_rb, bdiag(u0.astype(BF16))) + _bmm(sk[:, CHUNK:], bdv)
        parts = (jnp.concatenate([q, p], axis=1), jnp.concatenate([y0, u0], axis=1),
                 jnp.concatenate([bh, kh], axis=1), v)
        out += [tuple(z[n * nu:(n + 1) * nu] for z in parts) for n in range(len(cs))]
        yield

    wct = [wc_ref[r, 0] for r in range(rows)]
    state = [h_ref[...]]
    ys = []
    local = []

    def recur(c):
        qp, yu0, bk, v = local[c]
        yu = _bmm(qp, state[0].astype(BF16)) + yu0
        ys.append(yu[:, :CHUNK])
        upd = _bmm_tn(bk, jnp.concatenate([yu[:, CHUNK:].astype(BF16), v], axis=1))
        wcol = jnp.stack([wct[r][g * gl:(g + 1) * gl, c:c + 1] for r, g in units])
        state[0] = jnp.where(same_head, upd, 0.0) + wcol * state[0]

    assert sorted(c for cs in RWKV_SCHEDULE for c in cs) == list(range(nc))
    done = 0
    for cs in RWKV_SCHEDULE:
        pending = list(range(done, len(local)))
        slots = {(k + 1) * RWKV_STAGES // (len(pending) + 1): c for k, c in enumerate(pending)}
        for n, _ in enumerate(chunk_local(list(cs), local), start=1):
            if n in slots:
                recur(slots[n])
                done += 1
    for c in range(done, nc):
        recur(c)
    h_ref[...] = state[0]

    y = jnp.concatenate([jnp.concatenate([yc[r * ng + g] for g in range(ng)], axis=-1)
                         for r in range(rows) for yc in ys], axis=0)
    e = e_ref[...]

    def head_mean(z):
        return _dot(z.astype(BF16), e) * (1.0 / HEAD_DIM)

    d = y - head_mean(y)
    yn = d * lax.rsqrt(head_mean(d * d) + LNX_EPS)
    flat = lambda ref: ref[...].astype(F32).reshape(rows * tm, WIDTH_C)
    out = (yn * lg_ref[...] + lb_ref[...] + flat(bonus_ref)) * flat(g_ref)
    o_ref[...] = out.astype(BF16).reshape(rows, tm, WIDTH_C)


def _rwkv_chunk(rt, at, bt, kt, bh, kh, v, wc, bonus, g, lnx_g, lnx_b, e):
    b, t, _ = rt.shape
    tm = RWKV_TILE
    rows = math.gcd(RWKV_ROWS, b)
    gl = RWKV_GROUP * HEAD_DIM
    const = lambda shape: pl.BlockSpec(shape, lambda i, j: (0,) * len(shape))
    tok = lambda: pl.BlockSpec((rows, tm, WIDTH_C), lambda i, j: (i, j, 0))
    return pl.pallas_call(
        _rwkv_chunk_body,
        grid=(b // rows, t // tm),
        in_specs=[tok(), tok(), tok(), tok(), tok(), tok(), tok(),
                  pl.BlockSpec((rows, 1, WIDTH_C, LANES), lambda i, j: (i, j, 0, 0)),
                  tok(), tok(), const(lnx_g.shape), const(lnx_b.shape), const(e.shape)],
        out_specs=tok(),
        out_shape=jax.ShapeDtypeStruct((b, t, WIDTH_C), BF16),
        scratch_shapes=[pltpu.VMEM((rows * (N_HEADS_C // RWKV_GROUP), gl, gl), F32)],
        compiler_params=_cparams(2),
        name="rwkv_chunk",
    )(rt, at, bt, kt, bh, kh, v, wc, bonus, g, lnx_g, lnx_b, e)


def _outproj_body(x_ref, ya_ref, yb_ref, yc_ref, wa_ref, wb_ref, wc_ref, o_ref):
    acc = _dot_tn(ya_ref[0], wa_ref[...])
    acc = acc + _dot_tn(yb_ref[0], wb_ref[...])
    acc = acc + _dot(yc_ref[0], wc_ref[...])
    o_ref[0] = x_ref[0] + acc


def _outproj(x, yaT, ybT, yc, wa, wb, wc):
    b, t, d = x.shape
    tm = TOK_TILE
    const = lambda shape: pl.BlockSpec(shape, lambda i, j: (0,) * len(shape))
    return pl.pallas_call(
        _outproj_body,
        grid=(b, t // tm),
        in_specs=[pl.BlockSpec((1, tm, d), lambda i, j: (i, j, 0)),
                  pl.BlockSpec((1, WIDTH_A, tm), lambda i, j: (i, 0, j)),
                  pl.BlockSpec((1, WIDTH_B, tm), lambda i, j: (i, 0, j)),
                  pl.BlockSpec((1, tm, WIDTH_C), lambda i, j: (i, j, 0)),
                  const(wa.shape), const(wb.shape), const(wc.shape)],
        out_specs=pl.BlockSpec((1, tm, d), lambda i, j: (i, j, 0)),
        out_shape=jax.ShapeDtypeStruct((b, t, d), F32),
        compiler_params=_cparams(2),
        name="outproj",
    )(x, yaT, ybT, yc, wa, wb, wc)


def _ffn_body(x_ref, g_ref, wg_ref, wv_ref, cw_ref, cb_ref, wd_ref, o_ref, gbuf):
    tm = x_ref.shape[1]
    t = pl.program_id(1)

    @pl.when(t == 0)
    def _():
        gbuf[0:8, :] = jnp.zeros((8, D_FF), F32)

    x = x_ref[0]
    ms = jnp.mean(x * x, axis=-1, keepdims=True)
    h = (x * lax.rsqrt(ms + RMS_EPS) * g_ref[...]).astype(BF16)
    acc = x
    for lo, hi in zip(FF_SPLITS[:-1], FF_SPLITS[1:]):
        gate = _dot(h, wg_ref[:, lo:hi])
        val = _dot(h, wv_ref[:, lo:hi])
        gbuf[8:8 + tm, lo:hi] = gate
        g1 = gbuf[7:7 + tm, lo:hi]
        g2 = gbuf[6:6 + tm, lo:hi]
        gbuf[0:8, lo:hi] = gate[tm - 8:tm, :]
        conv = cb_ref[:, lo:hi] + g2 * cw_ref[0:1, lo:hi] + g1 * cw_ref[1:2, lo:hi] + gate * cw_ref[2:3, lo:hi]
        act = (conv * _sigmoid(conv) * val).astype(BF16)
        acc = acc + _dot(act, wd_ref[lo:hi, :])
    o_ref[0] = acc


def _ffn(x, g, wg, wv, cw, cb, wd):
    b, t, d = x.shape
    tm = TOK_TILE
    const = lambda shape: pl.BlockSpec(shape, lambda i, j: (0,) * len(shape),
                                       pipeline_mode=pl.Buffered(1))
    return pl.pallas_call(
        _ffn_body,
        grid=(b, t // tm),
        in_specs=[pl.BlockSpec((1, tm, d), lambda i, j: (i, j, 0)),
                  const(g.shape), const(wg.shape), const(wv.shape), const(cw.shape), const(cb.shape),
                  const(wd.shape)],
        out_specs=pl.BlockSpec((1, tm, d), lambda i, j: (i, j, 0)),
        out_shape=jax.ShapeDtypeStruct((b, t, d), F32),
        scratch_shapes=[pltpu.VMEM((tm + 8, D_FF), F32)],
        compiler_params=_cparams(2),
        name="ffn",
    )(x, g, wg, wv, cw, cb, wd)


def _pad_heads_cols(w, nh):
    d = w.shape[0]
    w = w.reshape(d, nh, HEAD_DIM)
    return jnp.pad(w, ((0, 0), (0, 0), (0, LANES - HEAD_DIM))).reshape(d, nh * LANES)


def _qk_bound(gq, gk):
    scale = HEAD_DIM ** -0.5
    return 1.01 * HEAD_DIM * scale * jnp.max(jnp.abs(gq)) * jnp.max(jnp.abs(gk)) + 0.05


def _layer_params(l, w_in, q_norm_a, k_norm_a, rel_bias, q_norm_b, k_norm_b, forget_bias):
    w = w_in[l]
    scale = HEAD_DIM ** -0.5
    a0, b0 = 0, 3 * WIDTH_A
    c0 = b0 + 3 * WIDTH_B + N_HEADS_B
    qa, ka, va = (w[:, a0 + i * WIDTH_A:a0 + (i + 1) * WIDTH_A] for i in range(3))
    qb, kb, vb = (w[:, b0 + i * WIDTH_B:b0 + (i + 1) * WIDTH_B] for i in range(3))
    fg = w[:, b0 + 3 * WIDTH_B:c0]
    wc = w[:, c0:]
    wn = jnp.concatenate([_pad_heads_cols(kb, N_HEADS_B),
                          jnp.pad(fg, ((0, 0), (0, LANES - N_HEADS_B))), wc], axis=1).astype(BF16)
    pad_v = lambda v, nh: jnp.pad(v.reshape(-1, nh, HEAD_DIM),
                                  ((0, 0), (0, 0), (0, VB_ROWS - HEAD_DIM))).reshape(-1, nh * VB_ROWS)
    assert QB_ROWS == VB_ROWS
    wt = jnp.concatenate([qa, ka, pad_v(va, N_HEADS_A), pad_v(qb, N_HEADS_B), pad_v(vb, N_HEADS_B),
                          jnp.pad(fg, ((0, 0), (0, 16 - N_HEADS_B)))], axis=1).T.astype(BF16)
    pad = LANES - HEAD_DIM
    gka = k_norm_a[l].reshape(HEAD_DIM, 1)
    gkb = jnp.pad(k_norm_b[l], (0, pad)).reshape(1, LANES)
    gqa = (q_norm_a[l] * (scale * LOG2E)).reshape(HEAD_DIM, 1)
    gqb = jnp.pad(q_norm_b[l] * (scale * LOG2E), (0, QB_ROWS - HEAD_DIM)).reshape(QB_ROWS, 1)
    fbn = jnp.pad(forget_bias[l], (0, LANES - N_HEADS_B)).reshape(1, LANES)
    fbt = jnp.pad(forget_bias[l], (0, 16 - N_HEADS_B)).reshape(16, 1)
    shift_b = (LOG2E * _qk_bound(q_norm_b[l], k_norm_b[l])).reshape(1, 1)
    bound_a = _qk_bound(q_norm_a[l], k_norm_a[l])
    tab = rel_bias[l]
    shift_a = LOG2E * (bound_a + jnp.max(tab, axis=1))
    depth_a = jnp.max(shift_a + LOG2E * (bound_a - tab[:, MAX_REL]))
    return wn, wt, gka, gkb, gqa, gqb, fbn, fbt, shift_b, shift_a, depth_a


def _selectors():
    import numpy as np
    selk = np.zeros((LANES, N_HEADS_B * LANES), np.float32)
    selq = np.zeros((N_HEADS_B * QB_ROWS, 3 * 16), np.float32)
    for p in range(3):
        for hh in range(N_HEADS_B):
            selk[p * N_HEADS_B + hh, hh * LANES + HEAD_DIM + 3 + p] = -1.0
            selq[hh * QB_ROWS + HEAD_DIM + p, 16 * p + hh] = 1.0
    e = np.kron(np.eye(N_HEADS_C, dtype=np.float32), np.ones((HEAD_DIM, HEAD_DIM), np.float32))
    return jnp.asarray(selk, BF16), jnp.asarray(selq, BF16), jnp.asarray(e, BF16)


def kernel(x, mix_norm_g, w_in, q_norm_a, k_norm_a, rel_bias, q_norm_b, k_norm_b, forget_bias, shift_mu, w0, w2,
           a0, a2, g2, k_k, k_a, r_k, lnx_g, lnx_b, w_out, ffn_norm_g, w_up, conv_w, conv_b, w_down):
    depth = w_in.shape[0]
    b, t, d = x.shape
    selk, selq, e = _selectors()
    row = lambda v: v.reshape(1, -1)
    for l in range(depth):
        wn, wt, gka, gkb, gqa, gqb, fbn, fbt, shift_b, shift_a, depth_a = _layer_params(
            l, w_in, q_norm_a, k_norm_a, rel_bias, q_norm_b, k_norm_b, forget_bias)
        rwkv_params = (row(shift_mu[l]), row(w0[l]), w2[l].astype(BF16), row(a0[l]), a2[l].astype(BF16),
                       g2[l].astype(BF16), row(k_k[l]), row(k_a[l]), row(r_k[l]), e)
        (qaT, vaT, kA, qbT, vbT, kB, rt, at, bt, kt, bh, kh, v, wcum, bonus, g) = _inproj(
            x, row(mix_norm_g[l]), wn, wt, gka, gkb, gqa, gqb, fbn, fbt, selk, selq, shift_b, rwkv_params)
        bias = _relbias(rel_bias[l], shift_a)
        yaT = lax.cond(depth_a <= 2 * FAST_MAX_SHIFT,
                       functools.partial(_attn_a, online=False), functools.partial(_attn_a, online=True),
                       qaT, kA, vaT, bias)
        ybT = lax.cond(shift_b[0, 0] <= FAST_MAX_SHIFT,
                       functools.partial(_fox, online=False), functools.partial(_fox, online=True),
                       qbT, kB, vbT)
        yc = _rwkv_chunk(rt, at, bt, kt, bh, kh, v, wcum, bonus, g, row(lnx_g[l]), row(lnx_b[l]), e)
        wo = w_out[l].astype(BF16)
        x = _outproj(x, yaT.reshape(b, WIDTH_A, t), ybT.reshape(b, WIDTH_B, t), yc,
                     wo[:WIDTH_A], wo[WIDTH_A:WIDTH_A + WIDTH_B], wo[WIDTH_A + WIDTH_B:])
        wu = w_up[l].astype(BF16)
        x = _ffn(x, row(ffn_norm_g[l]), wu[:, :D_FF], wu[:, D_FF:], conv_w[l], row(conv_b[l]),
                 w_down[l].astype(BF16))
    return x
```

```python
import functools
import math

import jax
import jax.numpy as jnp
from jax import lax
from jax.experimental import pallas as pl
from jax.experimental.pallas import tpu as pltpu

F32 = jnp.float32
BF16 = jnp.bfloat16

D_MODEL = 1024
HEAD_DIM = 64
CHUNK = 64
LEFT_CHUNKS = 8
MAX_REL = 128
N_HEADS_A = 4
N_HEADS_B = 4
N_HEADS_C = 8
WIDTH_A = N_HEADS_A * HEAD_DIM
WIDTH_B = N_HEADS_B * HEAD_DIM
WIDTH_C = N_HEADS_C * HEAD_DIM
DECAY_LORA = 64
AAA_LORA = 64
GATE_LORA = 128
COLS_C = 3 * WIDTH_C + DECAY_LORA + AAA_LORA + GATE_LORA
D_FF = 2816
RMS_EPS = 1e-6
LNX_EPS = 64e-5
NEG_INF = -1e30

LANES = 128
TOK_TILE = 512
ATT_A_TILE = 256
ATT_A_WIN = 3 * ATT_A_TILE
ATT_A_AHEAD = 3
FOX_TILE = 512
FOX_HEADS = 2
FOX_GROUP = 8
FOX_AHEAD = 2
RWKV_GROUP = 4
RWKV_TILE = 256
RWKV_ROWS = 4
RWKV_SCHEDULE = ((0, 1), (2, 3))
RWKV_STAGES = 9
FF_SPLITS = (0, 768, 1536, 2304, 2816)
VMEM_LIMIT = 56 * 1024 * 1024

PN_KB, PN_F, PN_C = 0, 512, 640
PN_COLS = PN_C + COLS_C
VB_ROWS = 80
PT_QA = 0
PT_KA = PT_QA + WIDTH_A
PT_VA = PT_KA + WIDTH_A
PT_QB = PT_VA + N_HEADS_A * VB_ROWS
QB_ROWS = 80
PT_VB = PT_QB + N_HEADS_B * QB_ROWS
PT_F = PT_VB + N_HEADS_B * VB_ROWS
PT_ROWS = PT_F + 16
LOG2E = 1.4426950408889634
FAST_MAX_SHIFT = 40.0


def _cparams(n_axes):
    return pltpu.CompilerParams(dimension_semantics=("arbitrary",) * n_axes,
                                vmem_limit_bytes=VMEM_LIMIT)


def _split3(x):
    hi = x.astype(BF16)
    r1 = x - hi.astype(F32)
    mid = r1.astype(BF16)
    lo = (r1 - mid.astype(F32)).astype(BF16)
    return hi, mid, lo


def _split2(x):
    hi = x.astype(BF16)
    lo = (x - hi.astype(F32)).astype(BF16)
    return hi, lo


def _dot(a, b):
    return jnp.dot(a, b, preferred_element_type=F32)


def _dot_nt(a, b):
    return lax.dot_general(a, b, (((1,), (1,)), ((), ())), preferred_element_type=F32)


def _dot_tn(a, b):
    return lax.dot_general(a, b, (((0,), (0,)), ((), ())), preferred_element_type=F32)


def _log_sigmoid(x):
    return jnp.minimum(x, 0.0) - jnp.log(1.0 + jnp.exp(-jnp.abs(x)))


def _softplus(x):
    return jnp.maximum(x, 0.0) + jnp.log(1.0 + jnp.exp(-jnp.abs(x)))


def _sigmoid(x):
    return 1.0 / (1.0 + jnp.exp(-x))


def _inproj_body(x_ref, g_ref, wn_ref, wt_ref, gka_ref, gkb_ref, gqa_ref, gqb_ref, fbn_ref, fbt_ref,
                 selk_ref, selq_ref, shift_ref, *rest):
    rwkv_params, rest = rest[:10], rest[10:]
    qaT_ref, vaT_ref, kA_ref, qbT_ref, vbT_ref, kB_ref = rest[:6]
    rwkv_outs = rest[6:16]
    carry_n, carry_t, sbuf = rest[16:]
    tm = x_ref.shape[1]
    t = pl.program_id(1)

    @pl.when(t == 0)
    def _():
        carry_n[...] = jnp.zeros_like(carry_n)
        carry_t[...] = jnp.zeros_like(carry_t)

    x = x_ref[0]
    ms = jnp.mean(x * x, axis=-1, keepdims=True)
    h = (x * lax.rsqrt(ms + RMS_EPS) * g_ref[...]).astype(BF16)
    rows_t = lambda start, n: _dot_nt(wt_ref[start:start + n, :], h)
    cols_n = lambda start, n: _dot(h, wn_ref[:, start:start + n])
    row_v = lax.broadcasted_iota(jnp.int32, (VB_ROWS, 1), 0)
    ones_v = jnp.where(row_v == HEAD_DIM, 1.0, 0.0)
    val = {}

    def norm_rows(p, gain_ref):
        p = p.reshape(N_HEADS_A, HEAD_DIM, tm)
        return (p * lax.rsqrt(jnp.mean(p * p, axis=1, keepdims=True) + RMS_EPS) * gain_ref[...][None]).astype(BF16)

    def lane_packed3(z):
        z_hi, z_mid, z_lo = _split3(z)
        return (z_hi.astype(F32) + pltpu.roll(z_mid.astype(F32), N_HEADS_B, axis=1)
                + pltpu.roll(z_lo.astype(F32), 2 * N_HEADS_B, axis=1)).astype(BF16)

    def gates_epilogue(p):
        pn_f, pt_f = p
        lane_n = lax.broadcasted_iota(jnp.int32, (1, LANES), 1)
        lf_n = jnp.where(lane_n < N_HEADS_B, LOG2E * _log_sigmoid(pn_f + fbn_ref[...]), 0.0)
        row_t = lax.broadcasted_iota(jnp.int32, (16, 1), 0)
        lf_t = jnp.where(row_t < N_HEADS_B, LOG2E * _log_sigmoid(pt_f + fbt_ref[...]), 0.0)
        ri = lax.broadcasted_iota(jnp.int32, (tm, tm), 0)
        ci = lax.broadcasted_iota(jnp.int32, (tm, tm), 1)
        low = (ci <= ri).astype(BF16)
        upp = (ri <= ci).astype(BF16)
        cum3 = _dot(low, lane_packed3(lf_n))
        cum = cum3 + pltpu.roll(cum3, LANES - N_HEADS_B, axis=1) + pltpu.roll(cum3, LANES - 2 * N_HEADS_B, axis=1)
        val["c_n"] = jnp.where(lane_n < N_HEADS_B, cum, 0.0) + carry_n[0:1, :]
        ht, mt, lt = _split3(lf_t)
        val["c_t"] = _dot(ht, upp) + _dot(mt, upp) + _dot(lt, upp) + carry_t[:, 0:1]
        carry_n[...] = carry_n[...] + jnp.sum(lf_n, axis=0, keepdims=True)
        carry_t[...] = carry_t[...] + jnp.sum(lf_t, axis=1, keepdims=True)

    def kb_epilogue(pn_kb):
        kaug = _dot(lane_packed3(val["c_n"]), selk_ref[...])
        lane_k = lax.broadcasted_iota(jnp.int32, (1, LANES), 1)
        ones_k = jnp.where((lane_k >= HEAD_DIM) & (lane_k < HEAD_DIM + 3), 1.0, 0.0)
        for hh in range(N_HEADS_B):
            k = pn_kb[:, LANES * hh:LANES * (hh + 1)]
            msk = jnp.sum(k * k, axis=-1, keepdims=True) * (1.0 / HEAD_DIM)
            kn = k * lax.rsqrt(msk + RMS_EPS) * gkb_ref[...]
            kB_ref[0, hh] = (kn + kaug[:, LANES * hh:LANES * (hh + 1)] + ones_k).astype(BF16)

    def qb_epilogue(pt_qb):
        c3 = jnp.concatenate(_split3(val["c_t"] - shift_ref[...]), axis=0)
        qaug = _dot(selq_ref[...], c3)
        qb = pt_qb.reshape(N_HEADS_B, QB_ROWS, tm)
        msq = jnp.sum(qb * qb, axis=1, keepdims=True) * (1.0 / HEAD_DIM)
        row_q = lax.broadcasted_iota(jnp.int32, (QB_ROWS, 1), 0)
        ones_q = jnp.where((row_q >= HEAD_DIM + 3) & (row_q < HEAD_DIM + 6), 1.0, 0.0)
        qn = qb * lax.rsqrt(msq + RMS_EPS) * gqb_ref[...][None]
        qbT_ref[0] = (qn + qaug.reshape(N_HEADS_B, QB_ROWS, tm) + ones_q[None]).astype(BF16)

    def store(ref, value):
        ref[...] = value

    sections = [
        (lambda: rows_t(PT_QA, WIDTH_A), lambda p: store(qaT_ref, norm_rows(p, gqa_ref)[None])),
        (lambda: rows_t(PT_KA, WIDTH_A), lambda p: store(kA_ref, norm_rows(p, gka_ref)[None])),
        (lambda: rows_t(PT_VA, N_HEADS_A * VB_ROWS),
         lambda p: store(vaT_ref, (p.reshape(N_HEADS_A, VB_ROWS, tm) + ones_v[None]).astype(BF16)[None])),
        (lambda: (cols_n(PN_F, LANES), rows_t(PT_F, 16)), gates_epilogue),
        (lambda: rows_t(PT_VB, N_HEADS_B * VB_ROWS),
         lambda p: store(vbT_ref, (p.reshape(N_HEADS_B, VB_ROWS, tm) + ones_v[None]).astype(BF16)[None, :, None])),
        (lambda: cols_n(PN_KB, N_HEADS_B * LANES), kb_epilogue),
        (lambda: rows_t(PT_QB, N_HEADS_B * QB_ROWS), qb_epilogue),
    ]
    slots = _rwkv_prep(lambda start, n: cols_n(PN_C + start, n), tm, t, *rwkv_params, *rwkv_outs, sbuf)
    pending = None
    for mm, epilogue in sections:
        p = mm()
        if pending is not None:
            pending[1](pending[0])
        pending = (p, epilogue)
        next(slots, None)
    pending[1](pending[0])
    for _ in slots:
        pass


def _inproj(x, g, wn, wt, gka, gkb, gqa, gqb, fbn, fbt, selk, selq, shift, rwkv_params):
    b, t, d = x.shape
    tm = TOK_TILE
    nt = t // tm
    const = lambda shape: pl.BlockSpec(shape, lambda i, j: (0,) * len(shape))
    tok = lambda: pl.BlockSpec((1, tm, WIDTH_C), lambda i, j: (i, j, 0))
    bf = jax.ShapeDtypeStruct((b, t, WIDTH_C), BF16)
    out_shape = (
        jax.ShapeDtypeStruct((b, N_HEADS_A, HEAD_DIM, t), BF16),
        jax.ShapeDtypeStruct((b, N_HEADS_A, VB_ROWS, t), BF16),
        jax.ShapeDtypeStruct((b, N_HEADS_A, HEAD_DIM, t), BF16),
        jax.ShapeDtypeStruct((b, N_HEADS_B, QB_ROWS, t), BF16),
        jax.ShapeDtypeStruct((b, N_HEADS_B, nt, VB_ROWS, tm), BF16),
        jax.ShapeDtypeStruct((b, N_HEADS_B, t, LANES), BF16),
        bf, bf, bf, bf, bf, bf, bf,
        jax.ShapeDtypeStruct((b, t // RWKV_TILE, WIDTH_C, LANES), F32),
        bf, bf,
    )
    out_specs = (
        pl.BlockSpec((1, N_HEADS_A, HEAD_DIM, tm), lambda i, j: (i, 0, 0, j)),
        pl.BlockSpec((1, N_HEADS_A, VB_ROWS, tm), lambda i, j: (i, 0, 0, j)),
        pl.BlockSpec((1, N_HEADS_A, HEAD_DIM, tm), lambda i, j: (i, 0, 0, j)),
        pl.BlockSpec((1, N_HEADS_B, QB_ROWS, tm), lambda i, j: (i, 0, 0, j)),
        pl.BlockSpec((1, N_HEADS_B, 1, VB_ROWS, tm), lambda i, j: (i, 0, j, 0, 0)),
        pl.BlockSpec((1, N_HEADS_B, tm, LANES), lambda i, j: (i, 0, j, 0)),
        tok(), tok(), tok(), tok(), tok(), tok(), tok(),
        pl.BlockSpec((1, tm // RWKV_TILE, WIDTH_C, LANES), lambda i, j: (i, j, 0, 0)),
        tok(), tok(),
    )
    in_specs = [
        pl.BlockSpec((1, tm, d), lambda i, j: (i, j, 0)),
        const((1, d)), const(wn.shape), const(wt.shape),
        const(gka.shape), const(gkb.shape), const(gqa.shape), const(gqb.shape),
        const(fbn.shape), const(fbt.shape), const(selk.shape), const(selq.shape), const(shift.shape),
    ] + [const(p.shape) for p in rwkv_params]
    return pl.pallas_call(
        _inproj_body,
        grid=(b, nt),
        in_specs=in_specs,
        out_specs=out_specs,
        out_shape=out_shape,
        scratch_shapes=[pltpu.VMEM((8, LANES), F32), pltpu.VMEM((16, LANES), F32),
                        pltpu.VMEM((tm + 8, COLS_C), F32)],
        compiler_params=_cparams(2),
        name="inproj",
    )(x, g, wn, wt, gka, gkb, gqa, gqb, fbn, fbt, selk, selq, shift, *rwkv_params)


def _relbias_body(tab_ref, shift_ref, o_ref):
    hh = pl.program_id(0)
    kj = lax.broadcasted_iota(jnp.int32, (ATT_A_WIN, ATT_A_TILE), 0)
    qi = lax.broadcasted_iota(jnp.int32, (ATT_A_WIN, ATT_A_TILE), 1)
    rel = jnp.clip(kj - LEFT_CHUNKS * CHUNK - qi, -MAX_REL, MAX_REL) + MAX_REL

    def body(r, acc):
        return jnp.where(rel == r, tab_ref[hh, r], acc)

    bias = lax.fori_loop(0, 2 * MAX_REL + 1, body, jnp.zeros((ATT_A_WIN, ATT_A_TILE), F32))
    kc = kj // CHUNK
    qc = qi // CHUNK
    band = (kc >= qc) & (kc <= qc + LEFT_CHUNKS)
    o_ref[0] = jnp.where(band, LOG2E * bias - shift_ref[hh], NEG_INF)


def _relbias(tab, shift):
    return pl.pallas_call(
        _relbias_body,
        grid=(N_HEADS_A,),
        in_specs=[pl.BlockSpec(memory_space=pltpu.SMEM), pl.BlockSpec(memory_space=pltpu.SMEM)],
        out_specs=pl.BlockSpec((1, ATT_A_WIN, ATT_A_TILE), lambda i: (i, 0, 0)),
        out_shape=jax.ShapeDtypeStruct((N_HEADS_A, ATT_A_WIN, ATT_A_TILE), F32),
        compiler_params=_cparams(1),
        name="relbias",
    )(tab, shift)


def _attn_a_body(q_ref, k0_ref, k1_ref, k2_ref, v0_ref, v1_ref, v2_ref, bias_ref, o_ref, *, online):
    i = pl.program_id(1)
    tq = ATT_A_TILE
    kj = lax.broadcasted_iota(jnp.int32, (tq, 1), 0)

    k_refs = (k0_ref, k1_ref, k2_ref)
    v_refs = (v0_ref, v1_ref, v2_ref)

    def run(mask_padding):
        def scores(hh):
            q = q_ref[0, hh]
            ss = []
            for d, k_ref in enumerate(k_refs):
                s = _dot_tn(k_ref[0, hh], q) + bias_ref[hh, d * tq:(d + 1) * tq, :]
                if mask_padding:
                    s = jnp.where(kj + (i - 2 + d) * tq >= 0, s, NEG_INF)
                ss.append(s)
            return ss

        def finish(hh, ss):
            if online:
                m = functools.reduce(jnp.maximum, [jnp.max(s, axis=0, keepdims=True) for s in ss])
                ss = [s - m for s in ss]
            acc = jnp.zeros((VB_ROWS, tq), F32)
            for s, v_ref in zip(ss, v_refs):
                acc = acc + _dot(v_ref[0, hh], jnp.exp2(s).astype(BF16))
            o_ref[0, hh] = (acc[0:HEAD_DIM] / acc[HEAD_DIM:HEAD_DIM + 1]).astype(BF16)

        pending = []
        for hh in range(N_HEADS_A):
            pending.append((hh, scores(hh)))
            if len(pending) > ATT_A_AHEAD:
                finish(*pending.pop(0))
        for item in pending:
            finish(*item)

    pl.when(i < 2)(functools.partial(run, True))
    pl.when(i >= 2)(functools.partial(run, False))


def _attn_a(qaT, kA, vaT, bias, online):
    b, nh, _, t = qaT.shape
    tq = ATT_A_TILE
    kspec = lambda d: pl.BlockSpec((1, nh, HEAD_DIM, tq),
                                   lambda bb, i: (bb, 0, 0, jnp.maximum(i - 2 + d, 0)))
    vspec = lambda d: pl.BlockSpec((1, nh, VB_ROWS, tq),
                                   lambda bb, i: (bb, 0, 0, jnp.maximum(i - 2 + d, 0)))
    return pl.pallas_call(
        functools.partial(_attn_a_body, online=online),
        grid=(b, t // tq),
        in_specs=[pl.BlockSpec((1, nh, HEAD_DIM, tq), lambda bb, i: (bb, 0, 0, i)),
                  kspec(0), kspec(1), kspec(2), vspec(0), vspec(1), vspec(2),
                  pl.BlockSpec((nh, ATT_A_WIN, tq), lambda bb, i: (0, 0, 0))],
        out_specs=pl.BlockSpec((1, nh, HEAD_DIM, tq), lambda bb, i: (bb, 0, 0, i)),
        out_shape=jax.ShapeDtypeStruct((b, nh, HEAD_DIM, t), BF16),
        compiler_params=_cparams(2),
        name="attn_a_online" if online else "attn_a",
    )(qaT, kA, kA, kA, vaT, vaT, vaT, bias)


def _fox_body(q_ref, k_ref, v_ref, o_ref, *, online):
    i = pl.program_id(2)
    tq = FOX_TILE
    nh = q_ref.shape[1]
    q = [jnp.concatenate([q_ref[0, hh], jnp.zeros((LANES - QB_ROWS, tq), BF16)], axis=0) for hh in range(nh)]

    def scores(item):
        j, hh = item
        ks = k_ref[0, hh, pl.ds(pl.multiple_of(j * tq, tq), tq), :]
        return _dot(ks, q[hh])

    def accumulate(item, s, m, acc, diagonal):
        j, hh = item
        vs = v_ref[0, hh, j]
        if diagonal:
            kj = lax.broadcasted_iota(jnp.int32, (tq, tq), 0)
            qi = lax.broadcasted_iota(jnp.int32, (tq, tq), 1)
            causal = kj <= qi
        if online:
            if diagonal:
                s = jnp.where(causal, s, NEG_INF)
            m_new = jnp.maximum(m, jnp.max(s, axis=0, keepdims=True))
            acc = jnp.exp2(m - m_new) * acc
            p = jnp.exp2(s - m_new)
            m = m_new
        else:
            p = jnp.exp2(s)
            if diagonal:
                p = jnp.where(causal, p, 0.0)
        return m, acc + _dot(vs, p.astype(BF16))

    def tiles(js, carry, last_diagonal):
        state = [list(c) for c in carry]
        items = [(j, hh) for j in js for hh in range(nh)]
        ss = []

        def finish(n):
            hh = items[n][1]
            diagonal = last_diagonal and n >= len(items) - nh
            state[hh] = list(accumulate(items[n], ss[n], state[hh][0], state[hh][1], diagonal))

        for n, item in enumerate(items):
            ss.append(scores(item))
            if n >= FOX_AHEAD:
                finish(n - FOX_AHEAD)
        for n in range(max(len(items) - FOX_AHEAD, 0), len(items)):
            finish(n)
        return tuple(tuple(c) for c in state)

    init = tuple((jnp.full((1, tq), NEG_INF, F32), jnp.zeros((VB_ROWS, tq), F32)) for _ in range(nh))
    if online:
        carry = lax.fori_loop(0, i, lambda j, c: tiles([j], c, False), init)
        carry = tiles([i], carry, True)
    else:
        g = FOX_GROUP
        carry = lax.fori_loop(0, i // g, lambda jj, c: tiles([g * jj + n for n in range(g)], c, False), init)
        tails = [functools.partial(lambda c, r: tiles([i - r + n for n in range(r + 1)], c, True), r=r)
                 for r in range(g)]
        carry = lax.switch(i % g, tails, carry)
    for hh, (_, acc) in enumerate(carry):
        o_ref[0, hh] = (acc[0:HEAD_DIM] / acc[HEAD_DIM:HEAD_DIM + 1]).astype(BF16)


def _fox(qbT, kB, vbT, online):
    b, nh, _, t = qbT.shape
    tq = FOX_TILE
    nk = t // tq
    return pl.pallas_call(
        functools.partial(_fox_body, online=online),
        grid=(b, nh // FOX_HEADS, nk),
        in_specs=[pl.BlockSpec((1, FOX_HEADS, QB_ROWS, tq), lambda bb, hh, i: (bb, hh, 0, i)),
                  pl.BlockSpec((1, FOX_HEADS, t, LANES), lambda bb, hh, i: (bb, hh, 0, 0)),
                  pl.BlockSpec((1, FOX_HEADS, nk, VB_ROWS, tq), lambda bb, hh, i: (bb, hh, 0, 0, 0))],
        out_specs=pl.BlockSpec((1, FOX_HEADS, HEAD_DIM, tq), lambda bb, hh, i: (bb, hh, 0, i)),
        out_shape=jax.ShapeDtypeStruct((b, nh, HEAD_DIM, t), BF16),
        compiler_params=_cparams(3),
        name="fox_online" if online else "fox",
    )(qbT, kB, vbT)


def _rwkv_prep(proj, tm, t, mu_ref, w0_ref, w2_ref, a0_ref, a2_ref, g2_ref, kk_ref, ka_ref, rk_ref, e_ref,
               rt_ref, at_ref, bt_ref, kt_ref, bh_ref, kh_ref, v_ref, wc_ref, bonus_ref, g_ref, sbuf):
    @pl.when(t == 0)
    def _():
        sbuf[0:8, :] = jnp.zeros((8, COLS_C), F32)

    def shifted(p, start):
        cols = slice(start, start + p.shape[1])
        sbuf[8:8 + tm, cols] = p
        prev = sbuf[7:7 + tm, cols]
        sbuf[0:8, cols] = p[tm - 8:tm, :]
        return p + (prev - p) * mu_ref[:, cols]

    c = WIDTH_C
    p_lo = proj(3 * c, COLS_C - 3 * c)
    p_k = proj(c, c)
    u_lo = shifted(p_lo, 3 * c)
    yield
    w_lo = u_lo[:, 0:DECAY_LORA]
    a_lo = u_lo[:, DECAY_LORA:DECAY_LORA + AAA_LORA]
    g_lo = u_lo[:, DECAY_LORA + AAA_LORA:]
    w = w0_ref[...] + _dot(jnp.tanh(w_lo).astype(BF16), w2_ref[...])
    w = -_softplus(-w) - 0.5
    ld = -jnp.exp(w)
    p_r = proj(0, c)
    yield
    a = _sigmoid(a0_ref[...] + _dot(a_lo.astype(BF16), a2_ref[...]))
    g_ref[0] = _dot(_sigmoid(g_lo).astype(BF16), g2_ref[...]).astype(BF16)
    k = shifted(p_k, c)
    p_v = proj(2 * c, c)
    yield

    e = e_ref[...]
    kk = k * kk_ref[...]
    nrm2 = _dot((kk * kk).astype(BF16), e)
    kkn = kk * lax.rsqrt(jnp.maximum(nrm2, 1e-24))
    k2 = k * (1.0 + (a - 1.0) * ka_ref[...])
    kka = kkn * a
    r = shifted(p_r, 0)
    v = shifted(p_v, 2 * c)
    v_ref[0] = v.astype(BF16)
    b_hi, b_lo = _split2(r * k2 * rk_ref[...])
    bonus_ref[0] = ((_dot(b_hi, e) + _dot(b_lo, e)) * v).astype(BF16)
    yield

    ri = lax.broadcasted_iota(jnp.int32, (2 * CHUNK, CHUNK), 0)
    ci = lax.broadcasted_iota(jnp.int32, (2 * CHUNK, CHUNK), 1)
    tri = ((ci <= ri) | (ri >= CHUNK)).astype(BF16)
    tri2 = jnp.concatenate([tri, tri], axis=1)
    ld_hi, ld_lo = _split2(ld)
    for cc in range(tm // CHUNK):
        sl = slice(cc * CHUNK, (cc + 1) * CHUNK)
        cum = _dot(tri2, jnp.concatenate([ld_hi[sl], ld_lo[sl]], axis=0))
        lc = cum[0:CHUNK]
        tot = cum[CHUNK:2 * CHUNK]
        e_neg = jnp.exp(-lc)
        e_rem = jnp.exp(tot - lc)
        rt_ref[0, sl, :] = (r[sl] * jnp.exp(lc)).astype(BF16)
        at_ref[0, sl, :] = (-kkn[sl] * jnp.exp(lc - ld[sl])).astype(BF16)
        bt_ref[0, sl, :] = (kka[sl] * e_neg).astype(BF16)
        kt_ref[0, sl, :] = (k2[sl] * e_neg).astype(BF16)
        bh_ref[0, sl, :] = (kka[sl] * e_rem).astype(BF16)
        kh_ref[0, sl, :] = (k2[sl] * e_rem).astype(BF16)
        if cc % 2 == 1:
            yield
    seg = (lax.broadcasted_iota(jnp.int32, (16, tm), 1) // CHUNK
           == lax.broadcasted_iota(jnp.int32, (16, tm), 0)).astype(BF16)
    tots3 = jnp.concatenate(_split3(_dot(seg, ld_hi) + _dot(seg, ld_lo)), axis=0)
    per_blk = RWKV_TILE // CHUNK
    for blk in range(tm // RWKV_TILE):
        put = (lax.broadcasted_iota(jnp.int32, (48, LANES), 0) % 16
               == lax.broadcasted_iota(jnp.int32, (48, LANES), 1) + blk * per_blk).astype(BF16)
        wc_ref[0, blk] = jnp.exp(_dot_tn(tots3, put))


def _bmm(a, b):
    return lax.dot_general(a, b, (((2,), (1,)), ((0,), (0,))), preferred_element_type=F32)


def _bmm_nt(a, b):
    return lax.dot_general(a, b, (((2,), (2,)), ((0,), (0,))), preferred_element_type=F32)


def _bmm_tn(a, b):
    return lax.dot_general(a, b, (((1,), (1,)), ((0,), (0,))), preferred_element_type=F32)


def _rwkv_chunk_body(rt_ref, at_ref, bt_ref, kt_ref, bh_ref, kh_ref, v_ref, wc_ref, bonus_ref, g_ref,
                     lg_ref, lb_ref, e_ref, o_ref, h_ref):
    rows, tm = rt_ref.shape[0], rt_ref.shape[1]
    nc = tm // CHUNK
    ng = N_HEADS_C // RWKV_GROUP
    gl = RWKV_GROUP * HEAD_DIM
    units = [(r, g) for r in range(rows) for g in range(ng)]
    nu = len(units)
    t = pl.program_id(1)

    @pl.when(t == 0)
    def _():
        h_ref[...] = jnp.zeros_like(h_ref)

    lane_head = lax.broadcasted_iota(jnp.int32, (1, CHUNK, gl), 2) // HEAD_DIM
    ri = lax.broadcasted_iota(jnp.int32, (1, 2 * CHUNK, gl), 1)
    ci = lax.broadcasted_iota(jnp.int32, (1, 2 * CHUNK, gl), 2) & (CHUNK - 1)
    rr = ri & (CHUNK - 1)
    keep = (rr > ci) | ((ri >= CHUNK) & (rr == ci))
    eye = (lax.broadcasted_iota(jnp.int32, (1, CHUNK, gl), 1)
           == (lax.broadcasted_iota(jnp.int32, (1, CHUNK, gl), 2) & (CHUNK - 1)))
    same_head = (lax.broadcasted_iota(jnp.int32, (1, gl, gl), 1) // HEAD_DIM
                 == lax.broadcasted_iota(jnp.int32, (1, gl, gl), 2) // HEAD_DIM)

    def bdiag(x):
        return jnp.concatenate([jnp.where(lane_head == hh, x, jnp.zeros_like(x)) for hh in range(RWKV_GROUP)],
                               axis=1)

    def chunk_local(cs, out):
        tile = lambda ref: jnp.stack([ref[r, c * CHUNK:(c + 1) * CHUNK, g * gl:(g + 1) * gl]
                                      for c in cs for r, g in units])
        rt, at, bt, kt, bh, kh, v = (tile(r) for r in (rt_ref, at_ref, bt_ref, kt_ref, bh_ref, kh_ref, v_ref))
        ar = jnp.concatenate([at, rt], axis=1)
        sb = jnp.where(keep, _bmm_nt(ar, bdiag(bt)), 0.0)
        sk = jnp.where(keep, _bmm_nt(ar, bdiag(kt)), 0.0).astype(BF16)
        m_rb = sb[:, CHUNK:].astype(BF16)
        yield
        a = sb[:, :CHUNK]
        tinv = jnp.where(eye, 1.0, 0.0) + a
        ab = a.astype(BF16)
        x = _bmm(ab, bdiag(ab))
        yield
        for _ in range(4):
            xb = x.astype(BF16)
            r = _bmm(jnp.concatenate([tinv.astype(BF16), xb], axis=1), bdiag(xb))
            tinv = tinv + r[:, :CHUNK]
            x = r[:, CHUNK:]
            yield
        tinv = (tinv + _bmm(tinv.astype(BF16), bdiag(x.astype(BF16)))).astype(BF16)
        bdv = bdiag(v)
        av = _bmm(sk[:, :CHUNK], bdv)
        yield
        p = _bmm(tinv, bdiag(at)).astype(BF16)
        u0 = _bmm(tinv, bdiag(av.astype(BF16)))
        yield
        q = (rt.astype(F32) + _bmm(m_rb, bdiag(p))).astype(BF16)
        y0 = _bmm(m_rb, bdiag(u0.astype(BF16))) + _bmm(sk[:, CHUNK:], bdv)
        parts = (jnp.concatenate([q, p], axis=1), jnp.concatenate([y0, u0], axis=1),
                 jnp.concatenate([bh, kh], axis=1), v)
        out += [tuple(z[n * nu:(n + 1) * nu] for z in parts) for n in range(len(cs))]
        yield

    wct = [wc_ref[r, 0] for r in range(rows)]
    state = [h_ref[...]]
    ys = []
    local = []

    def recur(c):
        qp, yu0, bk, v = local[c]
        yu = _bmm(qp, state[0].astype(BF16)) + yu0
        ys.append(yu[:, :CHUNK])
        upd = _bmm_tn(bk, jnp.concatenate([yu[:, CHUNK:].astype(BF16), v], axis=1))
        wcol = jnp.stack([wct[r][g * gl:(g + 1) * gl, c:c + 1] for r, g in units])
        state[0] = jnp.where(same_head, upd, 0.0) + wcol * state[0]

    assert sorted(c for cs in RWKV_SCHEDULE for c in cs) == list(range(nc))
    done = 0
    for cs in RWKV_SCHEDULE:
        pending = list(range(done, len(local)))
        slots = {(k + 1) * RWKV_STAGES // (len(pending) + 1): c for k, c in enumerate(pending)}
        for n, _ in enumerate(chunk_local(list(cs), local), start=1):
            if n in slots:
                recur(slots[n])
                done += 1
    for c in range(done, nc):
        recur(c)
    h_ref[...] = state[0]

    y = jnp.concatenate([jnp.concatenate([yc[r * ng + g] for g in range(ng)], axis=-1)
                         for r in range(rows) for yc in ys], axis=0)
    e = e_ref[...]

    def head_mean(z):
        return _dot(z.astype(BF16), e) * (1.0 / HEAD_DIM)

    d = y - head_mean(y)
    yn = d * lax.rsqrt(head_mean(d * d) + LNX_EPS)
    flat = lambda ref: ref[...].astype(F32).reshape(rows * tm, WIDTH_C)
    out = (yn * lg_ref[...] + lb_ref[...] + flat(bonus_ref)) * flat(g_ref)
    o_ref[...] = out.astype(BF16).reshape(rows, tm, WIDTH_C)


def _rwkv_chunk(rt, at, bt, kt, bh, kh, v, wc, bonus, g, lnx_g, lnx_b, e):
    b, t, _ = rt.shape
    tm = RWKV_TILE
    rows = math.gcd(RWKV_ROWS, b)
    gl = RWKV_GROUP * HEAD_DIM
    const = lambda shape: pl.BlockSpec(shape, lambda i, j: (0,) * len(shape))
    tok = lambda: pl.BlockSpec((rows, tm, WIDTH_C), lambda i, j: (i, j, 0))
    return pl.pallas_call(
        _rwkv_chunk_body,
        grid=(b // rows, t // tm),
        in_specs=[tok(), tok(), tok(), tok(), tok(), tok(), tok(),
                  pl.BlockSpec((rows, 1, WIDTH_C, LANES), lambda i, j: (i, j, 0, 0)),
                  tok(), tok(), const(lnx_g.shape), const(lnx_b.shape), const(e.shape)],
        out_specs=tok(),
        out_shape=jax.ShapeDtypeStruct((b, t, WIDTH_C), BF16),
        scratch_shapes=[pltpu.VMEM((rows * (N_HEADS_C // RWKV_GROUP), gl, gl), F32)],
        compiler_params=_cparams(2),
        name="rwkv_chunk",
    )(rt, at, bt, kt, bh, kh, v, wc, bonus, g, lnx_g, lnx_b, e)


def _outproj_body(x_ref, ya_ref, yb_ref, yc_ref, wa_ref, wb_ref, wc_ref, o_ref):
    acc = _dot_tn(ya_ref[0], wa_ref[...])
    acc = acc + _dot_tn(yb_ref[0], wb_ref[...])
    acc = acc + _dot(yc_ref[0], wc_ref[...])
    o_ref[0] = x_ref[0] + acc


def _outproj(x, yaT, ybT, yc, wa, wb, wc):
    b, t, d = x.shape
    tm = TOK_TILE
    const = lambda shape: pl.BlockSpec(shape, lambda i, j: (0,) * len(shape))
    return pl.pallas_call(
        _outproj_body,
        grid=(b, t // tm),
        in_specs=[pl.BlockSpec((1, tm, d), lambda i, j: (i, j, 0)),
                  pl.BlockSpec((1, WIDTH_A, tm), lambda i, j: (i, 0, j)),
                  pl.BlockSpec((1, WIDTH_B, tm), lambda i, j: (i, 0, j)),
                  pl.BlockSpec((1, tm, WIDTH_C), lambda i, j: (i, j, 0)),
                  const(wa.shape), const(wb.shape), const(wc.shape)],
        out_specs=pl.BlockSpec((1, tm, d), lambda i, j: (i, j, 0)),
        out_shape=jax.ShapeDtypeStruct((b, t, d), F32),
        compiler_params=_cparams(2),
        name="outproj",
    )(x, yaT, ybT, yc, wa, wb, wc)


def _ffn_body(x_ref, g_ref, wg_ref, wv_ref, cw_ref, cb_ref, wd_ref, o_ref, gbuf):
    tm = x_ref.shape[1]
    t = pl.program_id(1)

    @pl.when(t == 0)
    def _():
        gbuf[0:8, :] = jnp.zeros((8, D_FF), F32)

    x = x_ref[0]
    ms = jnp.mean(x * x, axis=-1, keepdims=True)
    h = (x * lax.rsqrt(ms + RMS_EPS) * g_ref[...]).astype(BF16)
    acc = x
    for lo, hi in zip(FF_SPLITS[:-1], FF_SPLITS[1:]):
        gate = _dot(h, wg_ref[:, lo:hi])
        val = _dot(h, wv_ref[:, lo:hi])
        gbuf[8:8 + tm, lo:hi] = gate
        g1 = gbuf[7:7 + tm, lo:hi]
        g2 = gbuf[6:6 + tm, lo:hi]
        gbuf[0:8, lo:hi] = gate[tm - 8:tm, :]
        conv = cb_ref[:, lo:hi] + g2 * cw_ref[0:1, lo:hi] + g1 * cw_ref[1:2, lo:hi] + gate * cw_ref[2:3, lo:hi]
        act = (conv * _sigmoid(conv) * val).astype(BF16)
        acc = acc + _dot(act, wd_ref[lo:hi, :])
    o_ref[0] = acc


def _ffn(x, g, wg, wv, cw, cb, wd):
    b, t, d = x.shape
    tm = TOK_TILE
    const = lambda shape: pl.BlockSpec(shape, lambda i, j: (0,) * len(shape),
                                       pipeline_mode=pl.Buffered(1))
    return pl.pallas_call(
        _ffn_body,
        grid=(b, t // tm),
        in_specs=[pl.BlockSpec((1, tm, d), lambda i, j: (i, j, 0)),
                  const(g.shape), const(wg.shape), const(wv.shape), const(cw.shape), const(cb.shape),
                  const(wd.shape)],
        out_specs=pl.BlockSpec((1, tm, d), lambda i, j: (i, j, 0)),
        out_shape=jax.ShapeDtypeStruct((b, t, d), F32),
        scratch_shapes=[pltpu.VMEM((tm + 8, D_FF), F32)],
        compiler_params=_cparams(2),
        name="ffn",
    )(x, g, wg, wv, cw, cb, wd)


def _pad_heads_cols(w, nh):
    d = w.shape[0]
    w = w.reshape(d, nh, HEAD_DIM)
    return jnp.pad(w, ((0, 0), (0, 0), (0, LANES - HEAD_DIM))).reshape(d, nh * LANES)


def _qk_bound(gq, gk):
    scale = HEAD_DIM ** -0.5
    return 1.01 * HEAD_DIM * scale * jnp.max(jnp.abs(gq)) * jnp.max(jnp.abs(gk)) + 0.05


def _layer_params(l, w_in, q_norm_a, k_norm_a, rel_bias, q_norm_b, k_norm_b, forget_bias):
    w = w_in[l]
    scale = HEAD_DIM ** -0.5
    a0, b0 = 0, 3 * WIDTH_A
    c0 = b0 + 3 * WIDTH_B + N_HEADS_B
    qa, ka, va = (w[:, a0 + i * WIDTH_A:a0 + (i + 1) * WIDTH_A] for i in range(3))
    qb, kb, vb = (w[:, b0 + i * WIDTH_B:b0 + (i + 1) * WIDTH_B] for i in range(3))
    fg = w[:, b0 + 3 * WIDTH_B:c0]
    wc = w[:, c0:]
    wn = jnp.concatenate([_pad_heads_cols(kb, N_HEADS_B),
                          jnp.pad(fg, ((0, 0), (0, LANES - N_HEADS_B))), wc], axis=1).astype(BF16)
    pad_v = lambda v, nh: jnp.pad(v.reshape(-1, nh, HEAD_DIM),
                                  ((0, 0), (0, 0), (0, VB_ROWS - HEAD_DIM))).reshape(-1, nh * VB_ROWS)
    assert QB_ROWS == VB_ROWS
    wt = jnp.concatenate([qa, ka, pad_v(va, N_HEADS_A), pad_v(qb, N_HEADS_B), pad_v(vb, N_HEADS_B),
                          jnp.pad(fg, ((0, 0), (0, 16 - N_HEADS_B)))], axis=1).T.astype(BF16)
    pad = LANES - HEAD_DIM
    gka = k_norm_a[l].reshape(HEAD_DIM, 1)
    gkb = jnp.pad(k_norm_b[l], (0, pad)).reshape(1, LANES)
    gqa = (q_norm_a[l] * (scale * LOG2E)).reshape(HEAD_DIM, 1)
    gqb = jnp.pad(q_norm_b[l] * (scale * LOG2E), (0, QB_ROWS - HEAD_DIM)).reshape(QB_ROWS, 1)
    fbn = jnp.pad(forget_bias[l], (0, LANES - N_HEADS_B)).reshape(1, LANES)
    fbt = jnp.pad(forget_bias[l], (0, 16 - N_HEADS_B)).reshape(16, 1)
    shift_b = (LOG2E * _qk_bound(q_norm_b[l], k_norm_b[l])).reshape(1, 1)
    bound_a = _qk_bound(q_norm_a[l], k_norm_a[l])
    tab = rel_bias[l]
    shift_a = LOG2E * (bound_a + jnp.max(tab, axis=1))
    depth_a = jnp.max(shift_a + LOG2E * (bound_a - tab[:, MAX_REL]))
    return wn, wt, gka, gkb, gqa, gqb, fbn, fbt, shift_b, shift_a, depth_a


def _selectors():
    import numpy as np
    selk = np.zeros((LANES, N_HEADS_B * LANES), np.float32)
    selq = np.zeros((N_HEADS_B * QB_ROWS, 3 * 16), np.float32)
    for p in range(3):
        for hh in range(N_HEADS_B):
            selk[p * N_HEADS_B + hh, hh * LANES + HEAD_DIM + 3 + p] = -1.0
            selq[hh * QB_ROWS + HEAD_DIM + p, 16 * p + hh] = 1.0
    e = np.kron(np.eye(N_HEADS_C, dtype=np.float32), np.ones((HEAD_DIM, HEAD_DIM), np.float32))
    return jnp.asarray(selk, BF16), jnp.asarray(selq, BF16), jnp.asarray(e, BF16)


def kernel(x, mix_norm_g, w_in, q_norm_a, k_norm_a, rel_bias, q_norm_b, k_norm_b, forget_bias, shift_mu, w0, w2,
           a0, a2, g2, k_k, k_a, r_k, lnx_g, lnx_b, w_out, ffn_norm_g, w_up, conv_w, conv_b, w_down):
    depth = w_in.shape[0]
    b, t, d = x.shape
    selk, selq, e = _selectors()
    row = lambda v: v.reshape(1, -1)
    for l in range(depth):
        wn, wt, gka, gkb, gqa, gqb, fbn, fbt, shift_b, shift_a, depth_a = _layer_params(
            l, w_in, q_norm_a, k_norm_a, rel_bias, q_norm_b, k_norm_b, forget_bias)
        rwkv_params = (row(shift_mu[l]), row(w0[l]), w2[l].astype(BF16), row(a0[l]), a2[l].astype(BF16),
                       g2[l].astype(BF16), row(k_k[l]), row(k_a[l]), row(r_k[l]), e)
        (qaT, vaT, kA, qbT, vbT, kB, rt, at, bt, kt, bh, kh, v, wcum, bonus, g) = _inproj(
            x, row(mix_norm_g[l]), wn, wt, gka, gkb, gqa, gqb, fbn, fbt, selk, selq, shift_b, rwkv_params)
        bias = _relbias(rel_bias[l], shift_a)
        yaT = lax.cond(depth_a <= 2 * FAST_MAX_SHIFT,
                       functools.partial(_attn_a, online=False), functools.partial(_attn_a, online=True),
                       qaT, kA, vaT, bias)
        ybT = lax.cond(shift_b[0, 0] <= FAST_MAX_SHIFT,
                       functools.partial(_fox, online=False), functools.partial(_fox, online=True),
                       qbT, kB, vbT)
        yc = _rwkv_chunk(rt, at, bt, kt, bh, kh, v, wcum, bonus, g, row(lnx_g[l]), row(lnx_b[l]), e)
        wo = w_out[l].astype(BF16)
        x = _outproj(x, yaT.reshape(b, WIDTH_A, t), ybT.reshape(b, WIDTH_B, t), yc,
                     wo[:WIDTH_A], wo[WIDTH_A:WIDTH_A + WIDTH_B], wo[WIDTH_A + WIDTH_B:])
        wu = w_up[l].astype(BF16)
        x = _ffn(x, row(ffn_norm_g[l]), wu[:, :D_FF], wu[:, D_FF:], conv_w[l], row(conv_b[l]),
                 w_down[l].astype(BF16))
    return x
```

```python
import functools
import math

import jax
import jax.numpy as jnp
from jax import lax
from jax.experimental import pallas as pl
from jax.experimental.pallas import tpu as pltpu

F32 = jnp.float32
BF16 = jnp.bfloat16

D_MODEL = 1024
HEAD_DIM = 64
CHUNK = 64
LEFT_CHUNKS = 8
MAX_REL = 128
N_HEADS_A = 4
N_HEADS_B = 4
N_HEADS_C = 8
WIDTH_A = N_HEADS_A * HEAD_DIM
WIDTH_B = N_HEADS_B * HEAD_DIM
WIDTH_C = N_HEADS_C * HEAD_DIM
DECAY_LORA = 64
AAA_LORA = 64
GATE_LORA = 128
COLS_C = 3 * WIDTH_C + DECAY_LORA + AAA_LORA + GATE_LORA
D_FF = 2816
RMS_EPS = 1e-6
LNX_EPS = 64e-5
NEG_INF = -1e30

LANES = 128
TOK_TILE = 512
ATT_A_TILE = 256
ATT_A_WIN = 3 * ATT_A_TILE
ATT_A_AHEAD = 3
FOX_TILE = 512
FOX_HEADS = 4
FOX_GROUP = 8
FOX_AHEAD = 2
RWKV_GROUP = 4
RWKV_TILE = 256
RWKV_ROWS = 4
RWKV_SCHEDULE = ((0, 1), (2, 3))
RWKV_STAGES = 9
FF_SPLITS = (0, 768, 1536, 2304, 2816)
VMEM_LIMIT = 56 * 1024 * 1024

PN_KB, PN_F, PN_C = 0, 512, 640
PN_COLS = PN_C + COLS_C
VB_ROWS = 80
PT_QA = 0
PT_KA = PT_QA + WIDTH_A
PT_VA = PT_KA + WIDTH_A
PT_QB = PT_VA + N_HEADS_A * VB_ROWS
QB_ROWS = 80
PT_VB = PT_QB + N_HEADS_B * QB_ROWS
PT_F = PT_VB + N_HEADS_B * VB_ROWS
PT_ROWS = PT_F + 16
LOG2E = 1.4426950408889634
FAST_MAX_SHIFT = 40.0


def _cparams(n_axes):
    return pltpu.CompilerParams(dimension_semantics=("arbitrary",) * n_axes,
                                vmem_limit_bytes=VMEM_LIMIT)


def _split3(x):
    hi = x.astype(BF16)
    r1 = x - hi.astype(F32)
    mid = r1.astype(BF16)
    lo = (r1 - mid.astype(F32)).astype(BF16)
    return hi, mid, lo


def _split2(x):
    hi = x.astype(BF16)
    lo = (x - hi.astype(F32)).astype(BF16)
    return hi, lo


def _dot(a, b):
    return jnp.dot(a, b, preferred_element_type=F32)


def _dot_nt(a, b):
    return lax.dot_general(a, b, (((1,), (1,)), ((), ())), preferred_element_type=F32)


def _dot_tn(a, b):
    return lax.dot_general(a, b, (((0,), (0,)), ((), ())), preferred_element_type=F32)


def _log_sigmoid(x):
    return jnp.minimum(x, 0.0) - jnp.log(1.0 + jnp.exp(-jnp.abs(x)))


def _softplus(x):
    return jnp.maximum(x, 0.0) + jnp.log(1.0 + jnp.exp(-jnp.abs(x)))


def _sigmoid(x):
    return 1.0 / (1.0 + jnp.exp(-x))


def _inproj_body(x_ref, g_ref, wn_ref, wt_ref, gka_ref, gkb_ref, gqa_ref, gqb_ref, fbn_ref, fbt_ref,
                 selk_ref, selq_ref, shift_ref, *rest):
    rwkv_params, rest = rest[:10], rest[10:]
    qaT_ref, vaT_ref, kA_ref, qbT_ref, vbT_ref, kB_ref = rest[:6]
    rwkv_outs = rest[6:16]
    carry_n, carry_t, sbuf = rest[16:]
    tm = x_ref.shape[1]
    t = pl.program_id(1)

    @pl.when(t == 0)
    def _():
        carry_n[...] = jnp.zeros_like(carry_n)
        carry_t[...] = jnp.zeros_like(carry_t)

    x = x_ref[0]
    ms = jnp.mean(x * x, axis=-1, keepdims=True)
    h = (x * lax.rsqrt(ms + RMS_EPS) * g_ref[...]).astype(BF16)
    rows_t = lambda start, n: _dot_nt(wt_ref[start:start + n, :], h)
    cols_n = lambda start, n: _dot(h, wn_ref[:, start:start + n])
    row_v = lax.broadcasted_iota(jnp.int32, (VB_ROWS, 1), 0)
    ones_v = jnp.where(row_v == HEAD_DIM, 1.0, 0.0)
    val = {}

    def norm_rows(p, gain_ref):
        p = p.reshape(N_HEADS_A, HEAD_DIM, tm)
        return (p * lax.rsqrt(jnp.mean(p * p, axis=1, keepdims=True) + RMS_EPS) * gain_ref[...][None]).astype(BF16)

    def lane_packed3(z):
        z_hi, z_mid, z_lo = _split3(z)
        return (z_hi.astype(F32) + pltpu.roll(z_mid.astype(F32), N_HEADS_B, axis=1)
                + pltpu.roll(z_lo.astype(F32), 2 * N_HEADS_B, axis=1)).astype(BF16)

    def gates_epilogue(p):
        pn_f, pt_f = p
        lane_n = lax.broadcasted_iota(jnp.int32, (1, LANES), 1)
        lf_n = jnp.where(lane_n < N_HEADS_B, LOG2E * _log_sigmoid(pn_f + fbn_ref[...]), 0.0)
        row_t = lax.broadcasted_iota(jnp.int32, (16, 1), 0)
        lf_t = jnp.where(row_t < N_HEADS_B, LOG2E * _log_sigmoid(pt_f + fbt_ref[...]), 0.0)
        ri = lax.broadcasted_iota(jnp.int32, (tm, tm), 0)
        ci = lax.broadcasted_iota(jnp.int32, (tm, tm), 1)
        low = (ci <= ri).astype(BF16)
        upp = (ri <= ci).astype(BF16)
        cum3 = _dot(low, lane_packed3(lf_n))
        cum = cum3 + pltpu.roll(cum3, LANES - N_HEADS_B, axis=1) + pltpu.roll(cum3, LANES - 2 * N_HEADS_B, axis=1)
        val["c_n"] = jnp.where(lane_n < N_HEADS_B, cum, 0.0) + carry_n[0:1, :]
        ht, mt, lt = _split3(lf_t)
        val["c_t"] = _dot(ht, upp) + _dot(mt, upp) + _dot(lt, upp) + carry_t[:, 0:1]
        carry_n[...] = carry_n[...] + jnp.sum(lf_n, axis=0, keepdims=True)
        carry_t[...] = carry_t[...] + jnp.sum(lf_t, axis=1, keepdims=True)

    def kb_epilogue(pn_kb):
        kaug = _dot(lane_packed3(val["c_n"]), selk_ref[...])
        lane_k = lax.broadcasted_iota(jnp.int32, (1, LANES), 1)
        ones_k = jnp.where((lane_k >= HEAD_DIM) & (lane_k < HEAD_DIM + 3), 1.0, 0.0)
        for hh in range(N_HEADS_B):
            k = pn_kb[:, LANES * hh:LANES * (hh + 1)]
            msk = jnp.sum(k * k, axis=-1, keepdims=True) * (1.0 / HEAD_DIM)
            kn = k * lax.rsqrt(msk + RMS_EPS) * gkb_ref[...]
            kB_ref[0, hh] = (kn + kaug[:, LANES * hh:LANES * (hh + 1)] + ones_k).astype(BF16)

    def qb_epilogue(pt_qb):
        c3 = jnp.concatenate(_split3(val["c_t"] - shift_ref[...]), axis=0)
        qaug = _dot(selq_ref[...], c3)
        qb = pt_qb.reshape(N_HEADS_B, QB_ROWS, tm)
        msq = jnp.sum(qb * qb, axis=1, keepdims=True) * (1.0 / HEAD_DIM)
        row_q = lax.broadcasted_iota(jnp.int32, (QB_ROWS, 1), 0)
        ones_q = jnp.where((row_q >= HEAD_DIM + 3) & (row_q < HEAD_DIM + 6), 1.0, 0.0)
        qn = qb * lax.rsqrt(msq + RMS_EPS) * gqb_ref[...][None]
        qbT_ref[0] = (qn + qaug.reshape(N_HEADS_B, QB_ROWS, tm) + ones_q[None]).astype(BF16)

    def store(ref, value):
        ref[...] = value

    sections = [
        (lambda: rows_t(PT_QA, WIDTH_A), lambda p: store(qaT_ref, norm_rows(p, gqa_ref)[None])),
        (lambda: rows_t(PT_KA, WIDTH_A), lambda p: store(kA_ref, norm_rows(p, gka_ref)[None])),
        (lambda: rows_t(PT_VA, N_HEADS_A * VB_ROWS),
         lambda p: store(vaT_ref, (p.reshape(N_HEADS_A, VB_ROWS, tm) + ones_v[None]).astype(BF16)[None])),
        (lambda: (cols_n(PN_F, LANES), rows_t(PT_F, 16)), gates_epilogue),
        (lambda: rows_t(PT_VB, N_HEADS_B * VB_ROWS),
         lambda p: store(vbT_ref, (p.reshape(N_HEADS_B, VB_ROWS, tm) + ones_v[None]).astype(BF16)[None, :, None])),
        (lambda: cols_n(PN_KB, N_HEADS_B * LANES), kb_epilogue),
        (lambda: rows_t(PT_QB, N_HEADS_B * QB_ROWS), qb_epilogue),
    ]
    slots = _rwkv_prep(lambda start, n: cols_n(PN_C + start, n), tm, t, *rwkv_params, *rwkv_outs, sbuf)
    pending = None
    for mm, epilogue in sections:
        p = mm()
        if pending is not None:
            pending[1](pending[0])
        pending = (p, epilogue)
        next(slots, None)
    pending[1](pending[0])
    for _ in slots:
        pass


def _inproj(x, g, wn, wt, gka, gkb, gqa, gqb, fbn, fbt, selk, selq, shift, rwkv_params):
    b, t, d = x.shape
    tm = TOK_TILE
    nt = t // tm
    const = lambda shape: pl.BlockSpec(shape, lambda i, j: (0,) * len(shape))
    tok = lambda: pl.BlockSpec((1, tm, WIDTH_C), lambda i, j: (i, j, 0))
    bf = jax.ShapeDtypeStruct((b, t, WIDTH_C), BF16)
    out_shape = (
        jax.ShapeDtypeStruct((b, N_HEADS_A, HEAD_DIM, t), BF16),
        jax.ShapeDtypeStruct((b, N_HEADS_A, VB_ROWS, t), BF16),
        jax.ShapeDtypeStruct((b, N_HEADS_A, HEAD_DIM, t), BF16),
        jax.ShapeDtypeStruct((b, N_HEADS_B, QB_ROWS, t), BF16),
        jax.ShapeDtypeStruct((b, N_HEADS_B, nt, VB_ROWS, tm), BF16),
        jax.ShapeDtypeStruct((b, N_HEADS_B, t, LANES), BF16),
        bf, bf, bf, bf, bf, bf, bf,
        jax.ShapeDtypeStruct((b, t // RWKV_TILE, WIDTH_C, LANES), F32),
        bf, bf,
    )
    out_specs = (
        pl.BlockSpec((1, N_HEADS_A, HEAD_DIM, tm), lambda i, j: (i, 0, 0, j)),
        pl.BlockSpec((1, N_HEADS_A, VB_ROWS, tm), lambda i, j: (i, 0, 0, j)),
        pl.BlockSpec((1, N_HEADS_A, HEAD_DIM, tm), lambda i, j: (i, 0, 0, j)),
        pl.BlockSpec((1, N_HEADS_B, QB_ROWS, tm), lambda i, j: (i, 0, 0, j)),
        pl.BlockSpec((1, N_HEADS_B, 1, VB_ROWS, tm), lambda i, j: (i, 0, j, 0, 0)),
        pl.BlockSpec((1, N_HEADS_B, tm, LANES), lambda i, j: (i, 0, j, 0)),
        tok(), tok(), tok(), tok(), tok(), tok(), tok(),
        pl.BlockSpec((1, tm // RWKV_TILE, WIDTH_C, LANES), lambda i, j: (i, j, 0, 0)),
        tok(), tok(),
    )
    in_specs = [
        pl.BlockSpec((1, tm, d), lambda i, j: (i, j, 0)),
        const((1, d)), const(wn.shape), const(wt.shape),
        const(gka.shape), const(gkb.shape), const(gqa.shape), const(gqb.shape),
        const(fbn.shape), const(fbt.shape), const(selk.shape), const(selq.shape), const(shift.shape),
    ] + [const(p.shape) for p in rwkv_params]
    return pl.pallas_call(
        _inproj_body,
        grid=(b, nt),
        in_specs=in_specs,
        out_specs=out_specs,
        out_shape=out_shape,
        scratch_shapes=[pltpu.VMEM((8, LANES), F32), pltpu.VMEM((16, LANES), F32),
                        pltpu.VMEM((tm + 8, COLS_C), F32)],
        compiler_params=_cparams(2),
        name="inproj",
    )(x, g, wn, wt, gka, gkb, gqa, gqb, fbn, fbt, selk, selq, shift, *rwkv_params)


def _relbias_body(tab_ref, shift_ref, o_ref):
    hh = pl.program_id(0)
    kj = lax.broadcasted_iota(jnp.int32, (ATT_A_WIN, ATT_A_TILE), 0)
    qi = lax.broadcasted_iota(jnp.int32, (ATT_A_WIN, ATT_A_TILE), 1)
    rel = jnp.clip(kj - LEFT_CHUNKS * CHUNK - qi, -MAX_REL, MAX_REL) + MAX_REL

    def body(r, acc):
        return jnp.where(rel == r, tab_ref[hh, r], acc)

    bias = lax.fori_loop(0, 2 * MAX_REL + 1, body, jnp.zeros((ATT_A_WIN, ATT_A_TILE), F32))
    kc = kj // CHUNK
    qc = qi // CHUNK
    band = (kc >= qc) & (kc <= qc + LEFT_CHUNKS)
    o_ref[0] = jnp.where(band, LOG2E * bias - shift_ref[hh], NEG_INF)


def _relbias(tab, shift):
    return pl.pallas_call(
        _relbias_body,
        grid=(N_HEADS_A,),
        in_specs=[pl.BlockSpec(memory_space=pltpu.SMEM), pl.BlockSpec(memory_space=pltpu.SMEM)],
        out_specs=pl.BlockSpec((1, ATT_A_WIN, ATT_A_TILE), lambda i: (i, 0, 0)),
        out_shape=jax.ShapeDtypeStruct((N_HEADS_A, ATT_A_WIN, ATT_A_TILE), F32),
        compiler_params=_cparams(1),
        name="relbias",
    )(tab, shift)


def _attn_a_body(q_ref, k0_ref, k1_ref, k2_ref, v0_ref, v1_ref, v2_ref, bias_ref, o_ref, *, online):
    i = pl.program_id(1)
    tq = ATT_A_TILE
    kj = lax.broadcasted_iota(jnp.int32, (tq, 1), 0)

    k_refs = (k0_ref, k1_ref, k2_ref)
    v_refs = (v0_ref, v1_ref, v2_ref)

    def run(mask_padding):
        def scores(hh):
            q = q_ref[0, hh]
            ss = []
            for d, k_ref in enumerate(k_refs):
                s = _dot_tn(k_ref[0, hh], q) + bias_ref[hh, d * tq:(d + 1) * tq, :]
                if mask_padding:
                    s = jnp.where(kj + (i - 2 + d) * tq >= 0, s, NEG_INF)
                ss.append(s)
            return ss

        def finish(hh, ss):
            if online:
                m = functools.reduce(jnp.maximum, [jnp.max(s, axis=0, keepdims=True) for s in ss])
                ss = [s - m for s in ss]
            acc = jnp.zeros((VB_ROWS, tq), F32)
            for s, v_ref in zip(ss, v_refs):
                acc = acc + _dot(v_ref[0, hh], jnp.exp2(s).astype(BF16))
            o_ref[0, hh] = (acc[0:HEAD_DIM] / acc[HEAD_DIM:HEAD_DIM + 1]).astype(BF16)

        pending = []
        for hh in range(N_HEADS_A):
            pending.append((hh, scores(hh)))
            if len(pending) > ATT_A_AHEAD:
                finish(*pending.pop(0))
        for item in pending:
            finish(*item)

    pl.when(i < 2)(functools.partial(run, True))
    pl.when(i >= 2)(functools.partial(run, False))


def _attn_a(qaT, kA, vaT, bias, online):
    b, nh, _, t = qaT.shape
    tq = ATT_A_TILE
    kspec = lambda d: pl.BlockSpec((1, nh, HEAD_DIM, tq),
                                   lambda bb, i: (bb, 0, 0, jnp.maximum(i - 2 + d, 0)))
    vspec = lambda d: pl.BlockSpec((1, nh, VB_ROWS, tq),
                                   lambda bb, i: (bb, 0, 0, jnp.maximum(i - 2 + d, 0)))
    return pl.pallas_call(
        functools.partial(_attn_a_body, online=online),
        grid=(b, t // tq),
        in_specs=[pl.BlockSpec((1, nh, HEAD_DIM, tq), lambda bb, i: (bb, 0, 0, i)),
                  kspec(0), kspec(1), kspec(2), vspec(0), vspec(1), vspec(2),
                  pl.BlockSpec((nh, ATT_A_WIN, tq), lambda bb, i: (0, 0, 0))],
        out_specs=pl.BlockSpec((1, nh, HEAD_DIM, tq), lambda bb, i: (bb, 0, 0, i)),
        out_shape=jax.ShapeDtypeStruct((b, nh, HEAD_DIM, t), BF16),
        compiler_params=_cparams(2),
        name="attn_a_online" if online else "attn_a",
    )(qaT, kA, kA, kA, vaT, vaT, vaT, bias)


def _fox_body(q_ref, k_ref, v_ref, o_ref, *, online):
    i = pl.program_id(2)
    tq = FOX_TILE
    nh = q_ref.shape[1]
    q = [jnp.concatenate([q_ref[0, hh], jnp.zeros((LANES - QB_ROWS, tq), BF16)], axis=0) for hh in range(nh)]

    def scores(item):
        j, hh = item
        ks = k_ref[0, hh, pl.ds(pl.multiple_of(j * tq, tq), tq), :]
        return _dot(ks, q[hh])

    def accumulate(item, s, m, acc, diagonal):
        j, hh = item
        vs = v_ref[0, hh, j]
        if diagonal:
            kj = lax.broadcasted_iota(jnp.int32, (tq, tq), 0)
            qi = lax.broadcasted_iota(jnp.int32, (tq, tq), 1)
            causal = kj <= qi
        if online:
            if diagonal:
                s = jnp.where(causal, s, NEG_INF)
            m_new = jnp.maximum(m, jnp.max(s, axis=0, keepdims=True))
            acc = jnp.exp2(m - m_new) * acc
            p = jnp.exp2(s - m_new)
            m = m_new
        else:
            p = jnp.exp2(s)
            if diagonal:
                p = jnp.where(causal, p, 0.0)
        return m, acc + _dot(vs, p.astype(BF16))

    def tiles(js, carry, last_diagonal):
        state = [list(c) for c in carry]
        items = [(j, hh) for j in js for hh in range(nh)]
        ss = []

        def finish(n):
            hh = items[n][1]
            diagonal = last_diagonal and n >= len(items) - nh
            state[hh] = list(accumulate(items[n], ss[n], state[hh][0], state[hh][1], diagonal))

        for n, item in enumerate(items):
            ss.append(scores(item))
            if n >= FOX_AHEAD:
                finish(n - FOX_AHEAD)
        for n in range(max(len(items) - FOX_AHEAD, 0), len(items)):
            finish(n)
        return tuple(tuple(c) for c in state)

    init = tuple((jnp.full((1, tq), NEG_INF, F32), jnp.zeros((VB_ROWS, tq), F32)) for _ in range(nh))
    if online:
        carry = lax.fori_loop(0, i, lambda j, c: tiles([j], c, False), init)
        carry = tiles([i], carry, True)
    else:
        g = FOX_GROUP
        carry = lax.fori_loop(0, i // g, lambda jj, c: tiles([g * jj + n for n in range(g)], c, False), init)
        tails = [functools.partial(lambda c, r: tiles([i - r + n for n in range(r + 1)], c, True), r=r)
                 for r in range(g)]
        carry = lax.switch(i % g, tails, carry)
    for hh, (_, acc) in enumerate(carry):
        o_ref[0, hh] = (acc[0:HEAD_DIM] / acc[HEAD_DIM:HEAD_DIM + 1]).astype(BF16)


def _fox(qbT, kB, vbT, online):
    b, nh, _, t = qbT.shape
    tq = FOX_TILE
    nk = t // tq
    return pl.pallas_call(
        functools.partial(_fox_body, online=online),
        grid=(b, nh // FOX_HEADS, nk),
        in_specs=[pl.BlockSpec((1, FOX_HEADS, QB_ROWS, tq), lambda bb, hh, i: (bb, hh, 0, i)),
                  pl.BlockSpec((1, FOX_HEADS, t, LANES), lambda bb, hh, i: (bb, hh, 0, 0)),
                  pl.BlockSpec((1, FOX_HEADS, nk, VB_ROWS, tq), lambda bb, hh, i: (bb, hh, 0, 0, 0))],
        out_specs=pl.BlockSpec((1, FOX_HEADS, HEAD_DIM, tq), lambda bb, hh, i: (bb, hh, 0, i)),
        out_shape=jax.ShapeDtypeStruct((b, nh, HEAD_DIM, t), BF16),
        compiler_params=_cparams(3),
        name="fox_online" if online else "fox",
    )(qbT, kB, vbT)


def _rwkv_prep(proj, tm, t, mu_ref, w0_ref, w2_ref, a0_ref, a2_ref, g2_ref, kk_ref, ka_ref, rk_ref, e_ref,
               rt_ref, at_ref, bt_ref, kt_ref, bh_ref, kh_ref, v_ref, wc_ref, bonus_ref, g_ref, sbuf):
    @pl.when(t == 0)
    def _():
        sbuf[0:8, :] = jnp.zeros((8, COLS_C), F32)

    def shifted(p, start):
        cols = slice(start, start + p.shape[1])
        sbuf[8:8 + tm, cols] = p
        prev = sbuf[7:7 + tm, cols]
        sbuf[0:8, cols] = p[tm - 8:tm, :]
        return p + (prev - p) * mu_ref[:, cols]

    c = WIDTH_C
    p_lo = proj(3 * c, COLS_C - 3 * c)
    p_k = proj(c, c)
    u_lo = shifted(p_lo, 3 * c)
    yield
    w_lo = u_lo[:, 0:DECAY_LORA]
    a_lo = u_lo[:, DECAY_LORA:DECAY_LORA + AAA_LORA]
    g_lo = u_lo[:, DECAY_LORA + AAA_LORA:]
    w = w0_ref[...] + _dot(jnp.tanh(w_lo).astype(BF16), w2_ref[...])
    w = -_softplus(-w) - 0.5
    ld = -jnp.exp(w)
    p_r = proj(0, c)
    yield
    a = _sigmoid(a0_ref[...] + _dot(a_lo.astype(BF16), a2_ref[...]))
    g_ref[0] = _dot(_sigmoid(g_lo).astype(BF16), g2_ref[...]).astype(BF16)
    k = shifted(p_k, c)
    p_v = proj(2 * c, c)
    yield

    e = e_ref[...]
    kk = k * kk_ref[...]
    nrm2 = _dot((kk * kk).astype(BF16), e)
    kkn = kk * lax.rsqrt(jnp.maximum(nrm2, 1e-24))
    k2 = k * (1.0 + (a - 1.0) * ka_ref[...])
    kka = kkn * a
    r = shifted(p_r, 0)
    v = shifted(p_v, 2 * c)
    v_ref[0] = v.astype(BF16)
    bonus_ref[0] = (_dot((r * k2 * rk_ref[...]).astype(BF16), e) * v).astype(BF16)
    yield

    ri = lax.broadcasted_iota(jnp.int32, (2 * CHUNK, CHUNK), 0)
    ci = lax.broadcasted_iota(jnp.int32, (2 * CHUNK, CHUNK), 1)
    tri = ((ci <= ri) | (ri >= CHUNK)).astype(BF16)
    tri2 = jnp.concatenate([tri, tri], axis=1)
    ld_hi, ld_lo = _split2(ld)
    for cc in range(tm // CHUNK):
        sl = slice(cc * CHUNK, (cc + 1) * CHUNK)
        cum = _dot(tri2, jnp.concatenate([ld_hi[sl], ld_lo[sl]], axis=0))
        lc = cum[0:CHUNK]
        tot = cum[CHUNK:2 * CHUNK]
        e_neg = jnp.exp(-lc)
        e_rem = jnp.exp(tot - lc)
        rt_ref[0, sl, :] = (r[sl] * jnp.exp(lc)).astype(BF16)
        at_ref[0, sl, :] = (-kkn[sl] * jnp.exp(lc - ld[sl])).astype(BF16)
        bt_ref[0, sl, :] = (kka[sl] * e_neg).astype(BF16)
        kt_ref[0, sl, :] = (k2[sl] * e_neg).astype(BF16)
        bh_ref[0, sl, :] = (kka[sl] * e_rem).astype(BF16)
        kh_ref[0, sl, :] = (k2[sl] * e_rem).astype(BF16)
        if cc % 2 == 1:
            yield
    seg = (lax.broadcasted_iota(jnp.int32, (16, tm), 1) // CHUNK
           == lax.broadcasted_iota(jnp.int32, (16, tm), 0)).astype(BF16)
    tots3 = jnp.concatenate(_split3(_dot(seg, ld_hi) + _dot(seg, ld_lo)), axis=0)
    per_blk = RWKV_TILE // CHUNK
    for blk in range(tm // RWKV_TILE):
        put = (lax.broadcasted_iota(jnp.int32, (48, LANES), 0) % 16
               == lax.broadcasted_iota(jnp.int32, (48, LANES), 1) + blk * per_blk).astype(BF16)
        wc_ref[0, blk] = jnp.exp(_dot_tn(tots3, put))


def _bmm(a, b):
    return lax.dot_general(a, b, (((2,), (1,)), ((0,), (0,))), preferred_element_type=F32)


def _bmm_nt(a, b):
    return lax.dot_general(a, b, (((2,), (2,)), ((0,), (0,))), preferred_element_type=F32)


def _bmm_tn(a, b):
    return lax.dot_general(a, b, (((1,), (1,)), ((0,), (0,))), preferred_element_type=F32)


def _rwkv_chunk_body(rt_ref, at_ref, bt_ref, kt_ref, bh_ref, kh_ref, v_ref, wc_ref, bonus_ref, g_ref,
                     lg_ref, lb_ref, e_ref, o_ref, h_ref):
    rows, tm = rt_ref.shape[0], rt_ref.shape[1]
    nc = tm // CHUNK
    ng = N_HEADS_C // RWKV_GROUP
    gl = RWKV_GROUP * HEAD_DIM
    units = [(r, g) for r in range(rows) for g in range(ng)]
    nu = len(units)
    t = pl.program_id(1)

    @pl.when(t == 0)
    def _():
        h_ref[...] = jnp.zeros_like(h_ref)

    lane_head = lax.broadcasted_iota(jnp.int32, (1, CHUNK, gl), 2) // HEAD_DIM
    ri = lax.broadcasted_iota(jnp.int32, (1, 2 * CHUNK, gl), 1)
    ci = lax.broadcasted_iota(jnp.int32, (1, 2 * CHUNK, gl), 2) & (CHUNK - 1)
    rr = ri & (CHUNK - 1)
    keep = (rr > ci) | ((ri >= CHUNK) & (rr == ci))
    eye = (lax.broadcasted_iota(jnp.int32, (1, CHUNK, gl), 1)
           == (lax.broadcasted_iota(jnp.int32, (1, CHUNK, gl), 2) & (CHUNK - 1)))
    same_head = (lax.broadcasted_iota(jnp.int32, (1, gl, gl), 1) // HEAD_DIM
                 == lax.broadcasted_iota(jnp.int32, (1, gl, gl), 2) // HEAD_DIM)

    def bdiag(x):
        return jnp.concatenate([jnp.where(lane_head == hh, x, jnp.zeros_like(x)) for hh in range(RWKV_GROUP)],
                               axis=1)

    def chunk_local(cs, out):
        tile = lambda ref: jnp.stack([ref[r, c * CHUNK:(c + 1) * CHUNK, g * gl:(g + 1) * gl]
                                      for c in cs for r, g in units])
        rt, at, bt, kt, bh, kh, v = (tile(r) for r in (rt_ref, at_ref, bt_ref, kt_ref, bh_ref, kh_ref, v_ref))
        ar = jnp.concatenate([at, rt], axis=1)
        sb = jnp.where(keep, _bmm_nt(ar, bdiag(bt)), 0.0)
        sk = jnp.where(keep, _bmm_nt(ar, bdiag(kt)), 0.0).astype(BF16)
        m_rb = sb[:, CHUNK:].astype(BF16)
        yield
        a = sb[:, :CHUNK]
        tinv = jnp.where(eye, 1.0, 0.0) + a
        ab = a.astype(BF16)
        x = _bmm(ab, bdiag(ab))
        yield
        for _ in range(4):
            xb = x.astype(BF16)
            r = _bmm(jnp.concatenate([tinv.astype(BF16), xb], axis=1), bdiag(xb))
            tinv = tinv + r[:, :CHUNK]
            x = r[:, CHUNK:]
            yield
        tinv = (tinv + _bmm(tinv.astype(BF16), bdiag(x.astype(BF16)))).astype(BF16)
        bdv = bdiag(v)
        av = _bmm(sk[:, :CHUNK], bdv)
        yield
        p = _bmm(tinv, bdiag(at)).astype(BF16)
        u0 = _bmm(tinv, bdiag(av.astype(BF16)))
        yield
        q = (rt.astype(F32) + _bmm(m_rb, bdiag(p))).astype(BF16)
        y0 = _bmm(m_rb, bdiag(u0.astype(BF16))) + _bmm(sk[:, CHUNK:], bdv)
        parts = (jnp.concatenate([q, p], axis=1), jnp.concatenate([y0, u0], axis=1),
                 jnp.concatenate([bh, kh], axis=1), v)
        out += [tuple(z[n * nu:(n + 1) * nu] for z in parts) for n in range(len(cs))]
        yield

    wct = [wc_ref[r, 0] for r in range(rows)]
    state = [h_ref[...]]
    ys = []
    local = []

    def recur(c):
        qp, yu0, bk, v = local[c]
        yu = _bmm(qp, state[0].astype(BF16)) + yu0
        ys.append(yu[:, :CHUNK])
        upd = _bmm_tn(bk, jnp.concatenate([yu[:, CHUNK:].astype(BF16), v], axis=1))
        wcol = jnp.stack([wct[r][g * gl:(g + 1) * gl, c:c + 1] for r, g in units])
        state[0] = jnp.where(same_head, upd, 0.0) + wcol * state[0]

    assert sorted(c for cs in RWKV_SCHEDULE for c in cs) == list(range(nc))
    done = 0
    for cs in RWKV_SCHEDULE:
        pending = list(range(done, len(local)))
        slots = {(k + 1) * RWKV_STAGES // (len(pending) + 1): c for k, c in enumerate(pending)}
        for n, _ in enumerate(chunk_local(list(cs), local), start=1):
            if n in slots:
                recur(slots[n])
                done += 1
    for c in range(done, nc):
        recur(c)
    h_ref[...] = state[0]

    y = jnp.concatenate([jnp.concatenate([yc[r * ng + g] for g in range(ng)], axis=-1)
                         for r in range(rows) for yc in ys], axis=0)
    e = e_ref[...]

    def head_mean(z):
        return _dot(z.astype(BF16), e) * (1.0 / HEAD_DIM)

    d = y - head_mean(y)
    yn = d * lax.rsqrt(head_mean(d * d) + LNX_EPS)
    flat = lambda ref: ref[...].astype(F32).reshape(rows * tm, WIDTH_C)
    out = (yn * lg_ref[...] + lb_ref[...] + flat(bonus_ref)) * flat(g_ref)
    o_ref[...] = out.astype(BF16).reshape(rows, tm, WIDTH_C)


def _rwkv_chunk(rt, at, bt, kt, bh, kh, v, wc, bonus, g, lnx_g, lnx_b, e):
    b, t, _ = rt.shape
    tm = RWKV_TILE
    rows = math.gcd(RWKV_ROWS, b)
    gl = RWKV_GROUP * HEAD_DIM
    const = lambda shape: pl.BlockSpec(shape, lambda i, j: (0,) * len(shape))
    tok = lambda: pl.BlockSpec((rows, tm, WIDTH_C), lambda i, j: (i, j, 0))
    return pl.pallas_call(
        _rwkv_chunk_body,
        grid=(b // rows, t // tm),
        in_specs=[tok(), tok(), tok(), tok(), tok(), tok(), tok(),
                  pl.BlockSpec((rows, 1, WIDTH_C, LANES), lambda i, j: (i, j, 0, 0)),
                  tok(), tok(), const(lnx_g.shape), const(lnx_b.shape), const(e.shape)],
        out_specs=tok(),
        out_shape=jax.ShapeDtypeStruct((b, t, WIDTH_C), BF16),
        scratch_shapes=[pltpu.VMEM((rows * (N_HEADS_C // RWKV_GROUP), gl, gl), F32)],
        compiler_params=_cparams(2),
        name="rwkv_chunk",
    )(rt, at, bt, kt, bh, kh, v, wc, bonus, g, lnx_g, lnx_b, e)


def _outproj_body(x_ref, ya_ref, yb_ref, yc_ref, wa_ref, wb_ref, wc_ref, o_ref):
    acc = _dot_tn(ya_ref[0], wa_ref[...])
    acc = acc + _dot_tn(yb_ref[0], wb_ref[...])
    acc = acc + _dot(yc_ref[0], wc_ref[...])
    o_ref[0] = x_ref[0] + acc


def _outproj(x, yaT, ybT, yc, wa, wb, wc):
    b, t, d = x.shape
    tm = TOK_TILE
    const = lambda shape: pl.BlockSpec(shape, lambda i, j: (0,) * len(shape))
    return pl.pallas_call(
        _outproj_body,
        grid=(b, t // tm),
        in_specs=[pl.BlockSpec((1, tm, d), lambda i, j: (i, j, 0)),
                  pl.BlockSpec((1, WIDTH_A, tm), lambda i, j: (i, 0, j)),
                  pl.BlockSpec((1, WIDTH_B, tm), lambda i, j: (i, 0, j)),
                  pl.BlockSpec((1, tm, WIDTH_C), lambda i, j: (i, j, 0)),
                  const(wa.shape), const(wb.shape), const(wc.shape)],
        out_specs=pl.BlockSpec((1, tm, d), lambda i, j: (i, j, 0)),
        out_shape=jax.ShapeDtypeStruct((b, t, d), F32),
        compiler_params=_cparams(2),
        name="outproj",
    )(x, yaT, ybT, yc, wa, wb, wc)


def _ffn_body(x_ref, g_ref, wg_ref, wv_ref, cw_ref, cb_ref, wd_ref, o_ref, gbuf):
    tm = x_ref.shape[1]
    t = pl.program_id(1)

    @pl.when(t == 0)
    def _():
        gbuf[0:8, :] = jnp.zeros((8, D_FF), F32)

    x = x_ref[0]
    ms = jnp.mean(x * x, axis=-1, keepdims=True)
    h = (x * lax.rsqrt(ms + RMS_EPS) * g_ref[...]).astype(BF16)
    acc = x
    for lo, hi in zip(FF_SPLITS[:-1], FF_SPLITS[1:]):
        gate = _dot(h, wg_ref[:, lo:hi])
        val = _dot(h, wv_ref[:, lo:hi])
        gbuf[8:8 + tm, lo:hi] = gate
        g1 = gbuf[7:7 + tm, lo:hi]
        g2 = gbuf[6:6 + tm, lo:hi]
        gbuf[0:8, lo:hi] = gate[tm - 8:tm, :]
        conv = cb_ref[:, lo:hi] + g2 * cw_ref[0:1, lo:hi] + g1 * cw_ref[1:2, lo:hi] + gate * cw_ref[2:3, lo:hi]
        act = (conv * _sigmoid(conv) * val).astype(BF16)
        acc = acc + _dot(act, wd_ref[lo:hi, :])
    o_ref[0] = acc


def _ffn(x, g, wg, wv, cw, cb, wd):
    b, t, d = x.shape
    tm = TOK_TILE
    const = lambda shape: pl.BlockSpec(shape, lambda i, j: (0,) * len(shape),
                                       pipeline_mode=pl.Buffered(1))
    return pl.pallas_call(
        _ffn_body,
        grid=(b, t // tm),
        in_specs=[pl.BlockSpec((1, tm, d), lambda i, j: (i, j, 0)),
                  const(g.shape), const(wg.shape), const(wv.shape), const(cw.shape), const(cb.shape),
                  const(wd.shape)],
        out_specs=pl.BlockSpec((1, tm, d), lambda i, j: (i, j, 0)),
        out_shape=jax.ShapeDtypeStruct((b, t, d), F32),
        scratch_shapes=[pltpu.VMEM((tm + 8, D_FF), F32)],
        compiler_params=_cparams(2),
        name="ffn",
    )(x, g, wg, wv, cw, cb, wd)


def _pad_heads_cols(w, nh):
    d = w.shape[0]
    w = w.reshape(d, nh, HEAD_DIM)
    return jnp.pad(w, ((0, 0), (0, 0), (0, LANES - HEAD_DIM))).reshape(d, nh * LANES)


def _qk_bound(gq, gk):
    scale = HEAD_DIM ** -0.5
    return 1.01 * HEAD_DIM * scale * jnp.max(jnp.abs(gq)) * jnp.max(jnp.abs(gk)) + 0.05


def _layer_params(l, w_in, q_norm_a, k_norm_a, rel_bias, q_norm_b, k_norm_b, forget_bias):
    w = w_in[l]
    scale = HEAD_DIM ** -0.5
    a0, b0 = 0, 3 * WIDTH_A
    c0 = b0 + 3 * WIDTH_B + N_HEADS_B
    qa, ka, va = (w[:, a0 + i * WIDTH_A:a0 + (i + 1) * WIDTH_A] for i in range(3))
    qb, kb, vb = (w[:, b0 + i * WIDTH_B:b0 + (i + 1) * WIDTH_B] for i in range(3))
    fg = w[:, b0 + 3 * WIDTH_B:c0]
    wc = w[:, c0:]
    wn = jnp.concatenate([_pad_heads_cols(kb, N_HEADS_B),
                          jnp.pad(fg, ((0, 0), (0, LANES - N_HEADS_B))), wc], axis=1).astype(BF16)
    pad_v = lambda v, nh: jnp.pad(v.reshape(-1, nh, HEAD_DIM),
                                  ((0, 0), (0, 0), (0, VB_ROWS - HEAD_DIM))).reshape(-1, nh * VB_ROWS)
    assert QB_ROWS == VB_ROWS
    wt = jnp.concatenate([qa, ka, pad_v(va, N_HEADS_A), pad_v(qb, N_HEADS_B), pad_v(vb, N_HEADS_B),
                          jnp.pad(fg, ((0, 0), (0, 16 - N_HEADS_B)))], axis=1).T.astype(BF16)
    pad = LANES - HEAD_DIM
    gka = k_norm_a[l].reshape(HEAD_DIM, 1)
    gkb = jnp.pad(k_norm_b[l], (0, pad)).reshape(1, LANES)
    gqa = (q_norm_a[l] * (scale * LOG2E)).reshape(HEAD_DIM, 1)
    gqb = jnp.pad(q_norm_b[l] * (scale * LOG2E), (0, QB_ROWS - HEAD_DIM)).reshape(QB_ROWS, 1)
    fbn = jnp.pad(forget_bias[l], (0, LANES - N_HEADS_B)).reshape(1, LANES)
    fbt = jnp.pad(forget_bias[l], (0, 16 - N_HEADS_B)).reshape(16, 1)
    shift_b = (LOG2E * _qk_bound(q_norm_b[l], k_norm_b[l])).reshape(1, 1)
    bound_a = _qk_bound(q_norm_a[l], k_norm_a[l])
    tab = rel_bias[l]
    shift_a = LOG2E * (bound_a + jnp.max(tab, axis=1))
    depth_a = jnp.max(shift_a + LOG2E * (bound_a - tab[:, MAX_REL]))
    return wn, wt, gka, gkb, gqa, gqb, fbn, fbt, shift_b, shift_a, depth_a


def _selectors():
    import numpy as np
    selk = np.zeros((LANES, N_HEADS_B * LANES), np.float32)
    selq = np.zeros((N_HEADS_B * QB_ROWS, 3 * 16), np.float32)
    for p in range(3):
        for hh in range(N_HEADS_B):
            selk[p * N_HEADS_B + hh, hh * LANES + HEAD_DIM + 3 + p] = -1.0
            selq[hh * QB_ROWS + HEAD_DIM + p, 16 * p + hh] = 1.0
    e = np.kron(np.eye(N_HEADS_C, dtype=np.float32), np.ones((HEAD_DIM, HEAD_DIM), np.float32))
    return jnp.asarray(selk, BF16), jnp.asarray(selq, BF16), jnp.asarray(e, BF16)


def kernel(x, mix_norm_g, w_in, q_norm_a, k_norm_a, rel_bias, q_norm_b, k_norm_b, forget_bias, shift_mu, w0, w2,
           a0, a2, g2, k_k, k_a, r_k, lnx_g, lnx_b, w_out, ffn_norm_g, w_up, conv_w, conv_b, w_down):
    depth = w_in.shape[0]
    b, t, d = x.shape
    selk, selq, e = _selectors()
    row = lambda v: v.reshape(1, -1)
    for l in range(depth):
        wn, wt, gka, gkb, gqa, gqb, fbn, fbt, shift_b, shift_a, depth_a = _layer_params(
            l, w_in, q_norm_a, k_norm_a, rel_bias, q_norm_b, k_norm_b, forget_bias)
        rwkv_params = (row(shift_mu[l]), row(w0[l]), w2[l].astype(BF16), row(a0[l]), a2[l].astype(BF16),
                       g2[l].astype(BF16), row(k_k[l]), row(k_a[l]), row(r_k[l]), e)
        (qaT, vaT, kA, qbT, vbT, kB, rt, at, bt, kt, bh, kh, v, wcum, bonus, g) = _inproj(
            x, row(mix_norm_g[l]), wn, wt, gka, gkb, gqa, gqb, fbn, fbt, selk, selq, shift_b, rwkv_params)
        bias = _relbias(rel_bias[l], shift_a)
        yaT = lax.cond(depth_a <= 2 * FAST_MAX_SHIFT,
                       functools.partial(_attn_a, online=False), functools.partial(_attn_a, online=True),
                       qaT, kA, vaT, bias)
        ybT = lax.cond(shift_b[0, 0] <= FAST_MAX_SHIFT,
                       functools.partial(_fox, online=False), functools.partial(_fox, online=True),
                       qbT, kB, vbT)
        yc = _rwkv_chunk(rt, at, bt, kt, bh, kh, v, wcum, bonus, g, row(lnx_g[l]), row(lnx_b[l]), e)
        wo = w_out[l].astype(BF16)
        x = _outproj(x, yaT.reshape(b, WIDTH_A, t), ybT.reshape(b, WIDTH_B, t), yc,
                     wo[:WIDTH_A], wo[WIDTH_A:WIDTH_A + WIDTH_B], wo[WIDTH_A + WIDTH_B:])
        wu = w_up[l].astype(BF16)
        x = _ffn(x, row(ffn_norm_g[l]), wu[:, :D_FF], wu[:, D_FF:], conv_w[l], row(conv_b[l]),
                 w_down[l].astype(BF16))
    return x
```

```python
import functools
import math

import jax
import jax.numpy as jnp
from jax import lax
from jax.experimental import pallas as pl
from jax.experimental.pallas import tpu as pltpu

F32 = jnp.float32
BF16 = jnp.bfloat16

D_MODEL = 1024
HEAD_DIM = 64
CHUNK = 64
LEFT_CHUNKS = 8
MAX_REL = 128
N_HEADS_A = 4
N_HEADS_B = 4
N_HEADS_C = 8
WIDTH_A = N_HEADS_A * HEAD_DIM
WIDTH_B = N_HEADS_B * HEAD_DIM
WIDTH_C = N_HEADS_C * HEAD_DIM
DECAY_LORA = 64
AAA_LORA = 64
GATE_LORA = 128
COLS_C = 3 * WIDTH_C + DECAY_LORA + AAA_LORA + GATE_LORA
D_FF = 2816
RMS_EPS = 1e-6
LNX_EPS = 64e-5
NEG_INF = -1e30

LANES = 128
TOK_TILE = 512
ATT_A_TILE = 256
ATT_A_WIN = 3 * ATT_A_TILE
ATT_A_AHEAD = 3
FOX_TILE = 512
FOX_HEADS = 2
FOX_GROUP = 8
FOX_AHEAD = 2
RWKV_GROUP = 4
RWKV_TILE = 256
RWKV_ROWS = 4
RWKV_SCHEDULE = ((0, 1), (2, 3))
RWKV_STAGES = 9
FF_SPLITS = (0, 768, 1536, 2304, 2816)
VMEM_LIMIT = 56 * 1024 * 1024

PN_KB, PN_F, PN_C = 0, 512, 640
PN_COLS = PN_C + COLS_C
VB_ROWS = 80
PT_QA = 0
PT_KA = PT_QA + WIDTH_A
PT_VA = PT_KA + WIDTH_A
PT_QB = PT_VA + N_HEADS_A * VB_ROWS
QB_ROWS = 80
PT_VB = PT_QB + N_HEADS_B * QB_ROWS
PT_F = PT_VB + N_HEADS_B * VB_ROWS
PT_ROWS = PT_F + 16
LOG2E = 1.4426950408889634
FAST_MAX_SHIFT = 40.0


def _cparams(n_axes):
    return pltpu.CompilerParams(dimension_semantics=("arbitrary",) * n_axes,
                                vmem_limit_bytes=VMEM_LIMIT)


def _split3(x):
    hi = x.astype(BF16)
    r1 = x - hi.astype(F32)
    mid = r1.astype(BF16)
    lo = (r1 - mid.astype(F32)).astype(BF16)
    return hi, mid, lo


def _split2(x):
    hi = x.astype(BF16)
    lo = (x - hi.astype(F32)).astype(BF16)
    return hi, lo


def _dot(a, b):
    return jnp.dot(a, b, preferred_element_type=F32)


def _dot_nt(a, b):
    return lax.dot_general(a, b, (((1,), (1,)), ((), ())), preferred_element_type=F32)


def _dot_tn(a, b):
    return lax.dot_general(a, b, (((0,), (0,)), ((), ())), preferred_element_type=F32)


def _log_sigmoid(x):
    return jnp.minimum(x, 0.0) - jnp.log(1.0 + jnp.exp(-jnp.abs(x)))


def _softplus(x):
    return jnp.maximum(x, 0.0) + jnp.log(1.0 + jnp.exp(-jnp.abs(x)))


def _sigmoid(x):
    return 1.0 / (1.0 + jnp.exp(-x))


def _inproj_body(x_ref, g_ref, wn_ref, wt_ref, gka_ref, gkb_ref, gqa_ref, gqb_ref, fbn_ref, fbt_ref,
                 selk_ref, selq_ref, shift_ref, *rest):
    rwkv_params, rest = rest[:10], rest[10:]
    qaT_ref, vaT_ref, kA_ref, qbT_ref, vbT_ref, kB_ref = rest[:6]
    rwkv_outs = rest[6:16]
    carry_n, carry_t, sbuf = rest[16:]
    tm = x_ref.shape[1]
    t = pl.program_id(1)

    @pl.when(t == 0)
    def _():
        carry_n[...] = jnp.zeros_like(carry_n)
        carry_t[...] = jnp.zeros_like(carry_t)

    x = x_ref[0]
    ms = jnp.mean(x * x, axis=-1, keepdims=True)
    h = (x * lax.rsqrt(ms + RMS_EPS) * g_ref[...]).astype(BF16)
    rows_t = lambda start, n: _dot_nt(wt_ref[start:start + n, :], h)
    cols_n = lambda start, n: _dot(h, wn_ref[:, start:start + n])
    row_v = lax.broadcasted_iota(jnp.int32, (VB_ROWS, 1), 0)
    ones_v = jnp.where(row_v == HEAD_DIM, 1.0, 0.0)
    val = {}

    def norm_rows(p, gain_ref):
        p = p.reshape(N_HEADS_A, HEAD_DIM, tm)
        return (p * lax.rsqrt(jnp.mean(p * p, axis=1, keepdims=True) + RMS_EPS) * gain_ref[...][None]).astype(BF16)

    def lane_packed3(z):
        z_hi, z_mid, z_lo = _split3(z)
        return (z_hi.astype(F32) + pltpu.roll(z_mid.astype(F32), N_HEADS_B, axis=1)
                + pltpu.roll(z_lo.astype(F32), 2 * N_HEADS_B, axis=1)).astype(BF16)

    def gates_epilogue(p):
        pn_f, pt_f = p
        lane_n = lax.broadcasted_iota(jnp.int32, (1, LANES), 1)
        lf_n = jnp.where(lane_n < N_HEADS_B, LOG2E * _log_sigmoid(pn_f + fbn_ref[...]), 0.0)
        row_t = lax.broadcasted_iota(jnp.int32, (16, 1), 0)
        lf_t = jnp.where(row_t < N_HEADS_B, LOG2E * _log_sigmoid(pt_f + fbt_ref[...]), 0.0)
        ri = lax.broadcasted_iota(jnp.int32, (tm, tm), 0)
        ci = lax.broadcasted_iota(jnp.int32, (tm, tm), 1)
        low = (ci <= ri).astype(BF16)
        upp = (ri <= ci).astype(BF16)
        cum3 = _dot(low, lane_packed3(lf_n))
        cum = cum3 + pltpu.roll(cum3, LANES - N_HEADS_B, axis=1) + pltpu.roll(cum3, LANES - 2 * N_HEADS_B, axis=1)
        val["c_n"] = jnp.where(lane_n < N_HEADS_B, cum, 0.0) + carry_n[0:1, :]
        ht, mt, lt = _split3(lf_t)
        val["c_t"] = _dot(ht, upp) + _dot(mt, upp) + _dot(lt, upp) + carry_t[:, 0:1]
        carry_n[...] = carry_n[...] + jnp.sum(lf_n, axis=0, keepdims=True)
        carry_t[...] = carry_t[...] + jnp.sum(lf_t, axis=1, keepdims=True)

    def kb_epilogue(pn_kb):
        kaug = _dot(lane_packed3(val["c_n"]), selk_ref[...])
        lane_k = lax.broadcasted_iota(jnp.int32, (1, LANES), 1)
        ones_k = jnp.where((lane_k >= HEAD_DIM) & (lane_k < HEAD_DIM + 3), 1.0, 0.0)
        for hh in range(N_HEADS_B):
            k = pn_kb[:, LANES * hh:LANES * (hh + 1)]
            msk = jnp.sum(k * k, axis=-1, keepdims=True) * (1.0 / HEAD_DIM)
            kn = k * lax.rsqrt(msk + RMS_EPS) * gkb_ref[...]
            kB_ref[0, hh] = (kn + kaug[:, LANES * hh:LANES * (hh + 1)] + ones_k).astype(BF16)

    def qb_epilogue(pt_qb):
        c3 = jnp.concatenate(_split3(val["c_t"] - shift_ref[...]), axis=0)
        qaug = _dot(selq_ref[...], c3)
        qb = pt_qb.reshape(N_HEADS_B, QB_ROWS, tm)
        msq = jnp.sum(qb * qb, axis=1, keepdims=True) * (1.0 / HEAD_DIM)
        row_q = lax.broadcasted_iota(jnp.int32, (QB_ROWS, 1), 0)
        ones_q = jnp.where((row_q >= HEAD_DIM + 3) & (row_q < HEAD_DIM + 6), 1.0, 0.0)
        qn = qb * lax.rsqrt(msq + RMS_EPS) * gqb_ref[...][None]
        qbT_ref[0] = (qn + qaug.reshape(N_HEADS_B, QB_ROWS, tm) + ones_q[None]).astype(BF16)

    def store(ref, value):
        ref[...] = value

    sections = [
        (lambda: rows_t(PT_QA, WIDTH_A), lambda p: store(qaT_ref, norm_rows(p, gqa_ref)[None])),
        (lambda: rows_t(PT_KA, WIDTH_A), lambda p: store(kA_ref, norm_rows(p, gka_ref)[None])),
        (lambda: rows_t(PT_VA, N_HEADS_A * VB_ROWS),
         lambda p: store(vaT_ref, (p.reshape(N_HEADS_A, VB_ROWS, tm) + ones_v[None]).astype(BF16)[None])),
        (lambda: (cols_n(PN_F, LANES), rows_t(PT_F, 16)), gates_epilogue),
        (lambda: rows_t(PT_VB, N_HEADS_B * VB_ROWS),
         lambda p: store(vbT_ref, (p.reshape(N_HEADS_B, VB_ROWS, tm) + ones_v[None]).astype(BF16)[None, :, None])),
        (lambda: cols_n(PN_KB, N_HEADS_B * LANES), kb_epilogue),
        (lambda: rows_t(PT_QB, N_HEADS_B * QB_ROWS), qb_epilogue),
    ]
    slots = _rwkv_prep(lambda start, n: cols_n(PN_C + start, n), tm, t, *rwkv_params, *rwkv_outs, sbuf)
    pending = None
    for mm, epilogue in sections:
        p = mm()
        if pending is not None:
            pending[1](pending[0])
        pending = (p, epilogue)
        next(slots, None)
    pending[1](pending[0])
    for _ in slots:
        pass


def _inproj(x, g, wn, wt, gka, gkb, gqa, gqb, fbn, fbt, selk, selq, shift, rwkv_params):
    b, t, d = x.shape
    tm = TOK_TILE
    nt = t // tm
    const = lambda shape: pl.BlockSpec(shape, lambda i, j: (0,) * len(shape))
    tok = lambda: pl.BlockSpec((1, tm, WIDTH_C), lambda i, j: (i, j, 0))
    bf = jax.ShapeDtypeStruct((b, t, WIDTH_C), BF16)
    out_shape = (
        jax.ShapeDtypeStruct((b, N_HEADS_A, HEAD_DIM, t), BF16),
        jax.ShapeDtypeStruct((b, N_HEADS_A, VB_ROWS, t), BF16),
        jax.ShapeDtypeStruct((b, N_HEADS_A, HEAD_DIM, t), BF16),
        jax.ShapeDtypeStruct((b, N_HEADS_B, QB_ROWS, t), BF16),
        jax.ShapeDtypeStruct((b, N_HEADS_B, nt, VB_ROWS, tm), BF16),
        jax.ShapeDtypeStruct((b, N_HEADS_B, t, LANES), BF16),
        bf, bf, bf, bf, bf, bf, bf,
        jax.ShapeDtypeStruct((b, t // RWKV_TILE, WIDTH_C, LANES), F32),
        bf, bf,
    )
    out_specs = (
        pl.BlockSpec((1, N_HEADS_A, HEAD_DIM, tm), lambda i, j: (i, 0, 0, j)),
        pl.BlockSpec((1, N_HEADS_A, VB_ROWS, tm), lambda i, j: (i, 0, 0, j)),
        pl.BlockSpec((1, N_HEADS_A, HEAD_DIM, tm), lambda i, j: (i, 0, 0, j)),
        pl.BlockSpec((1, N_HEADS_B, QB_ROWS, tm), lambda i, j: (i, 0, 0, j)),
        pl.BlockSpec((1, N_HEADS_B, 1, VB_ROWS, tm), lambda i, j: (i, 0, j, 0, 0)),
        pl.BlockSpec((1, N_HEADS_B, tm, LANES), lambda i, j: (i, 0, j, 0)),
        tok(), tok(), tok(), tok(), tok(), tok(), tok(),
        pl.BlockSpec((1, tm // RWKV_TILE, WIDTH_C, LANES), lambda i, j: (i, j, 0, 0)),
        tok(), tok(),
    )
    in_specs = [
        pl.BlockSpec((1, tm, d), lambda i, j: (i, j, 0)),
        const((1, d)), const(wn.shape), const(wt.shape),
        const(gka.shape), const(gkb.shape), const(gqa.shape), const(gqb.shape),
        const(fbn.shape), const(fbt.shape), const(selk.shape), const(selq.shape), const(shift.shape),
    ] + [const(p.shape) for p in rwkv_params]
    return pl.pallas_call(
        _inproj_body,
        grid=(b, nt),
        in_specs=in_specs,
        out_specs=out_specs,
        out_shape=out_shape,
        scratch_shapes=[pltpu.VMEM((8, LANES), F32), pltpu.VMEM((16, LANES), F32),
                        pltpu.VMEM((tm + 8, COLS_C), F32)],
        compiler_params=_cparams(2),
        name="inproj",
    )(x, g, wn, wt, gka, gkb, gqa, gqb, fbn, fbt, selk, selq, shift, *rwkv_params)


def _relbias_body(tab_ref, shift_ref, o_ref):
    hh = pl.program_id(0)
    kj = lax.broadcasted_iota(jnp.int32, (ATT_A_WIN, ATT_A_TILE), 0)
    qi = lax.broadcasted_iota(jnp.int32, (ATT_A_WIN, ATT_A_TILE), 1)
    rel = jnp.clip(kj - LEFT_CHUNKS * CHUNK - qi, -MAX_REL, MAX_REL) + MAX_REL

    def body(r, acc):
        return jnp.where(rel == r, tab_ref[hh, r], acc)

    bias = lax.fori_loop(0, 2 * MAX_REL + 1, body, jnp.zeros((ATT_A_WIN, ATT_A_TILE), F32))
    kc = kj // CHUNK
    qc = qi // CHUNK
    band = (kc >= qc) & (kc <= qc + LEFT_CHUNKS)
    o_ref[0] = jnp.where(band, LOG2E * bias - shift_ref[hh], NEG_INF)


def _relbias(tab, shift):
    return pl.pallas_call(
        _relbias_body,
        grid=(N_HEADS_A,),
        in_specs=[pl.BlockSpec(memory_space=pltpu.SMEM), pl.BlockSpec(memory_space=pltpu.SMEM)],
        out_specs=pl.BlockSpec((1, ATT_A_WIN, ATT_A_TILE), lambda i: (i, 0, 0)),
        out_shape=jax.ShapeDtypeStruct((N_HEADS_A, ATT_A_WIN, ATT_A_TILE), F32),
        compiler_params=_cparams(1),
        name="relbias",
    )(tab, shift)


def _attn_a_body(q_ref, k0_ref, k1_ref, k2_ref, v0_ref, v1_ref, v2_ref, bias_ref, o_ref, *, online):
    i = pl.program_id(1)
    tq = ATT_A_TILE
    kj = lax.broadcasted_iota(jnp.int32, (tq, 1), 0)

    k_refs = (k0_ref, k1_ref, k2_ref)
    v_refs = (v0_ref, v1_ref, v2_ref)

    def run(mask_padding):
        def scores(hh):
            q = q_ref[0, hh]
            ss = []
            for d, k_ref in enumerate(k_refs):
                s = _dot_tn(k_ref[0, hh], q) + bias_ref[hh, d * tq:(d + 1) * tq, :]
                if mask_padding:
                    s = jnp.where(kj + (i - 2 + d) * tq >= 0, s, NEG_INF)
                ss.append(s)
            return ss

        def finish(hh, ss):
            if online:
                m = functools.reduce(jnp.maximum, [jnp.max(s, axis=0, keepdims=True) for s in ss])
                ss = [s - m for s in ss]
            acc = jnp.zeros((VB_ROWS, tq), F32)
            for s, v_ref in zip(ss, v_refs):
                acc = acc + _dot(v_ref[0, hh], jnp.exp2(s).astype(BF16))
            o_ref[0, hh] = (acc[0:HEAD_DIM] / acc[HEAD_DIM:HEAD_DIM + 1]).astype(BF16)

        pending = []
        for hh in range(N_HEADS_A):
            pending.append((hh, scores(hh)))
            if len(pending) > ATT_A_AHEAD:
                finish(*pending.pop(0))
        for item in pending:
            finish(*item)

    pl.when(i < 2)(functools.partial(run, True))
    pl.when(i >= 2)(functools.partial(run, False))


def _attn_a(qaT, kA, vaT, bias, online):
    b, nh, _, t = qaT.shape
    tq = ATT_A_TILE
    kspec = lambda d: pl.BlockSpec((1, nh, HEAD_DIM, tq),
                                   lambda bb, i: (bb, 0, 0, jnp.maximum(i - 2 + d, 0)))
    vspec = lambda d: pl.BlockSpec((1, nh, VB_ROWS, tq),
                                   lambda bb, i: (bb, 0, 0, jnp.maximum(i - 2 + d, 0)))
    return pl.pallas_call(
        functools.partial(_attn_a_body, online=online),
        grid=(b, t // tq),
        in_specs=[pl.BlockSpec((1, nh, HEAD_DIM, tq), lambda bb, i: (bb, 0, 0, i)),
                  kspec(0), kspec(1), kspec(2), vspec(0), vspec(1), vspec(2),
                  pl.BlockSpec((nh, ATT_A_WIN, tq), lambda bb, i: (0, 0, 0))],
        out_specs=pl.BlockSpec((1, nh, HEAD_DIM, tq), lambda bb, i: (bb, 0, 0, i)),
        out_shape=jax.ShapeDtypeStruct((b, nh, HEAD_DIM, t), BF16),
        compiler_params=_cparams(2),
        name="attn_a_online" if online else "attn_a",
    )(qaT, kA, kA, kA, vaT, vaT, vaT, bias)


def _fox_body(q_ref, k_ref, v_ref, o_ref, *, online):
    i = pl.program_id(2)
    tq = FOX_TILE
    nh = q_ref.shape[1]
    q = [jnp.concatenate([q_ref[0, hh], jnp.zeros((LANES - QB_ROWS, tq), BF16)], axis=0) for hh in range(nh)]

    def scores(item):
        j, hh = item
        ks = k_ref[0, hh, pl.ds(pl.multiple_of(j * tq, tq), tq), :]
        return _dot(ks, q[hh])

    def accumulate(item, s, m, acc, diagonal):
        j, hh = item
        vs = v_ref[0, hh, j]
        if diagonal:
            kj = lax.broadcasted_iota(jnp.int32, (tq, tq), 0)
            qi = lax.broadcasted_iota(jnp.int32, (tq, tq), 1)
            causal = kj <= qi
        if online:
            if diagonal:
                s = jnp.where(causal, s, NEG_INF)
            m_new = jnp.maximum(m, jnp.max(s, axis=0, keepdims=True))
            acc = jnp.exp2(m - m_new) * acc
            p = jnp.exp2(s - m_new)
            m = m_new
        else:
            p = jnp.exp2(s)
            if diagonal:
                p = jnp.where(causal, p, 0.0)
        return m, acc + _dot(vs, p.astype(BF16))

    def tiles(js, carry, last_diagonal):
        state = [list(c) for c in carry]
        items = [(j, hh) for j in js for hh in range(nh)]
        ss = []

        def finish(n):
            hh = items[n][1]
            diagonal = last_diagonal and n >= len(items) - nh
            state[hh] = list(accumulate(items[n], ss[n], state[hh][0], state[hh][1], diagonal))

        for n, item in enumerate(items):
            ss.append(scores(item))
            if n >= FOX_AHEAD:
                finish(n - FOX_AHEAD)
        for n in range(max(len(items) - FOX_AHEAD, 0), len(items)):
            finish(n)
        return tuple(tuple(c) for c in state)

    init = tuple((jnp.full((1, tq), NEG_INF, F32), jnp.zeros((VB_ROWS, tq), F32)) for _ in range(nh))
    if online:
        carry = lax.fori_loop(0, i, lambda j, c: tiles([j], c, False), init)
        carry = tiles([i], carry, True)
    else:
        g = FOX_GROUP
        carry = lax.fori_loop(0, i // g, lambda jj, c: tiles([g * jj + n for n in range(g)], c, False), init)
        tails = [functools.partial(lambda c, r: tiles([i - r + n for n in range(r + 1)], c, True), r=r)
                 for r in range(g)]
        carry = lax.switch(i % g, tails, carry)
    for hh, (_, acc) in enumerate(carry):
        o_ref[0, hh] = (acc[0:HEAD_DIM] / acc[HEAD_DIM:HEAD_DIM + 1]).astype(BF16)


def _fox(qbT, kB, vbT, online):
    b, nh, _, t = qbT.shape
    tq = FOX_TILE
    nk = t // tq
    return pl.pallas_call(
        functools.partial(_fox_body, online=online),
        grid=(b, nh // FOX_HEADS, nk),
        in_specs=[pl.BlockSpec((1, FOX_HEADS, QB_ROWS, tq), lambda bb, hh, i: (bb, hh, 0, i)),
                  pl.BlockSpec((1, FOX_HEADS, t, LANES), lambda bb, hh, i: (bb, hh, 0, 0)),
                  pl.BlockSpec((1, FOX_HEADS, nk, VB_ROWS, tq), lambda bb, hh, i: (bb, hh, 0, 0, 0))],
        out_specs=pl.BlockSpec((1, FOX_HEADS, HEAD_DIM, tq), lambda bb, hh, i: (bb, hh, 0, i)),
        out_shape=jax.ShapeDtypeStruct((b, nh, HEAD_DIM, t), BF16),
        compiler_params=_cparams(3),
        name="fox_online" if online else "fox",
    )(qbT, kB, vbT)


def _rwkv_prep(proj, tm, t, mu_ref, w0_ref, w2_ref, a0_ref, a2_ref, g2_ref, kk_ref, ka_ref, rk_ref, e_ref,
               rt_ref, at_ref, bt_ref, kt_ref, bh_ref, kh_ref, v_ref, wc_ref, bonus_ref, g_ref, sbuf):
    @pl.when(t == 0)
    def _():
        sbuf[0:8, :] = jnp.zeros((8, COLS_C), F32)

    def shifted(p, start):
        cols = slice(start, start + p.shape[1])
        sbuf[8:8 + tm, cols] = p
        prev = sbuf[7:7 + tm, cols]
        sbuf[0:8, cols] = p[tm - 8:tm, :]
        return p + (prev - p) * mu_ref[:, cols]

    c = WIDTH_C
    p_lo = proj(3 * c, COLS_C - 3 * c)
    p_k = proj(c, c)
    u_lo = shifted(p_lo, 3 * c)
    yield
    w_lo = u_lo[:, 0:DECAY_LORA]
    a_lo = u_lo[:, DECAY_LORA:DECAY_LORA + AAA_LORA]
    g_lo = u_lo[:, DECAY_LORA + AAA_LORA:]
    w = w0_ref[...] + _dot(jnp.tanh(w_lo).astype(BF16), w2_ref[...])
    w = -_softplus(-w) - 0.5
    ld = -jnp.exp(w)
    p_r = proj(0, c)
    yield
    a = _sigmoid(a0_ref[...] + _dot(a_lo.astype(BF16), a2_ref[...]))
    g_ref[0] = _dot(_sigmoid(g_lo).astype(BF16), g2_ref[...]).astype(BF16)
    k = shifted(p_k, c)
    p_v = proj(2 * c, c)
    yield

    e = e_ref[...]
    kk = k * kk_ref[...]
    nrm2 = _dot((kk * kk).astype(BF16), e)
    kkn = kk * lax.rsqrt(jnp.maximum(nrm2, 1e-24))
    k2 = k * (1.0 + (a - 1.0) * ka_ref[...])
    kka = kkn * a
    r = shifted(p_r, 0)
    v = shifted(p_v, 2 * c)
    v_ref[0] = v.astype(BF16)
    bonus_ref[0] = (_dot((r * k2 * rk_ref[...]).astype(BF16), e) * v).astype(BF16)
    yield

    ri = lax.broadcasted_iota(jnp.int32, (2 * CHUNK, CHUNK), 0)
    ci = lax.broadcasted_iota(jnp.int32, (2 * CHUNK, CHUNK), 1)
    tri = ((ci <= ri) | (ri >= CHUNK)).astype(BF16)
    tri2 = jnp.concatenate([tri, tri], axis=1)
    ld_hi, ld_lo = _split2(ld)
    for cc in range(tm // CHUNK):
        sl = slice(cc * CHUNK, (cc + 1) * CHUNK)
        cum = _dot(tri2, jnp.concatenate([ld_hi[sl], ld_lo[sl]], axis=0))
        lc = cum[0:CHUNK]
        tot = cum[CHUNK:2 * CHUNK]
        e_neg = jnp.exp(-lc)
        e_rem = jnp.exp(tot - lc)
        rt_ref[0, sl, :] = (r[sl] * jnp.exp(lc)).astype(BF16)
        at_ref[0, sl, :] = (-kkn[sl] * jnp.exp(lc - ld[sl])).astype(BF16)
        bt_ref[0, sl, :] = (kka[sl] * e_neg).astype(BF16)
        kt_ref[0, sl, :] = (k2[sl] * e_neg).astype(BF16)
        bh_ref[0, sl, :] = (kka[sl] * e_rem).astype(BF16)
        kh_ref[0, sl, :] = (k2[sl] * e_rem).astype(BF16)
        if cc % 2 == 1:
            yield
    seg = (lax.broadcasted_iota(jnp.int32, (16, tm), 1) // CHUNK
           == lax.broadcasted_iota(jnp.int32, (16, tm), 0)).astype(BF16)
    tots3 = jnp.concatenate(_split3(_dot(seg, ld_hi) + _dot(seg, ld_lo)), axis=0)
    per_blk = RWKV_TILE // CHUNK
    for blk in range(tm // RWKV_TILE):
        put = (lax.broadcasted_iota(jnp.int32, (48, LANES), 0) % 16
               == lax.broadcasted_iota(jnp.int32, (48, LANES), 1) + blk * per_blk).astype(BF16)
        wc_ref[0, blk] = jnp.exp(_dot_tn(tots3, put))


def _bmm(a, b):
    return lax.dot_general(a, b, (((2,), (1,)), ((0,), (0,))), preferred_element_type=F32)


def _bmm_nt(a, b):
    return lax.dot_general(a, b, (((2,), (2,)), ((0,), (0,))), preferred_element_type=F32)


def _bmm_tn(a, b):
    return lax.dot_general(a, b, (((1,), (1,)), ((0,), (0,))), preferred_element_type=F32)


def _rwkv_chunk_body(rt_ref, at_ref, bt_ref, kt_ref, bh_ref, kh_ref, v_ref, wc_ref, bonus_ref, g_ref,
                     lg_ref, lb_ref, e_ref, o_ref, h_ref):
    rows, tm = rt_ref.shape[0], rt_ref.shape[1]
    nc = tm // CHUNK
    ng = N_HEADS_C // RWKV_GROUP
    gl = RWKV_GROUP * HEAD_DIM
    units = [(r, g) for r in range(rows) for g in range(ng)]
    nu = len(units)
    t = pl.program_id(1)

    @pl.when(t == 0)
    def _():
        h_ref[...] = jnp.zeros_like(h_ref)

    lane_head = lax.broadcasted_iota(jnp.int32, (1, CHUNK, gl), 2) // HEAD_DIM
    ri = lax.broadcasted_iota(jnp.int32, (1, 2 * CHUNK, gl), 1)
    ci = lax.broadcasted_iota(jnp.int32, (1, 2 * CHUNK, gl), 2) & (CHUNK - 1)
    rr = ri & (CHUNK - 1)
    keep = (rr > ci) | ((ri >= CHUNK) & (rr == ci))
    eye = (lax.broadcasted_iota(jnp.int32, (1, CHUNK, gl), 1)
           == (lax.broadcasted_iota(jnp.int32, (1, CHUNK, gl), 2) & (CHUNK - 1)))
    same_head = (lax.broadcasted_iota(jnp.int32, (1, gl, gl), 1) // HEAD_DIM
                 == lax.broadcasted_iota(jnp.int32, (1, gl, gl), 2) // HEAD_DIM)

    def bdiag(x):
        return jnp.concatenate([jnp.where(lane_head == hh, x, jnp.zeros_like(x)) for hh in range(RWKV_GROUP)],
                               axis=1)

    def chunk_local(cs, out):
        tile = lambda ref: jnp.stack([ref[r, c * CHUNK:(c + 1) * CHUNK, g * gl:(g + 1) * gl]
                                      for c in cs for r, g in units])
        rt, at, bt, kt, bh, kh, v = (tile(r) for r in (rt_ref, at_ref, bt_ref, kt_ref, bh_ref, kh_ref, v_ref))
        ar = jnp.concatenate([at, rt], axis=1)
        sb = jnp.where(keep, _bmm_nt(ar, bdiag(bt)), 0.0)
        sk = jnp.where(keep, _bmm_nt(ar, bdiag(kt)), 0.0).astype(BF16)
        m_rb = sb[:, CHUNK:].astype(BF16)
        yield
        a = sb[:, :CHUNK]
        tinv = jnp.where(eye, 1.0, 0.0) + a
        ab = a.astype(BF16)
        x = _bmm(ab, bdiag(ab))
        yield
        for _ in range(4):
            xb = x.astype(BF16)
            r = _bmm(jnp.concatenate([tinv.astype(BF16), xb], axis=1), bdiag(xb))
            tinv = tinv + r[:, :CHUNK]
            x = r[:, CHUNK:]
            yield
        tinv = (tinv + _bmm(tinv.astype(BF16), bdiag(x.astype(BF16)))).astype(BF16)
        bdv = bdiag(v)
        av = _bmm(sk[:, :CHUNK], bdv)
        yield
        p = _bmm(tinv, bdiag(at)).astype(BF16)
        u0 = _bmm(tinv, bdiag(av.astype(BF16)))
        yield
        q = (rt.astype(F32) + _bmm(m_rb, bdiag(p))).astype(BF16)
        y0 = _bmm(m_rb, bdiag(u0.astype(BF16))) + _bmm(sk[:, CHUNK:], bdv)
        parts = (jnp.concatenate([q, p], axis=1), jnp.concatenate([y0, u0], axis=1),
                 jnp.concatenate([bh, kh], axis=1), v)
        out += [tuple(z[n * nu:(n + 1) * nu] for z in parts) for n in range(len(cs))]
        yield

    wct = [wc_ref[r, 0] for r in range(rows)]
    state = [h_ref[...]]
    ys = []
    local = []

    def recur(c):
        qp, yu0, bk, v = local[c]
        yu = _bmm(qp, state[0].astype(BF16)) + yu0
        ys.append(yu[:, :CHUNK])
        upd = _bmm_tn(bk, jnp.concatenate([yu[:, CHUNK:].astype(BF16), v], axis=1))
        wcol = jnp.stack([wct[r][g * gl:(g + 1) * gl, c:c + 1] for r, g in units])
        state[0] = jnp.where(same_head, upd, 0.0) + wcol * state[0]

    assert sorted(c for cs in RWKV_SCHEDULE for c in cs) == list(range(nc))
    done = 0
    for cs in RWKV_SCHEDULE:
        pending = list(range(done, len(local)))
        slots = {(k + 1) * RWKV_STAGES // (len(pending) + 1): c for k, c in enumerate(pending)}
        for n, _ in enumerate(chunk_local(list(cs), local), start=1):
            if n in slots:
                recur(slots[n])
                done += 1
    for c in range(done, nc):
        recur(c)
    h_ref[...] = state[0]

    y = jnp.concatenate([jnp.concatenate([yc[r * ng + g] for g in range(ng)], axis=-1)
                         for r in range(rows) for yc in ys], axis=0)
    e = e_ref[...]

    def head_mean(z):
        return _dot(z.astype(BF16), e) * (1.0 / HEAD_DIM)

    d = y - head_mean(y)
    yn = d * lax.rsqrt(head_mean(d * d) + LNX_EPS)
    flat = lambda ref: ref[...].astype(F32).reshape(rows * tm, WIDTH_C)
    out = (yn * lg_ref[...] + lb_ref[...] + flat(bonus_ref)) * flat(g_ref)
    o_ref[...] = out.astype(BF16).reshape(rows, tm, WIDTH_C)


def _rwkv_chunk(rt, at, bt, kt, bh, kh, v, wc, bonus, g, lnx_g, lnx_b, e):
    b, t, _ = rt.shape
    tm = RWKV_TILE
    rows = math.gcd(RWKV_ROWS, b)
    gl = RWKV_GROUP * HEAD_DIM
    const = lambda shape: pl.BlockSpec(shape, lambda i, j: (0,) * len(shape))
    tok = lambda: pl.BlockSpec((rows, tm, WIDTH_C), lambda i, j: (i, j, 0))
    return pl.pallas_call(
        _rwkv_chunk_body,
        grid=(b // rows, t // tm),
        in_specs=[tok(), tok(), tok(), tok(), tok(), tok(), tok(),
                  pl.BlockSpec((rows, 1, WIDTH_C, LANES), lambda i, j: (i, j, 0, 0)),
                  tok(), tok(), const(lnx_g.shape), const(lnx_b.shape), const(e.shape)],
        out_specs=tok(),
        out_shape=jax.ShapeDtypeStruct((b, t, WIDTH_C), BF16),
        scratch_shapes=[pltpu.VMEM((rows * (N_HEADS_C // RWKV_GROUP), gl, gl), F32)],
        compiler_params=_cparams(2),
        name="rwkv_chunk",
    )(rt, at, bt, kt, bh, kh, v, wc, bonus, g, lnx_g, lnx_b, e)


def _outproj_body(x_ref, ya_ref, yb_ref, yc_ref, wa_ref, wb_ref, wc_ref, o_ref):
    acc = _dot_tn(ya_ref[0], wa_ref[...])
    acc = acc + _dot_tn(yb_ref[0], wb_ref[...])
    acc = acc + _dot(yc_ref[0], wc_ref[...])
    o_ref[0] = x_ref[0] + acc


def _outproj(x, yaT, ybT, yc, wa, wb, wc):
    b, t, d = x.shape
    tm = TOK_TILE
    const = lambda shape: pl.BlockSpec(shape, lambda i, j: (0,) * len(shape))
    return pl.pallas_call(
        _outproj_body,
        grid=(b, t // tm),
        in_specs=[pl.BlockSpec((1, tm, d), lambda i, j: (i, j, 0)),
                  pl.BlockSpec((1, WIDTH_A, tm), lambda i, j: (i, 0, j)),
                  pl.BlockSpec((1, WIDTH_B, tm), lambda i, j: (i, 0, j)),
                  pl.BlockSpec((1, tm, WIDTH_C), lambda i, j: (i, j, 0)),
                  const(wa.shape), const(wb.shape), const(wc.shape)],
        out_specs=pl.BlockSpec((1, tm, d), lambda i, j: (i, j, 0)),
        out_shape=jax.ShapeDtypeStruct((b, t, d), F32),
        compiler_params=_cparams(2),
        name="outproj",
    )(x, yaT, ybT, yc, wa, wb, wc)


def _ffn_body(x_ref, g_ref, wg_ref, wv_ref, cw_ref, cb_ref, wd_ref, o_ref, gbuf):
    tm = x_ref.shape[1]
    t = pl.program_id(1)

    @pl.when(t == 0)
    def _():
        gbuf[0:8, :] = jnp.zeros((8, D_FF), F32)

    x = x_ref[0]
    ms = jnp.mean(x * x, axis=-1, keepdims=True)
    h = (x * lax.rsqrt(ms + RMS_EPS) * g_ref[...]).astype(BF16)
    acc = x
    for lo, hi in zip(FF_SPLITS[:-1], FF_SPLITS[1:]):
        gate = _dot(h, wg_ref[:, lo:hi])
        val = _dot(h, wv_ref[:, lo:hi])
        gbuf[8:8 + tm, lo:hi] = gate
        g1 = gbuf[7:7 + tm, lo:hi]
        g2 = gbuf[6:6 + tm, lo:hi]
        gbuf[0:8, lo:hi] = gate[tm - 8:tm, :]
        conv = cb_ref[:, lo:hi] + g2 * cw_ref[0:1, lo:hi] + g1 * cw_ref[1:2, lo:hi] + gate * cw_ref[2:3, lo:hi]
        act = (conv * _sigmoid(conv) * val).astype(BF16)
        acc = acc + _dot(act, wd_ref[lo:hi, :])
    o_ref[0] = acc


def _ffn(x, g, wg, wv, cw, cb, wd):
    b, t, d = x.shape
    tm = TOK_TILE
    const = lambda shape: pl.BlockSpec(shape, lambda i, j: (0,) * len(shape),
                                       pipeline_mode=pl.Buffered(1))
    return pl.pallas_call(
        _ffn_body,
        grid=(b, t // tm),
        in_specs=[pl.BlockSpec((1, tm, d), lambda i, j: (i, j, 0)),
                  const(g.shape), const(wg.shape), const(wv.shape), const(cw.shape), const(cb.shape),
                  const(wd.shape)],
        out_specs=pl.BlockSpec((1, tm, d), lambda i, j: (i, j, 0)),
        out_shape=jax.ShapeDtypeStruct((b, t, d), F32),
        scratch_shapes=[pltpu.VMEM((tm + 8, D_FF), F32)],
        compiler_params=_cparams(2),
        name="ffn",
    )(x, g, wg, wv, cw, cb, wd)


def _pad_heads_cols(w, nh):
    d = w.shape[0]
    w = w.reshape(d, nh, HEAD_DIM)
    return jnp.pad(w, ((0, 0), (0, 0), (0, LANES - HEAD_DIM))).reshape(d, nh * LANES)


def _qk_bound(gq, gk):
    scale = HEAD_DIM ** -0.5
    return 1.01 * HEAD_DIM * scale * jnp.max(jnp.abs(gq)) * jnp.max(jnp.abs(gk)) + 0.05


def _layer_params(l, w_in, q_norm_a, k_norm_a, rel_bias, q_norm_b, k_norm_b, forget_bias):
    w = w_in[l]
    scale = HEAD_DIM ** -0.5
    a0, b0 = 0, 3 * WIDTH_A
    c0 = b0 + 3 * WIDTH_B + N_HEADS_B
    qa, ka, va = (w[:, a0 + i * WIDTH_A:a0 + (i + 1) * WIDTH_A] for i in range(3))
    qb, kb, vb = (w[:, b0 + i * WIDTH_B:b0 + (i + 1) * WIDTH_B] for i in range(3))
    fg = w[:, b0 + 3 * WIDTH_B:c0]
    wc = w[:, c0:]
    wn = jnp.concatenate([_pad_heads_cols(kb, N_HEADS_B),
                          jnp.pad(fg, ((0, 0), (0, LANES - N_HEADS_B))), wc], axis=1).astype(BF16)
    pad_v = lambda v, nh: jnp.pad(v.reshape(-1, nh, HEAD_DIM),
                                  ((0, 0), (0, 0), (0, VB_ROWS - HEAD_DIM))).reshape(-1, nh * VB_ROWS)
    assert QB_ROWS == VB_ROWS
    wt = jnp.concatenate([qa, ka, pad_v(va, N_HEADS_A), pad_v(qb, N_HEADS_B), pad_v(vb, N_HEADS_B),
                          jnp.pad(fg, ((0, 0), (0, 16 - N_HEADS_B)))], axis=1).T.astype(BF16)
    pad = LANES - HEAD_DIM
    gka = k_norm_a[l].reshape(HEAD_DIM, 1)
    gkb = jnp.pad(k_norm_b[l], (0, pad)).reshape(1, LANES)
    gqa = (q_norm_a[l] * (scale * LOG2E)).reshape(HEAD_DIM, 1)
    gqb = jnp.pad(q_norm_b[l] * (scale * LOG2E), (0, QB_ROWS - HEAD_DIM)).reshape(QB_ROWS, 1)
    fbn = jnp.pad(forget_bias[l], (0, LANES - N_HEADS_B)).reshape(1, LANES)
    fbt = jnp.pad(forget_bias[l], (0, 16 - N_HEADS_B)).reshape(16, 1)
    shift_b = (LOG2E * _qk_bound(q_norm_b[l], k_norm_b[l])).reshape(1, 1)
    bound_a = _qk_bound(q_norm_a[l], k_norm_a[l])
    tab = rel_bias[l]
    shift_a = LOG2E * (bound_a + jnp.max(tab, axis=1))
    depth_a = jnp.max(shift_a + LOG2E * (bound_a - tab[:, MAX_REL]))
    return wn, wt, gka, gkb, gqa, gqb, fbn, fbt, shift_b, shift_a, depth_a


def _selectors():
    import numpy as np
    selk = np.zeros((LANES, N_HEADS_B * LANES), np.float32)
    selq = np.zeros((N_HEADS_B * QB_ROWS, 3 * 16), np.float32)
    for p in range(3):
        for hh in range(N_HEADS_B):
            selk[p * N_HEADS_B + hh, hh * LANES + HEAD_DIM + 3 + p] = -1.0
            selq[hh * QB_ROWS + HEAD_DIM + p, 16 * p + hh] = 1.0
    e = np.kron(np.eye(N_HEADS_C, dtype=np.float32), np.ones((HEAD_DIM, HEAD_DIM), np.float32))
    return jnp.asarray(selk, BF16), jnp.asarray(selq, BF16), jnp.asarray(e, BF16)


def kernel(x, mix_norm_g, w_in, q_norm_a, k_norm_a, rel_bias, q_norm_b, k_norm_b, forget_bias, shift_mu, w0, w2,
           a0, a2, g2, k_k, k_a, r_k, lnx_g, lnx_b, w_out, ffn_norm_g, w_up, conv_w, conv_b, w_down):
    depth = w_in.shape[0]
    b, t, d = x.shape
    selk, selq, e = _selectors()
    row = lambda v: v.reshape(1, -1)
    for l in range(depth):
        wn, wt, gka, gkb, gqa, gqb, fbn, fbt, shift_b, shift_a, depth_a = _layer_params(
            l, w_in, q_norm_a, k_norm_a, rel_bias, q_norm_b, k_norm_b, forget_bias)
        rwkv_params = (row(shift_mu[l]), row(w0[l]), w2[l].astype(BF16), row(a0[l]), a2[l].astype(BF16),
                       g2[l].astype(BF16), row(k_k[l]), row(k_a[l]), row(r_k[l]), e)
        (qaT, vaT, kA, qbT, vbT, kB, rt, at, bt, kt, bh, kh, v, wcum, bonus, g) = _inproj(
            x, row(mix_norm_g[l]), wn, wt, gka, gkb, gqa, gqb, fbn, fbt, selk, selq, shift_b, rwkv_params)
        bias = _relbias(rel_bias[l], shift_a)
        yaT = lax.cond(depth_a <= 2 * FAST_MAX_SHIFT,
                       functools.partial(_attn_a, online=False), functools.partial(_attn_a, online=True),
                       qaT, kA, vaT, bias)
        ybT = lax.cond(shift_b[0, 0] <= FAST_MAX_SHIFT,
                       functools.partial(_fox, online=False), functools.partial(_fox, online=True),
                       qbT, kB, vbT)
        yc = _rwkv_chunk(rt, at, bt, kt, bh, kh, v, wcum, bonus, g, row(lnx_g[l]), row(lnx_b[l]), e)
        wo = w_out[l].astype(BF16)
        x = _outproj(x, yaT.reshape(b, WIDTH_A, t), ybT.reshape(b, WIDTH_B, t), yc,
                     wo[:WIDTH_A], wo[WIDTH_A:WIDTH_A + WIDTH_B], wo[WIDTH_A + WIDTH_B:])
        wu = w_up[l].astype(BF16)
        x = _ffn(x, row(ffn_norm_g[l]), wu[:, :D_FF], wu[:, D_FF:], conv_w[l], row(conv_b[l]),
                 w_down[l].astype(BF16))
    return x
```

```python
import functools
import math

import jax
import jax.numpy as jnp
from jax import lax
from jax.experimental import pallas as pl
from jax.experimental.pallas import tpu as pltpu

F32 = jnp.float32
BF16 = jnp.bfloat16

D_MODEL = 1024
HEAD_DIM = 64
CHUNK = 64
LEFT_CHUNKS = 8
MAX_REL = 128
N_HEADS_A = 4
N_HEADS_B = 4
N_HEADS_C = 8
WIDTH_A = N_HEADS_A * HEAD_DIM
WIDTH_B = N_HEADS_B * HEAD_DIM
WIDTH_C = N_HEADS_C * HEAD_DIM
DECAY_LORA = 64
AAA_LORA = 64
GATE_LORA = 128
COLS_C = 3 * WIDTH_C + DECAY_LORA + AAA_LORA + GATE_LORA
D_FF = 2816
RMS_EPS = 1e-6
LNX_EPS = 64e-5
NEG_INF = -1e30

LANES = 128
TOK_TILE = 512
ATT_A_TILE = 256
ATT_A_WIN = 3 * ATT_A_TILE
ATT_A_AHEAD = 3
FOX_TILE = 512
FOX_HEADS = 2
FOX_GROUP = 8
FOX_AHEAD = 2
RWKV_GROUP = 4
RWKV_TILE = 256
RWKV_ROWS = 4
RWKV_SCHEDULE = ((0, 1), (2, 3))
RWKV_STAGES = 9
FF_SPLITS = (0, 1024, 2048, 2816)
VMEM_LIMIT = 56 * 1024 * 1024

PN_KB, PN_F, PN_C = 0, 512, 640
PN_COLS = PN_C + COLS_C
VB_ROWS = 80
PT_QA = 0
PT_KA = PT_QA + WIDTH_A
PT_VA = PT_KA + WIDTH_A
PT_QB = PT_VA + N_HEADS_A * VB_ROWS
QB_ROWS = 80
PT_VB = PT_QB + N_HEADS_B * QB_ROWS
PT_F = PT_VB + N_HEADS_B * VB_ROWS
PT_ROWS = PT_F + 16
LOG2E = 1.4426950408889634
FAST_MAX_SHIFT = 40.0


def _cparams(n_axes):
    return pltpu.CompilerParams(dimension_semantics=("arbitrary",) * n_axes,
                                vmem_limit_bytes=VMEM_LIMIT)


def _split3(x):
    hi = x.astype(BF16)
    r1 = x - hi.astype(F32)
    mid = r1.astype(BF16)
    lo = (r1 - mid.astype(F32)).astype(BF16)
    return hi, mid, lo


def _split2(x):
    hi = x.astype(BF16)
    lo = (x - hi.astype(F32)).astype(BF16)
    return hi, lo


def _dot(a, b):
    return jnp.dot(a, b, preferred_element_type=F32)


def _dot_nt(a, b):
    return lax.dot_general(a, b, (((1,), (1,)), ((), ())), preferred_element_type=F32)


def _dot_tn(a, b):
    return lax.dot_general(a, b, (((0,), (0,)), ((), ())), preferred_element_type=F32)


def _log_sigmoid(x):
    return jnp.minimum(x, 0.0) - jnp.log(1.0 + jnp.exp(-jnp.abs(x)))


def _softplus(x):
    return jnp.maximum(x, 0.0) + jnp.log(1.0 + jnp.exp(-jnp.abs(x)))


def _sigmoid(x):
    return 1.0 / (1.0 + jnp.exp(-x))


def _inproj_body(x_ref, g_ref, wn_ref, wt_ref, gka_ref, gkb_ref, gqa_ref, gqb_ref, fbn_ref, fbt_ref,
                 selk_ref, selq_ref, shift_ref, *rest):
    rwkv_params, rest = rest[:10], rest[10:]
    qaT_ref, vaT_ref, kA_ref, qbT_ref, vbT_ref, kB_ref = rest[:6]
    rwkv_outs = rest[6:16]
    carry_n, carry_t, sbuf = rest[16:]
    tm = x_ref.shape[1]
    t = pl.program_id(1)

    @pl.when(t == 0)
    def _():
        carry_n[...] = jnp.zeros_like(carry_n)
        carry_t[...] = jnp.zeros_like(carry_t)

    x = x_ref[0]
    ms = jnp.mean(x * x, axis=-1, keepdims=True)
    h = (x * lax.rsqrt(ms + RMS_EPS) * g_ref[...]).astype(BF16)
    rows_t = lambda start, n: _dot_nt(wt_ref[start:start + n, :], h)
    cols_n = lambda start, n: _dot(h, wn_ref[:, start:start + n])
    row_v = lax.broadcasted_iota(jnp.int32, (VB_ROWS, 1), 0)
    ones_v = jnp.where(row_v == HEAD_DIM, 1.0, 0.0)
    val = {}

    def norm_rows(p, gain_ref):
        p = p.reshape(N_HEADS_A, HEAD_DIM, tm)
        return (p * lax.rsqrt(jnp.mean(p * p, axis=1, keepdims=True) + RMS_EPS) * gain_ref[...][None]).astype(BF16)

    def lane_packed3(z):
        z_hi, z_mid, z_lo = _split3(z)
        return (z_hi.astype(F32) + pltpu.roll(z_mid.astype(F32), N_HEADS_B, axis=1)
                + pltpu.roll(z_lo.astype(F32), 2 * N_HEADS_B, axis=1)).astype(BF16)

    def gates_epilogue(p):
        pn_f, pt_f = p
        lane_n = lax.broadcasted_iota(jnp.int32, (1, LANES), 1)
        lf_n = jnp.where(lane_n < N_HEADS_B, LOG2E * _log_sigmoid(pn_f + fbn_ref[...]), 0.0)
        row_t = lax.broadcasted_iota(jnp.int32, (16, 1), 0)
        lf_t = jnp.where(row_t < N_HEADS_B, LOG2E * _log_sigmoid(pt_f + fbt_ref[...]), 0.0)
        ri = lax.broadcasted_iota(jnp.int32, (tm, tm), 0)
        ci = lax.broadcasted_iota(jnp.int32, (tm, tm), 1)
        low = (ci <= ri).astype(BF16)
        upp = (ri <= ci).astype(BF16)
        cum3 = _dot(low, lane_packed3(lf_n))
        cum = cum3 + pltpu.roll(cum3, LANES - N_HEADS_B, axis=1) + pltpu.roll(cum3, LANES - 2 * N_HEADS_B, axis=1)
        val["c_n"] = jnp.where(lane_n < N_HEADS_B, cum, 0.0) + carry_n[0:1, :]
        ht, mt, lt = _split3(lf_t)
        val["c_t"] = _dot(ht, upp) + _dot(mt, upp) + _dot(lt, upp) + carry_t[:, 0:1]
        carry_n[...] = carry_n[...] + jnp.sum(lf_n, axis=0, keepdims=True)
        carry_t[...] = carry_t[...] + jnp.sum(lf_t, axis=1, keepdims=True)

    def kb_epilogue(pn_kb):
        kaug = _dot(lane_packed3(val["c_n"]), selk_ref[...])
        lane_k = lax.broadcasted_iota(jnp.int32, (1, LANES), 1)
        ones_k = jnp.where((lane_k >= HEAD_DIM) & (lane_k < HEAD_DIM + 3), 1.0, 0.0)
        for hh in range(N_HEADS_B):
            k = pn_kb[:, LANES * hh:LANES * (hh + 1)]
            msk = jnp.sum(k * k, axis=-1, keepdims=True) * (1.0 / HEAD_DIM)
            kn = k * lax.rsqrt(msk + RMS_EPS) * gkb_ref[...]
            kB_ref[0, hh] = (kn + kaug[:, LANES * hh:LANES * (hh + 1)] + ones_k).astype(BF16)

    def qb_epilogue(pt_qb):
        c3 = jnp.concatenate(_split3(val["c_t"] - shift_ref[...]), axis=0)
        qaug = _dot(selq_ref[...], c3)
        qb = pt_qb.reshape(N_HEADS_B, QB_ROWS, tm)
        msq = jnp.sum(qb * qb, axis=1, keepdims=True) * (1.0 / HEAD_DIM)
        row_q = lax.broadcasted_iota(jnp.int32, (QB_ROWS, 1), 0)
        ones_q = jnp.where((row_q >= HEAD_DIM + 3) & (row_q < HEAD_DIM + 6), 1.0, 0.0)
        qn = qb * lax.rsqrt(msq + RMS_EPS) * gqb_ref[...][None]
        qbT_ref[0] = (qn + qaug.reshape(N_HEADS_B, QB_ROWS, tm) + ones_q[None]).astype(BF16)

    def store(ref, value):
        ref[...] = value

    sections = [
        (lambda: rows_t(PT_QA, WIDTH_A), lambda p: store(qaT_ref, norm_rows(p, gqa_ref)[None])),
        (lambda: rows_t(PT_KA, WIDTH_A), lambda p: store(kA_ref, norm_rows(p, gka_ref)[None])),
        (lambda: rows_t(PT_VA, N_HEADS_A * VB_ROWS),
         lambda p: store(vaT_ref, (p.reshape(N_HEADS_A, VB_ROWS, tm) + ones_v[None]).astype(BF16)[None])),
        (lambda: (cols_n(PN_F, LANES), rows_t(PT_F, 16)), gates_epilogue),
        (lambda: rows_t(PT_VB, N_HEADS_B * VB_ROWS),
         lambda p: store(vbT_ref, (p.reshape(N_HEADS_B, VB_ROWS, tm) + ones_v[None]).astype(BF16)[None, :, None])),
        (lambda: cols_n(PN_KB, N_HEADS_B * LANES), kb_epilogue),
        (lambda: rows_t(PT_QB, N_HEADS_B * QB_ROWS), qb_epilogue),
    ]
    slots = _rwkv_prep(lambda start, n: cols_n(PN_C + start, n), tm, t, *rwkv_params, *rwkv_outs, sbuf)
    pending = None
    for mm, epilogue in sections:
        p = mm()
        if pending is not None:
            pending[1](pending[0])
        pending = (p, epilogue)
        next(slots, None)
    pending[1](pending[0])
    for _ in slots:
        pass


def _inproj(x, g, wn, wt, gka, gkb, gqa, gqb, fbn, fbt, selk, selq, shift, rwkv_params):
    b, t, d = x.shape
    tm = TOK_TILE
    nt = t // tm
    const = lambda shape: pl.BlockSpec(shape, lambda i, j: (0,) * len(shape))
    tok = lambda: pl.BlockSpec((1, tm, WIDTH_C), lambda i, j: (i, j, 0))
    bf = jax.ShapeDtypeStruct((b, t, WIDTH_C), BF16)
    out_shape = (
        jax.ShapeDtypeStruct((b, N_HEADS_A, HEAD_DIM, t), BF16),
        jax.ShapeDtypeStruct((b, N_HEADS_A, VB_ROWS, t), BF16),
        jax.ShapeDtypeStruct((b, N_HEADS_A, HEAD_DIM, t), BF16),
        jax.ShapeDtypeStruct((b, N_HEADS_B, QB_ROWS, t), BF16),
        jax.ShapeDtypeStruct((b, N_HEADS_B, nt, VB_ROWS, tm), BF16),
        jax.ShapeDtypeStruct((b, N_HEADS_B, t, LANES), BF16),
        bf, bf, bf, bf, bf, bf, bf,
        jax.ShapeDtypeStruct((b, t // RWKV_TILE, WIDTH_C, LANES), F32),
        bf, bf,
    )
    out_specs = (
        pl.BlockSpec((1, N_HEADS_A, HEAD_DIM, tm), lambda i, j: (i, 0, 0, j)),
        pl.BlockSpec((1, N_HEADS_A, VB_ROWS, tm), lambda i, j: (i, 0, 0, j)),
        pl.BlockSpec((1, N_HEADS_A, HEAD_DIM, tm), lambda i, j: (i, 0, 0, j)),
        pl.BlockSpec((1, N_HEADS_B, QB_ROWS, tm), lambda i, j: (i, 0, 0, j)),
        pl.BlockSpec((1, N_HEADS_B, 1, VB_ROWS, tm), lambda i, j: (i, 0, j, 0, 0)),
        pl.BlockSpec((1, N_HEADS_B, tm, LANES), lambda i, j: (i, 0, j, 0)),
        tok(), tok(), tok(), tok(), tok(), tok(), tok(),
        pl.BlockSpec((1, tm // RWKV_TILE, WIDTH_C, LANES), lambda i, j: (i, j, 0, 0)),
        tok(), tok(),
    )
    in_specs = [
        pl.BlockSpec((1, tm, d), lambda i, j: (i, j, 0)),
        const((1, d)), const(wn.shape), const(wt.shape),
        const(gka.shape), const(gkb.shape), const(gqa.shape), const(gqb.shape),
        const(fbn.shape), const(fbt.shape), const(selk.shape), const(selq.shape), const(shift.shape),
    ] + [const(p.shape) for p in rwkv_params]
    return pl.pallas_call(
        _inproj_body,
        grid=(b, nt),
        in_specs=in_specs,
        out_specs=out_specs,
        out_shape=out_shape,
        scratch_shapes=[pltpu.VMEM((8, LANES), F32), pltpu.VMEM((16, LANES), F32),
                        pltpu.VMEM((tm + 8, COLS_C), F32)],
        compiler_params=_cparams(2),
        name="inproj",
    )(x, g, wn, wt, gka, gkb, gqa, gqb, fbn, fbt, selk, selq, shift, *rwkv_params)


def _relbias_body(tab_ref, shift_ref, o_ref):
    hh = pl.program_id(0)
    kj = lax.broadcasted_iota(jnp.int32, (ATT_A_WIN, ATT_A_TILE), 0)
    qi = lax.broadcasted_iota(jnp.int32, (ATT_A_WIN, ATT_A_TILE), 1)
    rel = jnp.clip(kj - LEFT_CHUNKS * CHUNK - qi, -MAX_REL, MAX_REL) + MAX_REL

    def body(r, acc):
        return jnp.where(rel == r, tab_ref[hh, r], acc)

    bias = lax.fori_loop(0, 2 * MAX_REL + 1, body, jnp.zeros((ATT_A_WIN, ATT_A_TILE), F32))
    kc = kj // CHUNK
    qc = qi // CHUNK
    band = (kc >= qc) & (kc <= qc + LEFT_CHUNKS)
    o_ref[0] = jnp.where(band, LOG2E * bias - shift_ref[hh], NEG_INF)


def _relbias(tab, shift):
    return pl.pallas_call(
        _relbias_body,
        grid=(N_HEADS_A,),
        in_specs=[pl.BlockSpec(memory_space=pltpu.SMEM), pl.BlockSpec(memory_space=pltpu.SMEM)],
        out_specs=pl.BlockSpec((1, ATT_A_WIN, ATT_A_TILE), lambda i: (i, 0, 0)),
        out_shape=jax.ShapeDtypeStruct((N_HEADS_A, ATT_A_WIN, ATT_A_TILE), F32),
        compiler_params=_cparams(1),
        name="relbias",
    )(tab, shift)


def _attn_a_body(q_ref, k0_ref, k1_ref, k2_ref, v0_ref, v1_ref, v2_ref, bias_ref, o_ref, *, online):
    i = pl.program_id(1)
    tq = ATT_A_TILE
    kj = lax.broadcasted_iota(jnp.int32, (tq, 1), 0)

    k_refs = (k0_ref, k1_ref, k2_ref)
    v_refs = (v0_ref, v1_ref, v2_ref)

    def run(mask_padding):
        def scores(hh):
            q = q_ref[0, hh]
            ss = []
            for d, k_ref in enumerate(k_refs):
                s = _dot_tn(k_ref[0, hh], q) + bias_ref[hh, d * tq:(d + 1) * tq, :]
                if mask_padding:
                    s = jnp.where(kj + (i - 2 + d) * tq >= 0, s, NEG_INF)
                ss.append(s)
            return ss

        def finish(hh, ss):
            if online:
                m = functools.reduce(jnp.maximum, [jnp.max(s, axis=0, keepdims=True) for s in ss])
                ss = [s - m for s in ss]
            acc = jnp.zeros((VB_ROWS, tq), F32)
            for s, v_ref in zip(ss, v_refs):
                acc = acc + _dot(v_ref[0, hh], jnp.exp2(s).astype(BF16))
            o_ref[0, hh] = (acc[0:HEAD_DIM] / acc[HEAD_DIM:HEAD_DIM + 1]).astype(BF16)

        pending = []
        for hh in range(N_HEADS_A):
            pending.append((hh, scores(hh)))
            if len(pending) > ATT_A_AHEAD:
                finish(*pending.pop(0))
        for item in pending:
            finish(*item)

    pl.when(i < 2)(functools.partial(run, True))
    pl.when(i >= 2)(functools.partial(run, False))


def _attn_a(qaT, kA, vaT, bias, online):
    b, nh, _, t = qaT.shape
    tq = ATT_A_TILE
    kspec = lambda d: pl.BlockSpec((1, nh, HEAD_DIM, tq),
                                   lambda bb, i: (bb, 0, 0, jnp.maximum(i - 2 + d, 0)))
    vspec = lambda d: pl.BlockSpec((1, nh, VB_ROWS, tq),
                                   lambda bb, i: (bb, 0, 0, jnp.maximum(i - 2 + d, 0)))
    return pl.pallas_call(
        functools.partial(_attn_a_body, online=online),
        grid=(b, t // tq),
        in_specs=[pl.BlockSpec((1, nh, HEAD_DIM, tq), lambda bb, i: (bb, 0, 0, i)),
                  kspec(0), kspec(1), kspec(2), vspec(0), vspec(1), vspec(2),
                  pl.BlockSpec((nh, ATT_A_WIN, tq), lambda bb, i: (0, 0, 0))],
        out_specs=pl.BlockSpec((1, nh, HEAD_DIM, tq), lambda bb, i: (bb, 0, 0, i)),
        out_shape=jax.ShapeDtypeStruct((b, nh, HEAD_DIM, t), BF16),
        compiler_params=_cparams(2),
        name="attn_a_online" if online else "attn_a",
    )(qaT, kA, kA, kA, vaT, vaT, vaT, bias)


def _fox_body(q_ref, k_ref, v_ref, o_ref, *, online):
    i = pl.program_id(2)
    tq = FOX_TILE
    nh = q_ref.shape[1]
    q = [jnp.concatenate([q_ref[0, hh], jnp.zeros((LANES - QB_ROWS, tq), BF16)], axis=0) for hh in range(nh)]

    def scores(item):
        j, hh = item
        ks = k_ref[0, hh, pl.ds(pl.multiple_of(j * tq, tq), tq), :]
        return _dot(ks, q[hh])

    def accumulate(item, s, m, acc, diagonal):
        j, hh = item
        vs = v_ref[0, hh, j]
        if diagonal:
            kj = lax.broadcasted_iota(jnp.int32, (tq, tq), 0)
            qi = lax.broadcasted_iota(jnp.int32, (tq, tq), 1)
            causal = kj <= qi
        if online:
            if diagonal:
                s = jnp.where(causal, s, NEG_INF)
            m_new = jnp.maximum(m, jnp.max(s, axis=0, keepdims=True))
            acc = jnp.exp2(m - m_new) * acc
            p = jnp.exp2(s - m_new)
            m = m_new
        else:
            p = jnp.exp2(s)
            if diagonal:
                p = jnp.where(causal, p, 0.0)
        return m, acc + _dot(vs, p.astype(BF16))

    def tiles(js, carry, last_diagonal):
        state = [list(c) for c in carry]
        items = [(j, hh) for j in js for hh in range(nh)]
        ss = []

        def finish(n):
            hh = items[n][1]
            diagonal = last_diagonal and n >= len(items) - nh
            state[hh] = list(accumulate(items[n], ss[n], state[hh][0], state[hh][1], diagonal))

        for n, item in enumerate(items):
            ss.append(scores(item))
            if n >= FOX_AHEAD:
                finish(n - FOX_AHEAD)
        for n in range(max(len(items) - FOX_AHEAD, 0), len(items)):
            finish(n)
        return tuple(tuple(c) for c in state)

    init = tuple((jnp.full((1, tq), NEG_INF, F32), jnp.zeros((VB_ROWS, tq), F32)) for _ in range(nh))
    if online:
        carry = lax.fori_loop(0, i, lambda j, c: tiles([j], c, False), init)
        carry = tiles([i], carry, True)
    else:
        g = FOX_GROUP
        carry = lax.fori_loop(0, i // g, lambda jj, c: tiles([g * jj + n for n in range(g)], c, False), init)
        tails = [functools.partial(lambda c, r: tiles([i - r + n for n in range(r + 1)], c, True), r=r)
                 for r in range(g)]
        carry = lax.switch(i % g, tails, carry)
    for hh, (_, acc) in enumerate(carry):
        o_ref[0, hh] = (acc[0:HEAD_DIM] / acc[HEAD_DIM:HEAD_DIM + 1]).astype(BF16)


def _fox(qbT, kB, vbT, online):
    b, nh, _, t = qbT.shape
    tq = FOX_TILE
    nk = t // tq
    return pl.pallas_call(
        functools.partial(_fox_body, online=online),
        grid=(b, nh // FOX_HEADS, nk),
        in_specs=[pl.BlockSpec((1, FOX_HEADS, QB_ROWS, tq), lambda bb, hh, i: (bb, hh, 0, i)),
                  pl.BlockSpec((1, FOX_HEADS, t, LANES), lambda bb, hh, i: (bb, hh, 0, 0)),
                  pl.BlockSpec((1, FOX_HEADS, nk, VB_ROWS, tq), lambda bb, hh, i: (bb, hh, 0, 0, 0))],
        out_specs=pl.BlockSpec((1, FOX_HEADS, HEAD_DIM, tq), lambda bb, hh, i: (bb, hh, 0, i)),
        out_shape=jax.ShapeDtypeStruct((b, nh, HEAD_DIM, t), BF16),
        compiler_params=_cparams(3),
        name="fox_online" if online else "fox",
    )(qbT, kB, vbT)


def _rwkv_prep(proj, tm, t, mu_ref, w0_ref, w2_ref, a0_ref, a2_ref, g2_ref, kk_ref, ka_ref, rk_ref, e_ref,
               rt_ref, at_ref, bt_ref, kt_ref, bh_ref, kh_ref, v_ref, wc_ref, bonus_ref, g_ref, sbuf):
    @pl.when(t == 0)
    def _():
        sbuf[0:8, :] = jnp.zeros((8, COLS_C), F32)

    def shifted(p, start):
        cols = slice(start, start + p.shape[1])
        sbuf[8:8 + tm, cols] = p
        prev = sbuf[7:7 + tm, cols]
        sbuf[0:8, cols] = p[tm - 8:tm, :]
        return p + (prev - p) * mu_ref[:, cols]

    c = WIDTH_C
    p_lo = proj(3 * c, COLS_C - 3 * c)
    p_k = proj(c, c)
    u_lo = shifted(p_lo, 3 * c)
    yield
    w_lo = u_lo[:, 0:DECAY_LORA]
    a_lo = u_lo[:, DECAY_LORA:DECAY_LORA + AAA_LORA]
    g_lo = u_lo[:, DECAY_LORA + AAA_LORA:]
    w = w0_ref[...] + _dot(jnp.tanh(w_lo).astype(BF16), w2_ref[...])
    w = -_softplus(-w) - 0.5
    ld = -jnp.exp(w)
    p_r = proj(0, c)
    yield
    a = _sigmoid(a0_ref[...] + _dot(a_lo.astype(BF16), a2_ref[...]))
    g_ref[0] = _dot(_sigmoid(g_lo).astype(BF16), g2_ref[...]).astype(BF16)
    k = shifted(p_k, c)
    p_v = proj(2 * c, c)
    yield

    e = e_ref[...]
    kk = k * kk_ref[...]
    nrm2 = _dot((kk * kk).astype(BF16), e)
    kkn = kk * lax.rsqrt(jnp.maximum(nrm2, 1e-24))
    k2 = k * (1.0 + (a - 1.0) * ka_ref[...])
    kka = kkn * a
    r = shifted(p_r, 0)
    v = shifted(p_v, 2 * c)
    v_ref[0] = v.astype(BF16)
    bonus_ref[0] = (_dot((r * k2 * rk_ref[...]).astype(BF16), e) * v).astype(BF16)
    yield

    ri = lax.broadcasted_iota(jnp.int32, (2 * CHUNK, CHUNK), 0)
    ci = lax.broadcasted_iota(jnp.int32, (2 * CHUNK, CHUNK), 1)
    tri = ((ci <= ri) | (ri >= CHUNK)).astype(BF16)
    tri2 = jnp.concatenate([tri, tri], axis=1)
    ld_hi, ld_lo = _split2(ld)
    for cc in range(tm // CHUNK):
        sl = slice(cc * CHUNK, (cc + 1) * CHUNK)
        cum = _dot(tri2, jnp.concatenate([ld_hi[sl], ld_lo[sl]], axis=0))
        lc = cum[0:CHUNK]
        tot = cum[CHUNK:2 * CHUNK]
        e_neg = jnp.exp(-lc)
        e_rem = jnp.exp(tot - lc)
        rt_ref[0, sl, :] = (r[sl] * jnp.exp(lc)).astype(BF16)
        at_ref[0, sl, :] = (-kkn[sl] * jnp.exp(lc - ld[sl])).astype(BF16)
        bt_ref[0, sl, :] = (kka[sl] * e_neg).astype(BF16)
        kt_ref[0, sl, :] = (k2[sl] * e_neg).astype(BF16)
        bh_ref[0, sl, :] = (kka[sl] * e_rem).astype(BF16)
        kh_ref[0, sl, :] = (k2[sl] * e_rem).astype(BF16)
        if cc % 2 == 1:
            yield
    seg = (lax.broadcasted_iota(jnp.int32, (16, tm), 1) // CHUNK
           == lax.broadcasted_iota(jnp.int32, (16, tm), 0)).astype(BF16)
    tots3 = jnp.concatenate(_split3(_dot(seg, ld_hi) + _dot(seg, ld_lo)), axis=0)
    per_blk = RWKV_TILE // CHUNK
    for blk in range(tm // RWKV_TILE):
        put = (lax.broadcasted_iota(jnp.int32, (48, LANES), 0) % 16
               == lax.broadcasted_iota(jnp.int32, (48, LANES), 1) + blk * per_blk).astype(BF16)
        wc_ref[0, blk] = jnp.exp(_dot_tn(tots3, put))


def _bmm(a, b):
    return lax.dot_general(a, b, (((2,), (1,)), ((0,), (0,))), preferred_element_type=F32)


def _bmm_nt(a, b):
    return lax.dot_general(a, b, (((2,), (2,)), ((0,), (0,))), preferred_element_type=F32)


def _bmm_tn(a, b):
    return lax.dot_general(a, b, (((1,), (1,)), ((0,), (0,))), preferred_element_type=F32)


def _rwkv_chunk_body(rt_ref, at_ref, bt_ref, kt_ref, bh_ref, kh_ref, v_ref, wc_ref, bonus_ref, g_ref,
                     lg_ref, lb_ref, e_ref, o_ref, h_ref):
    rows, tm = rt_ref.shape[0], rt_ref.shape[1]
    nc = tm // CHUNK
    ng = N_HEADS_C // RWKV_GROUP
    gl = RWKV_GROUP * HEAD_DIM
    units = [(r, g) for r in range(rows) for g in range(ng)]
    nu = len(units)
    t = pl.program_id(1)

    @pl.when(t == 0)
    def _():
        h_ref[...] = jnp.zeros_like(h_ref)

    lane_head = lax.broadcasted_iota(jnp.int32, (1, CHUNK, gl), 2) // HEAD_DIM
    ri = lax.broadcasted_iota(jnp.int32, (1, 2 * CHUNK, gl), 1)
    ci = lax.broadcasted_iota(jnp.int32, (1, 2 * CHUNK, gl), 2) & (CHUNK - 1)
    rr = ri & (CHUNK - 1)
    keep = (rr > ci) | ((ri >= CHUNK) & (rr == ci))
    eye = (lax.broadcasted_iota(jnp.int32, (1, CHUNK, gl), 1)
           == (lax.broadcasted_iota(jnp.int32, (1, CHUNK, gl), 2) & (CHUNK - 1)))
    same_head = (lax.broadcasted_iota(jnp.int32, (1, gl, gl), 1) // HEAD_DIM
                 == lax.broadcasted_iota(jnp.int32, (1, gl, gl), 2) // HEAD_DIM)

    def bdiag(x):
        return jnp.concatenate([jnp.where(lane_head == hh, x, jnp.zeros_like(x)) for hh in range(RWKV_GROUP)],
                               axis=1)

    def chunk_local(cs, out):
        tile = lambda ref: jnp.stack([ref[r, c * CHUNK:(c + 1) * CHUNK, g * gl:(g + 1) * gl]
                                      for c in cs for r, g in units])
        rt, at, bt, kt, bh, kh, v = (tile(r) for r in (rt_ref, at_ref, bt_ref, kt_ref, bh_ref, kh_ref, v_ref))
        ar = jnp.concatenate([at, rt], axis=1)
        sb = jnp.where(keep, _bmm_nt(ar, bdiag(bt)), 0.0)
        sk = jnp.where(keep, _bmm_nt(ar, bdiag(kt)), 0.0).astype(BF16)
        m_rb = sb[:, CHUNK:].astype(BF16)
        yield
        a = sb[:, :CHUNK]
        tinv = jnp.where(eye, 1.0, 0.0) + a
        ab = a.astype(BF16)
        x = _bmm(ab, bdiag(ab))
        yield
        for _ in range(4):
            xb = x.astype(BF16)
            r = _bmm(jnp.concatenate([tinv.astype(BF16), xb], axis=1), bdiag(xb))
            tinv = tinv + r[:, :CHUNK]
            x = r[:, CHUNK:]
            yield
        tinv = (tinv + _bmm(tinv.astype(BF16), bdiag(x.astype(BF16)))).astype(BF16)
        bdv = bdiag(v)
        av = _bmm(sk[:, :CHUNK], bdv)
        yield
        p = _bmm(tinv, bdiag(at)).astype(BF16)
        u0 = _bmm(tinv, bdiag(av.astype(BF16)))
        yield
        q = (rt.astype(F32) + _bmm(m_rb, bdiag(p))).astype(BF16)
        y0 = _bmm(m_rb, bdiag(u0.astype(BF16))) + _bmm(sk[:, CHUNK:], bdv)
        parts = (jnp.concatenate([q, p], axis=1), jnp.concatenate([y0, u0], axis=1),
                 jnp.concatenate([bh, kh], axis=1), v)
        out += [tuple(z[n * nu:(n + 1) * nu] for z in parts) for n in range(len(cs))]
        yield

    wct = [wc_ref[r, 0] for r in range(rows)]
    state = [h_ref[...]]
    ys = []
    local = []

    def recur(c):
        qp, yu0, bk, v = local[c]
        yu = _bmm(qp, state[0].astype(BF16)) + yu0
        ys.append(yu[:, :CHUNK])
        upd = _bmm_tn(bk, jnp.concatenate([yu[:, CHUNK:].astype(BF16), v], axis=1))
        wcol = jnp.stack([wct[r][g * gl:(g + 1) * gl, c:c + 1] for r, g in units])
        state[0] = jnp.where(same_head, upd, 0.0) + wcol * state[0]

    assert sorted(c for cs in RWKV_SCHEDULE for c in cs) == list(range(nc))
    done = 0
    for cs in RWKV_SCHEDULE:
        pending = list(range(done, len(local)))
        slots = {(k + 1) * RWKV_STAGES // (len(pending) + 1): c for k, c in enumerate(pending)}
        for n, _ in enumerate(chunk_local(list(cs), local), start=1):
            if n in slots:
                recur(slots[n])
                done += 1
    for c in range(done, nc):
        recur(c)
    h_ref[...] = state[0]

    y = jnp.concatenate([jnp.concatenate([yc[r * ng + g] for g in range(ng)], axis=-1)
                         for r in range(rows) for yc in ys], axis=0)
    e = e_ref[...]

    def head_mean(z):
        return _dot(z.astype(BF16), e) * (1.0 / HEAD_DIM)

    d = y - head_mean(y)
    yn = d * lax.rsqrt(head_mean(d * d) + LNX_EPS)
    flat = lambda ref: ref[...].astype(F32).reshape(rows * tm, WIDTH_C)
    out = (yn * lg_ref[...] + lb_ref[...] + flat(bonus_ref)) * flat(g_ref)
    o_ref[...] = out.astype(BF16).reshape(rows, tm, WIDTH_C)


def _rwkv_chunk(rt, at, bt, kt, bh, kh, v, wc, bonus, g, lnx_g, lnx_b, e):
    b, t, _ = rt.shape
    tm = RWKV_TILE
    rows = math.gcd(RWKV_ROWS, b)
    gl = RWKV_GROUP * HEAD_DIM
    const = lambda shape: pl.BlockSpec(shape, lambda i, j: (0,) * len(shape))
    tok = lambda: pl.BlockSpec((rows, tm, WIDTH_C), lambda i, j: (i, j, 0))
    return pl.pallas_call(
        _rwkv_chunk_body,
        grid=(b // rows, t // tm),
        in_specs=[tok(), tok(), tok(), tok(), tok(), tok(), tok(),
                  pl.BlockSpec((rows, 1, WIDTH_C, LANES), lambda i, j: (i, j, 0, 0)),
                  tok(), tok(), const(lnx_g.shape), const(lnx_b.shape), const(e.shape)],
        out_specs=tok(),
        out_shape=jax.ShapeDtypeStruct((b, t, WIDTH_C), BF16),
        scratch_shapes=[pltpu.VMEM((rows * (N_HEADS_C // RWKV_GROUP), gl, gl), F32)],
        compiler_params=_cparams(2),
        name="rwkv_chunk",
    )(rt, at, bt, kt, bh, kh, v, wc, bonus, g, lnx_g, lnx_b, e)


def _outproj_body(x_ref, ya_ref, yb_ref, yc_ref, wa_ref, wb_ref, wc_ref, o_ref):
    acc = _dot_tn(ya_ref[0], wa_ref[...])
    acc = acc + _dot_tn(yb_ref[0], wb_ref[...])
    acc = acc + _dot(yc_ref[0], wc_ref[...])
    o_ref[0] = x_ref[0] + acc


def _outproj(x, yaT, ybT, yc, wa, wb, wc):
    b, t, d = x.shape
    tm = TOK_TILE
    const = lambda shape: pl.BlockSpec(shape, lambda i, j: (0,) * len(shape))
    return pl.pallas_call(
        _outproj_body,
        grid=(b, t // tm),
        in_specs=[pl.BlockSpec((1, tm, d), lambda i, j: (i, j, 0)),
                  pl.BlockSpec((1, WIDTH_A, tm), lambda i, j: (i, 0, j)),
                  pl.BlockSpec((1, WIDTH_B, tm), lambda i, j: (i, 0, j)),
                  pl.BlockSpec((1, tm, WIDTH_C), lambda i, j: (i, j, 0)),
                  const(wa.shape), const(wb.shape), const(wc.shape)],
        out_specs=pl.BlockSpec((1, tm, d), lambda i, j: (i, j, 0)),
        out_shape=jax.ShapeDtypeStruct((b, t, d), F32),
        compiler_params=_cparams(2),
        name="outproj",
    )(x, yaT, ybT, yc, wa, wb, wc)


def _ffn_body(x_ref, g_ref, wg_ref, wv_ref, cw_ref, cb_ref, wd_ref, o_ref, gbuf):
    tm = x_ref.shape[1]
    t = pl.program_id(1)

    @pl.when(t == 0)
    def _():
        gbuf[0:8, :] = jnp.zeros((8, D_FF), F32)

    x = x_ref[0]
    ms = jnp.mean(x * x, axis=-1, keepdims=True)
    h = (x * lax.rsqrt(ms + RMS_EPS) * g_ref[...]).astype(BF16)
    acc = x
    for lo, hi in zip(FF_SPLITS[:-1], FF_SPLITS[1:]):
        gate = _dot(h, wg_ref[:, lo:hi])
        val = _dot(h, wv_ref[:, lo:hi])
        gbuf[8:8 + tm, lo:hi] = gate
        g1 = gbuf[7:7 + tm, lo:hi]
        g2 = gbuf[6:6 + tm, lo:hi]
        gbuf[0:8, lo:hi] = gate[tm - 8:tm, :]
        conv = cb_ref[:, lo:hi] + g2 * cw_ref[0:1, lo:hi] + g1 * cw_ref[1:2, lo:hi] + gate * cw_ref[2:3, lo:hi]
        act = (conv * _sigmoid(conv) * val).astype(BF16)
        acc = acc + _dot(act, wd_ref[lo:hi, :])
    o_ref[0] = acc


def _ffn(x, g, wg, wv, cw, cb, wd):
    b, t, d = x.shape
    tm = TOK_TILE
    const = lambda shape: pl.BlockSpec(shape, lambda i, j: (0,) * len(shape),
                                       pipeline_mode=pl.Buffered(1))
    return pl.pallas_call(
        _ffn_body,
        grid=(b, t // tm),
        in_specs=[pl.BlockSpec((1, tm, d), lambda i, j: (i, j, 0)),
                  const(g.shape), const(wg.shape), const(wv.shape), const(cw.shape), const(cb.shape),
                  const(wd.shape)],
        out_specs=pl.BlockSpec((1, tm, d), lambda i, j: (i, j, 0)),
        out_shape=jax.ShapeDtypeStruct((b, t, d), F32),
        scratch_shapes=[pltpu.VMEM((tm + 8, D_FF), F32)],
        compiler_params=_cparams(2),
        name="ffn",
    )(x, g, wg, wv, cw, cb, wd)


def _pad_heads_cols(w, nh):
    d = w.shape[0]
    w = w.reshape(d, nh, HEAD_DIM)
    return jnp.pad(w, ((0, 0), (0, 0), (0, LANES - HEAD_DIM))).reshape(d, nh * LANES)


def _qk_bound(gq, gk):
    scale = HEAD_DIM ** -0.5
    return 1.01 * HEAD_DIM * scale * jnp.max(jnp.abs(gq)) * jnp.max(jnp.abs(gk)) + 0.05


def _layer_params(l, w_in, q_norm_a, k_norm_a, rel_bias, q_norm_b, k_norm_b, forget_bias):
    w = w_in[l]
    scale = HEAD_DIM ** -0.5
    a0, b0 = 0, 3 * WIDTH_A
    c0 = b0 + 3 * WIDTH_B + N_HEADS_B
    qa, ka, va = (w[:, a0 + i * WIDTH_A:a0 + (i + 1) * WIDTH_A] for i in range(3))
    qb, kb, vb = (w[:, b0 + i * WIDTH_B:b0 + (i + 1) * WIDTH_B] for i in range(3))
    fg = w[:, b0 + 3 * WIDTH_B:c0]
    wc = w[:, c0:]
    wn = jnp.concatenate([_pad_heads_cols(kb, N_HEADS_B),
                          jnp.pad(fg, ((0, 0), (0, LANES - N_HEADS_B))), wc], axis=1).astype(BF16)
    pad_v = lambda v, nh: jnp.pad(v.reshape(-1, nh, HEAD_DIM),
                                  ((0, 0), (0, 0), (0, VB_ROWS - HEAD_DIM))).reshape(-1, nh * VB_ROWS)
    assert QB_ROWS == VB_ROWS
    wt = jnp.concatenate([qa, ka, pad_v(va, N_HEADS_A), pad_v(qb, N_HEADS_B), pad_v(vb, N_HEADS_B),
                          jnp.pad(fg, ((0, 0), (0, 16 - N_HEADS_B)))], axis=1).T.astype(BF16)
    pad = LANES - HEAD_DIM
    gka = k_norm_a[l].reshape(HEAD_DIM, 1)
    gkb = jnp.pad(k_norm_b[l], (0, pad)).reshape(1, LANES)
    gqa = (q_norm_a[l] * (scale * LOG2E)).reshape(HEAD_DIM, 1)
    gqb = jnp.pad(q_norm_b[l] * (scale * LOG2E), (0, QB_ROWS - HEAD_DIM)).reshape(QB_ROWS, 1)
    fbn = jnp.pad(forget_bias[l], (0, LANES - N_HEADS_B)).reshape(1, LANES)
    fbt = jnp.pad(forget_bias[l], (0, 16 - N_HEADS_B)).reshape(16, 1)
    shift_b = (LOG2E * _qk_bound(q_norm_b[l], k_norm_b[l])).reshape(1, 1)
    bound_a = _qk_bound(q_norm_a[l], k_norm_a[l])
    tab = rel_bias[l]
    shift_a = LOG2E * (bound_a + jnp.max(tab, axis=1))
    depth_a = jnp.max(shift_a + LOG2E * (bound_a - tab[:, MAX_REL]))
    return wn, wt, gka, gkb, gqa, gqb, fbn, fbt, shift_b, shift_a, depth_a


def _selectors():
    import numpy as np
    selk = np.zeros((LANES, N_HEADS_B * LANES), np.float32)
    selq = np.zeros((N_HEADS_B * QB_ROWS, 3 * 16), np.float32)
    for p in range(3):
        for hh in range(N_HEADS_B):
            selk[p * N_HEADS_B + hh, hh * LANES + HEAD_DIM + 3 + p] = -1.0
            selq[hh * QB_ROWS + HEAD_DIM + p, 16 * p + hh] = 1.0
    e = np.kron(np.eye(N_HEADS_C, dtype=np.float32), np.ones((HEAD_DIM, HEAD_DIM), np.float32))
    return jnp.asarray(selk, BF16), jnp.asarray(selq, BF16), jnp.asarray(e, BF16)


def kernel(x, mix_norm_g, w_in, q_norm_a, k_norm_a, rel_bias, q_norm_b, k_norm_b, forget_bias, shift_mu, w0, w2,
           a0, a2, g2, k_k, k_a, r_k, lnx_g, lnx_b, w_out, ffn_norm_g, w_up, conv_w, conv_b, w_down):
    depth = w_in.shape[0]
    b, t, d = x.shape
    selk, selq, e = _selectors()
    row = lambda v: v.reshape(1, -1)
    for l in range(depth):
        wn, wt, gka, gkb, gqa, gqb, fbn, fbt, shift_b, shift_a, depth_a = _layer_params(
            l, w_in, q_norm_a, k_norm_a, rel_bias, q_norm_b, k_norm_b, forget_bias)
        rwkv_params = (row(shift_mu[l]), row(w0[l]), w2[l].astype(BF16), row(a0[l]), a2[l].astype(BF16),
                       g2[l].astype(BF16), row(k_k[l]), row(k_a[l]), row(r_k[l]), e)
        (qaT, vaT, kA, qbT, vbT, kB, rt, at, bt, kt, bh, kh, v, wcum, bonus, g) = _inproj(
            x, row(mix_norm_g[l]), wn, wt, gka, gkb, gqa, gqb, fbn, fbt, selk, selq, shift_b, rwkv_params)
        bias = _relbias(rel_bias[l], shift_a)
        yaT = lax.cond(depth_a <= 2 * FAST_MAX_SHIFT,
                       functools.partial(_attn_a, online=False), functools.partial(_attn_a, online=True),
                       qaT, kA, vaT, bias)
        ybT = lax.cond(shift_b[0, 0] <= FAST_MAX_SHIFT,
                       functools.partial(_fox, online=False), functools.partial(_fox, online=True),
                       qbT, kB, vbT)
        yc = _rwkv_chunk(rt, at, bt, kt, bh, kh, v, wcum, bonus, g, row(lnx_g[l]), row(lnx_b[l]), e)
        wo = w_out[l].astype(BF16)
        x = _outproj(x, yaT.reshape(b, WIDTH_A, t), ybT.reshape(b, WIDTH_B, t), yc,
                     wo[:WIDTH_A], wo[WIDTH_A:WIDTH_A + WIDTH_B], wo[WIDTH_A + WIDTH_B:])
        wu = w_up[l].astype(BF16)
        x = _ffn(x, row(ffn_norm_g[l]), wu[:, :D_FF], wu[:, D_FF:], conv_w[l], row(conv_b[l]),
                 w_down[l].astype(BF16))
    return x
```

```python
import functools
import math

import jax
import jax.numpy as jnp
from jax import lax
from jax.experimental import pallas as pl
from jax.experimental.pallas import tpu as pltpu

F32 = jnp.float32
BF16 = jnp.bfloat16

D_MODEL = 1024
HEAD_DIM = 64
CHUNK = 64
LEFT_CHUNKS = 8
MAX_REL = 128
N_HEADS_A = 4
N_HEADS_B = 4
N_HEADS_C = 8
WIDTH_A = N_HEADS_A * HEAD_DIM
WIDTH_B = N_HEADS_B * HEAD_DIM
WIDTH_C = N_HEADS_C * HEAD_DIM
DECAY_LORA = 64
AAA_LORA = 64
GATE_LORA = 128
COLS_C = 3 * WIDTH_C + DECAY_LORA + AAA_LORA + GATE_LORA
D_FF = 2816
RMS_EPS = 1e-6
LNX_EPS = 64e-5
NEG_INF = -1e30

LANES = 128
TOK_TILE = 512
ATT_A_TILE = 256
ATT_A_WIN = 3 * ATT_A_TILE
ATT_A_AHEAD = 3
FOX_TILE = 512
FOX_HEADS = 2
FOX_GROUP = 8
FOX_AHEAD = 2
RWKV_GROUP = 4
RWKV_TILE = 256
RWKV_ROWS = 4
RWKV_SCHEDULE = ((0, 1), (2, 3))
RWKV_STAGES = 9
FF_SPLITS = (0, 1024, 2816)
VMEM_LIMIT = 56 * 1024 * 1024

PN_KB, PN_F, PN_C = 0, 512, 640
PN_COLS = PN_C + COLS_C
VB_ROWS = 80
PT_QA = 0
PT_KA = PT_QA + WIDTH_A
PT_VA = PT_KA + WIDTH_A
PT_QB = PT_VA + N_HEADS_A * VB_ROWS
QB_ROWS = 80
PT_VB = PT_QB + N_HEADS_B * QB_ROWS
PT_F = PT_VB + N_HEADS_B * VB_ROWS
PT_ROWS = PT_F + 16
LOG2E = 1.4426950408889634
FAST_MAX_SHIFT = 40.0


def _cparams(n_axes):
    return pltpu.CompilerParams(dimension_semantics=("arbitrary",) * n_axes,
                                vmem_limit_bytes=VMEM_LIMIT)


def _split3(x):
    hi = x.astype(BF16)
    r1 = x - hi.astype(F32)
    mid = r1.astype(BF16)
    lo = (r1 - mid.astype(F32)).astype(BF16)
    return hi, mid, lo


def _split2(x):
    hi = x.astype(BF16)
    lo = (x - hi.astype(F32)).astype(BF16)
    return hi, lo


def _dot(a, b):
    return jnp.dot(a, b, preferred_element_type=F32)


def _dot_nt(a, b):
    return lax.dot_general(a, b, (((1,), (1,)), ((), ())), preferred_element_type=F32)


def _dot_tn(a, b):
    return lax.dot_general(a, b, (((0,), (0,)), ((), ())), preferred_element_type=F32)


def _log_sigmoid(x):
    return jnp.minimum(x, 0.0) - jnp.log(1.0 + jnp.exp(-jnp.abs(x)))


def _softplus(x):
    return jnp.maximum(x, 0.0) + jnp.log(1.0 + jnp.exp(-jnp.abs(x)))


def _sigmoid(x):
    return 1.0 / (1.0 + jnp.exp(-x))


def _inproj_body(x_ref, g_ref, wn_ref, wt_ref, gka_ref, gkb_ref, gqa_ref, gqb_ref, fbn_ref, fbt_ref,
                 selk_ref, selq_ref, shift_ref, *rest):
    rwkv_params, rest = rest[:10], rest[10:]
    qaT_ref, vaT_ref, kA_ref, qbT_ref, vbT_ref, kB_ref = rest[:6]
    rwkv_outs = rest[6:16]
    carry_n, carry_t, sbuf = rest[16:]
    tm = x_ref.shape[1]
    t = pl.program_id(1)

    @pl.when(t == 0)
    def _():
        carry_n[...] = jnp.zeros_like(carry_n)
        carry_t[...] = jnp.zeros_like(carry_t)

    x = x_ref[0]
    ms = jnp.mean(x * x, axis=-1, keepdims=True)
    h = (x * lax.rsqrt(ms + RMS_EPS) * g_ref[...]).astype(BF16)
    rows_t = lambda start, n: _dot_nt(wt_ref[start:start + n, :], h)
    cols_n = lambda start, n: _dot(h, wn_ref[:, start:start + n])
    row_v = lax.broadcasted_iota(jnp.int32, (VB_ROWS, 1), 0)
    ones_v = jnp.where(row_v == HEAD_DIM, 1.0, 0.0)
    val = {}

    def norm_rows(p, gain_ref):
        p = p.reshape(N_HEADS_A, HEAD_DIM, tm)
        return (p * lax.rsqrt(jnp.mean(p * p, axis=1, keepdims=True) + RMS_EPS) * gain_ref[...][None]).astype(BF16)

    def lane_packed3(z):
        z_hi, z_mid, z_lo = _split3(z)
        return (z_hi.astype(F32) + pltpu.roll(z_mid.astype(F32), N_HEADS_B, axis=1)
                + pltpu.roll(z_lo.astype(F32), 2 * N_HEADS_B, axis=1)).astype(BF16)

    def gates_epilogue(p):
        pn_f, pt_f = p
        lane_n = lax.broadcasted_iota(jnp.int32, (1, LANES), 1)
        lf_n = jnp.where(lane_n < N_HEADS_B, LOG2E * _log_sigmoid(pn_f + fbn_ref[...]), 0.0)
        row_t = lax.broadcasted_iota(jnp.int32, (16, 1), 0)
        lf_t = jnp.where(row_t < N_HEADS_B, LOG2E * _log_sigmoid(pt_f + fbt_ref[...]), 0.0)
        ri = lax.broadcasted_iota(jnp.int32, (tm, tm), 0)
        ci = lax.broadcasted_iota(jnp.int32, (tm, tm), 1)
        low = (ci <= ri).astype(BF16)
        upp = (ri <= ci).astype(BF16)
        cum3 = _dot(low, lane_packed3(lf_n))
        cum = cum3 + pltpu.roll(cum3, LANES - N_HEADS_B, axis=1) + pltpu.roll(cum3, LANES - 2 * N_HEADS_B, axis=1)
        val["c_n"] = jnp.where(lane_n < N_HEADS_B, cum, 0.0) + carry_n[0:1, :]
        ht, mt, lt = _split3(lf_t)
        val["c_t"] = _dot(ht, upp) + _dot(mt, upp) + _dot(lt, upp) + carry_t[:, 0:1]
        carry_n[...] = carry_n[...] + jnp.sum(lf_n, axis=0, keepdims=True)
        carry_t[...] = carry_t[...] + jnp.sum(lf_t, axis=1, keepdims=True)

    def kb_epilogue(pn_kb):
        kaug = _dot(lane_packed3(val["c_n"]), selk_ref[...])
        lane_k = lax.broadcasted_iota(jnp.int32, (1, LANES), 1)
        ones_k = jnp.where((lane_k >= HEAD_DIM) & (lane_k < HEAD_DIM + 3), 1.0, 0.0)
        for hh in range(N_HEADS_B):
            k = pn_kb[:, LANES * hh:LANES * (hh + 1)]
            msk = jnp.sum(k * k, axis=-1, keepdims=True) * (1.0 / HEAD_DIM)
            kn = k * lax.rsqrt(msk + RMS_EPS) * gkb_ref[...]
            kB_ref[0, hh] = (kn + kaug[:, LANES * hh:LANES * (hh + 1)] + ones_k).astype(BF16)

    def qb_epilogue(pt_qb):
        c3 = jnp.concatenate(_split3(val["c_t"] - shift_ref[...]), axis=0)
        qaug = _dot(selq_ref[...], c3)
        qb = pt_qb.reshape(N_HEADS_B, QB_ROWS, tm)
        msq = jnp.sum(qb * qb, axis=1, keepdims=True) * (1.0 / HEAD_DIM)
        row_q = lax.broadcasted_iota(jnp.int32, (QB_ROWS, 1), 0)
        ones_q = jnp.where((row_q >= HEAD_DIM + 3) & (row_q < HEAD_DIM + 6), 1.0, 0.0)
        qn = qb * lax.rsqrt(msq + RMS_EPS) * gqb_ref[...][None]
        qbT_ref[0] = (qn + qaug.reshape(N_HEADS_B, QB_ROWS, tm) + ones_q[None]).astype(BF16)

    def store(ref, value):
        ref[...] = value

    sections = [
        (lambda: rows_t(PT_QA, WIDTH_A), lambda p: store(qaT_ref, norm_rows(p, gqa_ref)[None])),
        (lambda: rows_t(PT_KA, WIDTH_A), lambda p: store(kA_ref, norm_rows(p, gka_ref)[None])),
        (lambda: rows_t(PT_VA, N_HEADS_A * VB_ROWS),
         lambda p: store(vaT_ref, (p.reshape(N_HEADS_A, VB_ROWS, tm) + ones_v[None]).astype(BF16)[None])),
        (lambda: (cols_n(PN_F, LANES), rows_t(PT_F, 16)), gates_epilogue),
        (lambda: rows_t(PT_VB, N_HEADS_B * VB_ROWS),
         lambda p: store(vbT_ref, (p.reshape(N_HEADS_B, VB_ROWS, tm) + ones_v[None]).astype(BF16)[None, :, None])),
        (lambda: cols_n(PN_KB, N_HEADS_B * LANES), kb_epilogue),
        (lambda: rows_t(PT_QB, N_HEADS_B * QB_ROWS), qb_epilogue),
    ]
    slots = _rwkv_prep(lambda start, n: cols_n(PN_C + start, n), tm, t, *rwkv_params, *rwkv_outs, sbuf)
    pending = None
    for mm, epilogue in sections:
        p = mm()
        if pending is not None:
            pending[1](pending[0])
        pending = (p, epilogue)
        next(slots, None)
    pending[1](pending[0])
    for _ in slots:
        pass


def _inproj(x, g, wn, wt, gka, gkb, gqa, gqb, fbn, fbt, selk, selq, shift, rwkv_params):
    b, t, d = x.shape
    tm = TOK_TILE
    nt = t // tm
    const = lambda shape: pl.BlockSpec(shape, lambda i, j: (0,) * len(shape))
    tok = lambda: pl.BlockSpec((1, tm, WIDTH_C), lambda i, j: (i, j, 0))
    bf = jax.ShapeDtypeStruct((b, t, WIDTH_C), BF16)
    out_shape = (
        jax.ShapeDtypeStruct((b, N_HEADS_A, HEAD_DIM, t), BF16),
        jax.ShapeDtypeStruct((b, N_HEADS_A, VB_ROWS, t), BF16),
        jax.ShapeDtypeStruct((b, N_HEADS_A, HEAD_DIM, t), BF16),
        jax.ShapeDtypeStruct((b, N_HEADS_B, QB_ROWS, t), BF16),
        jax.ShapeDtypeStruct((b, N_HEADS_B, nt, VB_ROWS, tm), BF16),
        jax.ShapeDtypeStruct((b, N_HEADS_B, t, LANES), BF16),
        bf, bf, bf, bf, bf, bf, bf,
        jax.ShapeDtypeStruct((b, t // RWKV_TILE, WIDTH_C, LANES), F32),
        bf, bf,
    )
    out_specs = (
        pl.BlockSpec((1, N_HEADS_A, HEAD_DIM, tm), lambda i, j: (i, 0, 0, j)),
        pl.BlockSpec((1, N_HEADS_A, VB_ROWS, tm), lambda i, j: (i, 0, 0, j)),
        pl.BlockSpec((1, N_HEADS_A, HEAD_DIM, tm), lambda i, j: (i, 0, 0, j)),
        pl.BlockSpec((1, N_HEADS_B, QB_ROWS, tm), lambda i, j: (i, 0, 0, j)),
        pl.BlockSpec((1, N_HEADS_B, 1, VB_ROWS, tm), lambda i, j: (i, 0, j, 0, 0)),
        pl.BlockSpec((1, N_HEADS_B, tm, LANES), lambda i, j: (i, 0, j, 0)),
        tok(), tok(), tok(), tok(), tok(), tok(), tok(),
        pl.BlockSpec((1, tm // RWKV_TILE, WIDTH_C, LANES), lambda i, j: (i, j, 0, 0)),
        tok(), tok(),
    )
    in_specs = [
        pl.BlockSpec((1, tm, d), lambda i, j: (i, j, 0)),
        const((1, d)), const(wn.shape), const(wt.shape),
        const(gka.shape), const(gkb.shape), const(gqa.shape), const(gqb.shape),
        const(fbn.shape), const(fbt.shape), const(selk.shape), const(selq.shape), const(shift.shape),
    ] + [const(p.shape) for p in rwkv_params]
    return pl.pallas_call(
        _inproj_body,
        grid=(b, nt),
        in_specs=in_specs,
        out_specs=out_specs,
        out_shape=out_shape,
        scratch_shapes=[pltpu.VMEM((8, LANES), F32), pltpu.VMEM((16, LANES), F32),
                        pltpu.VMEM((tm + 8, COLS_C), F32)],
        compiler_params=_cparams(2),
        name="inproj",
    )(x, g, wn, wt, gka, gkb, gqa, gqb, fbn, fbt, selk, selq, shift, *rwkv_params)


def _relbias_body(tab_ref, shift_ref, o_ref):
    hh = pl.program_id(0)
    kj = lax.broadcasted_iota(jnp.int32, (ATT_A_WIN, ATT_A_TILE), 0)
    qi = lax.broadcasted_iota(jnp.int32, (ATT_A_WIN, ATT_A_TILE), 1)
    rel = jnp.clip(kj - LEFT_CHUNKS * CHUNK - qi, -MAX_REL, MAX_REL) + MAX_REL

    def body(r, acc):
        return jnp.where(rel == r, tab_ref[hh, r], acc)

    bias = lax.fori_loop(0, 2 * MAX_REL + 1, body, jnp.zeros((ATT_A_WIN, ATT_A_TILE), F32))
    kc = kj // CHUNK
    qc = qi // CHUNK
    band = (kc >= qc) & (kc <= qc + LEFT_CHUNKS)
    o_ref[0] = jnp.where(band, LOG2E * bias - shift_ref[hh], NEG_INF)


def _relbias(tab, shift):
    return pl.pallas_call(
        _relbias_body,
        grid=(N_HEADS_A,),
        in_specs=[pl.BlockSpec(memory_space=pltpu.SMEM), pl.BlockSpec(memory_space=pltpu.SMEM)],
        out_specs=pl.BlockSpec((1, ATT_A_WIN, ATT_A_TILE), lambda i: (i, 0, 0)),
        out_shape=jax.ShapeDtypeStruct((N_HEADS_A, ATT_A_WIN, ATT_A_TILE), F32),
        compiler_params=_cparams(1),
        name="relbias",
    )(tab, shift)


def _attn_a_body(q_ref, k0_ref, k1_ref, k2_ref, v0_ref, v1_ref, v2_ref, bias_ref, o_ref, *, online):
    i = pl.program_id(1)
    tq = ATT_A_TILE
    kj = lax.broadcasted_iota(jnp.int32, (tq, 1), 0)

    k_refs = (k0_ref, k1_ref, k2_ref)
    v_refs = (v0_ref, v1_ref, v2_ref)

    def run(mask_padding):
        def scores(hh):
            q = q_ref[0, hh]
            ss = []
            for d, k_ref in enumerate(k_refs):
                s = _dot_tn(k_ref[0, hh], q) + bias_ref[hh, d * tq:(d + 1) * tq, :]
                if mask_padding:
                    s = jnp.where(kj + (i - 2 + d) * tq >= 0, s, NEG_INF)
                ss.append(s)
            return ss

        def finish(hh, ss):
            if online:
                m = functools.reduce(jnp.maximum, [jnp.max(s, axis=0, keepdims=True) for s in ss])
                ss = [s - m for s in ss]
            acc = jnp.zeros((VB_ROWS, tq), F32)
            for s, v_ref in zip(ss, v_refs):
                acc = acc + _dot(v_ref[0, hh], jnp.exp2(s).astype(BF16))
            o_ref[0, hh] = (acc[0:HEAD_DIM] / acc[HEAD_DIM:HEAD_DIM + 1]).astype(BF16)

        pending = []
        for hh in range(N_HEADS_A):
            pending.append((hh, scores(hh)))
            if len(pending) > ATT_A_AHEAD:
                finish(*pending.pop(0))
        for item in pending:
            finish(*item)

    pl.when(i < 2)(functools.partial(run, True))
    pl.when(i >= 2)(functools.partial(run, False))


def _attn_a(qaT, kA, vaT, bias, online):
    b, nh, _, t = qaT.shape
    tq = ATT_A_TILE
    kspec = lambda d: pl.BlockSpec((1, nh, HEAD_DIM, tq),
                                   lambda bb, i: (bb, 0, 0, jnp.maximum(i - 2 + d, 0)))
    vspec = lambda d: pl.BlockSpec((1, nh, VB_ROWS, tq),
                                   lambda bb, i: (bb, 0, 0, jnp.maximum(i - 2 + d, 0)))
    return pl.pallas_call(
        functools.partial(_attn_a_body, online=online),
        grid=(b, t // tq),
        in_specs=[pl.BlockSpec((1, nh, HEAD_DIM, tq), lambda bb, i: (bb, 0, 0, i)),
                  kspec(0), kspec(1), kspec(2), vspec(0), vspec(1), vspec(2),
                  pl.BlockSpec((nh, ATT_A_WIN, tq), lambda bb, i: (0, 0, 0))],
        out_specs=pl.BlockSpec((1, nh, HEAD_DIM, tq), lambda bb, i: (bb, 0, 0, i)),
        out_shape=jax.ShapeDtypeStruct((b, nh, HEAD_DIM, t), BF16),
        compiler_params=_cparams(2),
        name="attn_a_online" if online else "attn_a",
    )(qaT, kA, kA, kA, vaT, vaT, vaT, bias)


def _fox_body(q_ref, k_ref, v_ref, o_ref, *, online):
    i = pl.program_id(2)
    tq = FOX_TILE
    nh = q_ref.shape[1]
    q = [jnp.concatenate([q_ref[0, hh], jnp.zeros((LANES - QB_ROWS, tq), BF16)], axis=0) for hh in range(nh)]

    def scores(item):
        j, hh = item
        ks = k_ref[0, hh, pl.ds(pl.multiple_of(j * tq, tq), tq), :]
        return _dot(ks, q[hh])

    def accumulate(item, s, m, acc, diagonal):
        j, hh = item
        vs = v_ref[0, hh, j]
        if diagonal:
            kj = lax.broadcasted_iota(jnp.int32, (tq, tq), 0)
            qi = lax.broadcasted_iota(jnp.int32, (tq, tq), 1)
            causal = kj <= qi
        if online:
            if diagonal:
                s = jnp.where(causal, s, NEG_INF)
            m_new = jnp.maximum(m, jnp.max(s, axis=0, keepdims=True))
            acc = jnp.exp2(m - m_new) * acc
            p = jnp.exp2(s - m_new)
            m = m_new
        else:
            p = jnp.exp2(s)
            if diagonal:
                p = jnp.where(causal, p, 0.0)
        return m, acc + _dot(vs, p.astype(BF16))

    def tiles(js, carry, last_diagonal):
        state = [list(c) for c in carry]
        items = [(j, hh) for j in js for hh in range(nh)]
        ss = []

        def finish(n):
            hh = items[n][1]
            diagonal = last_diagonal and n >= len(items) - nh
            state[hh] = list(accumulate(items[n], ss[n], state[hh][0], state[hh][1], diagonal))

        for n, item in enumerate(items):
            ss.append(scores(item))
            if n >= FOX_AHEAD:
                finish(n - FOX_AHEAD)
        for n in range(max(len(items) - FOX_AHEAD, 0), len(items)):
            finish(n)
        return tuple(tuple(c) for c in state)

    init = tuple((jnp.full((1, tq), NEG_INF, F32), jnp.zeros((VB_ROWS, tq), F32)) for _ in range(nh))
    if online:
        carry = lax.fori_loop(0, i, lambda j, c: tiles([j], c, False), init)
        carry = tiles([i], carry, True)
    else:
        g = FOX_GROUP
        carry = lax.fori_loop(0, i // g, lambda jj, c: tiles([g * jj + n for n in range(g)], c, False), init)
        tails = [functools.partial(lambda c, r: tiles([i - r + n for n in range(r + 1)], c, True), r=r)
                 for r in range(g)]
        carry = lax.switch(i % g, tails, carry)
    for hh, (_, acc) in enumerate(carry):
        o_ref[0, hh] = (acc[0:HEAD_DIM] / acc[HEAD_DIM:HEAD_DIM + 1]).astype(BF16)


def _fox(qbT, kB, vbT, online):
    b, nh, _, t = qbT.shape
    tq = FOX_TILE
    nk = t // tq
    return pl.pallas_call(
        functools.partial(_fox_body, online=online),
        grid=(b, nh // FOX_HEADS, nk),
        in_specs=[pl.BlockSpec((1, FOX_HEADS, QB_ROWS, tq), lambda bb, hh, i: (bb, hh, 0, i)),
                  pl.BlockSpec((1, FOX_HEADS, t, LANES), lambda bb, hh, i: (bb, hh, 0, 0)),
                  pl.BlockSpec((1, FOX_HEADS, nk, VB_ROWS, tq), lambda bb, hh, i: (bb, hh, 0, 0, 0))],
        out_specs=pl.BlockSpec((1, FOX_HEADS, HEAD_DIM, tq), lambda bb, hh, i: (bb, hh, 0, i)),
        out_shape=jax.ShapeDtypeStruct((b, nh, HEAD_DIM, t), BF16),
        compiler_params=_cparams(3),
        name="fox_online" if online else "fox",
    )(qbT, kB, vbT)


def _rwkv_prep(proj, tm, t, mu_ref, w0_ref, w2_ref, a0_ref, a2_ref, g2_ref, kk_ref, ka_ref, rk_ref, e_ref,
               rt_ref, at_ref, bt_ref, kt_ref, bh_ref, kh_ref, v_ref, wc_ref, bonus_ref, g_ref, sbuf):
    @pl.when(t == 0)
    def _():
        sbuf[0:8, :] = jnp.zeros((8, COLS_C), F32)

    def shifted(p, start):
        cols = slice(start, start + p.shape[1])
        sbuf[8:8 + tm, cols] = p
        prev = sbuf[7:7 + tm, cols]
        sbuf[0:8, cols] = p[tm - 8:tm, :]
        return p + (prev - p) * mu_ref[:, cols]

    c = WIDTH_C
    p_lo = proj(3 * c, COLS_C - 3 * c)
    p_k = proj(c, c)
    u_lo = shifted(p_lo, 3 * c)
    yield
    w_lo = u_lo[:, 0:DECAY_LORA]
    a_lo = u_lo[:, DECAY_LORA:DECAY_LORA + AAA_LORA]
    g_lo = u_lo[:, DECAY_LORA + AAA_LORA:]
    w = w0_ref[...] + _dot(jnp.tanh(w_lo).astype(BF16), w2_ref[...])
    w = -_softplus(-w) - 0.5
    ld = -jnp.exp(w)
    p_r = proj(0, c)
    yield
    a = _sigmoid(a0_ref[...] + _dot(a_lo.astype(BF16), a2_ref[...]))
    g_ref[0] = _dot(_sigmoid(g_lo).astype(BF16), g2_ref[...]).astype(BF16)
    k = shifted(p_k, c)
    p_v = proj(2 * c, c)
    yield

    e = e_ref[...]
    kk = k * kk_ref[...]
    nrm2 = _dot((kk * kk).astype(BF16), e)
    kkn = kk * lax.rsqrt(jnp.maximum(nrm2, 1e-24))
    k2 = k * (1.0 + (a - 1.0) * ka_ref[...])
    kka = kkn * a
    r = shifted(p_r, 0)
    v = shifted(p_v, 2 * c)
    v_ref[0] = v.astype(BF16)
    bonus_ref[0] = (_dot((r * k2 * rk_ref[...]).astype(BF16), e) * v).astype(BF16)
    yield

    ri = lax.broadcasted_iota(jnp.int32, (2 * CHUNK, CHUNK), 0)
    ci = lax.broadcasted_iota(jnp.int32, (2 * CHUNK, CHUNK), 1)
    tri = ((ci <= ri) | (ri >= CHUNK)).astype(BF16)
    tri2 = jnp.concatenate([tri, tri], axis=1)
    ld_hi, ld_lo = _split2(ld)
    for cc in range(tm // CHUNK):
        sl = slice(cc * CHUNK, (cc + 1) * CHUNK)
        cum = _dot(tri2, jnp.concatenate([ld_hi[sl], ld_lo[sl]], axis=0))
        lc = cum[0:CHUNK]
        tot = cum[CHUNK:2 * CHUNK]
        e_neg = jnp.exp(-lc)
        e_rem = jnp.exp(tot - lc)
        rt_ref[0, sl, :] = (r[sl] * jnp.exp(lc)).astype(BF16)
        at_ref[0, sl, :] = (-kkn[sl] * jnp.exp(lc - ld[sl])).astype(BF16)
        bt_ref[0, sl, :] = (kka[sl] * e_neg).astype(BF16)
        kt_ref[0, sl, :] = (k2[sl] * e_neg).astype(BF16)
        bh_ref[0, sl, :] = (kka[sl] * e_rem).astype(BF16)
        kh_ref[0, sl, :] = (k2[sl] * e_rem).astype(BF16)
        if cc % 2 == 1:
            yield
    seg = (lax.broadcasted_iota(jnp.int32, (16, tm), 1) // CHUNK
           == lax.broadcasted_iota(jnp.int32, (16, tm), 0)).astype(BF16)
    tots3 = jnp.concatenate(_split3(_dot(seg, ld_hi) + _dot(seg, ld_lo)), axis=0)
    per_blk = RWKV_TILE // CHUNK
    for blk in range(tm // RWKV_TILE):
        put = (lax.broadcasted_iota(jnp.int32, (48, LANES), 0) % 16
               == lax.broadcasted_iota(jnp.int32, (48, LANES), 1) + blk * per_blk).astype(BF16)
        wc_ref[0, blk] = jnp.exp(_dot_tn(tots3, put))


def _bmm(a, b):
    return lax.dot_general(a, b, (((2,), (1,)), ((0,), (0,))), preferred_element_type=F32)


def _bmm_nt(a, b):
    return lax.dot_general(a, b, (((2,), (2,)), ((0,), (0,))), preferred_element_type=F32)


def _bmm_tn(a, b):
    return lax.dot_general(a, b, (((1,), (1,)), ((0,), (0,))), preferred_element_type=F32)


def _rwkv_chunk_body(rt_ref, at_ref, bt_ref, kt_ref, bh_ref, kh_ref, v_ref, wc_ref, bonus_ref, g_ref,
                     lg_ref, lb_ref, e_ref, o_ref, h_ref):
    rows, tm = rt_ref.shape[0], rt_ref.shape[1]
    nc = tm // CHUNK
    ng = N_HEADS_C // RWKV_GROUP
    gl = RWKV_GROUP * HEAD_DIM
    units = [(r, g) for r in range(rows) for g in range(ng)]
    nu = len(units)
    t = pl.program_id(1)

    @pl.when(t == 0)
    def _():
        h_ref[...] = jnp.zeros_like(h_ref)

    lane_head = lax.broadcasted_iota(jnp.int32, (1, CHUNK, gl), 2) // HEAD_DIM
    ri = lax.broadcasted_iota(jnp.int32, (1, 2 * CHUNK, gl), 1)
    ci = lax.broadcasted_iota(jnp.int32, (1, 2 * CHUNK, gl), 2) & (CHUNK - 1)
    rr = ri & (CHUNK - 1)
    keep = (rr > ci) | ((ri >= CHUNK) & (rr == ci))
    eye = (lax.broadcasted_iota(jnp.int32, (1, CHUNK, gl), 1)
           == (lax.broadcasted_iota(jnp.int32, (1, CHUNK, gl), 2) & (CHUNK - 1)))
    same_head = (lax.broadcasted_iota(jnp.int32, (1, gl, gl), 1) // HEAD_DIM
                 == lax.broadcasted_iota(jnp.int32, (1, gl, gl), 2) // HEAD_DIM)

    def bdiag(x):
        return jnp.concatenate([jnp.where(lane_head == hh, x, jnp.zeros_like(x)) for hh in range(RWKV_GROUP)],
                               axis=1)

    def chunk_local(cs, out):
        tile = lambda ref: jnp.stack([ref[r, c * CHUNK:(c + 1) * CHUNK, g * gl:(g + 1) * gl]
                                      for c in cs for r, g in units])
        rt, at, bt, kt, bh, kh, v = (tile(r) for r in (rt_ref, at_ref, bt_ref, kt_ref, bh_ref, kh_ref, v_ref))
        ar = jnp.concatenate([at, rt], axis=1)
        sb = jnp.where(keep, _bmm_nt(ar, bdiag(bt)), 0.0)
        sk = jnp.where(keep, _bmm_nt(ar, bdiag(kt)), 0.0).astype(BF16)
        m_rb = sb[:, CHUNK:].astype(BF16)
        yield
        a = sb[:, :CHUNK]
        tinv = jnp.where(eye, 1.0, 0.0) + a
        ab = a.astype(BF16)
        x = _bmm(ab, bdiag(ab))
        yield
        for _ in range(4):
            xb = x.astype(BF16)
            r = _bmm(jnp.concatenate([tinv.astype(BF16), xb], axis=1), bdiag(xb))
            tinv = tinv + r[:, :CHUNK]
            x = r[:, CHUNK:]
            yield
        tinv = (tinv + _bmm(tinv.astype(BF16), bdiag(x.astype(BF16)))).astype(BF16)
        bdv = bdiag(v)
        av = _bmm(sk[:, :CHUNK], bdv)
        yield
        p = _bmm(tinv, bdiag(at)).astype(BF16)
        u0 = _bmm(tinv, bdiag(av.astype(BF16)))
        yield
        q = (rt.astype(F32) + _bmm(m_rb, bdiag(p))).astype(BF16)
        y0 = _bmm(m_rb, bdiag(u0.astype(BF16))) + _bmm(sk[:, CHUNK:], bdv)
        parts = (jnp.concatenate([q, p], axis=1), jnp.concatenate([y0, u0], axis=1),
                 jnp.concatenate([bh, kh], axis=1), v)
        out += [tuple(z[n * nu:(n + 1) * nu] for z in parts) for n in range(len(cs))]
        yield

    wct = [wc_ref[r, 0] for r in range(rows)]
    state = [h_ref[...]]
    ys = []
    local = []

    def recur(c):
        qp, yu0, bk, v = local[c]
        yu = _bmm(qp, state[0].astype(BF16)) + yu0
        ys.append(yu[:, :CHUNK])
        upd = _bmm_tn(bk, jnp.concatenate([yu[:, CHUNK:].astype(BF16), v], axis=1))
        wcol = jnp.stack([wct[r][g * gl:(g + 1) * gl, c:c + 1] for r, g in units])
        state[0] = jnp.where(same_head, upd, 0.0) + wcol * state[0]

    assert sorted(c for cs in RWKV_SCHEDULE for c in cs) == list(range(nc))
    done = 0
    for cs in RWKV_SCHEDULE:
        pending = list(range(done, len(local)))
        slots = {(k + 1) * RWKV_STAGES // (len(pending) + 1): c for k, c in enumerate(pending)}
        for n, _ in enumerate(chunk_local(list(cs), local), start=1):
            if n in slots:
                recur(slots[n])
                done += 1
    for c in range(done, nc):
        recur(c)
    h_ref[...] = state[0]

    y = jnp.concatenate([jnp.concatenate([yc[r * ng + g] for g in range(ng)], axis=-1)
                         for r in range(rows) for yc in ys], axis=0)
    e = e_ref[...]

    def head_mean(z):
        return _dot(z.astype(BF16), e) * (1.0 / HEAD_DIM)

    d = y - head_mean(y)
    yn = d * lax.rsqrt(head_mean(d * d) + LNX_EPS)
    flat = lambda ref: ref[...].astype(F32).reshape(rows * tm, WIDTH_C)
    out = (yn * lg_ref[...] + lb_ref[...] + flat(bonus_ref)) * flat(g_ref)
    o_ref[...] = out.astype(BF16).reshape(rows, tm, WIDTH_C)


def _rwkv_chunk(rt, at, bt, kt, bh, kh, v, wc, bonus, g, lnx_g, lnx_b, e):
    b, t, _ = rt.shape
    tm = RWKV_TILE
    rows = math.gcd(RWKV_ROWS, b)
    gl = RWKV_GROUP * HEAD_DIM
    const = lambda shape: pl.BlockSpec(shape, lambda i, j: (0,) * len(shape))
    tok = lambda: pl.BlockSpec((rows, tm, WIDTH_C), lambda i, j: (i, j, 0))
    return pl.pallas_call(
        _rwkv_chunk_body,
        grid=(b // rows, t // tm),
        in_specs=[tok(), tok(), tok(), tok(), tok(), tok(), tok(),
                  pl.BlockSpec((rows, 1, WIDTH_C, LANES), lambda i, j: (i, j, 0, 0)),
                  tok(), tok(), const(lnx_g.shape), const(lnx_b.shape), const(e.shape)],
        out_specs=tok(),
        out_shape=jax.ShapeDtypeStruct((b, t, WIDTH_C), BF16),
        scratch_shapes=[pltpu.VMEM((rows * (N_HEADS_C // RWKV_GROUP), gl, gl), F32)],
        compiler_params=_cparams(2),
        name="rwkv_chunk",
    )(rt, at, bt, kt, bh, kh, v, wc, bonus, g, lnx_g, lnx_b, e)


def _outproj_body(x_ref, ya_ref, yb_ref, yc_ref, wa_ref, wb_ref, wc_ref, o_ref):
    acc = _dot_tn(ya_ref[0], wa_ref[...])
    acc = acc + _dot_tn(yb_ref[0], wb_ref[...])
    acc = acc + _dot(yc_ref[0], wc_ref[...])
    o_ref[0] = x_ref[0] + acc


def _outproj(x, yaT, ybT, yc, wa, wb, wc):
    b, t, d = x.shape
    tm = TOK_TILE
    const = lambda shape: pl.BlockSpec(shape, lambda i, j: (0,) * len(shape))
    return pl.pallas_call(
        _outproj_body,
        grid=(b, t // tm),
        in_specs=[pl.BlockSpec((1, tm, d), lambda i, j: (i, j, 0)),
                  pl.BlockSpec((1, WIDTH_A, tm), lambda i, j: (i, 0, j)),
                  pl.BlockSpec((1, WIDTH_B, tm), lambda i, j: (i, 0, j)),
                  pl.BlockSpec((1, tm, WIDTH_C), lambda i, j: (i, j, 0)),
                  const(wa.shape), const(wb.shape), const(wc.shape)],
        out_specs=pl.BlockSpec((1, tm, d), lambda i, j: (i, j, 0)),
        out_shape=jax.ShapeDtypeStruct((b, t, d), F32),
        compiler_params=_cparams(2),
        name="outproj",
    )(x, yaT, ybT, yc, wa, wb, wc)


def _ffn_body(x_ref, g_ref, wg_ref, wv_ref, cw_ref, cb_ref, wd_ref, o_ref, gbuf):
    tm = x_ref.shape[1]
    t = pl.program_id(1)

    @pl.when(t == 0)
    def _():
        gbuf[0:8, :] = jnp.zeros((8, D_FF), F32)

    x = x_ref[0]
    ms = jnp.mean(x * x, axis=-1, keepdims=True)
    h = (x * lax.rsqrt(ms + RMS_EPS) * g_ref[...]).astype(BF16)
    acc = x
    for lo, hi in zip(FF_SPLITS[:-1], FF_SPLITS[1:]):
        gate = _dot(h, wg_ref[:, lo:hi])
        val = _dot(h, wv_ref[:, lo:hi])
        gbuf[8:8 + tm, lo:hi] = gate
        g1 = gbuf[7:7 + tm, lo:hi]
        g2 = gbuf[6:6 + tm, lo:hi]
        gbuf[0:8, lo:hi] = gate[tm - 8:tm, :]
        conv = cb_ref[:, lo:hi] + g2 * cw_ref[0:1, lo:hi] + g1 * cw_ref[1:2, lo:hi] + gate * cw_ref[2:3, lo:hi]
        act = (conv * _sigmoid(conv) * val).astype(BF16)
        acc = acc + _dot(act, wd_ref[lo:hi, :])
    o_ref[0] = acc


def _ffn(x, g, wg, wv, cw, cb, wd):
    b, t, d = x.shape
    tm = TOK_TILE
    const = lambda shape: pl.BlockSpec(shape, lambda i, j: (0,) * len(shape),
                                       pipeline_mode=pl.Buffered(1))
    return pl.pallas_call(
        _ffn_body,
        grid=(b, t // tm),
        in_specs=[pl.BlockSpec((1, tm, d), lambda i, j: (i, j, 0)),
                  const(g.shape), const(wg.shape), const(wv.shape), const(cw.shape), const(cb.shape),
                  const(wd.shape)],
        out_specs=pl.BlockSpec((1, tm, d), lambda i, j: (i, j, 0)),
        out_shape=jax.ShapeDtypeStruct((b, t, d), F32),
        scratch_shapes=[pltpu.VMEM((tm + 8, D_FF), F32)],
        compiler_params=_cparams(2),
        name="ffn",
    )(x, g, wg, wv, cw, cb, wd)


def _pad_heads_cols(w, nh):
    d = w.shape[0]
    w = w.reshape(d, nh, HEAD_DIM)
    return jnp.pad(w, ((0, 0), (0, 0), (0, LANES - HEAD_DIM))).reshape(d, nh * LANES)


def _qk_bound(gq, gk):
    scale = HEAD_DIM ** -0.5
    return 1.01 * HEAD_DIM * scale * jnp.max(jnp.abs(gq)) * jnp.max(jnp.abs(gk)) + 0.05


def _layer_params(l, w_in, q_norm_a, k_norm_a, rel_bias, q_norm_b, k_norm_b, forget_bias):
    w = w_in[l]
    scale = HEAD_DIM ** -0.5
    a0, b0 = 0, 3 * WIDTH_A
    c0 = b0 + 3 * WIDTH_B + N_HEADS_B
    qa, ka, va = (w[:, a0 + i * WIDTH_A:a0 + (i + 1) * WIDTH_A] for i in range(3))
    qb, kb, vb = (w[:, b0 + i * WIDTH_B:b0 + (i + 1) * WIDTH_B] for i in range(3))
    fg = w[:, b0 + 3 * WIDTH_B:c0]
    wc = w[:, c0:]
    wn = jnp.concatenate([_pad_heads_cols(kb, N_HEADS_B),
                          jnp.pad(fg, ((0, 0), (0, LANES - N_HEADS_B))), wc], axis=1).astype(BF16)
    pad_v = lambda v, nh: jnp.pad(v.reshape(-1, nh, HEAD_DIM),
                                  ((0, 0), (0, 0), (0, VB_ROWS - HEAD_DIM))).reshape(-1, nh * VB_ROWS)
    assert QB_ROWS == VB_ROWS
    wt = jnp.concatenate([qa, ka, pad_v(va, N_HEADS_A), pad_v(qb, N_HEADS_B), pad_v(vb, N_HEADS_B),
                          jnp.pad(fg, ((0, 0), (0, 16 - N_HEADS_B)))], axis=1).T.astype(BF16)
    pad = LANES - HEAD_DIM
    gka = k_norm_a[l].reshape(HEAD_DIM, 1)
    gkb = jnp.pad(k_norm_b[l], (0, pad)).reshape(1, LANES)
    gqa = (q_norm_a[l] * (scale * LOG2E)).reshape(HEAD_DIM, 1)
    gqb = jnp.pad(q_norm_b[l] * (scale * LOG2E), (0, QB_ROWS - HEAD_DIM)).reshape(QB_ROWS, 1)
    fbn = jnp.pad(forget_bias[l], (0, LANES - N_HEADS_B)).reshape(1, LANES)
    fbt = jnp.pad(forget_bias[l], (0, 16 - N_HEADS_B)).reshape(16, 1)
    shift_b = (LOG2E * _qk_bound(q_norm_b[l], k_norm_b[l])).reshape(1, 1)
    bound_a = _qk_bound(q_norm_a[l], k_norm_a[l])
    tab = rel_bias[l]
    shift_a = LOG2E * (bound_a + jnp.max(tab, axis=1))
    depth_a = jnp.max(shift_a + LOG2E * (bound_a - tab[:, MAX_REL]))
    return wn, wt, gka, gkb, gqa, gqb, fbn, fbt, shift_b, shift_a, depth_a


def _selectors():
    import numpy as np
    selk = np.zeros((LANES, N_HEADS_B * LANES), np.float32)
    selq = np.zeros((N_HEADS_B * QB_ROWS, 3 * 16), np.float32)
    for p in range(3):
        for hh in range(N_HEADS_B):
            selk[p * N_HEADS_B + hh, hh * LANES + HEAD_DIM + 3 + p] = -1.0
            selq[hh * QB_ROWS + HEAD_DIM + p, 16 * p + hh] = 1.0
    e = np.kron(np.eye(N_HEADS_C, dtype=np.float32), np.ones((HEAD_DIM, HEAD_DIM), np.float32))
    return jnp.asarray(selk, BF16), jnp.asarray(selq, BF16), jnp.asarray(e, BF16)


def kernel(x, mix_norm_g, w_in, q_norm_a, k_norm_a, rel_bias, q_norm_b, k_norm_b, forget_bias, shift_mu, w0, w2,
           a0, a2, g2, k_k, k_a, r_k, lnx_g, lnx_b, w_out, ffn_norm_g, w_up, conv_w, conv_b, w_down):
    depth = w_in.shape[0]
    b, t, d = x.shape
    selk, selq, e = _selectors()
    row = lambda v: v.reshape(1, -1)
    for l in range(depth):
        wn, wt, gka, gkb, gqa, gqb, fbn, fbt, shift_b, shift_a, depth_a = _layer_params(
            l, w_in, q_norm_a, k_norm_a, rel_bias, q_norm_b, k_norm_b, forget_bias)
        rwkv_params = (row(shift_mu[l]), row(w0[l]), w2[l].astype(BF16), row(a0[l]), a2[l].astype(BF16),
                       g2[l].astype(BF16), row(k_k[l]), row(k_a[l]), row(r_k[l]), e)
        (qaT, vaT, kA, qbT, vbT, kB, rt, at, bt, kt, bh, kh, v, wcum, bonus, g) = _inproj(
            x, row(mix_norm_g[l]), wn, wt, gka, gkb, gqa, gqb, fbn, fbt, selk, selq, shift_b, rwkv_params)
        bias = _relbias(rel_bias[l], shift_a)
        yaT = lax.cond(depth_a <= 2 * FAST_MAX_SHIFT,
                       functools.partial(_attn_a, online=False), functools.partial(_attn_a, online=True),
                       qaT, kA, vaT, bias)
        ybT = lax.cond(shift_b[0, 0] <= FAST_MAX_SHIFT,
                       functools.partial(_fox, online=False), functools.partial(_fox, online=True),
                       qbT, kB, vbT)
        yc = _rwkv_chunk(rt, at, bt, kt, bh, kh, v, wcum, bonus, g, row(lnx_g[l]), row(lnx_b[l]), e)
        wo = w_out[l].astype(BF16)
        x = _outproj(x, yaT.reshape(b, WIDTH_A, t), ybT.reshape(b, WIDTH_B, t), yc,
                     wo[:WIDTH_A], wo[WIDTH_A:WIDTH_A + WIDTH_B], wo[WIDTH_A + WIDTH_B:])
        wu = w_up[l].astype(BF16)
        x = _ffn(x, row(ffn_norm_g[l]), wu[:, :D_FF], wu[:, D_FF:], conv_w[l], row(conv_b[l]),
                 w_down[l].astype(BF16))
    return x
```

```python
import functools
import math

import jax
import jax.numpy as jnp
from jax import lax
from jax.experimental import pallas as pl
from jax.experimental.pallas import tpu as pltpu

F32 = jnp.float32
BF16 = jnp.bfloat16

D_MODEL = 1024
HEAD_DIM = 64
CHUNK = 64
LEFT_CHUNKS = 8
MAX_REL = 128
N_HEADS_A = 4
N_HEADS_B = 4
N_HEADS_C = 8
WIDTH_A = N_HEADS_A * HEAD_DIM
WIDTH_B = N_HEADS_B * HEAD_DIM
WIDTH_C = N_HEADS_C * HEAD_DIM
DECAY_LORA = 64
AAA_LORA = 64
GATE_LORA = 128
COLS_C = 3 * WIDTH_C + DECAY_LORA + AAA_LORA + GATE_LORA
D_FF = 2816
RMS_EPS = 1e-6
LNX_EPS = 64e-5
NEG_INF = -1e30

LANES = 128
TOK_TILE = 512
OUTPROJ_TILE = 1024
ATT_A_TILE = 256
ATT_A_WIN = 3 * ATT_A_TILE
ATT_A_AHEAD = 3
FOX_TILE = 512
FOX_HEADS = 2
FOX_GROUP = 8
FOX_AHEAD = 2
RWKV_GROUP = 4
RWKV_TILE = 256
RWKV_ROWS = 4
RWKV_SCHEDULE = ((0, 1), (2, 3))
RWKV_STAGES = 9
FF_SPLITS = (0, 1024, 2816)
VMEM_LIMIT = 56 * 1024 * 1024

PN_KB, PN_F, PN_C = 0, 512, 640
PN_COLS = PN_C + COLS_C
VB_ROWS = 80
PT_QA = 0
PT_KA = PT_QA + WIDTH_A
PT_VA = PT_KA + WIDTH_A
PT_QB = PT_VA + N_HEADS_A * VB_ROWS
QB_ROWS = 80
PT_VB = PT_QB + N_HEADS_B * QB_ROWS
PT_F = PT_VB + N_HEADS_B * VB_ROWS
PT_ROWS = PT_F + 16
LOG2E = 1.4426950408889634
FAST_MAX_SHIFT = 40.0


def _cparams(n_axes):
    return pltpu.CompilerParams(dimension_semantics=("arbitrary",) * n_axes,
                                vmem_limit_bytes=VMEM_LIMIT)


def _split3(x):
    hi = x.astype(BF16)
    r1 = x - hi.astype(F32)
    mid = r1.astype(BF16)
    lo = (r1 - mid.astype(F32)).astype(BF16)
    return hi, mid, lo


def _split2(x):
    hi = x.astype(BF16)
    lo = (x - hi.astype(F32)).astype(BF16)
    return hi, lo


def _dot(a, b):
    return jnp.dot(a, b, preferred_element_type=F32)


def _dot_nt(a, b):
    return lax.dot_general(a, b, (((1,), (1,)), ((), ())), preferred_element_type=F32)


def _dot_tn(a, b):
    return lax.dot_general(a, b, (((0,), (0,)), ((), ())), preferred_element_type=F32)


def _log_sigmoid(x):
    return jnp.minimum(x, 0.0) - jnp.log(1.0 + jnp.exp(-jnp.abs(x)))


def _softplus(x):
    return jnp.maximum(x, 0.0) + jnp.log(1.0 + jnp.exp(-jnp.abs(x)))


def _sigmoid(x):
    return 1.0 / (1.0 + jnp.exp(-x))


def _inproj_body(x_ref, g_ref, wn_ref, wt_ref, gka_ref, gkb_ref, gqa_ref, gqb_ref, fbn_ref, fbt_ref,
                 selk_ref, selq_ref, shift_ref, *rest):
    rwkv_params, rest = rest[:10], rest[10:]
    qaT_ref, vaT_ref, kA_ref, qbT_ref, vbT_ref, kB_ref = rest[:6]
    rwkv_outs = rest[6:16]
    carry_n, carry_t, sbuf = rest[16:]
    tm = x_ref.shape[1]
    t = pl.program_id(1)

    @pl.when(t == 0)
    def _():
        carry_n[...] = jnp.zeros_like(carry_n)
        carry_t[...] = jnp.zeros_like(carry_t)

    x = x_ref[0]
    ms = jnp.mean(x * x, axis=-1, keepdims=True)
    h = (x * lax.rsqrt(ms + RMS_EPS) * g_ref[...]).astype(BF16)
    rows_t = lambda start, n: _dot_nt(wt_ref[start:start + n, :], h)
    cols_n = lambda start, n: _dot(h, wn_ref[:, start:start + n])
    row_v = lax.broadcasted_iota(jnp.int32, (VB_ROWS, 1), 0)
    ones_v = jnp.where(row_v == HEAD_DIM, 1.0, 0.0)
    val = {}

    def norm_rows(p, gain_ref):
        p = p.reshape(N_HEADS_A, HEAD_DIM, tm)
        return (p * lax.rsqrt(jnp.mean(p * p, axis=1, keepdims=True) + RMS_EPS) * gain_ref[...][None]).astype(BF16)

    def lane_packed3(z):
        z_hi, z_mid, z_lo = _split3(z)
        return (z_hi.astype(F32) + pltpu.roll(z_mid.astype(F32), N_HEADS_B, axis=1)
                + pltpu.roll(z_lo.astype(F32), 2 * N_HEADS_B, axis=1)).astype(BF16)

    def gates_epilogue(p):
        pn_f, pt_f = p
        lane_n = lax.broadcasted_iota(jnp.int32, (1, LANES), 1)
        lf_n = jnp.where(lane_n < N_HEADS_B, LOG2E * _log_sigmoid(pn_f + fbn_ref[...]), 0.0)
        row_t = lax.broadcasted_iota(jnp.int32, (16, 1), 0)
        lf_t = jnp.where(row_t < N_HEADS_B, LOG2E * _log_sigmoid(pt_f + fbt_ref[...]), 0.0)
        ri = lax.broadcasted_iota(jnp.int32, (tm, tm), 0)
        ci = lax.broadcasted_iota(jnp.int32, (tm, tm), 1)
        low = (ci <= ri).astype(BF16)
        upp = (ri <= ci).astype(BF16)
        cum3 = _dot(low, lane_packed3(lf_n))
        cum = cum3 + pltpu.roll(cum3, LANES - N_HEADS_B, axis=1) + pltpu.roll(cum3, LANES - 2 * N_HEADS_B, axis=1)
        val["c_n"] = jnp.where(lane_n < N_HEADS_B, cum, 0.0) + carry_n[0:1, :]
        ht, mt, lt = _split3(lf_t)
        val["c_t"] = _dot(ht, upp) + _dot(mt, upp) + _dot(lt, upp) + carry_t[:, 0:1]
        carry_n[...] = carry_n[...] + jnp.sum(lf_n, axis=0, keepdims=True)
        carry_t[...] = carry_t[...] + jnp.sum(lf_t, axis=1, keepdims=True)

    def kb_epilogue(pn_kb):
        kaug = _dot(lane_packed3(val["c_n"]), selk_ref[...])
        lane_k = lax.broadcasted_iota(jnp.int32, (1, LANES), 1)
        ones_k = jnp.where((lane_k >= HEAD_DIM) & (lane_k < HEAD_DIM + 3), 1.0, 0.0)
        for hh in range(N_HEADS_B):
            k = pn_kb[:, LANES * hh:LANES * (hh + 1)]
            msk = jnp.sum(k * k, axis=-1, keepdims=True) * (1.0 / HEAD_DIM)
            kn = k * lax.rsqrt(msk + RMS_EPS) * gkb_ref[...]
            kB_ref[0, hh] = (kn + kaug[:, LANES * hh:LANES * (hh + 1)] + ones_k).astype(BF16)

    def qb_epilogue(pt_qb):
        c3 = jnp.concatenate(_split3(val["c_t"] - shift_ref[...]), axis=0)
        qaug = _dot(selq_ref[...], c3)
        qb = pt_qb.reshape(N_HEADS_B, QB_ROWS, tm)
        msq = jnp.sum(qb * qb, axis=1, keepdims=True) * (1.0 / HEAD_DIM)
        row_q = lax.broadcasted_iota(jnp.int32, (QB_ROWS, 1), 0)
        ones_q = jnp.where((row_q >= HEAD_DIM + 3) & (row_q < HEAD_DIM + 6), 1.0, 0.0)
        qn = qb * lax.rsqrt(msq + RMS_EPS) * gqb_ref[...][None]
        qbT_ref[0] = (qn + qaug.reshape(N_HEADS_B, QB_ROWS, tm) + ones_q[None]).astype(BF16)

    def store(ref, value):
        ref[...] = value

    sections = [
        (lambda: rows_t(PT_QA, WIDTH_A), lambda p: store(qaT_ref, norm_rows(p, gqa_ref)[None])),
        (lambda: rows_t(PT_KA, WIDTH_A), lambda p: store(kA_ref, norm_rows(p, gka_ref)[None])),
        (lambda: rows_t(PT_VA, N_HEADS_A * VB_ROWS),
         lambda p: store(vaT_ref, (p.reshape(N_HEADS_A, VB_ROWS, tm) + ones_v[None]).astype(BF16)[None])),
        (lambda: (cols_n(PN_F, LANES), rows_t(PT_F, 16)), gates_epilogue),
        (lambda: rows_t(PT_VB, N_HEADS_B * VB_ROWS),
         lambda p: store(vbT_ref, (p.reshape(N_HEADS_B, VB_ROWS, tm) + ones_v[None]).astype(BF16)[None, :, None])),
        (lambda: cols_n(PN_KB, N_HEADS_B * LANES), kb_epilogue),
        (lambda: rows_t(PT_QB, N_HEADS_B * QB_ROWS), qb_epilogue),
    ]
    slots = _rwkv_prep(lambda start, n: cols_n(PN_C + start, n), tm, t, *rwkv_params, *rwkv_outs, sbuf)
    pending = None
    for mm, epilogue in sections:
        p = mm()
        if pending is not None:
            pending[1](pending[0])
        pending = (p, epilogue)
        next(slots, None)
    pending[1](pending[0])
    for _ in slots:
        pass


def _inproj(x, g, wn, wt, gka, gkb, gqa, gqb, fbn, fbt, selk, selq, shift, rwkv_params):
    b, t, d = x.shape
    tm = TOK_TILE
    nt = t // tm
    const = lambda shape: pl.BlockSpec(shape, lambda i, j: (0,) * len(shape))
    tok = lambda: pl.BlockSpec((1, tm, WIDTH_C), lambda i, j: (i, j, 0))
    bf = jax.ShapeDtypeStruct((b, t, WIDTH_C), BF16)
    out_shape = (
        jax.ShapeDtypeStruct((b, N_HEADS_A, HEAD_DIM, t), BF16),
        jax.ShapeDtypeStruct((b, N_HEADS_A, VB_ROWS, t), BF16),
        jax.ShapeDtypeStruct((b, N_HEADS_A, HEAD_DIM, t), BF16),
        jax.ShapeDtypeStruct((b, N_HEADS_B, QB_ROWS, t), BF16),
        jax.ShapeDtypeStruct((b, N_HEADS_B, nt, VB_ROWS, tm), BF16),
        jax.ShapeDtypeStruct((b, N_HEADS_B, t, LANES), BF16),
        bf, bf, bf, bf, bf, bf, bf,
        jax.ShapeDtypeStruct((b, t // RWKV_TILE, WIDTH_C, LANES), F32),
        bf, bf,
    )
    out_specs = (
        pl.BlockSpec((1, N_HEADS_A, HEAD_DIM, tm), lambda i, j: (i, 0, 0, j)),
        pl.BlockSpec((1, N_HEADS_A, VB_ROWS, tm), lambda i, j: (i, 0, 0, j)),
        pl.BlockSpec((1, N_HEADS_A, HEAD_DIM, tm), lambda i, j: (i, 0, 0, j)),
        pl.BlockSpec((1, N_HEADS_B, QB_ROWS, tm), lambda i, j: (i, 0, 0, j)),
        pl.BlockSpec((1, N_HEADS_B, 1, VB_ROWS, tm), lambda i, j: (i, 0, j, 0, 0)),
        pl.BlockSpec((1, N_HEADS_B, tm, LANES), lambda i, j: (i, 0, j, 0)),
        tok(), tok(), tok(), tok(), tok(), tok(), tok(),
        pl.BlockSpec((1, tm // RWKV_TILE, WIDTH_C, LANES), lambda i, j: (i, j, 0, 0)),
        tok(), tok(),
    )
    in_specs = [
        pl.BlockSpec((1, tm, d), lambda i, j: (i, j, 0)),
        const((1, d)), const(wn.shape), const(wt.shape),
        const(gka.shape), const(gkb.shape), const(gqa.shape), const(gqb.shape),
        const(fbn.shape), const(fbt.shape), const(selk.shape), const(selq.shape), const(shift.shape),
    ] + [const(p.shape) for p in rwkv_params]
    return pl.pallas_call(
        _inproj_body,
        grid=(b, nt),
        in_specs=in_specs,
        out_specs=out_specs,
        out_shape=out_shape,
        scratch_shapes=[pltpu.VMEM((8, LANES), F32), pltpu.VMEM((16, LANES), F32),
                        pltpu.VMEM((tm + 8, COLS_C), F32)],
        compiler_params=_cparams(2),
        name="inproj",
    )(x, g, wn, wt, gka, gkb, gqa, gqb, fbn, fbt, selk, selq, shift, *rwkv_params)


def _relbias_body(tab_ref, shift_ref, o_ref):
    hh = pl.program_id(0)
    kj = lax.broadcasted_iota(jnp.int32, (ATT_A_WIN, ATT_A_TILE), 0)
    qi = lax.broadcasted_iota(jnp.int32, (ATT_A_WIN, ATT_A_TILE), 1)
    rel = jnp.clip(kj - LEFT_CHUNKS * CHUNK - qi, -MAX_REL, MAX_REL) + MAX_REL

    def body(r, acc):
        return jnp.where(rel == r, tab_ref[hh, r], acc)

    bias = lax.fori_loop(0, 2 * MAX_REL + 1, body, jnp.zeros((ATT_A_WIN, ATT_A_TILE), F32))
    kc = kj // CHUNK
    qc = qi // CHUNK
    band = (kc >= qc) & (kc <= qc + LEFT_CHUNKS)
    o_ref[0] = jnp.where(band, LOG2E * bias - shift_ref[hh], NEG_INF)


def _relbias(tab, shift):
    return pl.pallas_call(
        _relbias_body,
        grid=(N_HEADS_A,),
        in_specs=[pl.BlockSpec(memory_space=pltpu.SMEM), pl.BlockSpec(memory_space=pltpu.SMEM)],
        out_specs=pl.BlockSpec((1, ATT_A_WIN, ATT_A_TILE), lambda i: (i, 0, 0)),
        out_shape=jax.ShapeDtypeStruct((N_HEADS_A, ATT_A_WIN, ATT_A_TILE), F32),
        compiler_params=_cparams(1),
        name="relbias",
    )(tab, shift)


def _attn_a_body(q_ref, k0_ref, k1_ref, k2_ref, v0_ref, v1_ref, v2_ref, bias_ref, o_ref, *, online):
    i = pl.program_id(1)
    tq = ATT_A_TILE
    kj = lax.broadcasted_iota(jnp.int32, (tq, 1), 0)

    k_refs = (k0_ref, k1_ref, k2_ref)
    v_refs = (v0_ref, v1_ref, v2_ref)

    def run(mask_padding):
        def scores(hh):
            q = q_ref[0, hh]
            ss = []
            for d, k_ref in enumerate(k_refs):
                s = _dot_tn(k_ref[0, hh], q) + bias_ref[hh, d * tq:(d + 1) * tq, :]
                if mask_padding:
                    s = jnp.where(kj + (i - 2 + d) * tq >= 0, s, NEG_INF)
                ss.append(s)
            return ss

        def finish(hh, ss):
            if online:
                m = functools.reduce(jnp.maximum, [jnp.max(s, axis=0, keepdims=True) for s in ss])
                ss = [s - m for s in ss]
            acc = jnp.zeros((VB_ROWS, tq), F32)
            for s, v_ref in zip(ss, v_refs):
                acc = acc + _dot(v_ref[0, hh], jnp.exp2(s).astype(BF16))
            o_ref[0, hh] = (acc[0:HEAD_DIM] / acc[HEAD_DIM:HEAD_DIM + 1]).astype(BF16)

        pending = []
        for hh in range(N_HEADS_A):
            pending.append((hh, scores(hh)))
            if len(pending) > ATT_A_AHEAD:
                finish(*pending.pop(0))
        for item in pending:
            finish(*item)

    pl.when(i < 2)(functools.partial(run, True))
    pl.when(i >= 2)(functools.partial(run, False))


def _attn_a(qaT, kA, vaT, bias, online):
    b, nh, _, t = qaT.shape
    tq = ATT_A_TILE
    kspec = lambda d: pl.BlockSpec((1, nh, HEAD_DIM, tq),
                                   lambda bb, i: (bb, 0, 0, jnp.maximum(i - 2 + d, 0)))
    vspec = lambda d: pl.BlockSpec((1, nh, VB_ROWS, tq),
                                   lambda bb, i: (bb, 0, 0, jnp.maximum(i - 2 + d, 0)))
    return pl.pallas_call(
        functools.partial(_attn_a_body, online=online),
        grid=(b, t // tq),
        in_specs=[pl.BlockSpec((1, nh, HEAD_DIM, tq), lambda bb, i: (bb, 0, 0, i)),
                  kspec(0), kspec(1), kspec(2), vspec(0), vspec(1), vspec(2),
                  pl.BlockSpec((nh, ATT_A_WIN, tq), lambda bb, i: (0, 0, 0))],
        out_specs=pl.BlockSpec((1, nh, HEAD_DIM, tq), lambda bb, i: (bb, 0, 0, i)),
        out_shape=jax.ShapeDtypeStruct((b, nh, HEAD_DIM, t), BF16),
        compiler_params=_cparams(2),
        name="attn_a_online" if online else "attn_a",
    )(qaT, kA, kA, kA, vaT, vaT, vaT, bias)


def _fox_body(q_ref, k_ref, v_ref, o_ref, *, online):
    i = pl.program_id(2)
    tq = FOX_TILE
    nh = q_ref.shape[1]
    q = [jnp.concatenate([q_ref[0, hh], jnp.zeros((LANES - QB_ROWS, tq), BF16)], axis=0) for hh in range(nh)]

    def scores(item):
        j, hh = item
        ks = k_ref[0, hh, pl.ds(pl.multiple_of(j * tq, tq), tq), :]
        return _dot(ks, q[hh])

    def accumulate(item, s, m, acc, diagonal):
        j, hh = item
        vs = v_ref[0, hh, j]
        if diagonal:
            kj = lax.broadcasted_iota(jnp.int32, (tq, tq), 0)
            qi = lax.broadcasted_iota(jnp.int32, (tq, tq), 1)
            causal = kj <= qi
        if online:
            if diagonal:
                s = jnp.where(causal, s, NEG_INF)
            m_new = jnp.maximum(m, jnp.max(s, axis=0, keepdims=True))
            acc = jnp.exp2(m - m_new) * acc
            p = jnp.exp2(s - m_new)
            m = m_new
        else:
            p = jnp.exp2(s)
            if diagonal:
                p = jnp.where(causal, p, 0.0)
        return m, acc + _dot(vs, p.astype(BF16))

    def tiles(js, carry, last_diagonal):
        state = [list(c) for c in carry]
        items = [(j, hh) for j in js for hh in range(nh)]
        ss = []

        def finish(n):
            hh = items[n][1]
            diagonal = last_diagonal and n >= len(items) - nh
            state[hh] = list(accumulate(items[n], ss[n], state[hh][0], state[hh][1], diagonal))

        for n, item in enumerate(items):
            ss.append(scores(item))
            if n >= FOX_AHEAD:
                finish(n - FOX_AHEAD)
        for n in range(max(len(items) - FOX_AHEAD, 0), len(items)):
            finish(n)
        return tuple(tuple(c) for c in state)

    init = tuple((jnp.full((1, tq), NEG_INF, F32), jnp.zeros((VB_ROWS, tq), F32)) for _ in range(nh))
    if online:
        carry = lax.fori_loop(0, i, lambda j, c: tiles([j], c, False), init)
        carry = tiles([i], carry, True)
    else:
        g = FOX_GROUP
        carry = lax.fori_loop(0, i // g, lambda jj, c: tiles([g * jj + n for n in range(g)], c, False), init)
        tails = [functools.partial(lambda c, r: tiles([i - r + n for n in range(r + 1)], c, True), r=r)
                 for r in range(g)]
        carry = lax.switch(i % g, tails, carry)
    for hh, (_, acc) in enumerate(carry):
        o_ref[0, hh] = (acc[0:HEAD_DIM] / acc[HEAD_DIM:HEAD_DIM + 1]).astype(BF16)


def _fox(qbT, kB, vbT, online):
    b, nh, _, t = qbT.shape
    tq = FOX_TILE
    nk = t // tq
    return pl.pallas_call(
        functools.partial(_fox_body, online=online),
        grid=(b, nh // FOX_HEADS, nk),
        in_specs=[pl.BlockSpec((1, FOX_HEADS, QB_ROWS, tq), lambda bb, hh, i: (bb, hh, 0, i)),
                  pl.BlockSpec((1, FOX_HEADS, t, LANES), lambda bb, hh, i: (bb, hh, 0, 0)),
                  pl.BlockSpec((1, FOX_HEADS, nk, VB_ROWS, tq), lambda bb, hh, i: (bb, hh, 0, 0, 0))],
        out_specs=pl.BlockSpec((1, FOX_HEADS, HEAD_DIM, tq), lambda bb, hh, i: (bb, hh, 0, i)),
        out_shape=jax.ShapeDtypeStruct((b, nh, HEAD_DIM, t), BF16),
        compiler_params=_cparams(3),
        name="fox_online" if online else "fox",
    )(qbT, kB, vbT)


def _rwkv_prep(proj, tm, t, mu_ref, w0_ref, w2_ref, a0_ref, a2_ref, g2_ref, kk_ref, ka_ref, rk_ref, e_ref,
               rt_ref, at_ref, bt_ref, kt_ref, bh_ref, kh_ref, v_ref, wc_ref, bonus_ref, g_ref, sbuf):
    @pl.when(t == 0)
    def _():
        sbuf[0:8, :] = jnp.zeros((8, COLS_C), F32)

    def shifted(p, start):
        cols = slice(start, start + p.shape[1])
        sbuf[8:8 + tm, cols] = p
        prev = sbuf[7:7 + tm, cols]
        sbuf[0:8, cols] = p[tm - 8:tm, :]
        return p + (prev - p) * mu_ref[:, cols]

    c = WIDTH_C
    p_lo = proj(3 * c, COLS_C - 3 * c)
    p_k = proj(c, c)
    u_lo = shifted(p_lo, 3 * c)
    yield
    w_lo = u_lo[:, 0:DECAY_LORA]
    a_lo = u_lo[:, DECAY_LORA:DECAY_LORA + AAA_LORA]
    g_lo = u_lo[:, DECAY_LORA + AAA_LORA:]
    w = w0_ref[...] + _dot(jnp.tanh(w_lo).astype(BF16), w2_ref[...])
    w = -_softplus(-w) - 0.5
    ld = -jnp.exp(w)
    p_r = proj(0, c)
    yield
    a = _sigmoid(a0_ref[...] + _dot(a_lo.astype(BF16), a2_ref[...]))
    g_ref[0] = _dot(_sigmoid(g_lo).astype(BF16), g2_ref[...]).astype(BF16)
    k = shifted(p_k, c)
    p_v = proj(2 * c, c)
    yield

    e = e_ref[...]
    kk = k * kk_ref[...]
    nrm2 = _dot((kk * kk).astype(BF16), e)
    kkn = kk * lax.rsqrt(jnp.maximum(nrm2, 1e-24))
    k2 = k * (1.0 + (a - 1.0) * ka_ref[...])
    kka = kkn * a
    r = shifted(p_r, 0)
    v = shifted(p_v, 2 * c)
    v_ref[0] = v.astype(BF16)
    bonus_ref[0] = (_dot((r * k2 * rk_ref[...]).astype(BF16), e) * v).astype(BF16)
    yield

    ri = lax.broadcasted_iota(jnp.int32, (2 * CHUNK, CHUNK), 0)
    ci = lax.broadcasted_iota(jnp.int32, (2 * CHUNK, CHUNK), 1)
    tri = ((ci <= ri) | (ri >= CHUNK)).astype(BF16)
    tri2 = jnp.concatenate([tri, tri], axis=1)
    ld_hi, ld_lo = _split2(ld)
    for cc in range(tm // CHUNK):
        sl = slice(cc * CHUNK, (cc + 1) * CHUNK)
        cum = _dot(tri2, jnp.concatenate([ld_hi[sl], ld_lo[sl]], axis=0))
        lc = cum[0:CHUNK]
        tot = cum[CHUNK:2 * CHUNK]
        e_neg = jnp.exp(-lc)
        e_rem = jnp.exp(tot - lc)
        rt_ref[0, sl, :] = (r[sl] * jnp.exp(lc)).astype(BF16)
        at_ref[0, sl, :] = (-kkn[sl] * jnp.exp(lc - ld[sl])).astype(BF16)
        bt_ref[0, sl, :] = (kka[sl] * e_neg).astype(BF16)
        kt_ref[0, sl, :] = (k2[sl] * e_neg).astype(BF16)
        bh_ref[0, sl, :] = (kka[sl] * e_rem).astype(BF16)
        kh_ref[0, sl, :] = (k2[sl] * e_rem).astype(BF16)
        if cc % 2 == 1:
            yield
    seg = (lax.broadcasted_iota(jnp.int32, (16, tm), 1) // CHUNK
           == lax.broadcasted_iota(jnp.int32, (16, tm), 0)).astype(BF16)
    tots3 = jnp.concatenate(_split3(_dot(seg, ld_hi) + _dot(seg, ld_lo)), axis=0)
    per_blk = RWKV_TILE // CHUNK
    for blk in range(tm // RWKV_TILE):
        put = (lax.broadcasted_iota(jnp.int32, (48, LANES), 0) % 16
               == lax.broadcasted_iota(jnp.int32, (48, LANES), 1) + blk * per_blk).astype(BF16)
        wc_ref[0, blk] = jnp.exp(_dot_tn(tots3, put))


def _bmm(a, b):
    return lax.dot_general(a, b, (((2,), (1,)), ((0,), (0,))), preferred_element_type=F32)


def _bmm_nt(a, b):
    return lax.dot_general(a, b, (((2,), (2,)), ((0,), (0,))), preferred_element_type=F32)


def _bmm_tn(a, b):
    return lax.dot_general(a, b, (((1,), (1,)), ((0,), (0,))), preferred_element_type=F32)


def _rwkv_chunk_body(rt_ref, at_ref, bt_ref, kt_ref, bh_ref, kh_ref, v_ref, wc_ref, bonus_ref, g_ref,
                     lg_ref, lb_ref, e_ref, o_ref, h_ref):
    rows, tm = rt_ref.shape[0], rt_ref.shape[1]
    nc = tm // CHUNK
    ng = N_HEADS_C // RWKV_GROUP
    gl = RWKV_GROUP * HEAD_DIM
    units = [(r, g) for r in range(rows) for g in range(ng)]
    nu = len(units)
    t = pl.program_id(1)

    @pl.when(t == 0)
    def _():
        h_ref[...] = jnp.zeros_like(h_ref)

    lane_head = lax.broadcasted_iota(jnp.int32, (1, CHUNK, gl), 2) // HEAD_DIM
    ri = lax.broadcasted_iota(jnp.int32, (1, 2 * CHUNK, gl), 1)
    ci = lax.broadcasted_iota(jnp.int32, (1, 2 * CHUNK, gl), 2) & (CHUNK - 1)
    rr = ri & (CHUNK - 1)
    keep = (rr > ci) | ((ri >= CHUNK) & (rr == ci))
    eye = (lax.broadcasted_iota(jnp.int32, (1, CHUNK, gl), 1)
           == (lax.broadcasted_iota(jnp.int32, (1, CHUNK, gl), 2) & (CHUNK - 1)))
    same_head = (lax.broadcasted_iota(jnp.int32, (1, gl, gl), 1) // HEAD_DIM
                 == lax.broadcasted_iota(jnp.int32, (1, gl, gl), 2) // HEAD_DIM)

    def bdiag(x):
        return jnp.concatenate([jnp.where(lane_head == hh, x, jnp.zeros_like(x)) for hh in range(RWKV_GROUP)],
                               axis=1)

    def chunk_local(cs, out):
        tile = lambda ref: jnp.stack([ref[r, c * CHUNK:(c + 1) * CHUNK, g * gl:(g + 1) * gl]
                                      for c in cs for r, g in units])
        rt, at, bt, kt, bh, kh, v = (tile(r) for r in (rt_ref, at_ref, bt_ref, kt_ref, bh_ref, kh_ref, v_ref))
        ar = jnp.concatenate([at, rt], axis=1)
        sb = jnp.where(keep, _bmm_nt(ar, bdiag(bt)), 0.0)
        sk = jnp.where(keep, _bmm_nt(ar, bdiag(kt)), 0.0).astype(BF16)
        m_rb = sb[:, CHUNK:].astype(BF16)
        yield
        a = sb[:, :CHUNK]
        tinv = jnp.where(eye, 1.0, 0.0) + a
        ab = a.astype(BF16)
        x = _bmm(ab, bdiag(ab))
        yield
        for _ in range(4):
            xb = x.astype(BF16)
            r = _bmm(jnp.concatenate([tinv.astype(BF16), xb], axis=1), bdiag(xb))
            tinv = tinv + r[:, :CHUNK]
            x = r[:, CHUNK:]
            yield
        tinv = (tinv + _bmm(tinv.astype(BF16), bdiag(x.astype(BF16)))).astype(BF16)
        bdv = bdiag(v)
        av = _bmm(sk[:, :CHUNK], bdv)
        yield
        p = _bmm(tinv, bdiag(at)).astype(BF16)
        u0 = _bmm(tinv, bdiag(av.astype(BF16)))
        yield
        q = (rt.astype(F32) + _bmm(m_rb, bdiag(p))).astype(BF16)
        y0 = _bmm(m_rb, bdiag(u0.astype(BF16))) + _bmm(sk[:, CHUNK:], bdv)
        parts = (jnp.concatenate([q, p], axis=1), jnp.concatenate([y0, u0], axis=1),
                 jnp.concatenate([bh, kh], axis=1), v)
        out += [tuple(z[n * nu:(n + 1) * nu] for z in parts) for n in range(len(cs))]
        yield

    wct = [wc_ref[r, 0] for r in range(rows)]
    state = [h_ref[...]]
    ys = []
    local = []

    def recur(c):
        qp, yu0, bk, v = local[c]
        yu = _bmm(qp, state[0].astype(BF16)) + yu0
        ys.append(yu[:, :CHUNK])
        upd = _bmm_tn(bk, jnp.concatenate([yu[:, CHUNK:].astype(BF16), v], axis=1))
        wcol = jnp.stack([wct[r][g * gl:(g + 1) * gl, c:c + 1] for r, g in units])
        state[0] = jnp.where(same_head, upd, 0.0) + wcol * state[0]

    assert sorted(c for cs in RWKV_SCHEDULE for c in cs) == list(range(nc))
    done = 0
    for cs in RWKV_SCHEDULE:
        pending = list(range(done, len(local)))
        slots = {(k + 1) * RWKV_STAGES // (len(pending) + 1): c for k, c in enumerate(pending)}
        for n, _ in enumerate(chunk_local(list(cs), local), start=1):
            if n in slots:
                recur(slots[n])
                done += 1
    for c in range(done, nc):
        recur(c)
    h_ref[...] = state[0]

    y = jnp.concatenate([jnp.concatenate([yc[r * ng + g] for g in range(ng)], axis=-1)
                         for r in range(rows) for yc in ys], axis=0)
    e = e_ref[...]

    def head_mean(z):
        return _dot(z.astype(BF16), e) * (1.0 / HEAD_DIM)

    d = y - head_mean(y)
    yn = d * lax.rsqrt(head_mean(d * d) + LNX_EPS)
    flat = lambda ref: ref[...].astype(F32).reshape(rows * tm, WIDTH_C)
    out = (yn * lg_ref[...] + lb_ref[...] + flat(bonus_ref)) * flat(g_ref)
    o_ref[...] = out.astype(BF16).reshape(rows, tm, WIDTH_C)


def _rwkv_chunk(rt, at, bt, kt, bh, kh, v, wc, bonus, g, lnx_g, lnx_b, e):
    b, t, _ = rt.shape
    tm = RWKV_TILE
    rows = math.gcd(RWKV_ROWS, b)
    gl = RWKV_GROUP * HEAD_DIM
    const = lambda shape: pl.BlockSpec(shape, lambda i, j: (0,) * len(shape))
    tok = lambda: pl.BlockSpec((rows, tm, WIDTH_C), lambda i, j: (i, j, 0))
    return pl.pallas_call(
        _rwkv_chunk_body,
        grid=(b // rows, t // tm),
        in_specs=[tok(), tok(), tok(), tok(), tok(), tok(), tok(),
                  pl.BlockSpec((rows, 1, WIDTH_C, LANES), lambda i, j: (i, j, 0, 0)),
                  tok(), tok(), const(lnx_g.shape), const(lnx_b.shape), const(e.shape)],
        out_specs=tok(),
        out_shape=jax.ShapeDtypeStruct((b, t, WIDTH_C), BF16),
        scratch_shapes=[pltpu.VMEM((rows * (N_HEADS_C // RWKV_GROUP), gl, gl), F32)],
        compiler_params=_cparams(2),
        name="rwkv_chunk",
    )(rt, at, bt, kt, bh, kh, v, wc, bonus, g, lnx_g, lnx_b, e)


def _outproj_body(x_ref, ya_ref, yb_ref, yc_ref, wa_ref, wb_ref, wc_ref, o_ref):
    acc = _dot_tn(ya_ref[0], wa_ref[...])
    acc = acc + _dot_tn(yb_ref[0], wb_ref[...])
    acc = acc + _dot(yc_ref[0], wc_ref[...])
    o_ref[0] = x_ref[0] + acc


def _outproj(x, yaT, ybT, yc, wa, wb, wc):
    b, t, d = x.shape
    tm = math.gcd(t, OUTPROJ_TILE)
    const = lambda shape: pl.BlockSpec(shape, lambda i, j: (0,) * len(shape))
    return pl.pallas_call(
        _outproj_body,
        grid=(b, t // tm),
        in_specs=[pl.BlockSpec((1, tm, d), lambda i, j: (i, j, 0)),
                  pl.BlockSpec((1, WIDTH_A, tm), lambda i, j: (i, 0, j)),
                  pl.BlockSpec((1, WIDTH_B, tm), lambda i, j: (i, 0, j)),
                  pl.BlockSpec((1, tm, WIDTH_C), lambda i, j: (i, j, 0)),
                  const(wa.shape), const(wb.shape), const(wc.shape)],
        out_specs=pl.BlockSpec((1, tm, d), lambda i, j: (i, j, 0)),
        out_shape=jax.ShapeDtypeStruct((b, t, d), F32),
        compiler_params=_cparams(2),
        name="outproj",
    )(x, yaT, ybT, yc, wa, wb, wc)


def _ffn_body(x_ref, g_ref, wg_ref, wv_ref, cw_ref, cb_ref, wd_ref, o_ref, gbuf):
    tm = x_ref.shape[1]
    t = pl.program_id(1)

    @pl.when(t == 0)
    def _():
        gbuf[0:8, :] = jnp.zeros((8, D_FF), F32)

    x = x_ref[0]
    ms = jnp.mean(x * x, axis=-1, keepdims=True)
    h = (x * lax.rsqrt(ms + RMS_EPS) * g_ref[...]).astype(BF16)
    acc = x
    for lo, hi in zip(FF_SPLITS[:-1], FF_SPLITS[1:]):
        gate = _dot(h, wg_ref[:, lo:hi])
        val = _dot(h, wv_ref[:, lo:hi])
        gbuf[8:8 + tm, lo:hi] = gate
        g1 = gbuf[7:7 + tm, lo:hi]
        g2 = gbuf[6:6 + tm, lo:hi]
        gbuf[0:8, lo:hi] = gate[tm - 8:tm, :]
        conv = cb_ref[:, lo:hi] + g2 * cw_ref[0:1, lo:hi] + g1 * cw_ref[1:2, lo:hi] + gate * cw_ref[2:3, lo:hi]
        act = (conv * _sigmoid(conv) * val).astype(BF16)
        acc = acc + _dot(act, wd_ref[lo:hi, :])
    o_ref[0] = acc


def _ffn(x, g, wg, wv, cw, cb, wd):
    b, t, d = x.shape
    tm = TOK_TILE
    const = lambda shape: pl.BlockSpec(shape, lambda i, j: (0,) * len(shape),
                                       pipeline_mode=pl.Buffered(1))
    return pl.pallas_call(
        _ffn_body,
        grid=(b, t // tm),
        in_specs=[pl.BlockSpec((1, tm, d), lambda i, j: (i, j, 0)),
                  const(g.shape), const(wg.shape), const(wv.shape), const(cw.shape), const(cb.shape),
                  const(wd.shape)],
        out_specs=pl.BlockSpec((1, tm, d), lambda i, j: (i, j, 0)),
        out_shape=jax.ShapeDtypeStruct((b, t, d), F32),
        scratch_shapes=[pltpu.VMEM((tm + 8, D_FF), F32)],
        compiler_params=_cparams(2),
        name="ffn",
    )(x, g, wg, wv, cw, cb, wd)


def _pad_heads_cols(w, nh):
    d = w.shape[0]
    w = w.reshape(d, nh, HEAD_DIM)
    return jnp.pad(w, ((0, 0), (0, 0), (0, LANES - HEAD_DIM))).reshape(d, nh * LANES)


def _qk_bound(gq, gk):
    scale = HEAD_DIM ** -0.5
    return 1.01 * HEAD_DIM * scale * jnp.max(jnp.abs(gq)) * jnp.max(jnp.abs(gk)) + 0.05


def _layer_params(l, w_in, q_norm_a, k_norm_a, rel_bias, q_norm_b, k_norm_b, forget_bias):
    w = w_in[l]
    scale = HEAD_DIM ** -0.5
    a0, b0 = 0, 3 * WIDTH_A
    c0 = b0 + 3 * WIDTH_B + N_HEADS_B
    qa, ka, va = (w[:, a0 + i * WIDTH_A:a0 + (i + 1) * WIDTH_A] for i in range(3))
    qb, kb, vb = (w[:, b0 + i * WIDTH_B:b0 + (i + 1) * WIDTH_B] for i in range(3))
    fg = w[:, b0 + 3 * WIDTH_B:c0]
    wc = w[:, c0:]
    wn = jnp.concatenate([_pad_heads_cols(kb, N_HEADS_B),
                          jnp.pad(fg, ((0, 0), (0, LANES - N_HEADS_B))), wc], axis=1).astype(BF16)
    pad_v = lambda v, nh: jnp.pad(v.reshape(-1, nh, HEAD_DIM),
                                  ((0, 0), (0, 0), (0, VB_ROWS - HEAD_DIM))).reshape(-1, nh * VB_ROWS)
    assert QB_ROWS == VB_ROWS
    wt = jnp.concatenate([qa, ka, pad_v(va, N_HEADS_A), pad_v(qb, N_HEADS_B), pad_v(vb, N_HEADS_B),
                          jnp.pad(fg, ((0, 0), (0, 16 - N_HEADS_B)))], axis=1).T.astype(BF16)
    pad = LANES - HEAD_DIM
    gka = k_norm_a[l].reshape(HEAD_DIM, 1)
    gkb = jnp.pad(k_norm_b[l], (0, pad)).reshape(1, LANES)
    gqa = (q_norm_a[l] * (scale * LOG2E)).reshape(HEAD_DIM, 1)
    gqb = jnp.pad(q_norm_b[l] * (scale * LOG2E), (0, QB_ROWS - HEAD_DIM)).reshape(QB_ROWS, 1)
    fbn = jnp.pad(forget_bias[l], (0, LANES - N_HEADS_B)).reshape(1, LANES)
    fbt = jnp.pad(forget_bias[l], (0, 16 - N_HEADS_B)).reshape(16, 1)
    shift_b = (LOG2E * _qk_bound(q_norm_b[l], k_norm_b[l])).reshape(1, 1)
    bound_a = _qk_bound(q_norm_a[l], k_norm_a[l])
    tab = rel_bias[l]
    shift_a = LOG2E * (bound_a + jnp.max(tab, axis=1))
    depth_a = jnp.max(shift_a + LOG2E * (bound_a - tab[:, MAX_REL]))
    return wn, wt, gka, gkb, gqa, gqb, fbn, fbt, shift_b, shift_a, depth_a


def _selectors():
    import numpy as np
    selk = np.zeros((LANES, N_HEADS_B * LANES), np.float32)
    selq = np.zeros((N_HEADS_B * QB_ROWS, 3 * 16), np.float32)
    for p in range(3):
        for hh in range(N_HEADS_B):
            selk[p * N_HEADS_B + hh, hh * LANES + HEAD_DIM + 3 + p] = -1.0
            selq[hh * QB_ROWS + HEAD_DIM + p, 16 * p + hh] = 1.0
    e = np.kron(np.eye(N_HEADS_C, dtype=np.float32), np.ones((HEAD_DIM, HEAD_DIM), np.float32))
    return jnp.asarray(selk, BF16), jnp.asarray(selq, BF16), jnp.asarray(e, BF16)


def kernel(x, mix_norm_g, w_in, q_norm_a, k_norm_a, rel_bias, q_norm_b, k_norm_b, forget_bias, shift_mu, w0, w2,
           a0, a2, g2, k_k, k_a, r_k, lnx_g, lnx_b, w_out, ffn_norm_g, w_up, conv_w, conv_b, w_down):
    depth = w_in.shape[0]
    b, t, d = x.shape
    selk, selq, e = _selectors()
    row = lambda v: v.reshape(1, -1)
    for l in range(depth):
        wn, wt, gka, gkb, gqa, gqb, fbn, fbt, shift_b, shift_a, depth_a = _layer_params(
            l, w_in, q_norm_a, k_norm_a, rel_bias, q_norm_b, k_norm_b, forget_bias)
        rwkv_params = (row(shift_mu[l]), row(w0[l]), w2[l].astype(BF16), row(a0[l]), a2[l].astype(BF16),
                       g2[l].astype(BF16), row(k_k[l]), row(k_a[l]), row(r_k[l]), e)
        (qaT, vaT, kA, qbT, vbT, kB, rt, at, bt, kt, bh, kh, v, wcum, bonus, g) = _inproj(
            x, row(mix_norm_g[l]), wn, wt, gka, gkb, gqa, gqb, fbn, fbt, selk, selq, shift_b, rwkv_params)
        bias = _relbias(rel_bias[l], shift_a)
        yaT = lax.cond(depth_a <= 2 * FAST_MAX_SHIFT,
                       functools.partial(_attn_a, online=False), functools.partial(_attn_a, online=True),
                       qaT, kA, vaT, bias)
        ybT = lax.cond(shift_b[0, 0] <= FAST_MAX_SHIFT,
                       functools.partial(_fox, online=False), functools.partial(_fox, online=True),
                       qbT, kB, vbT)
        yc = _rwkv_chunk(rt, at, bt, kt, bh, kh, v, wcum, bonus, g, row(lnx_g[l]), row(lnx_b[l]), e)
        wo = w_out[l].astype(BF16)
        x = _outproj(x, yaT.reshape(b, WIDTH_A, t), ybT.reshape(b, WIDTH_B, t), yc,
                     wo[:WIDTH_A], wo[WIDTH_A:WIDTH_A + WIDTH_B], wo[WIDTH_A + WIDTH_B:])
        wu = w_up[l].astype(BF16)
        x = _ffn(x, row(ffn_norm_g[l]), wu[:, :D_FF], wu[:, D_FF:], conv_w[l], row(conv_b[l]),
                 w_down[l].astype(BF16))
    return x
```

```python
import functools
import math

import jax
import jax.numpy as jnp
from jax import lax
from jax.experimental import pallas as pl
from jax.experimental.pallas import tpu as pltpu

F32 = jnp.float32
BF16 = jnp.bfloat16

D_MODEL = 1024
HEAD_DIM = 64
CHUNK = 64
LEFT_CHUNKS = 8
MAX_REL = 128
N_HEADS_A = 4
N_HEADS_B = 4
N_HEADS_C = 8
WIDTH_A = N_HEADS_A * HEAD_DIM
WIDTH_B = N_HEADS_B * HEAD_DIM
WIDTH_C = N_HEADS_C * HEAD_DIM
DECAY_LORA = 64
AAA_LORA = 64
GATE_LORA = 128
COLS_C = 3 * WIDTH_C + DECAY_LORA + AAA_LORA + GATE_LORA
D_FF = 2816
RMS_EPS = 1e-6
LNX_EPS = 64e-5
NEG_INF = -1e30

LANES = 128
TOK_TILE = 512
OUTPROJ_TILE = 2048
ATT_A_TILE = 256
ATT_A_WIN = 3 * ATT_A_TILE
ATT_A_AHEAD = 3
FOX_TILE = 512
FOX_HEADS = 2
FOX_GROUP = 8
FOX_AHEAD = 2
RWKV_GROUP = 4
RWKV_TILE = 256
RWKV_ROWS = 4
RWKV_SCHEDULE = ((0, 1), (2, 3))
RWKV_STAGES = 9
FF_SPLITS = (0, 1024, 2816)
VMEM_LIMIT = 56 * 1024 * 1024

PN_KB, PN_F, PN_C = 0, 512, 640
PN_COLS = PN_C + COLS_C
VB_ROWS = 80
PT_QA = 0
PT_KA = PT_QA + WIDTH_A
PT_VA = PT_KA + WIDTH_A
PT_QB = PT_VA + N_HEADS_A * VB_ROWS
QB_ROWS = 80
PT_VB = PT_QB + N_HEADS_B * QB_ROWS
PT_F = PT_VB + N_HEADS_B * VB_ROWS
PT_ROWS = PT_F + 16
LOG2E = 1.4426950408889634
FAST_MAX_SHIFT = 40.0


def _cparams(n_axes):
    return pltpu.CompilerParams(dimension_semantics=("arbitrary",) * n_axes,
                                vmem_limit_bytes=VMEM_LIMIT)


def _split3(x):
    hi = x.astype(BF16)
    r1 = x - hi.astype(F32)
    mid = r1.astype(BF16)
    lo = (r1 - mid.astype(F32)).astype(BF16)
    return hi, mid, lo


def _split2(x):
    hi = x.astype(BF16)
    lo = (x - hi.astype(F32)).astype(BF16)
    return hi, lo


def _dot(a, b):
    return jnp.dot(a, b, preferred_element_type=F32)


def _dot_nt(a, b):
    return lax.dot_general(a, b, (((1,), (1,)), ((), ())), preferred_element_type=F32)


def _dot_tn(a, b):
    return lax.dot_general(a, b, (((0,), (0,)), ((), ())), preferred_element_type=F32)


def _log_sigmoid(x):
    return jnp.minimum(x, 0.0) - jnp.log(1.0 + jnp.exp(-jnp.abs(x)))


def _softplus(x):
    return jnp.maximum(x, 0.0) + jnp.log(1.0 + jnp.exp(-jnp.abs(x)))


def _sigmoid(x):
    return 1.0 / (1.0 + jnp.exp(-x))


def _inproj_body(x_ref, g_ref, wn_ref, wt_ref, gka_ref, gkb_ref, gqa_ref, gqb_ref, fbn_ref, fbt_ref,
                 selk_ref, selq_ref, shift_ref, *rest):
    rwkv_params, rest = rest[:10], rest[10:]
    qaT_ref, vaT_ref, kA_ref, qbT_ref, vbT_ref, kB_ref = rest[:6]
    rwkv_outs = rest[6:16]
    carry_n, carry_t, sbuf = rest[16:]
    tm = x_ref.shape[1]
    t = pl.program_id(1)

    @pl.when(t == 0)
    def _():
        carry_n[...] = jnp.zeros_like(carry_n)
        carry_t[...] = jnp.zeros_like(carry_t)

    x = x_ref[0]
    ms = jnp.mean(x * x, axis=-1, keepdims=True)
    h = (x * lax.rsqrt(ms + RMS_EPS) * g_ref[...]).astype(BF16)
    rows_t = lambda start, n: _dot_nt(wt_ref[start:start + n, :], h)
    cols_n = lambda start, n: _dot(h, wn_ref[:, start:start + n])
    row_v = lax.broadcasted_iota(jnp.int32, (VB_ROWS, 1), 0)
    ones_v = jnp.where(row_v == HEAD_DIM, 1.0, 0.0)
    val = {}

    def norm_rows(p, gain_ref):
        p = p.reshape(N_HEADS_A, HEAD_DIM, tm)
        return (p * lax.rsqrt(jnp.mean(p * p, axis=1, keepdims=True) + RMS_EPS) * gain_ref[...][None]).astype(BF16)

    def lane_packed3(z):
        z_hi, z_mid, z_lo = _split3(z)
        return (z_hi.astype(F32) + pltpu.roll(z_mid.astype(F32), N_HEADS_B, axis=1)
                + pltpu.roll(z_lo.astype(F32), 2 * N_HEADS_B, axis=1)).astype(BF16)

    def gates_epilogue(p):
        pn_f, pt_f = p
        lane_n = lax.broadcasted_iota(jnp.int32, (1, LANES), 1)
        lf_n = jnp.where(lane_n < N_HEADS_B, LOG2E * _log_sigmoid(pn_f + fbn_ref[...]), 0.0)
        row_t = lax.broadcasted_iota(jnp.int32, (16, 1), 0)
        lf_t = jnp.where(row_t < N_HEADS_B, LOG2E * _log_sigmoid(pt_f + fbt_ref[...]), 0.0)
        ri = lax.broadcasted_iota(jnp.int32, (tm, tm), 0)
        ci = lax.broadcasted_iota(jnp.int32, (tm, tm), 1)
        low = (ci <= ri).astype(BF16)
        upp = (ri <= ci).astype(BF16)
        cum3 = _dot(low, lane_packed3(lf_n))
        cum = cum3 + pltpu.roll(cum3, LANES - N_HEADS_B, axis=1) + pltpu.roll(cum3, LANES - 2 * N_HEADS_B, axis=1)
        val["c_n"] = jnp.where(lane_n < N_HEADS_B, cum, 0.0) + carry_n[0:1, :]
        ht, mt, lt = _split3(lf_t)
        val["c_t"] = _dot(ht, upp) + _dot(mt, upp) + _dot(lt, upp) + carry_t[:, 0:1]
        carry_n[...] = carry_n[...] + jnp.sum(lf_n, axis=0, keepdims=True)
        carry_t[...] = carry_t[...] + jnp.sum(lf_t, axis=1, keepdims=True)

    def kb_epilogue(pn_kb):
        kaug = _dot(lane_packed3(val["c_n"]), selk_ref[...])
        lane_k = lax.broadcasted_iota(jnp.int32, (1, LANES), 1)
        ones_k = jnp.where((lane_k >= HEAD_DIM) & (lane_k < HEAD_DIM + 3), 1.0, 0.0)
        for hh in range(N_HEADS_B):
            k = pn_kb[:, LANES * hh:LANES * (hh + 1)]
            msk = jnp.sum(k * k, axis=-1, keepdims=True) * (1.0 / HEAD_DIM)
            kn = k * lax.rsqrt(msk + RMS_EPS) * gkb_ref[...]
            kB_ref[0, hh] = (kn + kaug[:, LANES * hh:LANES * (hh + 1)] + ones_k).astype(BF16)

    def qb_epilogue(pt_qb):
        c3 = jnp.concatenate(_split3(val["c_t"] - shift_ref[...]), axis=0)
        qaug = _dot(selq_ref[...], c3)
        qb = pt_qb.reshape(N_HEADS_B, QB_ROWS, tm)
        msq = jnp.sum(qb * qb, axis=1, keepdims=True) * (1.0 / HEAD_DIM)
        row_q = lax.broadcasted_iota(jnp.int32, (QB_ROWS, 1), 0)
        ones_q = jnp.where((row_q >= HEAD_DIM + 3) & (row_q < HEAD_DIM + 6), 1.0, 0.0)
        qn = qb * lax.rsqrt(msq + RMS_EPS) * gqb_ref[...][None]
        qbT_ref[0] = (qn + qaug.reshape(N_HEADS_B, QB_ROWS, tm) + ones_q[None]).astype(BF16)

    def store(ref, value):
        ref[...] = value

    sections = [
        (lambda: rows_t(PT_QA, WIDTH_A), lambda p: store(qaT_ref, norm_rows(p, gqa_ref)[None])),
        (lambda: rows_t(PT_KA, WIDTH_A), lambda p: store(kA_ref, norm_rows(p, gka_ref)[None])),
        (lambda: rows_t(PT_VA, N_HEADS_A * VB_ROWS),
         lambda p: store(vaT_ref, (p.reshape(N_HEADS_A, VB_ROWS, tm) + ones_v[None]).astype(BF16)[None])),
        (lambda: (cols_n(PN_F, LANES), rows_t(PT_F, 16)), gates_epilogue),
        (lambda: rows_t(PT_VB, N_HEADS_B * VB_ROWS),
         lambda p: store(vbT_ref, (p.reshape(N_HEADS_B, VB_ROWS, tm) + ones_v[None]).astype(BF16)[None, :, None])),
        (lambda: cols_n(PN_KB, N_HEADS_B * LANES), kb_epilogue),
        (lambda: rows_t(PT_QB, N_HEADS_B * QB_ROWS), qb_epilogue),
    ]
    slots = _rwkv_prep(lambda start, n: cols_n(PN_C + start, n), tm, t, *rwkv_params, *rwkv_outs, sbuf)
    pending = None
    for mm, epilogue in sections:
        p = mm()
        if pending is not None:
            pending[1](pending[0])
        pending = (p, epilogue)
        next(slots, None)
    pending[1](pending[0])
    for _ in slots:
        pass


def _inproj(x, g, wn, wt, gka, gkb, gqa, gqb, fbn, fbt, selk, selq, shift, rwkv_params):
    b, t, d = x.shape
    tm = TOK_TILE
    nt = t // tm
    const = lambda shape: pl.BlockSpec(shape, lambda i, j: (0,) * len(shape))
    tok = lambda: pl.BlockSpec((1, tm, WIDTH_C), lambda i, j: (i, j, 0))
    bf = jax.ShapeDtypeStruct((b, t, WIDTH_C), BF16)
    out_shape = (
        jax.ShapeDtypeStruct((b, N_HEADS_A, HEAD_DIM, t), BF16),
        jax.ShapeDtypeStruct((b, N_HEADS_A, VB_ROWS, t), BF16),
        jax.ShapeDtypeStruct((b, N_HEADS_A, HEAD_DIM, t), BF16),
        jax.ShapeDtypeStruct((b, N_HEADS_B, QB_ROWS, t), BF16),
        jax.ShapeDtypeStruct((b, N_HEADS_B, nt, VB_ROWS, tm), BF16),
        jax.ShapeDtypeStruct((b, N_HEADS_B, t, LANES), BF16),
        bf, bf, bf, bf, bf, bf, bf,
        jax.ShapeDtypeStruct((b, t // RWKV_TILE, WIDTH_C, LANES), F32),
        bf, bf,
    )
    out_specs = (
        pl.BlockSpec((1, N_HEADS_A, HEAD_DIM, tm), lambda i, j: (i, 0, 0, j)),
        pl.BlockSpec((1, N_HEADS_A, VB_ROWS, tm), lambda i, j: (i, 0, 0, j)),
        pl.BlockSpec((1, N_HEADS_A, HEAD_DIM, tm), lambda i, j: (i, 0, 0, j)),
        pl.BlockSpec((1, N_HEADS_B, QB_ROWS, tm), lambda i, j: (i, 0, 0, j)),
        pl.BlockSpec((1, N_HEADS_B, 1, VB_ROWS, tm), lambda i, j: (i, 0, j, 0, 0)),
        pl.BlockSpec((1, N_HEADS_B, tm, LANES), lambda i, j: (i, 0, j, 0)),
        tok(), tok(), tok(), tok(), tok(), tok(), tok(),
        pl.BlockSpec((1, tm // RWKV_TILE, WIDTH_C, LANES), lambda i, j: (i, j, 0, 0)),
        tok(), tok(),
    )
    in_specs = [
        pl.BlockSpec((1, tm, d), lambda i, j: (i, j, 0)),
        const((1, d)), const(wn.shape), const(wt.shape),
        const(gka.shape), const(gkb.shape), const(gqa.shape), const(gqb.shape),
        const(fbn.shape), const(fbt.shape), const(selk.shape), const(selq.shape), const(shift.shape),
    ] + [const(p.shape) for p in rwkv_params]
    return pl.pallas_call(
        _inproj_body,
        grid=(b, nt),
        in_specs=in_specs,
        out_specs=out_specs,
        out_shape=out_shape,
        scratch_shapes=[pltpu.VMEM((8, LANES), F32), pltpu.VMEM((16, LANES), F32),
                        pltpu.VMEM((tm + 8, COLS_C), F32)],
        compiler_params=_cparams(2),
        name="inproj",
    )(x, g, wn, wt, gka, gkb, gqa, gqb, fbn, fbt, selk, selq, shift, *rwkv_params)


def _relbias_body(tab_ref, shift_ref, o_ref):
    hh = pl.program_id(0)
    kj = lax.broadcasted_iota(jnp.int32, (ATT_A_WIN, ATT_A_TILE), 0)
    qi = lax.broadcasted_iota(jnp.int32, (ATT_A_WIN, ATT_A_TILE), 1)
    rel = jnp.clip(kj - LEFT_CHUNKS * CHUNK - qi, -MAX_REL, MAX_REL) + MAX_REL

    def body(r, acc):
        return jnp.where(rel == r, tab_ref[hh, r], acc)

    bias = lax.fori_loop(0, 2 * MAX_REL + 1, body, jnp.zeros((ATT_A_WIN, ATT_A_TILE), F32))
    kc = kj // CHUNK
    qc = qi // CHUNK
    band = (kc >= qc) & (kc <= qc + LEFT_CHUNKS)
    o_ref[0] = jnp.where(band, LOG2E * bias - shift_ref[hh], NEG_INF)


def _relbias(tab, shift):
    return pl.pallas_call(
        _relbias_body,
        grid=(N_HEADS_A,),
        in_specs=[pl.BlockSpec(memory_space=pltpu.SMEM), pl.BlockSpec(memory_space=pltpu.SMEM)],
        out_specs=pl.BlockSpec((1, ATT_A_WIN, ATT_A_TILE), lambda i: (i, 0, 0)),
        out_shape=jax.ShapeDtypeStruct((N_HEADS_A, ATT_A_WIN, ATT_A_TILE), F32),
        compiler_params=_cparams(1),
        name="relbias",
    )(tab, shift)


def _attn_a_body(q_ref, k0_ref, k1_ref, k2_ref, v0_ref, v1_ref, v2_ref, bias_ref, o_ref, *, online):
    i = pl.program_id(1)
    tq = ATT_A_TILE
    kj = lax.broadcasted_iota(jnp.int32, (tq, 1), 0)

    k_refs = (k0_ref, k1_ref, k2_ref)
    v_refs = (v0_ref, v1_ref, v2_ref)

    def run(mask_padding):
        def scores(hh):
            q = q_ref[0, hh]
            ss = []
            for d, k_ref in enumerate(k_refs):
                s = _dot_tn(k_ref[0, hh], q) + bias_ref[hh, d * tq:(d + 1) * tq, :]
                if mask_padding:
                    s = jnp.where(kj + (i - 2 + d) * tq >= 0, s, NEG_INF)
                ss.append(s)
            return ss

        def finish(hh, ss):
            if online:
                m = functools.reduce(jnp.maximum, [jnp.max(s, axis=0, keepdims=True) for s in ss])
                ss = [s - m for s in ss]
            acc = jnp.zeros((VB_ROWS, tq), F32)
            for s, v_ref in zip(ss, v_refs):
                acc = acc + _dot(v_ref[0, hh], jnp.exp2(s).astype(BF16))
            o_ref[0, hh] = (acc[0:HEAD_DIM] / acc[HEAD_DIM:HEAD_DIM + 1]).astype(BF16)

        pending = []
        for hh in range(N_HEADS_A):
            pending.append((hh, scores(hh)))
            if len(pending) > ATT_A_AHEAD:
                finish(*pending.pop(0))
        for item in pending:
            finish(*item)

    pl.when(i < 2)(functools.partial(run, True))
    pl.when(i >= 2)(functools.partial(run, False))


def _attn_a(qaT, kA, vaT, bias, online):
    b, nh, _, t = qaT.shape
    tq = ATT_A_TILE
    kspec = lambda d: pl.BlockSpec((1, nh, HEAD_DIM, tq),
                                   lambda bb, i: (bb, 0, 0, jnp.maximum(i - 2 + d, 0)))
    vspec = lambda d: pl.BlockSpec((1, nh, VB_ROWS, tq),
                                   lambda bb, i: (bb, 0, 0, jnp.maximum(i - 2 + d, 0)))
    return pl.pallas_call(
        functools.partial(_attn_a_body, online=online),
        grid=(b, t // tq),
        in_specs=[pl.BlockSpec((1, nh, HEAD_DIM, tq), lambda bb, i: (bb, 0, 0, i)),
                  kspec(0), kspec(1), kspec(2), vspec(0), vspec(1), vspec(2),
                  pl.BlockSpec((nh, ATT_A_WIN, tq), lambda bb, i: (0, 0, 0))],
        out_specs=pl.BlockSpec((1, nh, HEAD_DIM, tq), lambda bb, i: (bb, 0, 0, i)),
        out_shape=jax.ShapeDtypeStruct((b, nh, HEAD_DIM, t), BF16),
        compiler_params=_cparams(2),
        name="attn_a_online" if online else "attn_a",
    )(qaT, kA, kA, kA, vaT, vaT, vaT, bias)


def _fox_body(q_ref, k_ref, v_ref, o_ref, *, online):
    i = pl.program_id(2)
    tq = FOX_TILE
    nh = q_ref.shape[1]
    q = [jnp.concatenate([q_ref[0, hh], jnp.zeros((LANES - QB_ROWS, tq), BF16)], axis=0) for hh in range(nh)]

    def scores(item):
        j, hh = item
        ks = k_ref[0, hh, pl.ds(pl.multiple_of(j * tq, tq), tq), :]
        return _dot(ks, q[hh])

    def accumulate(item, s, m, acc, diagonal):
        j, hh = item
        vs = v_ref[0, hh, j]
        if diagonal:
            kj = lax.broadcasted_iota(jnp.int32, (tq, tq), 0)
            qi = lax.broadcasted_iota(jnp.int32, (tq, tq), 1)
            causal = kj <= qi
        if online:
            if diagonal:
                s = jnp.where(causal, s, NEG_INF)
            m_new = jnp.maximum(m, jnp.max(s, axis=0, keepdims=True))
            acc = jnp.exp2(m - m_new) * acc
            p = jnp.exp2(s - m_new)
            m = m_new
        else:
            p = jnp.exp2(s)
            if diagonal:
                p = jnp.where(causal, p, 0.0)
        return m, acc + _dot(vs, p.astype(BF16))

    def tiles(js, carry, last_diagonal):
        state = [list(c) for c in carry]
        items = [(j, hh) for j in js for hh in range(nh)]
        ss = []

        def finish(n):
            hh = items[n][1]
            diagonal = last_diagonal and n >= len(items) - nh
            state[hh] = list(accumulate(items[n], ss[n], state[hh][0], state[hh][1], diagonal))

        for n, item in enumerate(items):
            ss.append(scores(item))
            if n >= FOX_AHEAD:
                finish(n - FOX_AHEAD)
        for n in range(max(len(items) - FOX_AHEAD, 0), len(items)):
            finish(n)
        return tuple(tuple(c) for c in state)

    init = tuple((jnp.full((1, tq), NEG_INF, F32), jnp.zeros((VB_ROWS, tq), F32)) for _ in range(nh))
    if online:
        carry = lax.fori_loop(0, i, lambda j, c: tiles([j], c, False), init)
        carry = tiles([i], carry, True)
    else:
        g = FOX_GROUP
        carry = lax.fori_loop(0, i // g, lambda jj, c: tiles([g * jj + n for n in range(g)], c, False), init)
        tails = [functools.partial(lambda c, r: tiles([i - r + n for n in range(r + 1)], c, True), r=r)
                 for r in range(g)]
        carry = lax.switch(i % g, tails, carry)
    for hh, (_, acc) in enumerate(carry):
        o_ref[0, hh] = (acc[0:HEAD_DIM] / acc[HEAD_DIM:HEAD_DIM + 1]).astype(BF16)


def _fox(qbT, kB, vbT, online):
    b, nh, _, t = qbT.shape
    tq = FOX_TILE
    nk = t // tq
    return pl.pallas_call(
        functools.partial(_fox_body, online=online),
        grid=(b, nh // FOX_HEADS, nk),
        in_specs=[pl.BlockSpec((1, FOX_HEADS, QB_ROWS, tq), lambda bb, hh, i: (bb, hh, 0, i)),
                  pl.BlockSpec((1, FOX_HEADS, t, LANES), lambda bb, hh, i: (bb, hh, 0, 0)),
                  pl.BlockSpec((1, FOX_HEADS, nk, VB_ROWS, tq), lambda bb, hh, i: (bb, hh, 0, 0, 0))],
        out_specs=pl.BlockSpec((1, FOX_HEADS, HEAD_DIM, tq), lambda bb, hh, i: (bb, hh, 0, i)),
        out_shape=jax.ShapeDtypeStruct((b, nh, HEAD_DIM, t), BF16),
        compiler_params=_cparams(3),
        name="fox_online" if online else "fox",
    )(qbT, kB, vbT)


def _rwkv_prep(proj, tm, t, mu_ref, w0_ref, w2_ref, a0_ref, a2_ref, g2_ref, kk_ref, ka_ref, rk_ref, e_ref,
               rt_ref, at_ref, bt_ref, kt_ref, bh_ref, kh_ref, v_ref, wc_ref, bonus_ref, g_ref, sbuf):
    @pl.when(t == 0)
    def _():
        sbuf[0:8, :] = jnp.zeros((8, COLS_C), F32)

    def shifted(p, start):
        cols = slice(start, start + p.shape[1])
        sbuf[8:8 + tm, cols] = p
        prev = sbuf[7:7 + tm, cols]
        sbuf[0:8, cols] = p[tm - 8:tm, :]
        return p + (prev - p) * mu_ref[:, cols]

    c = WIDTH_C
    p_lo = proj(3 * c, COLS_C - 3 * c)
    p_k = proj(c, c)
    u_lo = shifted(p_lo, 3 * c)
    yield
    w_lo = u_lo[:, 0:DECAY_LORA]
    a_lo = u_lo[:, DECAY_LORA:DECAY_LORA + AAA_LORA]
    g_lo = u_lo[:, DECAY_LORA + AAA_LORA:]
    w = w0_ref[...] + _dot(jnp.tanh(w_lo).astype(BF16), w2_ref[...])
    w = -_softplus(-w) - 0.5
    ld = -jnp.exp(w)
    p_r = proj(0, c)
    yield
    a = _sigmoid(a0_ref[...] + _dot(a_lo.astype(BF16), a2_ref[...]))
    g_ref[0] = _dot(_sigmoid(g_lo).astype(BF16), g2_ref[...]).astype(BF16)
    k = shifted(p_k, c)
    p_v = proj(2 * c, c)
    yield

    e = e_ref[...]
    kk = k * kk_ref[...]
    nrm2 = _dot((kk * kk).astype(BF16), e)
    kkn = kk * lax.rsqrt(jnp.maximum(nrm2, 1e-24))
    k2 = k * (1.0 + (a - 1.0) * ka_ref[...])
    kka = kkn * a
    r = shifted(p_r, 0)
    v = shifted(p_v, 2 * c)
    v_ref[0] = v.astype(BF16)
    bonus_ref[0] = (_dot((r * k2 * rk_ref[...]).astype(BF16), e) * v).astype(BF16)
    yield

    ri = lax.broadcasted_iota(jnp.int32, (2 * CHUNK, CHUNK), 0)
    ci = lax.broadcasted_iota(jnp.int32, (2 * CHUNK, CHUNK), 1)
    tri = ((ci <= ri) | (ri >= CHUNK)).astype(BF16)
    tri2 = jnp.concatenate([tri, tri], axis=1)
    ld_hi, ld_lo = _split2(ld)
    for cc in range(tm // CHUNK):
        sl = slice(cc * CHUNK, (cc + 1) * CHUNK)
        cum = _dot(tri2, jnp.concatenate([ld_hi[sl], ld_lo[sl]], axis=0))
        lc = cum[0:CHUNK]
        tot = cum[CHUNK:2 * CHUNK]
        e_neg = jnp.exp(-lc)
        e_rem = jnp.exp(tot - lc)
        rt_ref[0, sl, :] = (r[sl] * jnp.exp(lc)).astype(BF16)
        at_ref[0, sl, :] = (-kkn[sl] * jnp.exp(lc - ld[sl])).astype(BF16)
        bt_ref[0, sl, :] = (kka[sl] * e_neg).astype(BF16)
        kt_ref[0, sl, :] = (k2[sl] * e_neg).astype(BF16)
        bh_ref[0, sl, :] = (kka[sl] * e_rem).astype(BF16)
        kh_ref[0, sl, :] = (k2[sl] * e_rem).astype(BF16)
        if cc % 2 == 1:
            yield
    seg = (lax.broadcasted_iota(jnp.int32, (16, tm), 1) // CHUNK
           == lax.broadcasted_iota(jnp.int32, (16, tm), 0)).astype(BF16)
    tots3 = jnp.concatenate(_split3(_dot(seg, ld_hi) + _dot(seg, ld_lo)), axis=0)
    per_blk = RWKV_TILE // CHUNK
    for blk in range(tm // RWKV_TILE):
        put = (lax.broadcasted_iota(jnp.int32, (48, LANES), 0) % 16
               == lax.broadcasted_iota(jnp.int32, (48, LANES), 1) + blk * per_blk).astype(BF16)
        wc_ref[0, blk] = jnp.exp(_dot_tn(tots3, put))


def _bmm(a, b):
    return lax.dot_general(a, b, (((2,), (1,)), ((0,), (0,))), preferred_element_type=F32)


def _bmm_nt(a, b):
    return lax.dot_general(a, b, (((2,), (2,)), ((0,), (0,))), preferred_element_type=F32)


def _bmm_tn(a, b):
    return lax.dot_general(a, b, (((1,), (1,)), ((0,), (0,))), preferred_element_type=F32)


def _rwkv_chunk_body(rt_ref, at_ref, bt_ref, kt_ref, bh_ref, kh_ref, v_ref, wc_ref, bonus_ref, g_ref,
                     lg_ref, lb_ref, e_ref, o_ref, h_ref):
    rows, tm = rt_ref.shape[0], rt_ref.shape[1]
    nc = tm // CHUNK
    ng = N_HEADS_C // RWKV_GROUP
    gl = RWKV_GROUP * HEAD_DIM
    units = [(r, g) for r in range(rows) for g in range(ng)]
    nu = len(units)
    t = pl.program_id(1)

    @pl.when(t == 0)
    def _():
        h_ref[...] = jnp.zeros_like(h_ref)

    lane_head = lax.broadcasted_iota(jnp.int32, (1, CHUNK, gl), 2) // HEAD_DIM
    ri = lax.broadcasted_iota(jnp.int32, (1, 2 * CHUNK, gl), 1)
    ci = lax.broadcasted_iota(jnp.int32, (1, 2 * CHUNK, gl), 2) & (CHUNK - 1)
    rr = ri & (CHUNK - 1)
    keep = (rr > ci) | ((ri >= CHUNK) & (rr == ci))
    eye = (lax.broadcasted_iota(jnp.int32, (1, CHUNK, gl), 1)
           == (lax.broadcasted_iota(jnp.int32, (1, CHUNK, gl), 2) & (CHUNK - 1)))
    same_head = (lax.broadcasted_iota(jnp.int32, (1, gl, gl), 1) // HEAD_DIM
                 == lax.broadcasted_iota(jnp.int32, (1, gl, gl), 2) // HEAD_DIM)

    def bdiag(x):
        return jnp.concatenate([jnp.where(lane_head == hh, x, jnp.zeros_like(x)) for hh in range(RWKV_GROUP)],
                               axis=1)

    def chunk_local(cs, out):
        tile = lambda ref: jnp.stack([ref[r, c * CHUNK:(c + 1) * CHUNK, g * gl:(g + 1) * gl]
                                      for c in cs for r, g in units])
        rt, at, bt, kt, bh, kh, v = (tile(r) for r in (rt_ref, at_ref, bt_ref, kt_ref, bh_ref, kh_ref, v_ref))
        ar = jnp.concatenate([at, rt], axis=1)
        sb = jnp.where(keep, _bmm_nt(ar, bdiag(bt)), 0.0)
        sk = jnp.where(keep, _bmm_nt(ar, bdiag(kt)), 0.0).astype(BF16)
        m_rb = sb[:, CHUNK:].astype(BF16)
        yield
        a = sb[:, :CHUNK]
        tinv = jnp.where(eye, 1.0, 0.0) + a
        ab = a.astype(BF16)
        x = _bmm(ab, bdiag(ab))
        yield
        for _ in range(4):
            xb = x.astype(BF16)
            r = _bmm(jnp.concatenate([tinv.astype(BF16), xb], axis=1), bdiag(xb))
            tinv = tinv + r[:, :CHUNK]
            x = r[:, CHUNK:]
            yield
        tinv = (tinv + _bmm(tinv.astype(BF16), bdiag(x.astype(BF16)))).astype(BF16)
        bdv = bdiag(v)
        av = _bmm(sk[:, :CHUNK], bdv)
        yield
        p = _bmm(tinv, bdiag(at)).astype(BF16)
        u0 = _bmm(tinv, bdiag(av.astype(BF16)))
        yield
        q = (rt.astype(F32) + _bmm(m_rb, bdiag(p))).astype(BF16)
        y0 = _bmm(m_rb, bdiag(u0.astype(BF16))) + _bmm(sk[:, CHUNK:], bdv)
        parts = (jnp.concatenate([q, p], axis=1), jnp.concatenate([y0, u0], axis=1),
                 jnp.concatenate([bh, kh], axis=1), v)
        out += [tuple(z[n * nu:(n + 1) * nu] for z in parts) for n in range(len(cs))]
        yield

    wct = [wc_ref[r, 0] for r in range(rows)]
    state = [h_ref[...]]
    ys = []
    local = []

    def recur(c):
        qp, yu0, bk, v = local[c]
        yu = _bmm(qp, state[0].astype(BF16)) + yu0
        ys.append(yu[:, :CHUNK])
        upd = _bmm_tn(bk, jnp.concatenate([yu[:, CHUNK:].astype(BF16), v], axis=1))
        wcol = jnp.stack([wct[r][g * gl:(g + 1) * gl, c:c + 1] for r, g in units])
        state[0] = jnp.where(same_head, upd, 0.0) + wcol * state[0]

    assert sorted(c for cs in RWKV_SCHEDULE for c in cs) == list(range(nc))
    done = 0
    for cs in RWKV_SCHEDULE:
        pending = list(range(done, len(local)))
        slots = {(k + 1) * RWKV_STAGES // (len(pending) + 1): c for k, c in enumerate(pending)}
        for n, _ in enumerate(chunk_local(list(cs), local), start=1):
            if n in slots:
                recur(slots[n])
                done += 1
    for c in range(done, nc):
        recur(c)
    h_ref[...] = state[0]

    y = jnp.concatenate([jnp.concatenate([yc[r * ng + g] for g in range(ng)], axis=-1)
                         for r in range(rows) for yc in ys], axis=0)
    e = e_ref[...]

    def head_mean(z):
        return _dot(z.astype(BF16), e) * (1.0 / HEAD_DIM)

    d = y - head_mean(y)
    yn = d * lax.rsqrt(head_mean(d * d) + LNX_EPS)
    flat = lambda ref: ref[...].astype(F32).reshape(rows * tm, WIDTH_C)
    out = (yn * lg_ref[...] + lb_ref[...] + flat(bonus_ref)) * flat(g_ref)
    o_ref[...] = out.astype(BF16).reshape(rows, tm, WIDTH_C)


def _rwkv_chunk(rt, at, bt, kt, bh, kh, v, wc, bonus, g, lnx_g, lnx_b, e):
    b, t, _ = rt.shape
    tm = RWKV_TILE
    rows = math.gcd(RWKV_ROWS, b)
    gl = RWKV_GROUP * HEAD_DIM
    const = lambda shape: pl.BlockSpec(shape, lambda i, j: (0,) * len(shape))
    tok = lambda: pl.BlockSpec((rows, tm, WIDTH_C), lambda i, j: (i, j, 0))
    return pl.pallas_call(
        _rwkv_chunk_body,
        grid=(b // rows, t // tm),
        in_specs=[tok(), tok(), tok(), tok(), tok(), tok(), tok(),
                  pl.BlockSpec((rows, 1, WIDTH_C, LANES), lambda i, j: (i, j, 0, 0)),
                  tok(), tok(), const(lnx_g.shape), const(lnx_b.shape), const(e.shape)],
        out_specs=tok(),
        out_shape=jax.ShapeDtypeStruct((b, t, WIDTH_C), BF16),
        scratch_shapes=[pltpu.VMEM((rows * (N_HEADS_C // RWKV_GROUP), gl, gl), F32)],
        compiler_params=_cparams(2),
        name="rwkv_chunk",
    )(rt, at, bt, kt, bh, kh, v, wc, bonus, g, lnx_g, lnx_b, e)


def _outproj_body(x_ref, ya_ref, yb_ref, yc_ref, wa_ref, wb_ref, wc_ref, o_ref):
    acc = _dot_tn(ya_ref[0], wa_ref[...])
    acc = acc + _dot_tn(yb_ref[0], wb_ref[...])
    acc = acc + _dot(yc_ref[0], wc_ref[...])
    o_ref[0] = x_ref[0] + acc


def _outproj(x, yaT, ybT, yc, wa, wb, wc):
    b, t, d = x.shape
    tm = math.gcd(t, OUTPROJ_TILE)
    const = lambda shape: pl.BlockSpec(shape, lambda i, j: (0,) * len(shape))
    return pl.pallas_call(
        _outproj_body,
        grid=(b, t // tm),
        in_specs=[pl.BlockSpec((1, tm, d), lambda i, j: (i, j, 0)),
                  pl.BlockSpec((1, WIDTH_A, tm), lambda i, j: (i, 0, j)),
                  pl.BlockSpec((1, WIDTH_B, tm), lambda i, j: (i, 0, j)),
                  pl.BlockSpec((1, tm, WIDTH_C), lambda i, j: (i, j, 0)),
                  const(wa.shape), const(wb.shape), const(wc.shape)],
        out_specs=pl.BlockSpec((1, tm, d), lambda i, j: (i, j, 0)),
        out_shape=jax.ShapeDtypeStruct((b, t, d), F32),
        compiler_params=_cparams(2),
        name="outproj",
    )(x, yaT, ybT, yc, wa, wb, wc)


def _ffn_body(x_ref, g_ref, wg_ref, wv_ref, cw_ref, cb_ref, wd_ref, o_ref, gbuf):
    tm = x_ref.shape[1]
    t = pl.program_id(1)

    @pl.when(t == 0)
    def _():
        gbuf[0:8, :] = jnp.zeros((8, D_FF), F32)

    x = x_ref[0]
    ms = jnp.mean(x * x, axis=-1, keepdims=True)
    h = (x * lax.rsqrt(ms + RMS_EPS) * g_ref[...]).astype(BF16)
    acc = x
    for lo, hi in zip(FF_SPLITS[:-1], FF_SPLITS[1:]):
        gate = _dot(h, wg_ref[:, lo:hi])
        val = _dot(h, wv_ref[:, lo:hi])
        gbuf[8:8 + tm, lo:hi] = gate
        g1 = gbuf[7:7 + tm, lo:hi]
        g2 = gbuf[6:6 + tm, lo:hi]
        gbuf[0:8, lo:hi] = gate[tm - 8:tm, :]
        conv = cb_ref[:, lo:hi] + g2 * cw_ref[0:1, lo:hi] + g1 * cw_ref[1:2, lo:hi] + gate * cw_ref[2:3, lo:hi]
        act = (conv * _sigmoid(conv) * val).astype(BF16)
        acc = acc + _dot(act, wd_ref[lo:hi, :])
    o_ref[0] = acc


def _ffn(x, g, wg, wv, cw, cb, wd):
    b, t, d = x.shape
    tm = TOK_TILE
    const = lambda shape: pl.BlockSpec(shape, lambda i, j: (0,) * len(shape),
                                       pipeline_mode=pl.Buffered(1))
    return pl.pallas_call(
        _ffn_body,
        grid=(b, t // tm),
        in_specs=[pl.BlockSpec((1, tm, d), lambda i, j: (i, j, 0)),
                  const(g.shape), const(wg.shape), const(wv.shape), const(cw.shape), const(cb.shape),
                  const(wd.shape)],
        out_specs=pl.BlockSpec((1, tm, d), lambda i, j: (i, j, 0)),
        out_shape=jax.ShapeDtypeStruct((b, t, d), F32),
        scratch_shapes=[pltpu.VMEM((tm + 8, D_FF), F32)],
        compiler_params=_cparams(2),
        name="ffn",
    )(x, g, wg, wv, cw, cb, wd)


def _pad_heads_cols(w, nh):
    d = w.shape[0]
    w = w.reshape(d, nh, HEAD_DIM)
    return jnp.pad(w, ((0, 0), (0, 0), (0, LANES - HEAD_DIM))).reshape(d, nh * LANES)


def _qk_bound(gq, gk):
    scale = HEAD_DIM ** -0.5
    return 1.01 * HEAD_DIM * scale * jnp.max(jnp.abs(gq)) * jnp.max(jnp.abs(gk)) + 0.05


def _layer_params(l, w_in, q_norm_a, k_norm_a, rel_bias, q_norm_b, k_norm_b, forget_bias):
    w = w_in[l]
    scale = HEAD_DIM ** -0.5
    a0, b0 = 0, 3 * WIDTH_A
    c0 = b0 + 3 * WIDTH_B + N_HEADS_B
    qa, ka, va = (w[:, a0 + i * WIDTH_A:a0 + (i + 1) * WIDTH_A] for i in range(3))
    qb, kb, vb = (w[:, b0 + i * WIDTH_B:b0 + (i + 1) * WIDTH_B] for i in range(3))
    fg = w[:, b0 + 3 * WIDTH_B:c0]
    wc = w[:, c0:]
    wn = jnp.concatenate([_pad_heads_cols(kb, N_HEADS_B),
                          jnp.pad(fg, ((0, 0), (0, LANES - N_HEADS_B))), wc], axis=1).astype(BF16)
    pad_v = lambda v, nh: jnp.pad(v.reshape(-1, nh, HEAD_DIM),
                                  ((0, 0), (0, 0), (0, VB_ROWS - HEAD_DIM))).reshape(-1, nh * VB_ROWS)
    assert QB_ROWS == VB_ROWS
    wt = jnp.concatenate([qa, ka, pad_v(va, N_HEADS_A), pad_v(qb, N_HEADS_B), pad_v(vb, N_HEADS_B),
                          jnp.pad(fg, ((0, 0), (0, 16 - N_HEADS_B)))], axis=1).T.astype(BF16)
    pad = LANES - HEAD_DIM
    gka = k_norm_a[l].reshape(HEAD_DIM, 1)
    gkb = jnp.pad(k_norm_b[l], (0, pad)).reshape(1, LANES)
    gqa = (q_norm_a[l] * (scale * LOG2E)).reshape(HEAD_DIM, 1)
    gqb = jnp.pad(q_norm_b[l] * (scale * LOG2E), (0, QB_ROWS - HEAD_DIM)).reshape(QB_ROWS, 1)
    fbn = jnp.pad(forget_bias[l], (0, LANES - N_HEADS_B)).reshape(1, LANES)
    fbt = jnp.pad(forget_bias[l], (0, 16 - N_HEADS_B)).reshape(16, 1)
    shift_b = (LOG2E * _qk_bound(q_norm_b[l], k_norm_b[l])).reshape(1, 1)
    bound_a = _qk_bound(q_norm_a[l], k_norm_a[l])
    tab = rel_bias[l]
    shift_a = LOG2E * (bound_a + jnp.max(tab, axis=1))
    depth_a = jnp.max(shift_a + LOG2E * (bound_a - tab[:, MAX_REL]))
    return wn, wt, gka, gkb, gqa, gqb, fbn, fbt, shift_b, shift_a, depth_a


def _selectors():
    import numpy as np
    selk = np.zeros((LANES, N_HEADS_B * LANES), np.float32)
    selq = np.zeros((N_HEADS_B * QB_ROWS, 3 * 16), np.float32)
    for p in range(3):
        for hh in range(N_HEADS_B):
            selk[p * N_HEADS_B + hh, hh * LANES + HEAD_DIM + 3 + p] = -1.0
            selq[hh * QB_ROWS + HEAD_DIM + p, 16 * p + hh] = 1.0
    e = np.kron(np.eye(N_HEADS_C, dtype=np.float32), np.ones((HEAD_DIM, HEAD_DIM), np.float32))
    return jnp.asarray(selk, BF16), jnp.asarray(selq, BF16), jnp.asarray(e, BF16)


def kernel(x, mix_norm_g, w_in, q_norm_a, k_norm_a, rel_bias, q_norm_b, k_norm_b, forget_bias, shift_mu, w0, w2,
           a0, a2, g2, k_k, k_a, r_k, lnx_g, lnx_b, w_out, ffn_norm_g, w_up, conv_w, conv_b, w_down):
    depth = w_in.shape[0]
    b, t, d = x.shape
    selk, selq, e = _selectors()
    row = lambda v: v.reshape(1, -1)
    for l in range(depth):
        wn, wt, gka, gkb, gqa, gqb, fbn, fbt, shift_b, shift_a, depth_a = _layer_params(
            l, w_in, q_norm_a, k_norm_a, rel_bias, q_norm_b, k_norm_b, forget_bias)
        rwkv_params = (row(shift_mu[l]), row(w0[l]), w2[l].astype(BF16), row(a0[l]), a2[l].astype(BF16),
                       g2[l].astype(BF16), row(k_k[l]), row(k_a[l]), row(r_k[l]), e)
        (qaT, vaT, kA, qbT, vbT, kB, rt, at, bt, kt, bh, kh, v, wcum, bonus, g) = _inproj(
            x, row(mix_norm_g[l]), wn, wt, gka, gkb, gqa, gqb, fbn, fbt, selk, selq, shift_b, rwkv_params)
        bias = _relbias(rel_bias[l], shift_a)
        yaT = lax.cond(depth_a <= 2 * FAST_MAX_SHIFT,
                       functools.partial(_attn_a, online=False), functools.partial(_attn_a, online=True),
                       qaT, kA, vaT, bias)
        ybT = lax.cond(shift_b[0, 0] <= FAST_MAX_SHIFT,
                       functools.partial(_fox, online=False), functools.partial(_fox, online=True),
                       qbT, kB, vbT)
        yc = _rwkv_chunk(rt, at, bt, kt, bh, kh, v, wcum, bonus, g, row(lnx_g[l]), row(lnx_b[l]), e)
        wo = w_out[l].astype(BF16)
        x = _outproj(x, yaT.reshape(b, WIDTH_A, t), ybT.reshape(b, WIDTH_B, t), yc,
                     wo[:WIDTH_A], wo[WIDTH_A:WIDTH_A + WIDTH_B], wo[WIDTH_A + WIDTH_B:])
        wu = w_up[l].astype(BF16)
        x = _ffn(x, row(ffn_norm_g[l]), wu[:, :D_FF], wu[:, D_FF:], conv_w[l], row(conv_b[l]),
                 w_down[l].astype(BF16))
    return x
```
